```python
import math
import jax, jax.numpy as jnp
from jax import lax
import numpy as np

D_MODEL = 1024
BATCH = 8
SEQ = 2048
DEPTH = 2

CTX_LEN = 256
GRID_W = 64
N_BRANCH = 3
BRANCH_W = D_MODEL // 2
RET_HEADS = 4
RET_DK = BRANCH_W // RET_HEADS
DIFF_HEADS = 4
DIFF_DV = BRANCH_W // DIFF_HEADS
DIFF_DQK = DIFF_DV // 2
MLP_GROUPS = 4
MLP_GW = BRANCH_W // MLP_GROUPS
CHUNK = 128
RET_CHUNK = 128
Q_BLOCK = 128
ROPE_BASE = 10000.0
EPS = 1e-6
KV_SIZES = (BRANCH_W, BRANCH_W, BRANCH_W, BRANCH_W)
REST_SIZES = (BRANCH_W,) * 7 + (N_BRANCH * D_MODEL,)
IN_SIZES = KV_SIZES + REST_SIZES
KV_COLS = 4 * BRANCH_W
IN_COLS = KV_COLS + 7 * BRANCH_W + N_BRANCH * D_MODEL

kernel_name = "hybrid_retention_gmlp_diffattn_prefix_dit"


def rms_norm(x, g=None):
    xf = x.astype(jnp.float32)
    y = xf * lax.rsqrt(jnp.mean(xf * xf, axis=-1, keepdims=True) + EPS)
    if g is not None:
        y = y * g.astype(jnp.float32)
    return y.astype(x.dtype)


def layer_norm(x):
    xf = x.astype(jnp.float32)
    mu = jnp.mean(xf, axis=-1, keepdims=True)
    var = jnp.mean(jnp.square(xf - mu), axis=-1, keepdims=True)
    return ((xf - mu) * lax.rsqrt(var + EPS)).astype(x.dtype)


def modulate(x, g, shift, scale):
    return rms_norm(x, g) * (1.0 + scale) + shift


def split_cols(z, sizes):
    idx = np.cumsum(sizes)[:-1].tolist()
    return jnp.split(z, idx, axis=-1)


def to_heads(t, n_heads):
    b, n, w = t.shape
    return t.reshape(b, n, n_heads, w // n_heads).transpose(0, 2, 1, 3)


def to_diff_heads(t):
    b, n, _ = t.shape
    return t.reshape(b, n, DIFF_HEADS, 2, DIFF_DQK).transpose(0, 2, 3, 1, 4)


def from_heads(t):
    b, h, n, d = t.shape
    return t.transpose(0, 2, 1, 3).reshape(b, n, h * d)


def rope_angles(row_pos, col_pos, head_dim):
    n_freq = head_dim // 4
    inv = ROPE_BASE ** (-jnp.arange(n_freq, dtype=jnp.float32) / n_freq)
    ang = jnp.concatenate([row_pos[:, None] * inv, col_pos[:, None] * inv], axis=-1)
    return jnp.cos(ang), jnp.sin(ang)


def apply_rope(x, cos, sin):
    x1, x2 = jnp.split(x, 2, axis=-1)
    cos = cos.astype(x.dtype)
    sin = sin.astype(x.dtype)
    return jnp.concatenate([x1 * cos - x2 * sin, x1 * sin + x2 * cos], axis=-1)


def retention_chunkwise(q, k, v, log_gamma, state0):
    in_dtype = q.dtype
    q, k, v = q.astype(jnp.float32), k.astype(jnp.float32), v.astype(jnp.float32)
    b, h, n, dk = q.shape
    dv = v.shape[-1]
    nc = n // RET_CHUNK
    qc = q.reshape(b, h, nc, RET_CHUNK, dk)
    kc = k.reshape(b, h, nc, RET_CHUNK, dk)
    vc = v.reshape(b, h, nc, RET_CHUNK, dv)
    pos = jnp.arange(RET_CHUNK, dtype=jnp.float32)
    lg = log_gamma.astype(jnp.float32)[:, None]
    dist = pos[:, None] - pos[None, :]
    d_in = jnp.where(dist >= 0, jnp.exp(lg[:, :, None] * jnp.maximum(dist, 0.0)), 0.0)
    q_dec = jnp.exp(lg * (pos + 1.0))
    k_dec = jnp.exp(lg * (RET_CHUNK - 1.0 - pos))
    chunk_dec = jnp.exp(lg[:, 0] * RET_CHUNK)
    s = jnp.einsum('bhnid,bhnjd->bhnij', qc, kc) * d_in[None, :, None]
    inner = jnp.einsum('bhnij,bhnje->bhnie', s, vc)
    kv = jnp.einsum('bhnjd,hj,bhnje->bhnde', kc, k_dec, vc)

    def step(state, kv_n):
        return chunk_dec[None, :, None, None] * state + kv_n, state

    final, prev = lax.scan(step, state0.astype(jnp.float32), jnp.moveaxis(kv, 2, 0))
    prev = jnp.moveaxis(prev, 0, 2)
    cross = jnp.einsum('bhnid,hi,bhnde->bhnie', qc, q_dec, prev)
    out = (inner + cross).reshape(b, h, n, dv)
    return out.astype(in_dtype), final


def retention_final_state(k, v, log_gamma):
    n = k.shape[2]
    w = jnp.exp(log_gamma[:, None] * (n - 1.0 - jnp.arange(n, dtype=jnp.float32)))
    return jnp.einsum('bhjd,hj,bhje->bhde', k.astype(jnp.float32), w, v.astype(jnp.float32))


def bidir_retention(q, k, v, log_gamma, state_f, state_b):
    out_f, fin_f = retention_chunkwise(q, k, v, log_gamma[0], state_f)
    out_b, fin_b = retention_chunkwise(jnp.flip(q, 2), jnp.flip(k, 2), jnp.flip(v, 2), log_gamma[1], state_b)
    return out_f + jnp.flip(out_b, 2), fin_f, fin_b


def diff_attend(q, k, v, lam):
    s = jnp.einsum('bhmqd,bhmkd->bhmqk', q.astype(jnp.float32), k.astype(jnp.float32)) * (DIFF_DQK ** -0.5)
    p = jax.nn.softmax(s, axis=-1)
    a = p[:, :, 0] - lam * p[:, :, 1]
    return jnp.einsum('bhqk,bhkv->bhqv', a, v.astype(jnp.float32)).astype(v.dtype)


def diff_attend_blocked(q, k, v, lam):
    b, h, _, n, d = q.shape
    nb = n // Q_BLOCK
    qb = jnp.moveaxis(q.reshape(b, h, 2, nb, Q_BLOCK, d), 3, 0)
    o = lax.map(lambda blk: diff_attend(blk, k, v, lam), qb)
    return jnp.moveaxis(o, 0, 2).reshape(b, h, n, -1)


def chunk_mlp(u, v, gate, w_s, b_s):
    b, n, _ = u.shape
    nch = n // CHUNK
    vn = layer_norm(v).reshape(b, nch, CHUNK, MLP_GROUPS, MLP_GW)
    sp = jnp.einsum('gij,bnjgc->bnigc', w_s, vn) + b_s.T[:, :, None]
    return u * sp.reshape(b, n, BRANCH_W) * jax.nn.silu(gate)


def merge_branches(o_ret, o_mlp, o_diff, merge_logits, w_bo, w_o):
    g_ret, g_mlp, g_diff = jnp.split(jax.nn.sigmoid(merge_logits), N_BRANCH, axis=-1)
    y = g_ret * (o_ret @ w_bo[0]) + g_mlp * (o_mlp @ w_bo[1]) + g_diff * (o_diff @ w_bo[2])
    return y @ w_o


def setup_inputs(seed: int = 0) -> dict:
    key = jax.random.key(seed)
    ks = jax.random.split(key, 16)
    f32 = jnp.float32
    nrm = jax.random.normal
    x = nrm(ks[0], (BATCH, SEQ, D_MODEL), f32)
    c = nrm(ks[1], (BATCH, D_MODEL), f32)
    ctx = nrm(ks[2], (BATCH, CTX_LEN, D_MODEL), f32)
    c_ctx = nrm(ks[3], (D_MODEL,), f32)
    w_mod = nrm(ks[4], (DEPTH, D_MODEL, 3 * D_MODEL), f32) * (0.5 * D_MODEL ** -0.5)
    b_mod = 0.02 * nrm(ks[5], (DEPTH, 3 * D_MODEL), f32)
    g_pre = 1.0 + 0.05 * nrm(ks[6], (DEPTH, D_MODEL), f32)
    g_post = 1.0 + 0.05 * nrm(ks[7], (DEPTH, D_MODEL), f32)
    w_in = nrm(ks[8], (DEPTH, D_MODEL, IN_COLS), f32) * (D_MODEL ** -0.5)
    base = jnp.log(2.0 ** (5.0 + jnp.arange(RET_HEADS, dtype=f32)) - 1.0)
    ret_decay_logit = base[None, None, :] + 0.1 * nrm(ks[9], (DEPTH, 2, RET_HEADS), f32)
    mlp_w_s = nrm(ks[10], (DEPTH, MLP_GROUPS, CHUNK, CHUNK), f32) * (CHUNK ** -0.5)
    mlp_b_s = 1.0 + 0.1 * nrm(ks[11], (DEPTH, MLP_GROUPS, CHUNK), f32)
    diff_lambda_q = 0.1 * nrm(ks[12], (DEPTH, 2, DIFF_DQK), f32)
    diff_lambda_k = 0.1 * nrm(ks[13], (DEPTH, 2, DIFF_DQK), f32)
    w_branch_out = nrm(ks[14], (DEPTH, N_BRANCH, BRANCH_W, D_MODEL), f32) * (BRANCH_W ** -0.5)
    w_out = nrm(ks[15], (DEPTH, D_MODEL, D_MODEL), f32) * (D_MODEL ** -0.5)
    return {"x": x, "c": c, "ctx": ctx, "c_ctx": c_ctx, "w_mod": w_mod, "b_mod": b_mod,
            "g_pre": g_pre, "g_post": g_post, "w_in": w_in, "ret_decay_logit": ret_decay_logit,
            "mlp_w_s": mlp_w_s, "mlp_b_s": mlp_b_s, "diff_lambda_q": diff_lambda_q,
            "diff_lambda_k": diff_lambda_k, "w_branch_out": w_branch_out, "w_out": w_out}


def reference(x, c, ctx, c_ctx, w_mod, b_mod, g_pre, g_post, w_in, ret_decay_logit,
              mlp_w_s, mlp_b_s, diff_lambda_q, diff_lambda_k, w_branch_out, w_out):
    b = x.shape[0]
    n_lat = x.shape[1]
    rows = n_lat // GRID_W
    row_pos = jnp.repeat(jnp.arange(rows, dtype=jnp.float32), GRID_W)
    col_pos = jnp.tile(jnp.arange(GRID_W, dtype=jnp.float32), rows)
    cos_r, sin_r = rope_angles(row_pos, col_pos, RET_DK)
    cos_d, sin_d = rope_angles(row_pos, col_pos, DIFF_DQK)
    silu_c = jax.nn.silu(c)
    silu_cc = jax.nn.silu(c_ctx)
    ret_scale = RET_DK ** -0.5

    for l in range(DEPTH):
        last = l == DEPTH - 1
        shift, scale, gate = jnp.split((silu_c @ w_mod[l] + b_mod[l])[:, None, :], 3, axis=-1)
        shift_c, scale_c, gate_c = jnp.split(silu_cc @ w_mod[l] + b_mod[l], 3, axis=-1)
        h = modulate(x, g_pre[l], shift, scale)
        hc = modulate(ctx, g_pre[l], shift_c, scale_c)
        log_gamma = -jax.nn.softplus(-ret_decay_logit[l].astype(jnp.float32))
        lam_init = 0.8 - 0.6 * math.exp(-0.3 * l)
        lam = (jnp.exp(jnp.sum(diff_lambda_q[l, 0] * diff_lambda_k[l, 0]))
               - jnp.exp(jnp.sum(diff_lambda_q[l, 1] * diff_lambda_k[l, 1])) + lam_init).astype(jnp.float32)

        if last:
            c_rk, c_rv, c_dk, c_dv = split_cols(hc @ w_in[l][:, :KV_COLS], KV_SIZES)
        else:
            zc = split_cols(hc @ w_in[l], IN_SIZES)
            c_rk, c_rv, c_dk, c_dv = zc[:4]
        crk = to_heads(c_rk, RET_HEADS) * ret_scale
        crv = to_heads(c_rv, RET_HEADS)
        cdk = to_diff_heads(c_dk)
        cdv = to_heads(c_dv, DIFF_HEADS)
        if last:
            s_f = retention_final_state(crk, crv, log_gamma[0])
            s_b = retention_final_state(jnp.flip(crk, 2), jnp.flip(crv, 2), log_gamma[1])
        else:
            c_rq, c_rg, c_dq, c_dg, c_mu, c_mv, c_mg, c_mlog = zc[4:]
            zero = jnp.zeros((b, RET_HEADS, RET_DK, RET_DK), jnp.float32)
            c_ret, s_f, s_b = bidir_retention(to_heads(c_rq, RET_HEADS), crk, crv, log_gamma, zero, zero)
            co_ret = from_heads(rms_norm(c_ret)) * jax.nn.silu(c_rg)
            c_diff = diff_attend(to_diff_heads(c_dq), cdk, cdv, lam)
            co_diff = from_heads(rms_norm(c_diff) * (1.0 - lam_init)) * jax.nn.silu(c_dg)
            co_mlp = chunk_mlp(c_mu, c_mv, c_mg, mlp_w_s[l], mlp_b_s[l])
            c_out = merge_branches(co_ret, co_mlp, co_diff, c_mlog, w_branch_out[l], w_out[l])
            ctx_next = ctx + gate_c * rms_norm(c_out, g_post[l])

        rk_, rv_, dk_, dv_, rq_, rg_, dq_, dg_, mu_, mv_, mg_, mlog_ = split_cols(h @ w_in[l], IN_SIZES)
        rq = apply_rope(to_heads(rq_, RET_HEADS), cos_r, sin_r)
        rk = apply_rope(to_heads(rk_, RET_HEADS), cos_r, sin_r) * ret_scale
        rv = to_heads(rv_, RET_HEADS)
        ret_o, _, _ = bidir_retention(rq, rk, rv, log_gamma, s_f, s_b)
        o_ret = from_heads(rms_norm(ret_o)) * jax.nn.silu(rg_)

        dq = apply_rope(to_diff_heads(dq_), cos_d, sin_d)
        dk = apply_rope(to_diff_heads(dk_), cos_d, sin_d)
        dv = to_heads(dv_, DIFF_HEADS)
        k_all = jnp.concatenate([cdk, dk], axis=3)
        v_all = jnp.concatenate([cdv, dv], axis=2)
        diff_o = diff_attend_blocked(dq, k_all, v_all, lam)
        o_diff = from_heads(rms_norm(diff_o) * (1.0 - lam_init)) * jax.nn.silu(dg_)

        o_mlp = chunk_mlp(mu_, mv_, mg_, mlp_w_s[l], mlp_b_s[l])
        out = merge_branches(o_ret, o_mlp, o_diff, mlog_, w_branch_out[l], w_out[l])
        x = x + gate * rms_norm(out, g_post[l])
        if not last:
            ctx = ctx_next
    return x
```

```python
import functools
import math

import jax
import jax.numpy as jnp
import numpy as np
from jax import lax
from jax.experimental import pallas as pl
from jax.experimental.pallas import tpu as pltpu

F32 = jnp.float32
BF16 = jnp.bfloat16

D_MODEL = 1024
DEPTH = 2
CTX_LEN = 256
GRID_W = 64
N_BRANCH = 3
BRANCH_W = D_MODEL // 2
HEADS = 4
HEAD_W = BRANCH_W // HEADS
DIFF_DQK = HEAD_W // 2
CHUNK = 128
ROPE_BASE = 10000.0
EPS = 1e-6
RET_SCALE = HEAD_W ** -0.5
DIFF_Q_SCALE = (DIFF_DQK ** -0.5) * math.log2(math.e)

LANES = 128
SEC = BRANCH_W
MERGE_COLS = N_BRANCH * D_MODEL
KV_COLS = 4 * BRANCH_W
VMEM_LIMIT = 56 * 1024 * 1024

FULL_KINDS = ("sigmoid",) * 6 + ("rk", "plain", "dk", "plain", "rq", "silu", "dq", "silu",
                                 "plain", "ln", "silu")
FULL_SEC = dict(rk=6, rv=7, dk=8, dv=9, rq=10, rg=11, dq=12, dg=13, mu=14, mv=15, mg=16)
KV_KINDS = ("rk", "plain", "dk", "plain")
KV_SEC = dict(rk=0, rv=1, dk=2, dv=3)


def _dot(a, b):
    return jnp.dot(a, b, preferred_element_type=F32)


def _dot_nt(a, b):
    return lax.dot_general(a, b, (((1,), (1,)), ((), ())), preferred_element_type=F32)


def _rms(v):
    return v * lax.rsqrt(jnp.mean(v * v, axis=-1, keepdims=True) + EPS)


def _mod_kernel(c_ref, w_ref, b_ref, o_ref):
    cv = c_ref[...]
    sc = cv * jax.nn.sigmoid(cv)
    o_ref[0] = jnp.dot(sc, w_ref[0], preferred_element_type=F32,
                       precision=lax.Precision.HIGHEST) + b_ref[0]


def _modulation(cc, w_mod, b_mod):
    rows = cc.shape[0]
    return pl.pallas_call(
        _mod_kernel,
        grid=(DEPTH, 3),
        in_specs=[
            pl.BlockSpec((rows, D_MODEL), lambda l, j: (0, 0)),
            pl.BlockSpec((1, D_MODEL, D_MODEL), lambda l, j: (l, 0, j)),
            pl.BlockSpec((1, 1, D_MODEL), lambda l, j: (l, 0, j)),
        ],
        out_specs=pl.BlockSpec((1, rows, D_MODEL), lambda l, j: (l, 0, j)),
        out_shape=jax.ShapeDtypeStruct((DEPTH, rows, 3 * D_MODEL), F32),
        compiler_params=pltpu.CompilerParams(vmem_limit_bytes=VMEM_LIMIT),
        name="modulation",
    )(cc, w_mod, b_mod.reshape(DEPTH, 1, 3 * D_MODEL))


def _rope_ret(a, c, s):
    outs = []
    for h in range(HEADS):
        ah = a[:, h * LANES:(h + 1) * LANES]
        outs.append(ah * c + pltpu.roll(ah, LANES // 2, 1) * s)
    return jnp.concatenate(outs, axis=1)


def _rope_diff(a, c, s_lo, s_hi):
    outs = []
    for h in range(HEADS):
        ah = a[:, h * LANES:(h + 1) * LANES]
        outs.append(ah * c + pltpu.roll(ah, LANES - DIFF_DQK // 2, 1) * s_lo
                    + pltpu.roll(ah, DIFF_DQK // 2, 1) * s_hi)
    return jnp.concatenate(outs, axis=1)


def _proj_kernel(*refs, kinds, rope, n_rows, sub):
    if rope:
        (x_ref, g_ref, shift_ref, scale_ref, w_ref,
         cr_ref, sr_ref, cd_ref, sdl_ref, sdh_ref, z_ref, h_ref) = refs
    else:
        x_ref, g_ref, shift_ref, scale_ref, w_ref, z_ref, h_ref = refs
    j = pl.program_id(1)
    n_sub = n_rows // sub

    @pl.when(j == 0)
    def _():
        gs = g_ref[...] * (1.0 + scale_ref[0])
        sh = shift_ref[0]
        for r in range(n_sub):
            xs = x_ref[0, r * sub:(r + 1) * sub, :]
            h_ref[r * sub:(r + 1) * sub, :] = (_rms(xs) * gs + sh).astype(BF16)

    def run(epilogue):
        for r in range(n_sub):
            rows = slice(r * sub, (r + 1) * sub)
            acc = _dot(h_ref[rows, :], w_ref[...])
            z_ref[0, rows, :] = epilogue(acc, rows).astype(BF16)

    def epi_rk(a, rows):
        if rope:
            a = _rope_ret(a, cr_ref[rows, :], sr_ref[rows, :])
        return a * RET_SCALE

    def epi_rq(a, rows):
        return _rope_ret(a, cr_ref[rows, :], sr_ref[rows, :]) if rope else a

    def epi_dk(a, rows):
        return _rope_diff(a, cd_ref[rows, :], sdl_ref[rows, :], sdh_ref[rows, :]) if rope else a

    def epi_dq(a, rows):
        return epi_dk(a, rows) * DIFF_Q_SCALE

    def epi_ln(a, rows):
        mu = jnp.mean(a, axis=-1, keepdims=True)
        d = a - mu
        return d * lax.rsqrt(jnp.mean(d * d, axis=-1, keepdims=True) + EPS)

    epilogues = dict(
        plain=lambda a, rows: a,
        sigmoid=lambda a, rows: jax.nn.sigmoid(a),
        silu=lambda a, rows: a * jax.nn.sigmoid(a),
        rk=epi_rk, rq=epi_rq, dk=epi_dk, dq=epi_dq, ln=epi_ln)

    for kind in sorted(set(kinds)):
        idx = [i for i, k in enumerate(kinds) if k == kind]
        cond = j == idx[0]
        for i in idx[1:]:
            cond = cond | (j == i)
        pl.when(cond)(functools.partial(run, epilogues[kind]))


def _projection(xs, g_pre, shift, scale, w, kinds, tables):
    n_g, n_rows, _ = xs.shape
    n_sec = len(kinds)
    rope = tables is not None
    in_specs = [
        pl.BlockSpec((1, n_rows, D_MODEL), lambda b, j: (b, 0, 0)),
        pl.BlockSpec((1, D_MODEL), lambda b, j: (0, 0)),
        pl.BlockSpec((1, 1, D_MODEL), lambda b, j: (b, 0, 0)),
        pl.BlockSpec((1, 1, D_MODEL), lambda b, j: (b, 0, 0)),
        pl.BlockSpec((D_MODEL, SEC), lambda b, j: (0, j)),
    ]
    args = [xs, g_pre.reshape(1, D_MODEL), shift, scale, w]
    if rope:
        in_specs += [pl.BlockSpec((n_rows, LANES), lambda b, j: (0, 0))] * len(tables)
        args += list(tables)
    return pl.pallas_call(
        functools.partial(_proj_kernel, kinds=kinds, rope=rope, n_rows=n_rows, sub=512),
        grid=(n_g, n_sec),
        in_specs=in_specs,
        out_specs=pl.BlockSpec((1, n_rows, SEC), lambda b, j: (b, 0, j)),
        out_shape=jax.ShapeDtypeStruct((n_g, n_rows, n_sec * SEC), BF16),
        scratch_shapes=[pltpu.VMEM((n_rows, D_MODEL), BF16)],
        compiler_params=pltpu.CompilerParams(
            dimension_semantics=("arbitrary", "arbitrary"), vmem_limit_bytes=VMEM_LIMIT),
        name="projection_rope" if rope else "projection",
    )(*args)


def _ret_kernel(*refs, n, rc, with_out, with_state_out):
    refs = list(refs)
    lg_ref = refs.pop(0)
    if with_out:
        q_ref = refs.pop(0)
        g_ref = refs.pop(0)
    k_ref = refs.pop(0)
    v_ref = refs.pop(0)
    sf0_ref = refs.pop(0)
    sb0_ref = refs.pop(0)
    if with_out:
        o_ref = refs.pop(0)
    if with_state_out:
        sf_out_ref = refs.pop(0)
        sb_out_ref = refs.pop(0)
    if with_out:
        sbs_ref = refs.pop(0)

    h = pl.program_id(1)
    lgf = lg_ref[0, h]
    lgb = lg_ref[1, h]
    nc = n // rc
    pos = lax.broadcasted_iota(jnp.int32, (rc, LANES), 0).astype(F32)
    kdf = jnp.exp(lgf * (rc - 1.0 - pos))
    kdb = jnp.exp(lgb * pos)
    cdf = jnp.exp(jnp.full((1, LANES), lgf * rc, F32))
    cdb = jnp.exp(jnp.full((1, LANES), lgb * rc, F32))

    def kv_state(nn, kdec):
        rows = slice(nn * rc, (nn + 1) * rc)
        kd = (k_ref[0, rows, :].astype(F32) * kdec).T.astype(BF16)
        return _dot(kd, v_ref[0, rows, :])

    sb = sb0_ref[0, 0]
    for nn in reversed(range(nc)):
        if with_out:
            sbs_ref[nn] = sb.astype(BF16)
        sb = cdb * sb + kv_state(nn, kdb)

    if with_out:
        ii = lax.broadcasted_iota(jnp.int32, (rc, rc), 0)
        jj = lax.broadcasted_iota(jnp.int32, (rc, rc), 1)
        dist = (ii - jj).astype(F32)
        dmat = (jnp.where(dist >= 0, jnp.exp(lgf * jnp.maximum(dist, 0.0)), 0.0)
                + jnp.where(dist <= 0, jnp.exp(lgb * jnp.maximum(-dist, 0.0)), 0.0))
        qdf = jnp.exp(lgf * (pos + 1.0))
        qdb = jnp.exp(lgb * (rc - pos))

    sf = sf0_ref[0, 0]
    for nn in range(nc):
        rows = slice(nn * rc, (nn + 1) * rc)
        if with_out:
            q = q_ref[0, rows, :]
            qf = q.astype(F32)
            s = _dot_nt(q, k_ref[0, rows, :]) * dmat
            inner = _dot(s.astype(BF16), v_ref[0, rows, :])
            qcat = jnp.concatenate([(qf * qdf).astype(BF16), (qf * qdb).astype(BF16)], axis=1)
            scat = jnp.concatenate([sf.astype(BF16), sbs_ref[nn]], axis=0)
            out = inner + _dot(qcat, scat)
            o_ref[0, rows, :] = (_rms(out) * g_ref[0, rows, :].astype(F32)).astype(BF16)
        sf = cdf * sf + kv_state(nn, kdf)

    if with_state_out:
        sf_out_ref[0, 0] = sf
        sb_out_ref[0, 0] = sb


def _retention(log_gamma, z, sec, sf0, sb0, *, rc, with_out, with_state_out):
    n_b, n, _ = z.shape

    def col(name):
        base = sec[name] * HEADS
        return pl.BlockSpec((1, n, LANES), lambda b, h: (b, 0, base + h))

    state_spec = pl.BlockSpec((1, 1, HEAD_W, HEAD_W), lambda b, h: (b, h, 0, 0))
    in_specs = [pl.BlockSpec(memory_space=pltpu.SMEM)]
    args = [log_gamma]
    if with_out:
        in_specs += [col("rq"), col("rg")]
        args += [z, z]
    in_specs += [col("rk"), col("rv"), state_spec, state_spec]
    args += [z, z, sf0, sb0]
    out_specs, out_shape, scratch = [], [], []
    if with_out:
        out_specs.append(pl.BlockSpec((1, n, LANES), lambda b, h: (b, 0, h)))
        out_shape.append(jax.ShapeDtypeStruct((n_b, n, BRANCH_W), BF16))
        scratch.append(pltpu.VMEM((n // rc, HEAD_W, HEAD_W), BF16))
    if with_state_out:
        out_specs += [state_spec, state_spec]
        out_shape += [jax.ShapeDtypeStruct((n_b, HEADS, HEAD_W, HEAD_W), F32)] * 2
    return pl.pallas_call(
        functools.partial(_ret_kernel, n=n, rc=rc, with_out=with_out,
                          with_state_out=with_state_out),
        grid=(n_b, HEADS),
        in_specs=in_specs,
        out_specs=out_specs,
        out_shape=out_shape,
        scratch_shapes=scratch,
        compiler_params=pltpu.CompilerParams(vmem_limit_bytes=VMEM_LIMIT),
        name="retention" + ("_out" if with_out else "") + ("_state" if with_state_out else ""),
    )(*args)


def _diff_kernel(*refs, tq, n_kv, post_scale):
    lam_ref, q_ref, g_ref = refs[:3]
    k_refs = refs[3:3 + 2 * n_kv:2]
    v_refs = refs[4:4 + 2 * n_kv:2]
    o_ref = refs[3 + 2 * n_kv]
    lam = lam_ref[0]
    q = q_ref[0]
    lane = lax.broadcasted_iota(jnp.int32, (tq, LANES), 1)
    zero = jnp.zeros_like(q)
    qq = jnp.concatenate([jnp.where(lane < DIFF_DQK, q, zero),
                          jnp.where(lane >= DIFF_DQK, q, zero)], axis=0)
    scores = [_dot_nt(qq, k_ref[0]) for k_ref in k_refs]
    m = jnp.max(scores[0], axis=-1, keepdims=True)
    for s in scores[1:]:
        m = jnp.maximum(m, jnp.max(s, axis=-1, keepdims=True))
    acc = None
    den = None
    for s, v_ref in zip(scores, v_refs):
        p = jnp.exp2(s - m)
        ps = jnp.sum(p, axis=-1, keepdims=True)
        pv = _dot(p.astype(BF16), v_ref[0])
        acc = pv if acc is None else acc + pv
        den = ps if den is None else den + ps
    o = acc / den
    d = o[:tq] - lam * o[tq:]
    o_ref[0] = (_rms(d) * post_scale * g_ref[0].astype(F32)).astype(BF16)


def _diff_attention(lam, zq, sec_q, kvs, *, tq, post_scale):
    n_b, nq, _ = zq.shape
    qb, gb = sec_q["dq"] * HEADS, sec_q["dg"] * HEADS
    in_specs = [
        pl.BlockSpec(memory_space=pltpu.SMEM),
        pl.BlockSpec((1, tq, LANES), lambda b, h, i: (b, i, qb + h)),
        pl.BlockSpec((1, tq, LANES), lambda b, h, i: (b, i, gb + h)),
    ]
    args = [lam, zq, zq]
    for z, sec in kvs:
        nk = z.shape[1]
        kb, vb = sec["dk"] * HEADS, sec["dv"] * HEADS
        in_specs.append(pl.BlockSpec((1, nk, LANES), lambda b, h, i, kb=kb: (b, 0, kb + h)))
        in_specs.append(pl.BlockSpec((1, nk, LANES), lambda b, h, i, vb=vb: (b, 0, vb + h)))
        args += [z, z]
    return pl.pallas_call(
        functools.partial(_diff_kernel, tq=tq, n_kv=len(kvs), post_scale=post_scale),
        grid=(n_b, HEADS, nq // tq),
        in_specs=in_specs,
        out_specs=pl.BlockSpec((1, tq, LANES), lambda b, h, i: (b, i, h)),
        out_shape=jax.ShapeDtypeStruct((n_b, nq, BRANCH_W), BF16),
        compiler_params=pltpu.CompilerParams(vmem_limit_bytes=VMEM_LIMIT),
        name="diff_attention_%d" % len(kvs),
    )(*args)


def _merge_kernel(x_ref, gate_ref, gpost_ref, oret_ref, odiff_ref, u_ref, vn_ref, mg_ref,
                  gr_ref, gm_ref, gd_ref, ws_ref, bs_ref, wbo_ref, wo_ref, out_ref, *, tm):
    rows = []
    for c in range(tm // CHUNK):
        cols = []
        for g in range(HEADS):
            blk = vn_ref[0, c * CHUNK:(c + 1) * CHUNK, g * LANES:(g + 1) * LANES]
            cols.append(_dot(ws_ref[g], blk) + bs_ref[g])
        rows.append(jnp.concatenate(cols, axis=1))
    sp = jnp.concatenate(rows, axis=0)
    o_mlp = (u_ref[0].astype(F32) * sp * mg_ref[0].astype(F32)).astype(BF16)
    t = (gr_ref[0].astype(F32) * _dot(oret_ref[0], wbo_ref[0])
         + gm_ref[0].astype(F32) * _dot(o_mlp, wbo_ref[1])
         + gd_ref[0].astype(F32) * _dot(odiff_ref[0], wbo_ref[2]))
    y = _dot(t.astype(BF16), wo_ref[...])
    out_ref[0] = x_ref[0] + gate_ref[0] * (_rms(y) * gpost_ref[...])


def _merge(xs, gate, g_post, o_ret, o_diff, z, sec, ws, bs, wbo, wo, *, tm):
    n_b, n, _ = xs.shape
    shared_gate = gate.shape[0] == 1

    def zsec(name):
        s = sec[name]
        return pl.BlockSpec((1, tm, SEC), lambda b, i: (b, i, s))

    def zgate(k):
        return pl.BlockSpec((1, tm, D_MODEL), lambda b, i: (b, i, k))

    const2 = lambda b, i: (0, 0)
    const3 = lambda b, i: (0, 0, 0)
    in_specs = [
        pl.BlockSpec((1, tm, D_MODEL), lambda b, i: (b, i, 0)),
        pl.BlockSpec((1, 1, D_MODEL), (lambda b, i: (0, 0, 0)) if shared_gate
                     else (lambda b, i: (b, 0, 0))),
        pl.BlockSpec((1, D_MODEL), const2),
        pl.BlockSpec((1, tm, BRANCH_W), lambda b, i: (b, i, 0)),
        pl.BlockSpec((1, tm, BRANCH_W), lambda b, i: (b, i, 0)),
        zsec("mu"), zsec("mv"), zsec("mg"), zgate(0), zgate(1), zgate(2),
        pl.BlockSpec((HEADS, CHUNK, CHUNK), const3),
        pl.BlockSpec((HEADS, CHUNK, LANES), const3),
        pl.BlockSpec((N_BRANCH, BRANCH_W, D_MODEL), const3),
        pl.BlockSpec((D_MODEL, D_MODEL), const2),
    ]
    return pl.pallas_call(
        functools.partial(_merge_kernel, tm=tm),
        grid=(n_b, n // tm),
        in_specs=in_specs,
        out_specs=pl.BlockSpec((1, tm, D_MODEL), lambda b, i: (b, i, 0)),
        out_shape=jax.ShapeDtypeStruct(xs.shape, F32),
        compiler_params=pltpu.CompilerParams(vmem_limit_bytes=VMEM_LIMIT),
        name="merge",
    )(xs, gate, g_post.reshape(1, D_MODEL), o_ret, o_diff, z, z, z, z, z, z, ws, bs, wbo, wo)


def _rope_tables(n_lat):
    rows = n_lat // GRID_W
    row_pos = jnp.repeat(jnp.arange(rows, dtype=F32), GRID_W)
    col_pos = jnp.tile(jnp.arange(GRID_W, dtype=F32), rows)

    def angles(head_dim):
        n_freq = head_dim // 4
        inv = ROPE_BASE ** (-jnp.arange(n_freq, dtype=F32) / n_freq)
        ang = jnp.concatenate([row_pos[:, None] * inv, col_pos[:, None] * inv], axis=-1)
        return jnp.cos(ang), jnp.sin(ang)

    cos_r, sin_r = angles(HEAD_W)
    cos_d, sin_d = angles(DIFF_DQK)
    zeros = jnp.zeros_like(sin_d)
    c_r = jnp.concatenate([cos_r, cos_r], axis=-1)
    s_r = jnp.concatenate([-sin_r, sin_r], axis=-1)
    c_d = jnp.tile(jnp.concatenate([cos_d, cos_d], axis=-1), (1, 2))
    s_lo = jnp.tile(jnp.concatenate([-sin_d, zeros], axis=-1), (1, 2))
    s_hi = jnp.tile(jnp.concatenate([zeros, sin_d], axis=-1), (1, 2))
    return c_r, s_r, c_d, s_lo, s_hi


@jax.jit
def kernel(x, c, ctx, c_ctx, w_mod, b_mod, g_pre, g_post, w_in, ret_decay_logit, mlp_w_s,
           mlp_b_s, diff_lambda_q, diff_lambda_k, w_branch_out, w_out):
    n_b, n_lat, _ = x.shape
    n_ctx = ctx.shape[1]
    tables = _rope_tables(n_lat)

    cond_rows = 16
    cc = jnp.zeros((cond_rows, D_MODEL), F32).at[:n_b].set(c).at[n_b].set(c_ctx)
    mod = _modulation(cc, w_mod, b_mod)

    log_gamma = -jax.nn.softplus(-ret_decay_logit.astype(F32))
    zero_state = jnp.zeros((n_b, HEADS, HEAD_W, HEAD_W), F32)

    for l in range(DEPTH):
        last = l == DEPTH - 1
        lam_init = 0.8 - 0.6 * math.exp(-0.3 * l)
        lam = (jnp.exp(jnp.sum(diff_lambda_q[l, 0] * diff_lambda_k[l, 0]))
               - jnp.exp(jnp.sum(diff_lambda_q[l, 1] * diff_lambda_k[l, 1]))
               + lam_init).astype(F32).reshape(1)
        shift, scale, gate = [mod[l, :n_b, k * D_MODEL:(k + 1) * D_MODEL].reshape(n_b, 1, D_MODEL)
                              for k in range(3)]
        shift_c, scale_c, gate_c = [mod[l, n_b:n_b + 1, k * D_MODEL:(k + 1) * D_MODEL]
                                    .reshape(1, 1, D_MODEL) for k in range(3)]
        w_l = w_in[l]
        w_full = jnp.concatenate([w_l[:, -MERGE_COLS:], w_l[:, :-MERGE_COLS]], axis=1).astype(BF16)
        ws = mlp_w_s[l].astype(BF16)
        bs = jnp.broadcast_to(mlp_b_s[l][:, :, None], (HEADS, CHUNK, LANES)).astype(F32)
        wbo = w_branch_out[l].astype(BF16)
        wo = w_out[l].astype(BF16)

        ctx_flat = ctx.reshape(1, n_b * n_ctx, D_MODEL)
        if last:
            w_kv = w_l[:, :KV_COLS].astype(BF16)
            zc = _projection(ctx_flat, g_pre[l], shift_c, scale_c, w_kv, KV_KINDS, None)
            zc = zc.reshape(n_b, n_ctx, -1)
            sec_c = KV_SEC
            s_f, s_b = _retention(log_gamma[l], zc, sec_c, zero_state, zero_state,
                                  rc=128, with_out=False, with_state_out=True)
        else:
            zc = _projection(ctx_flat, g_pre[l], shift_c, scale_c, w_full, FULL_KINDS, None)
            zc = zc.reshape(n_b, n_ctx, -1)
            sec_c = FULL_SEC
            co_ret, s_f, s_b = _retention(log_gamma[l], zc, sec_c, zero_state, zero_state,
                                          rc=128, with_out=True, with_state_out=True)
            co_diff = _diff_attention(lam, zc, sec_c, [(zc, sec_c)], tq=n_ctx,
                                      post_scale=1.0 - lam_init)
            ctx_next = _merge(ctx, gate_c, g_post[l], co_ret, co_diff, zc, sec_c,
                              ws, bs, wbo, wo, tm=n_ctx)

        z = _projection(x, g_pre[l], shift, scale, w_full, FULL_KINDS, tables)
        (o_ret,) = _retention(log_gamma[l], z, FULL_SEC, s_f, s_b,
                              rc=128, with_out=True, with_state_out=False)
        o_diff = _diff_attention(lam, z, FULL_SEC, [(zc, sec_c), (z, FULL_SEC)], tq=256,
                                 post_scale=1.0 - lam_init)
        x = _merge(x, gate, g_post[l], o_ret, o_diff, z, FULL_SEC, ws, bs, wbo, wo, tm=512)
        if not last:
            ctx = ctx_next
    return x
```

```python
import functools
import math

import jax
import jax.numpy as jnp
import numpy as np
from jax import lax
from jax.experimental import pallas as pl
from jax.experimental.pallas import tpu as pltpu

F32 = jnp.float32
BF16 = jnp.bfloat16

D_MODEL = 1024
DEPTH = 2
CTX_LEN = 256
GRID_W = 64
N_BRANCH = 3
BRANCH_W = D_MODEL // 2
HEADS = 4
HEAD_W = BRANCH_W // HEADS
DIFF_DQK = HEAD_W // 2
CHUNK = 128
ROPE_BASE = 10000.0
EPS = 1e-6
RET_SCALE = HEAD_W ** -0.5
DIFF_Q_SCALE = (DIFF_DQK ** -0.5) * math.log2(math.e)

LANES = 128
SEC = BRANCH_W
MERGE_COLS = N_BRANCH * D_MODEL
KV_COLS = 4 * BRANCH_W
VMEM_LIMIT = 56 * 1024 * 1024

DIFF_TQ = 512
DIFF_SPLIT = 4

FULL_KINDS = ("sigmoid",) * 6 + ("rk", "plain", "dk", "plain", "rq", "silu", "dq", "silu",
                                 "plain", "ln", "silu")
FULL_SEC = dict(rk=6, rv=7, dk=8, dv=9, rq=10, rg=11, dq=12, dg=13, mu=14, mv=15, mg=16)
KV_KINDS = ("rk", "plain", "dk", "plain")
KV_SEC = dict(rk=0, rv=1, dk=2, dv=3)


def _dot(a, b):
    return jnp.dot(a, b, preferred_element_type=F32)


def _dot_nt(a, b):
    return lax.dot_general(a, b, (((1,), (1,)), ((), ())), preferred_element_type=F32)


def _rms(v):
    return v * lax.rsqrt(jnp.mean(v * v, axis=-1, keepdims=True) + EPS)


def _mod_kernel(c_ref, w_ref, b_ref, o_ref):
    cv = c_ref[...]
    sc = cv * jax.nn.sigmoid(cv)
    o_ref[0] = jnp.dot(sc, w_ref[0], preferred_element_type=F32,
                       precision=lax.Precision.HIGHEST) + b_ref[0]


def _modulation(cc, w_mod, b_mod):
    rows = cc.shape[0]
    return pl.pallas_call(
        _mod_kernel,
        grid=(DEPTH, 3),
        in_specs=[
            pl.BlockSpec((rows, D_MODEL), lambda l, j: (0, 0)),
            pl.BlockSpec((1, D_MODEL, D_MODEL), lambda l, j: (l, 0, j)),
            pl.BlockSpec((1, 1, D_MODEL), lambda l, j: (l, 0, j)),
        ],
        out_specs=pl.BlockSpec((1, rows, D_MODEL), lambda l, j: (l, 0, j)),
        out_shape=jax.ShapeDtypeStruct((DEPTH, rows, 3 * D_MODEL), F32),
        compiler_params=pltpu.CompilerParams(vmem_limit_bytes=VMEM_LIMIT),
        name="modulation",
    )(cc, w_mod, b_mod.reshape(DEPTH, 1, 3 * D_MODEL))


def _rope_ret(a, c, s):
    outs = []
    for h in range(HEADS):
        ah = a[:, h * LANES:(h + 1) * LANES]
        outs.append(ah * c + pltpu.roll(ah, LANES // 2, 1) * s)
    return jnp.concatenate(outs, axis=1)


def _rope_diff(a, c, s_lo, s_hi):
    outs = []
    for h in range(HEADS):
        ah = a[:, h * LANES:(h + 1) * LANES]
        outs.append(ah * c + pltpu.roll(ah, LANES - DIFF_DQK // 2, 1) * s_lo
                    + pltpu.roll(ah, DIFF_DQK // 2, 1) * s_hi)
    return jnp.concatenate(outs, axis=1)


def _proj_kernel(*refs, kinds, rope, n_rows, sub):
    if rope:
        (x_ref, g_ref, shift_ref, scale_ref, w_ref,
         cr_ref, sr_ref, cd_ref, sdl_ref, sdh_ref, z_ref, h_ref) = refs
    else:
        x_ref, g_ref, shift_ref, scale_ref, w_ref, z_ref, h_ref = refs
    j = pl.program_id(1)
    n_sub = n_rows // sub

    @pl.when(j == 0)
    def _():
        gs = g_ref[...] * (1.0 + scale_ref[0])
        sh = shift_ref[0]
        for r in range(n_sub):
            xs = x_ref[0, r * sub:(r + 1) * sub, :]
            h_ref[r * sub:(r + 1) * sub, :] = (_rms(xs) * gs + sh).astype(BF16)

    def run(epilogue):
        for r in range(n_sub):
            rows = slice(r * sub, (r + 1) * sub)
            acc = _dot(h_ref[rows, :], w_ref[...])
            z_ref[0, rows, :] = epilogue(acc, rows).astype(BF16)

    def epi_rk(a, rows):
        if rope:
            a = _rope_ret(a, cr_ref[rows, :], sr_ref[rows, :])
        return a * RET_SCALE

    def epi_rq(a, rows):
        return _rope_ret(a, cr_ref[rows, :], sr_ref[rows, :]) if rope else a

    def epi_dk(a, rows):
        return _rope_diff(a, cd_ref[rows, :], sdl_ref[rows, :], sdh_ref[rows, :]) if rope else a

    def epi_dq(a, rows):
        return epi_dk(a, rows) * DIFF_Q_SCALE

    def epi_ln(a, rows):
        mu = jnp.mean(a, axis=-1, keepdims=True)
        d = a - mu
        return d * lax.rsqrt(jnp.mean(d * d, axis=-1, keepdims=True) + EPS)

    epilogues = dict(
        plain=lambda a, rows: a,
        sigmoid=lambda a, rows: jax.nn.sigmoid(a),
        silu=lambda a, rows: a * jax.nn.sigmoid(a),
        rk=epi_rk, rq=epi_rq, dk=epi_dk, dq=epi_dq, ln=epi_ln)

    for kind in sorted(set(kinds)):
        idx = [i for i, k in enumerate(kinds) if k == kind]
        cond = j == idx[0]
        for i in idx[1:]:
            cond = cond | (j == i)
        pl.when(cond)(functools.partial(run, epilogues[kind]))


def _projection(xs, g_pre, shift, scale, w, kinds, tables):
    n_g, n_rows, _ = xs.shape
    n_sec = len(kinds)
    rope = tables is not None
    in_specs = [
        pl.BlockSpec((1, n_rows, D_MODEL), lambda b, j: (b, 0, 0)),
        pl.BlockSpec((1, D_MODEL), lambda b, j: (0, 0)),
        pl.BlockSpec((1, 1, D_MODEL), lambda b, j: (b, 0, 0)),
        pl.BlockSpec((1, 1, D_MODEL), lambda b, j: (b, 0, 0)),
        pl.BlockSpec((D_MODEL, SEC), lambda b, j: (0, j)),
    ]
    args = [xs, g_pre.reshape(1, D_MODEL), shift, scale, w]
    if rope:
        in_specs += [pl.BlockSpec((n_rows, LANES), lambda b, j: (0, 0))] * len(tables)
        args += list(tables)
    return pl.pallas_call(
        functools.partial(_proj_kernel, kinds=kinds, rope=rope, n_rows=n_rows, sub=512),
        grid=(n_g, n_sec),
        in_specs=in_specs,
        out_specs=pl.BlockSpec((1, n_rows, SEC), lambda b, j: (b, 0, j)),
        out_shape=jax.ShapeDtypeStruct((n_g, n_rows, n_sec * SEC), BF16),
        scratch_shapes=[pltpu.VMEM((n_rows, D_MODEL), BF16)],
        compiler_params=pltpu.CompilerParams(
            dimension_semantics=("arbitrary", "arbitrary"), vmem_limit_bytes=VMEM_LIMIT),
        name="projection_rope" if rope else "projection",
    )(*args)


def _ret_kernel(*refs, n, rc, with_out, with_state_out):
    refs = list(refs)
    lg_ref = refs.pop(0)
    if with_out:
        q_ref = refs.pop(0)
        g_ref = refs.pop(0)
    k_ref = refs.pop(0)
    v_ref = refs.pop(0)
    sf0_ref = refs.pop(0)
    sb0_ref = refs.pop(0)
    if with_out:
        o_ref = refs.pop(0)
    if with_state_out:
        sf_out_ref = refs.pop(0)
        sb_out_ref = refs.pop(0)
    if with_out:
        sbs_ref = refs.pop(0)

    h = pl.program_id(1)
    lgf = lg_ref[0, h]
    lgb = lg_ref[1, h]
    nc = n // rc
    pos = lax.broadcasted_iota(jnp.int32, (rc, LANES), 0).astype(F32)
    kdf = jnp.exp(lgf * (rc - 1.0 - pos))
    kdb = jnp.exp(lgb * pos)
    cdf = jnp.exp(jnp.full((1, LANES), lgf * rc, F32))
    cdb = jnp.exp(jnp.full((1, LANES), lgb * rc, F32))

    def kv_state(nn, kdec):
        rows = slice(nn * rc, (nn + 1) * rc)
        kd = (k_ref[0, rows, :].astype(F32) * kdec).T.astype(BF16)
        return _dot(kd, v_ref[0, rows, :])

    sb = sb0_ref[0, 0]
    for nn in reversed(range(nc)):
        if with_out:
            sbs_ref[nn] = sb.astype(BF16)
        sb = cdb * sb + kv_state(nn, kdb)

    if with_out:
        ii = lax.broadcasted_iota(jnp.int32, (rc, rc), 0)
        jj = lax.broadcasted_iota(jnp.int32, (rc, rc), 1)
        dist = (ii - jj).astype(F32)
        dmat = (jnp.where(dist >= 0, jnp.exp(lgf * jnp.maximum(dist, 0.0)), 0.0)
                + jnp.where(dist <= 0, jnp.exp(lgb * jnp.maximum(-dist, 0.0)), 0.0))
        qdf = jnp.exp(lgf * (pos + 1.0))
        qdb = jnp.exp(lgb * (rc - pos))

    sf = sf0_ref[0, 0]
    for nn in range(nc):
        rows = slice(nn * rc, (nn + 1) * rc)
        if with_out:
            q = q_ref[0, rows, :]
            qf = q.astype(F32)
            s = _dot_nt(q, k_ref[0, rows, :]) * dmat
            inner = _dot(s.astype(BF16), v_ref[0, rows, :])
            qcat = jnp.concatenate([(qf * qdf).astype(BF16), (qf * qdb).astype(BF16)], axis=1)
            scat = jnp.concatenate([sf.astype(BF16), sbs_ref[nn]], axis=0)
            out = inner + _dot(qcat, scat)
            o_ref[0, rows, :] = (_rms(out) * g_ref[0, rows, :].astype(F32)).astype(BF16)
        sf = cdf * sf + kv_state(nn, kdf)

    if with_state_out:
        sf_out_ref[0, 0] = sf
        sb_out_ref[0, 0] = sb


def _retention(log_gamma, z, sec, sf0, sb0, *, rc, with_out, with_state_out):
    n_b, n, _ = z.shape

    def col(name):
        base = sec[name] * HEADS
        return pl.BlockSpec((1, n, LANES), lambda b, h: (b, 0, base + h))

    state_spec = pl.BlockSpec((1, 1, HEAD_W, HEAD_W), lambda b, h: (b, h, 0, 0))
    in_specs = [pl.BlockSpec(memory_space=pltpu.SMEM)]
    args = [log_gamma]
    if with_out:
        in_specs += [col("rq"), col("rg")]
        args += [z, z]
    in_specs += [col("rk"), col("rv"), state_spec, state_spec]
    args += [z, z, sf0, sb0]
    out_specs, out_shape, scratch = [], [], []
    if with_out:
        out_specs.append(pl.BlockSpec((1, n, LANES), lambda b, h: (b, 0, h)))
        out_shape.append(jax.ShapeDtypeStruct((n_b, n, BRANCH_W), BF16))
        scratch.append(pltpu.VMEM((n // rc, HEAD_W, HEAD_W), BF16))
    if with_state_out:
        out_specs += [state_spec, state_spec]
        out_shape += [jax.ShapeDtypeStruct((n_b, HEADS, HEAD_W, HEAD_W), F32)] * 2
    return pl.pallas_call(
        functools.partial(_ret_kernel, n=n, rc=rc, with_out=with_out,
                          with_state_out=with_state_out),
        grid=(n_b, HEADS),
        in_specs=in_specs,
        out_specs=out_specs,
        out_shape=out_shape,
        scratch_shapes=scratch,
        compiler_params=pltpu.CompilerParams(vmem_limit_bytes=VMEM_LIMIT),
        name="retention" + ("_out" if with_out else "") + ("_state" if with_state_out else ""),
    )(*args)


def _diff_kernel(*refs, tq, n_split, n_kv, post_scale):
    lam_ref, q_ref, g_ref = refs[:3]
    k_refs = refs[3:3 + 2 * n_kv:2]
    v_refs = refs[4:4 + 2 * n_kv:2]
    o_ref = refs[3 + 2 * n_kv]
    lam = lam_ref[0]
    ts = tq // n_split
    lane = lax.broadcasted_iota(jnp.int32, (ts, LANES), 1)
    for r in range(n_split):
        rows = slice(r * ts, (r + 1) * ts)
        q = q_ref[0, rows, :]
        zero = jnp.zeros_like(q)
        qq = jnp.concatenate([jnp.where(lane < DIFF_DQK, q, zero),
                              jnp.where(lane >= DIFF_DQK, q, zero)], axis=0)
        scores = [_dot_nt(qq, k_ref[0]) for k_ref in k_refs]
        m = jnp.max(scores[0], axis=-1, keepdims=True)
        for s in scores[1:]:
            m = jnp.maximum(m, jnp.max(s, axis=-1, keepdims=True))
        acc = None
        den = None
        for s, v_ref in zip(scores, v_refs):
            p = jnp.exp2(s - m)
            ps = jnp.sum(p, axis=-1, keepdims=True)
            pv = _dot(p.astype(BF16), v_ref[0])
            acc = pv if acc is None else acc + pv
            den = ps if den is None else den + ps
        o = acc / den
        d = o[:ts] - lam * o[ts:]
        o_ref[0, rows, :] = (_rms(d) * post_scale * g_ref[0, rows, :].astype(F32)).astype(BF16)


def _diff_attention(lam, zq, sec_q, kvs, *, tq, n_split, post_scale):
    n_b, nq, _ = zq.shape
    qb, gb = sec_q["dq"] * HEADS, sec_q["dg"] * HEADS
    in_specs = [
        pl.BlockSpec(memory_space=pltpu.SMEM),
        pl.BlockSpec((1, tq, LANES), lambda b, h, i: (b, i, qb + h)),
        pl.BlockSpec((1, tq, LANES), lambda b, h, i: (b, i, gb + h)),
    ]
    args = [lam, zq, zq]
    for z, sec in kvs:
        nk = z.shape[1]
        kb, vb = sec["dk"] * HEADS, sec["dv"] * HEADS
        in_specs.append(pl.BlockSpec((1, nk, LANES), lambda b, h, i, kb=kb: (b, 0, kb + h)))
        in_specs.append(pl.BlockSpec((1, nk, LANES), lambda b, h, i, vb=vb: (b, 0, vb + h)))
        args += [z, z]
    return pl.pallas_call(
        functools.partial(_diff_kernel, tq=tq, n_split=n_split, n_kv=len(kvs),
                          post_scale=post_scale),
        grid=(n_b, HEADS, nq // tq),
        in_specs=in_specs,
        out_specs=pl.BlockSpec((1, tq, LANES), lambda b, h, i: (b, i, h)),
        out_shape=jax.ShapeDtypeStruct((n_b, nq, BRANCH_W), BF16),
        compiler_params=pltpu.CompilerParams(vmem_limit_bytes=VMEM_LIMIT),
        name="diff_attention_%d" % len(kvs),
    )(*args)


def _merge_kernel(x_ref, gate_ref, gpost_ref, oret_ref, odiff_ref, u_ref, vn_ref, mg_ref,
                  gr_ref, gm_ref, gd_ref, ws_ref, bs_ref, wbo_ref, wo_ref, out_ref, *, tm):
    rows = []
    for c in range(tm // CHUNK):
        cols = []
        for g in range(HEADS):
            blk = vn_ref[0, c * CHUNK:(c + 1) * CHUNK, g * LANES:(g + 1) * LANES]
            cols.append(_dot(ws_ref[g], blk) + bs_ref[g])
        rows.append(jnp.concatenate(cols, axis=1))
    sp = jnp.concatenate(rows, axis=0)
    o_mlp = (u_ref[0].astype(F32) * sp * mg_ref[0].astype(F32)).astype(BF16)
    t = (gr_ref[0].astype(F32) * _dot(oret_ref[0], wbo_ref[0])
         + gm_ref[0].astype(F32) * _dot(o_mlp, wbo_ref[1])
         + gd_ref[0].astype(F32) * _dot(odiff_ref[0], wbo_ref[2]))
    y = _dot(t.astype(BF16), wo_ref[...])
    out_ref[0] = x_ref[0] + gate_ref[0] * (_rms(y) * gpost_ref[...])


def _merge(xs, gate, g_post, o_ret, o_diff, z, sec, ws, bs, wbo, wo, *, tm):
    n_b, n, _ = xs.shape
    shared_gate = gate.shape[0] == 1

    def zsec(name):
        s = sec[name]
        return pl.BlockSpec((1, tm, SEC), lambda b, i: (b, i, s))

    def zgate(k):
        return pl.BlockSpec((1, tm, D_MODEL), lambda b, i: (b, i, k))

    const2 = lambda b, i: (0, 0)
    const3 = lambda b, i: (0, 0, 0)
    in_specs = [
        pl.BlockSpec((1, tm, D_MODEL), lambda b, i: (b, i, 0)),
        pl.BlockSpec((1, 1, D_MODEL), (lambda b, i: (0, 0, 0)) if shared_gate
                     else (lambda b, i: (b, 0, 0))),
        pl.BlockSpec((1, D_MODEL), const2),
        pl.BlockSpec((1, tm, BRANCH_W), lambda b, i: (b, i, 0)),
        pl.BlockSpec((1, tm, BRANCH_W), lambda b, i: (b, i, 0)),
        zsec("mu"), zsec("mv"), zsec("mg"), zgate(0), zgate(1), zgate(2),
        pl.BlockSpec((HEADS, CHUNK, CHUNK), const3),
        pl.BlockSpec((HEADS, CHUNK, LANES), const3),
        pl.BlockSpec((N_BRANCH, BRANCH_W, D_MODEL), const3),
        pl.BlockSpec((D_MODEL, D_MODEL), const2),
    ]
    return pl.pallas_call(
        functools.partial(_merge_kernel, tm=tm),
        grid=(n_b, n // tm),
        in_specs=in_specs,
        out_specs=pl.BlockSpec((1, tm, D_MODEL), lambda b, i: (b, i, 0)),
        out_shape=jax.ShapeDtypeStruct(xs.shape, F32),
        compiler_params=pltpu.CompilerParams(vmem_limit_bytes=VMEM_LIMIT),
        name="merge",
    )(xs, gate, g_post.reshape(1, D_MODEL), o_ret, o_diff, z, z, z, z, z, z, ws, bs, wbo, wo)


def _rope_tables(n_lat):
    rows = n_lat // GRID_W
    row_pos = jnp.repeat(jnp.arange(rows, dtype=F32), GRID_W)
    col_pos = jnp.tile(jnp.arange(GRID_W, dtype=F32), rows)

    def angles(head_dim):
        n_freq = head_dim // 4
        inv = ROPE_BASE ** (-jnp.arange(n_freq, dtype=F32) / n_freq)
        ang = jnp.concatenate([row_pos[:, None] * inv, col_pos[:, None] * inv], axis=-1)
        return jnp.cos(ang), jnp.sin(ang)

    cos_r, sin_r = angles(HEAD_W)
    cos_d, sin_d = angles(DIFF_DQK)
    zeros = jnp.zeros_like(sin_d)
    c_r = jnp.concatenate([cos_r, cos_r], axis=-1)
    s_r = jnp.concatenate([-sin_r, sin_r], axis=-1)
    c_d = jnp.tile(jnp.concatenate([cos_d, cos_d], axis=-1), (1, 2))
    s_lo = jnp.tile(jnp.concatenate([-sin_d, zeros], axis=-1), (1, 2))
    s_hi = jnp.tile(jnp.concatenate([zeros, sin_d], axis=-1), (1, 2))
    return c_r, s_r, c_d, s_lo, s_hi


@jax.jit
def kernel(x, c, ctx, c_ctx, w_mod, b_mod, g_pre, g_post, w_in, ret_decay_logit, mlp_w_s,
           mlp_b_s, diff_lambda_q, diff_lambda_k, w_branch_out, w_out):
    n_b, n_lat, _ = x.shape
    n_ctx = ctx.shape[1]
    tables = _rope_tables(n_lat)

    cond_rows = 16
    cc = jnp.zeros((cond_rows, D_MODEL), F32).at[:n_b].set(c).at[n_b].set(c_ctx)
    mod = _modulation(cc, w_mod, b_mod)

    log_gamma = -jax.nn.softplus(-ret_decay_logit.astype(F32))
    zero_state = jnp.zeros((n_b, HEADS, HEAD_W, HEAD_W), F32)

    for l in range(DEPTH):
        last = l == DEPTH - 1
        lam_init = 0.8 - 0.6 * math.exp(-0.3 * l)
        lam = (jnp.exp(jnp.sum(diff_lambda_q[l, 0] * diff_lambda_k[l, 0]))
               - jnp.exp(jnp.sum(diff_lambda_q[l, 1] * diff_lambda_k[l, 1]))
               + lam_init).astype(F32).reshape(1)
        shift, scale, gate = [mod[l, :n_b, k * D_MODEL:(k + 1) * D_MODEL].reshape(n_b, 1, D_MODEL)
                              for k in range(3)]
        shift_c, scale_c, gate_c = [mod[l, n_b:n_b + 1, k * D_MODEL:(k + 1) * D_MODEL]
                                    .reshape(1, 1, D_MODEL) for k in range(3)]
        w_l = w_in[l]
        w_full = jnp.concatenate([w_l[:, -MERGE_COLS:], w_l[:, :-MERGE_COLS]], axis=1).astype(BF16)
        ws = mlp_w_s[l].astype(BF16)
        bs = jnp.broadcast_to(mlp_b_s[l][:, :, None], (HEADS, CHUNK, LANES)).astype(F32)
        wbo = w_branch_out[l].astype(BF16)
        wo = w_out[l].astype(BF16)

        ctx_flat = ctx.reshape(1, n_b * n_ctx, D_MODEL)
        if last:
            w_kv = w_l[:, :KV_COLS].astype(BF16)
            zc = _projection(ctx_flat, g_pre[l], shift_c, scale_c, w_kv, KV_KINDS, None)
            zc = zc.reshape(n_b, n_ctx, -1)
            sec_c = KV_SEC
            s_f, s_b = _retention(log_gamma[l], zc, sec_c, zero_state, zero_state,
                                  rc=128, with_out=False, with_state_out=True)
        else:
            zc = _projection(ctx_flat, g_pre[l], shift_c, scale_c, w_full, FULL_KINDS, None)
            zc = zc.reshape(n_b, n_ctx, -1)
            sec_c = FULL_SEC
            co_ret, s_f, s_b = _retention(log_gamma[l], zc, sec_c, zero_state, zero_state,
                                          rc=128, with_out=True, with_state_out=True)
            co_diff = _diff_attention(lam, zc, sec_c, [(zc, sec_c)], tq=n_ctx, n_split=1,
                                      post_scale=1.0 - lam_init)
            ctx_next = _merge(ctx, gate_c, g_post[l], co_ret, co_diff, zc, sec_c,
                              ws, bs, wbo, wo, tm=n_ctx)

        z = _projection(x, g_pre[l], shift, scale, w_full, FULL_KINDS, tables)
        (o_ret,) = _retention(log_gamma[l], z, FULL_SEC, s_f, s_b,
                              rc=128, with_out=True, with_state_out=False)
        o_diff = _diff_attention(lam, z, FULL_SEC, [(zc, sec_c), (z, FULL_SEC)], tq=DIFF_TQ,
                                 n_split=DIFF_SPLIT, post_scale=1.0 - lam_init)
        x = _merge(x, gate, g_post[l], o_ret, o_diff, z, FULL_SEC, ws, bs, wbo, wo, tm=512)
        if not last:
            ctx = ctx_next
    return x
```

```python
import functools
import math

import jax
import jax.numpy as jnp
from jax import lax
from jax.experimental import pallas as pl
from jax.experimental.pallas import tpu as pltpu

F32 = jnp.float32
BF16 = jnp.bfloat16

D_MODEL = 1024
DEPTH = 2
GRID_W = 64
N_BRANCH = 3
BRANCH_W = D_MODEL // 2
HEADS = 4
HEAD_W = BRANCH_W // HEADS
DIFF_DQK = HEAD_W // 2
CHUNK = 128
ROPE_BASE = 10000.0
EPS = 1e-6
RET_SCALE = HEAD_W ** -0.5
DIFF_Q_SCALE = (DIFF_DQK ** -0.5) * math.log2(math.e)

LANES = 128
SEC = BRANCH_W
MERGE_COLS = N_BRANCH * D_MODEL
KV_COLS = 4 * BRANCH_W
VMEM_LIMIT = 56 * 1024 * 1024

PROJ_SUB = 512
DIFF_TS = 256
RET_CHUNK = 128
MERGE_TM = 512

FULL_KINDS = ("sigmoid",) * 6 + ("rk", "plain", "dk", "plain", "rq", "silu", "dq", "silu",
                                 "plain", "ln", "silu")
FULL_SEC = dict(rk=6, rv=7, dk=8, dv=9, rq=10, rg=11, dq=12, dg=13, mu=14, mv=15, mg=16)
KV_KINDS = ("rk", "plain", "dk", "plain")
KV_SEC = dict(rk=0, rv=1, dk=2, dv=3)


def _dot(a, b):
    return jnp.dot(a, b, preferred_element_type=F32)


def _dot_nt(a, b):
    return lax.dot_general(a, b, (((1,), (1,)), ((), ())), preferred_element_type=F32)


def _rms(v):
    return v * lax.rsqrt(jnp.mean(v * v, axis=-1, keepdims=True) + EPS)


def _split_bf16(a):
    hi = a.astype(BF16)
    return hi, (a - hi.astype(F32)).astype(BF16)


def _mod_kernel(c_ref, w_ref, b_ref, o_ref):
    cv = c_ref[...]
    c_hi, c_lo = _split_bf16(cv * jax.nn.sigmoid(cv))
    w_hi, w_lo = _split_bf16(w_ref[0])
    o_ref[0] = _dot(c_hi, w_hi) + _dot(c_hi, w_lo) + _dot(c_lo, w_hi) + b_ref[0]


def _modulation(cc, w_mod, b_mod):
    rows = cc.shape[0]
    return pl.pallas_call(
        _mod_kernel,
        grid=(DEPTH, 3),
        in_specs=[
            pl.BlockSpec((rows, D_MODEL), lambda l, j: (0, 0)),
            pl.BlockSpec((1, D_MODEL, D_MODEL), lambda l, j: (l, 0, j)),
            pl.BlockSpec((1, 1, D_MODEL), lambda l, j: (l, 0, j)),
        ],
        out_specs=pl.BlockSpec((1, rows, D_MODEL), lambda l, j: (l, 0, j)),
        out_shape=jax.ShapeDtypeStruct((DEPTH, rows, 3 * D_MODEL), F32),
        compiler_params=pltpu.CompilerParams(vmem_limit_bytes=VMEM_LIMIT),
        name="modulation",
    )(cc, w_mod, b_mod.reshape(DEPTH, 1, 3 * D_MODEL))


def _rope_ret(a, c, s):
    outs = []
    for h in range(HEADS):
        ah = a[:, h * LANES:(h + 1) * LANES]
        outs.append(ah * c + pltpu.roll(ah, LANES // 2, 1) * s)
    return jnp.concatenate(outs, axis=1)


def _rope_diff(a, c, s_lo, s_hi):
    outs = []
    for h in range(HEADS):
        ah = a[:, h * LANES:(h + 1) * LANES]
        outs.append(ah * c + pltpu.roll(ah, LANES - DIFF_DQK // 2, 1) * s_lo
                    + pltpu.roll(ah, DIFF_DQK // 2, 1) * s_hi)
    return jnp.concatenate(outs, axis=1)


def _proj_kernel(*refs, kinds, rope, n_rows, sub):
    if rope:
        (x_ref, g_ref, shift_ref, scale_ref, w_ref,
         cr_ref, sr_ref, cd_ref, sdl_ref, sdh_ref, z_ref, h_ref) = refs
    else:
        x_ref, g_ref, shift_ref, scale_ref, w_ref, z_ref, h_ref = refs
    j = pl.program_id(1)
    n_sub = n_rows // sub

    @pl.when(j == 0)
    def _():
        gs = g_ref[...] * (1.0 + scale_ref[0])
        sh = shift_ref[0]
        for r in range(n_sub):
            xs = x_ref[0, r * sub:(r + 1) * sub, :]
            h_ref[r * sub:(r + 1) * sub, :] = (_rms(xs) * gs + sh).astype(BF16)

    def run(epilogue):
        for r in range(n_sub):
            rows = slice(r * sub, (r + 1) * sub)
            acc = _dot(h_ref[rows, :], w_ref[...])
            z_ref[0, rows, :] = epilogue(acc, rows).astype(BF16)

    def epi_rk(a, rows):
        if rope:
            a = _rope_ret(a, cr_ref[rows, :], sr_ref[rows, :])
        return a * RET_SCALE

    def epi_rq(a, rows):
        return _rope_ret(a, cr_ref[rows, :], sr_ref[rows, :]) if rope else a

    def epi_dk(a, rows):
        return _rope_diff(a, cd_ref[rows, :], sdl_ref[rows, :], sdh_ref[rows, :]) if rope else a

    def epi_dq(a, rows):
        return epi_dk(a, rows) * DIFF_Q_SCALE

    def epi_ln(a, rows):
        mu = jnp.mean(a, axis=-1, keepdims=True)
        d = a - mu
        return d * lax.rsqrt(jnp.mean(d * d, axis=-1, keepdims=True) + EPS)

    epilogues = dict(
        plain=lambda a, rows: a,
        sigmoid=lambda a, rows: jax.nn.sigmoid(a),
        silu=lambda a, rows: a * jax.nn.sigmoid(a),
        rk=epi_rk, rq=epi_rq, dk=epi_dk, dq=epi_dq, ln=epi_ln)

    for kind in sorted(set(kinds)):
        idx = [i for i, k in enumerate(kinds) if k == kind]
        cond = j == idx[0]
        for i in idx[1:]:
            cond = cond | (j == i)
        pl.when(cond)(functools.partial(run, epilogues[kind]))


def _projection(xs, g_pre, shift, scale, w, kinds, tables):
    n_g, n_rows, _ = xs.shape
    n_sec = len(kinds)
    rope = tables is not None
    in_specs = [
        pl.BlockSpec((1, n_rows, D_MODEL), lambda b, j: (b, 0, 0)),
        pl.BlockSpec((1, D_MODEL), lambda b, j: (0, 0)),
        pl.BlockSpec((1, 1, D_MODEL), lambda b, j: (b, 0, 0)),
        pl.BlockSpec((1, 1, D_MODEL), lambda b, j: (b, 0, 0)),
        pl.BlockSpec((D_MODEL, SEC), lambda b, j: (0, j)),
    ]
    args = [xs, g_pre.reshape(1, D_MODEL), shift, scale, w]
    if rope:
        in_specs += [pl.BlockSpec((n_rows, LANES), lambda b, j: (0, 0))] * len(tables)
        args += list(tables)
    return pl.pallas_call(
        functools.partial(_proj_kernel, kinds=kinds, rope=rope, n_rows=n_rows, sub=PROJ_SUB),
        grid=(n_g, n_sec),
        in_specs=in_specs,
        out_specs=pl.BlockSpec((1, n_rows, SEC), lambda b, j: (b, 0, j)),
        out_shape=jax.ShapeDtypeStruct((n_g, n_rows, n_sec * SEC), BF16),
        scratch_shapes=[pltpu.VMEM((n_rows, D_MODEL), BF16)],
        compiler_params=pltpu.CompilerParams(
            dimension_semantics=("arbitrary", "arbitrary"), vmem_limit_bytes=VMEM_LIMIT),
        name="projection_rope" if rope else "projection",
    )(*args)


def _ret_kernel(*refs, n, n_ctx, rc):
    if n_ctx:
        lg_ref, q_ref, g_ref, k_ref, v_ref, kc_ref, vc_ref, o_ref, inc_ref, st_ref = refs
    else:
        lg_ref, q_ref, g_ref, k_ref, v_ref, o_ref, inc_ref, st_ref = refs
    h = pl.program_id(1)
    lgf = lg_ref[0, h]
    lgb = lg_ref[1, h]
    nc = n // rc
    pos = lax.broadcasted_iota(jnp.int32, (rc, LANES), 0).astype(F32)
    kdf = jnp.exp(lgf * (rc - 1.0 - pos)).astype(BF16)
    kdb = jnp.exp(lgb * pos).astype(BF16)
    cdf = jnp.exp(jnp.full((1, LANES), lgf * rc, F32))
    cdb = jnp.exp(jnp.full((1, LANES), lgb * rc, F32))

    def increment(kr, vr, nn):
        rows = slice(nn * rc, (nn + 1) * rc)
        kk = kr[0, rows, :]
        kcat = jnp.concatenate([kk * kdf, kk * kdb], axis=1)
        vt = vr[0, rows, :].astype(F32).T.astype(BF16)
        return _dot(vt, kcat)

    sf = jnp.zeros((HEAD_W, HEAD_W), F32)
    sb = jnp.zeros((HEAD_W, HEAD_W), F32)
    if n_ctx:
        incs = [increment(kc_ref, vc_ref, nn) for nn in range(n_ctx // rc)]
        for inc in incs:
            sf = cdf * sf + inc[:, :HEAD_W]
        for inc in reversed(incs):
            sb = cdb * sb + inc[:, HEAD_W:]

    for nn in range(nc):
        inc_ref[nn] = increment(k_ref, v_ref, nn)
    for nn in range(nc):
        st_ref[nn, :, 0:HEAD_W] = sf.astype(BF16)
        sf = cdf * sf + inc_ref[nn, :, 0:HEAD_W]
    for nn in reversed(range(nc)):
        st_ref[nn, :, HEAD_W:] = sb.astype(BF16)
        sb = cdb * sb + inc_ref[nn, :, HEAD_W:]

    ii = lax.broadcasted_iota(jnp.int32, (rc, rc), 0)
    jj = lax.broadcasted_iota(jnp.int32, (rc, rc), 1)
    dist = (ii - jj).astype(F32)
    dmat = (jnp.where(dist >= 0, jnp.exp(lgf * jnp.maximum(dist, 0.0)), 0.0)
            + jnp.where(dist <= 0, jnp.exp(lgb * jnp.maximum(-dist, 0.0)), 0.0))
    qdf = jnp.exp(lgf * (pos + 1.0)).astype(BF16)
    qdb = jnp.exp(lgb * (rc - pos)).astype(BF16)
    for nn in range(nc):
        rows = slice(nn * rc, (nn + 1) * rc)
        q = q_ref[0, rows, :]
        s = _dot_nt(q, k_ref[0, rows, :]) * dmat
        inner = _dot(s.astype(BF16), v_ref[0, rows, :])
        qcat = jnp.concatenate([q * qdf, q * qdb], axis=1)
        out = inner + _dot_nt(qcat, st_ref[nn])
        o_ref[0, rows, :] = _rms(out).astype(BF16) * g_ref[0, rows, :]


def _retention(log_gamma, z, sec, zc, sec_c, *, rc):
    n_b, n, _ = z.shape
    n_ctx = 0 if zc is None else zc.shape[1]

    def col(rows, s, name):
        base = s[name] * HEADS
        return pl.BlockSpec((1, rows, LANES), lambda b, h: (b, 0, base + h))

    in_specs = [pl.BlockSpec(memory_space=pltpu.SMEM),
                col(n, sec, "rq"), col(n, sec, "rg"), col(n, sec, "rk"), col(n, sec, "rv")]
    args = [log_gamma, z, z, z, z]
    if n_ctx:
        in_specs += [col(n_ctx, sec_c, "rk"), col(n_ctx, sec_c, "rv")]
        args += [zc, zc]
    return pl.pallas_call(
        functools.partial(_ret_kernel, n=n, n_ctx=n_ctx, rc=rc),
        grid=(n_b, HEADS),
        in_specs=in_specs,
        out_specs=pl.BlockSpec((1, n, LANES), lambda b, h: (b, 0, h)),
        out_shape=jax.ShapeDtypeStruct((n_b, n, BRANCH_W), BF16),
        scratch_shapes=[pltpu.VMEM((n // rc, HEAD_W, 2 * HEAD_W), F32),
                        pltpu.VMEM((n // rc, HEAD_W, 2 * HEAD_W), BF16)],
        compiler_params=pltpu.CompilerParams(vmem_limit_bytes=VMEM_LIMIT),
        name="retention_ctx_init" if n_ctx else "retention",
    )(*args)


def _split_maps(q, lane):
    zero = jnp.zeros_like(q)
    return jnp.concatenate([jnp.where(lane < DIFF_DQK, q, zero),
                            jnp.where(lane >= DIFF_DQK, q, zero)], axis=0)


def _diff_ctx_kernel(lam_ref, q_ref, g_ref, k_ref, v_ref, o_ref, *, tq, post_scale):
    lane = lax.broadcasted_iota(jnp.int32, (tq, LANES), 1)
    s = _dot_nt(_split_maps(q_ref[0], lane), k_ref[0])
    p = jnp.exp2(s - jnp.max(s, axis=-1, keepdims=True))
    o = _dot(p.astype(BF16), v_ref[0]) / jnp.sum(p, axis=-1, keepdims=True)
    d = o[:tq] - lam_ref[0] * o[tq:]
    o_ref[0] = (_rms(d) * post_scale * g_ref[0].astype(F32)).astype(BF16)


def _diff_attention_ctx(lam, zc, sec, *, post_scale):
    n_b, n, _ = zc.shape

    def col(name):
        base = sec[name] * HEADS
        return pl.BlockSpec((1, n, LANES), lambda b, h: (b, 0, base + h))

    return pl.pallas_call(
        functools.partial(_diff_ctx_kernel, tq=n, post_scale=post_scale),
        grid=(n_b, HEADS),
        in_specs=[pl.BlockSpec(memory_space=pltpu.SMEM),
                  col("dq"), col("dg"), col("dk"), col("dv")],
        out_specs=pl.BlockSpec((1, n, LANES), lambda b, h: (b, 0, h)),
        out_shape=jax.ShapeDtypeStruct((n_b, n, BRANCH_W), BF16),
        compiler_params=pltpu.CompilerParams(vmem_limit_bytes=VMEM_LIMIT),
        name="diff_attention_ctx",
    )(lam, zc, zc, zc, zc)


def _diff_pipe_kernel(lam_ref, q_ref, g_ref, kc_ref, vc_ref, kl_ref, vl_ref, o_ref,
                      kall, vall, s0, s1, p0, p1, *, n_sub, ts, post_scale):
    n_ctx = kc_ref.shape[1]
    nk = kall.shape[0]
    kall[0:n_ctx, :] = kc_ref[0]
    kall[n_ctx:, :] = kl_ref[0]
    vall[0:n_ctx, 0:LANES] = vc_ref[0]
    vall[n_ctx:, 0:LANES] = vl_ref[0]
    vall[:, LANES:] = jnp.ones((nk, LANES), BF16)
    lam = lam_ref[0]
    lane = lax.broadcasted_iota(jnp.int32, (ts, LANES), 1)
    s_bufs, p_bufs = (s0, s1), (p0, p1)

    def stage_a(t):
        rows = slice(t * ts, (t + 1) * ts)
        s_bufs[t % 2][...] = _dot_nt(_split_maps(q_ref[0, rows, :], lane), kall[...])

    def stage_b(t):
        s = s_bufs[t % 2][...]
        p_bufs[t % 2][...] = jnp.exp2(s - jnp.max(s, axis=-1, keepdims=True)).astype(BF16)

    def stage_c(t):
        rows = slice(t * ts, (t + 1) * ts)
        oe = _dot(p_bufs[t % 2][...], vall[...])
        o = oe[:, :LANES] / oe[:, LANES:]
        d = o[:ts] - lam * o[ts:]
        o_ref[0, rows, :] = (_rms(d) * post_scale * g_ref[0, rows, :].astype(F32)).astype(BF16)

    for t in range(n_sub + 2):
        if t >= 2:
            stage_c(t - 2)
        if 1 <= t <= n_sub:
            stage_b(t - 1)
        if t < n_sub:
            stage_a(t)


def _diff_attention_pipelined(lam, z, sec, zc, sec_c, *, ts, post_scale):
    n_b, n, _ = z.shape
    n_ctx = zc.shape[1]
    nk = n_ctx + n

    def col(arr_n, s, name):
        base = s[name] * HEADS
        return pl.BlockSpec((1, arr_n, LANES), lambda b, h: (b, 0, base + h))

    return pl.pallas_call(
        functools.partial(_diff_pipe_kernel, n_sub=n // ts, ts=ts, post_scale=post_scale),
        grid=(n_b, HEADS),
        in_specs=[pl.BlockSpec(memory_space=pltpu.SMEM),
                  col(n, sec, "dq"), col(n, sec, "dg"),
                  col(n_ctx, sec_c, "dk"), col(n_ctx, sec_c, "dv"),
                  col(n, sec, "dk"), col(n, sec, "dv")],
        out_specs=pl.BlockSpec((1, n, LANES), lambda b, h: (b, 0, h)),
        out_shape=jax.ShapeDtypeStruct((n_b, n, BRANCH_W), BF16),
        scratch_shapes=[pltpu.VMEM((nk, LANES), BF16), pltpu.VMEM((nk, 2 * LANES), BF16),
                        pltpu.VMEM((2 * ts, nk), F32), pltpu.VMEM((2 * ts, nk), F32),
                        pltpu.VMEM((2 * ts, nk), BF16), pltpu.VMEM((2 * ts, nk), BF16)],
        compiler_params=pltpu.CompilerParams(vmem_limit_bytes=VMEM_LIMIT),
        name="diff_attention_pipelined",
    )(lam, z, z, zc, zc, z, z)


def _merge_kernel(x_ref, gate_ref, gpost_ref, oret_ref, odiff_ref, u_ref, vn_ref, mg_ref,
                  gr_ref, gm_ref, gd_ref, ws_ref, bs_ref, wbo_ref, wo_ref, out_ref, *, tm):
    rows = []
    for c in range(tm // CHUNK):
        cols = []
        for g in range(HEADS):
            blk = vn_ref[0, c * CHUNK:(c + 1) * CHUNK, g * LANES:(g + 1) * LANES]
            cols.append(_dot(ws_ref[g], blk) + bs_ref[g])
        rows.append(jnp.concatenate(cols, axis=1))
    sp = jnp.concatenate(rows, axis=0)
    o_mlp = (u_ref[0].astype(F32) * sp * mg_ref[0].astype(F32)).astype(BF16)
    t = (gr_ref[0].astype(F32) * _dot(oret_ref[0], wbo_ref[0])
         + gm_ref[0].astype(F32) * _dot(o_mlp, wbo_ref[1])
         + gd_ref[0].astype(F32) * _dot(odiff_ref[0], wbo_ref[2]))
    y = _dot(t.astype(BF16), wo_ref[...])
    out_ref[0] = x_ref[0] + gate_ref[0] * (_rms(y) * gpost_ref[...])


def _merge(xs, gate, g_post, o_ret, o_diff, z, sec, ws, bs, wbo, wo, *, tm):
    n_b, n, _ = xs.shape
    shared_gate = gate.shape[0] == 1

    def zsec(name):
        s = sec[name]
        return pl.BlockSpec((1, tm, SEC), lambda b, i: (b, i, s))

    def zgate(k):
        return pl.BlockSpec((1, tm, D_MODEL), lambda b, i: (b, i, k))

    const2 = lambda b, i: (0, 0)
    const3 = lambda b, i: (0, 0, 0)
    in_specs = [
        pl.BlockSpec((1, tm, D_MODEL), lambda b, i: (b, i, 0)),
        pl.BlockSpec((1, 1, D_MODEL), (lambda b, i: (0, 0, 0)) if shared_gate
                     else (lambda b, i: (b, 0, 0))),
        pl.BlockSpec((1, D_MODEL), const2),
        pl.BlockSpec((1, tm, BRANCH_W), lambda b, i: (b, i, 0)),
        pl.BlockSpec((1, tm, BRANCH_W), lambda b, i: (b, i, 0)),
        zsec("mu"), zsec("mv"), zsec("mg"), zgate(0), zgate(1), zgate(2),
        pl.BlockSpec((HEADS, CHUNK, CHUNK), const3),
        pl.BlockSpec((HEADS, CHUNK, LANES), const3),
        pl.BlockSpec((N_BRANCH, BRANCH_W, D_MODEL), const3),
        pl.BlockSpec((D_MODEL, D_MODEL), const2),
    ]
    return pl.pallas_call(
        functools.partial(_merge_kernel, tm=tm),
        grid=(n_b, n // tm),
        in_specs=in_specs,
        out_specs=pl.BlockSpec((1, tm, D_MODEL), lambda b, i: (b, i, 0)),
        out_shape=jax.ShapeDtypeStruct(xs.shape, F32),
        compiler_params=pltpu.CompilerParams(vmem_limit_bytes=VMEM_LIMIT),
        name="merge",
    )(xs, gate, g_post.reshape(1, D_MODEL), o_ret, o_diff, z, z, z, z, z, z, ws, bs, wbo, wo)


def _rope_tables(n_lat):
    rows = n_lat // GRID_W
    row_pos = jnp.repeat(jnp.arange(rows, dtype=F32), GRID_W)
    col_pos = jnp.tile(jnp.arange(GRID_W, dtype=F32), rows)

    def angles(head_dim):
        n_freq = head_dim // 4
        inv = ROPE_BASE ** (-jnp.arange(n_freq, dtype=F32) / n_freq)
        ang = jnp.concatenate([row_pos[:, None] * inv, col_pos[:, None] * inv], axis=-1)
        return jnp.cos(ang), jnp.sin(ang)

    cos_r, sin_r = angles(HEAD_W)
    cos_d, sin_d = angles(DIFF_DQK)
    zeros = jnp.zeros_like(sin_d)
    c_r = jnp.concatenate([cos_r, cos_r], axis=-1)
    s_r = jnp.concatenate([-sin_r, sin_r], axis=-1)
    c_d = jnp.tile(jnp.concatenate([cos_d, cos_d], axis=-1), (1, 2))
    s_lo = jnp.tile(jnp.concatenate([-sin_d, zeros], axis=-1), (1, 2))
    s_hi = jnp.tile(jnp.concatenate([zeros, sin_d], axis=-1), (1, 2))
    return c_r, s_r, c_d, s_lo, s_hi


@jax.jit
def kernel(x, c, ctx, c_ctx, w_mod, b_mod, g_pre, g_post, w_in, ret_decay_logit, mlp_w_s,
           mlp_b_s, diff_lambda_q, diff_lambda_k, w_branch_out, w_out):
    n_b, n_lat, _ = x.shape
    n_ctx = ctx.shape[1]
    tables = _rope_tables(n_lat)

    cond_rows = 16
    cc = jnp.zeros((cond_rows, D_MODEL), F32).at[:n_b].set(c).at[n_b].set(c_ctx)
    mod = _modulation(cc, w_mod, b_mod)

    log_gamma = -jax.nn.softplus(-ret_decay_logit.astype(F32))

    for l in range(DEPTH):
        last = l == DEPTH - 1
        lam_init = 0.8 - 0.6 * math.exp(-0.3 * l)
        lam = (jnp.exp(jnp.sum(diff_lambda_q[l, 0] * diff_lambda_k[l, 0]))
               - jnp.exp(jnp.sum(diff_lambda_q[l, 1] * diff_lambda_k[l, 1]))
               + lam_init).astype(F32).reshape(1)
        shift, scale, gate = [mod[l, :n_b, k * D_MODEL:(k + 1) * D_MODEL].reshape(n_b, 1, D_MODEL)
                              for k in range(3)]
        shift_c, scale_c, gate_c = [mod[l, n_b:n_b + 1, k * D_MODEL:(k + 1) * D_MODEL]
                                    .reshape(1, 1, D_MODEL) for k in range(3)]
        w_l = w_in[l]
        w_full = jnp.concatenate([w_l[:, -MERGE_COLS:], w_l[:, :-MERGE_COLS]], axis=1).astype(BF16)
        ws = mlp_w_s[l].astype(BF16)
        bs = jnp.broadcast_to(mlp_b_s[l][:, :, None], (HEADS, CHUNK, LANES)).astype(F32)
        wbo = w_branch_out[l].astype(BF16)
        wo = w_out[l].astype(BF16)

        ctx_flat = ctx.reshape(1, n_b * n_ctx, D_MODEL)
        if last:
            w_kv = w_l[:, :KV_COLS].astype(BF16)
            zc = _projection(ctx_flat, g_pre[l], shift_c, scale_c, w_kv, KV_KINDS, None)
            zc = zc.reshape(n_b, n_ctx, -1)
            sec_c = KV_SEC
        else:
            zc = _projection(ctx_flat, g_pre[l], shift_c, scale_c, w_full, FULL_KINDS, None)
            zc = zc.reshape(n_b, n_ctx, -1)
            sec_c = FULL_SEC
            co_ret = _retention(log_gamma[l], zc, sec_c, None, None, rc=RET_CHUNK)
            co_diff = _diff_attention_ctx(lam, zc, sec_c, post_scale=1.0 - lam_init)
            ctx_next = _merge(ctx, gate_c, g_post[l], co_ret, co_diff, zc, sec_c,
                              ws, bs, wbo, wo, tm=n_ctx)

        z = _projection(x, g_pre[l], shift, scale, w_full, FULL_KINDS, tables)
        o_ret = _retention(log_gamma[l], z, FULL_SEC, zc, sec_c, rc=RET_CHUNK)
        o_diff = _diff_attention_pipelined(lam, z, FULL_SEC, zc, sec_c, ts=DIFF_TS,
                                           post_scale=1.0 - lam_init)
        x = _merge(x, gate, g_post[l], o_ret, o_diff, z, FULL_SEC, ws, bs, wbo, wo,
                   tm=MERGE_TM)
        if not last:
            ctx = ctx_next
    return x
```

```python
import functools
import math

import jax
import jax.numpy as jnp
from jax import lax
from jax.experimental import pallas as pl
from jax.experimental.pallas import tpu as pltpu

F32 = jnp.float32
BF16 = jnp.bfloat16

D_MODEL = 1024
DEPTH = 2
GRID_W = 64
N_BRANCH = 3
BRANCH_W = D_MODEL // 2
HEADS = 4
HEAD_W = BRANCH_W // HEADS
DIFF_DQK = HEAD_W // 2
CHUNK = 128
ROPE_BASE = 10000.0
EPS = 1e-6
RET_SCALE = HEAD_W ** -0.5
DIFF_Q_SCALE = (DIFF_DQK ** -0.5) * math.log2(math.e)

LANES = 128
SEC = BRANCH_W
MERGE_COLS = N_BRANCH * D_MODEL
KV_COLS = 4 * BRANCH_W
VMEM_LIMIT = 56 * 1024 * 1024

PROJ_SUB = 128
DIFF_TS = 256
RET_CHUNK = 128
MERGE_TM = 512
MERGE_SUB = 512

FULL_KINDS = ("sigmoid",) * 6 + ("rk", "plain", "dk", "plain", "rq", "silu", "dq", "silu",
                                 "plain", "ln", "silu")
FULL_SEC = dict(rk=6, rv=7, dk=8, dv=9, rq=10, rg=11, dq=12, dg=13, mu=14, mv=15, mg=16)
KV_KINDS = ("rk", "plain", "dk", "plain")
KV_SEC = dict(rk=0, rv=1, dk=2, dv=3)


def _dot(a, b):
    return jnp.dot(a, b, preferred_element_type=F32)


def _dot_nt(a, b):
    return lax.dot_general(a, b, (((1,), (1,)), ((), ())), preferred_element_type=F32)


def _rms(v):
    return v * lax.rsqrt(jnp.mean(v * v, axis=-1, keepdims=True) + EPS)


def _split_bf16(a):
    hi = a.astype(BF16)
    return hi, (a - hi.astype(F32)).astype(BF16)


def _mod_kernel(c_ref, w_ref, b_ref, o_ref):
    cv = c_ref[...]
    c_hi, c_lo = _split_bf16(cv * jax.nn.sigmoid(cv))
    w_hi, w_lo = _split_bf16(w_ref[0])
    o_ref[0] = _dot(c_hi, w_hi) + _dot(c_hi, w_lo) + _dot(c_lo, w_hi) + b_ref[0]


def _modulation(cc, w_mod, b_mod):
    rows = cc.shape[0]
    return pl.pallas_call(
        _mod_kernel,
        grid=(DEPTH, 3),
        in_specs=[
            pl.BlockSpec((rows, D_MODEL), lambda l, j: (0, 0)),
            pl.BlockSpec((1, D_MODEL, D_MODEL), lambda l, j: (l, 0, j)),
            pl.BlockSpec((1, 1, D_MODEL), lambda l, j: (l, 0, j)),
        ],
        out_specs=pl.BlockSpec((1, rows, D_MODEL), lambda l, j: (l, 0, j)),
        out_shape=jax.ShapeDtypeStruct((DEPTH, rows, 3 * D_MODEL), F32),
        compiler_params=pltpu.CompilerParams(vmem_limit_bytes=VMEM_LIMIT),
        name="modulation",
    )(cc, w_mod, b_mod.reshape(DEPTH, 1, 3 * D_MODEL))


def _rope_ret(a, c, s):
    outs = []
    for h in range(HEADS):
        ah = a[:, h * LANES:(h + 1) * LANES]
        outs.append(ah * c + pltpu.roll(ah, LANES // 2, 1) * s)
    return jnp.concatenate(outs, axis=1)


def _rope_diff(a, c, s_lo, s_hi):
    outs = []
    for h in range(HEADS):
        ah = a[:, h * LANES:(h + 1) * LANES]
        outs.append(ah * c + pltpu.roll(ah, LANES - DIFF_DQK // 2, 1) * s_lo
                    + pltpu.roll(ah, DIFF_DQK // 2, 1) * s_hi)
    return jnp.concatenate(outs, axis=1)


def _proj_kernel(*refs, kinds, rope, n_rows, sub):
    if rope:
        (x_ref, g_ref, shift_ref, scale_ref, w_ref,
         cr_ref, sr_ref, cd_ref, sdl_ref, sdh_ref, z_ref, h_ref) = refs
    else:
        x_ref, g_ref, shift_ref, scale_ref, w_ref, z_ref, h_ref = refs
    j = pl.program_id(1)
    n_sub = n_rows // sub

    @pl.when(j == 0)
    def _():
        gs = g_ref[...] * (1.0 + scale_ref[...])
        sh = shift_ref[...]
        for r in range(n_sub):
            xs = x_ref[0, r * sub:(r + 1) * sub, :]
            h_ref[r * sub:(r + 1) * sub, :] = (_rms(xs) * gs + sh).astype(BF16)

    def run(epilogue):
        for r in range(n_sub):
            rows = slice(r * sub, (r + 1) * sub)
            acc = _dot(h_ref[rows, :], w_ref[...])
            z_ref[0, rows, :] = epilogue(acc, rows).astype(BF16)

    def epi_rk(a, rows):
        if rope:
            a = _rope_ret(a, cr_ref[rows, :], sr_ref[rows, :])
        return a * RET_SCALE

    def epi_rq(a, rows):
        return _rope_ret(a, cr_ref[rows, :], sr_ref[rows, :]) if rope else a

    def epi_dk(a, rows):
        return _rope_diff(a, cd_ref[rows, :], sdl_ref[rows, :], sdh_ref[rows, :]) if rope else a

    def epi_dq(a, rows):
        return epi_dk(a, rows) * DIFF_Q_SCALE

    def epi_ln(a, rows):
        mu = jnp.mean(a, axis=-1, keepdims=True)
        d = a - mu
        return d * lax.rsqrt(jnp.mean(d * d, axis=-1, keepdims=True) + EPS)

    epilogues = dict(
        plain=lambda a, rows: a,
        sigmoid=lambda a, rows: jax.nn.sigmoid(a),
        silu=lambda a, rows: a * jax.nn.sigmoid(a),
        rk=epi_rk, rq=epi_rq, dk=epi_dk, dq=epi_dq, ln=epi_ln)

    for kind in sorted(set(kinds)):
        idx = [i for i, k in enumerate(kinds) if k == kind]
        cond = j == idx[0]
        for i in idx[1:]:
            cond = cond | (j == i)
        pl.when(cond)(functools.partial(run, epilogues[kind]))


def _mod_spec(l, row_of, k):
    return pl.BlockSpec((None, None, 1, D_MODEL), lambda b, i: (l, row_of(b), 0, k))


def _layer_spec(l, shape):
    zeros = (0,) * len(shape)
    return pl.BlockSpec((None,) + tuple(shape), lambda b, i: (l,) + zeros)


def _projection(xs, l, g_pre, mod4, row_of, w_all, src_block, kinds, tables):
    n_g, n_rows, _ = xs.shape
    n_sec = len(kinds)
    rope = tables is not None
    in_specs = [
        pl.BlockSpec((1, n_rows, D_MODEL), lambda b, j: (b, 0, 0)),
        _layer_spec(l, (1, D_MODEL)),
        _mod_spec(l, row_of, 0),
        _mod_spec(l, row_of, 1),
        pl.BlockSpec((None, D_MODEL, SEC), lambda b, j: (l, 0, src_block(j))),
    ]
    args = [xs, g_pre, mod4, mod4, w_all]
    if rope:
        in_specs += [pl.BlockSpec((n_rows, LANES), lambda b, j: (0, 0))] * len(tables)
        args += list(tables)
    return pl.pallas_call(
        functools.partial(_proj_kernel, kinds=kinds, rope=rope, n_rows=n_rows, sub=PROJ_SUB),
        grid=(n_g, n_sec),
        in_specs=in_specs,
        out_specs=pl.BlockSpec((1, n_rows, SEC), lambda b, j: (b, 0, j)),
        out_shape=jax.ShapeDtypeStruct((n_g, n_rows, n_sec * SEC), BF16),
        scratch_shapes=[pltpu.VMEM((n_rows, D_MODEL), BF16)],
        compiler_params=pltpu.CompilerParams(
            dimension_semantics=("arbitrary", "arbitrary"), vmem_limit_bytes=VMEM_LIMIT),
        name="projection_rope" if rope else "projection",
    )(*args)


def _ret_kernel(*refs, layer, n, n_ctx, rc, heads):
    if n_ctx:
        lg_ref, q_ref, g_ref, k_ref, v_ref, kc_ref, vc_ref, o_ref, inc_ref, st_ref = refs
    else:
        lg_ref, q_ref, g_ref, k_ref, v_ref, o_ref, inc_ref, st_ref = refs
        kc_ref = vc_ref = None
    for hh in range(heads):
        head = pl.program_id(1) * heads + hh
        _ret_one_head(lg_ref[layer, 0, head], lg_ref[layer, 1, head], slice(hh * LANES, (hh + 1) * LANES),
                      q_ref, g_ref, k_ref, v_ref, kc_ref, vc_ref, o_ref, inc_ref, st_ref,
                      n=n, n_ctx=n_ctx, rc=rc)


def _ret_one_head(lgf, lgb, cs, q_ref, g_ref, k_ref, v_ref, kc_ref, vc_ref, o_ref, inc_ref,
                  st_ref, *, n, n_ctx, rc):
    nc = n // rc
    pos = lax.broadcasted_iota(jnp.int32, (rc, LANES), 0).astype(F32)
    kdf = jnp.exp(lgf * (rc - 1.0 - pos)).astype(BF16)
    kdb = jnp.exp(lgb * pos).astype(BF16)
    cdf = jnp.exp(jnp.full((1, LANES), lgf * rc, F32))
    cdb = jnp.exp(jnp.full((1, LANES), lgb * rc, F32))

    def increment(kr, vr, nn):
        rows = slice(nn * rc, (nn + 1) * rc)
        kk = kr[0, rows, cs]
        kcat = jnp.concatenate([kk * kdf, kk * kdb], axis=1)
        vt = vr[0, rows, cs].astype(F32).T.astype(BF16)
        return _dot(vt, kcat)

    sf = jnp.zeros((HEAD_W, HEAD_W), F32)
    sb = jnp.zeros((HEAD_W, HEAD_W), F32)
    if n_ctx:
        incs = [increment(kc_ref, vc_ref, nn) for nn in range(n_ctx // rc)]
        for inc in incs:
            sf = cdf * sf + inc[:, :HEAD_W]
        for inc in reversed(incs):
            sb = cdb * sb + inc[:, HEAD_W:]

    for nn in range(nc):
        inc_ref[nn] = increment(k_ref, v_ref, nn)
    for nn in range(nc):
        st_ref[nn, :, 0:HEAD_W] = sf.astype(BF16)
        sf = cdf * sf + inc_ref[nn, :, 0:HEAD_W]
    for nn in reversed(range(nc)):
        st_ref[nn, :, HEAD_W:] = sb.astype(BF16)
        sb = cdb * sb + inc_ref[nn, :, HEAD_W:]

    ii = lax.broadcasted_iota(jnp.int32, (rc, rc), 0)
    jj = lax.broadcasted_iota(jnp.int32, (rc, rc), 1)
    dist = (ii - jj).astype(F32)
    dmat = (jnp.where(dist >= 0, jnp.exp(lgf * jnp.maximum(dist, 0.0)), 0.0)
            + jnp.where(dist <= 0, jnp.exp(lgb * jnp.maximum(-dist, 0.0)), 0.0))
    qdf = jnp.exp(lgf * (pos + 1.0)).astype(BF16)
    qdb = jnp.exp(lgb * (rc - pos)).astype(BF16)
    for nn in range(nc):
        rows = slice(nn * rc, (nn + 1) * rc)
        q = q_ref[0, rows, cs]
        s = _dot_nt(q, k_ref[0, rows, cs]) * dmat
        inner = _dot(s.astype(BF16), v_ref[0, rows, cs])
        qcat = jnp.concatenate([q * qdf, q * qdb], axis=1)
        out = inner + _dot_nt(qcat, st_ref[nn])
        o_ref[0, rows, cs] = _rms(out).astype(BF16) * g_ref[0, rows, cs]


def _retention(log_gamma, layer, z, sec, zc, sec_c, *, rc, heads):
    n_b, n, _ = z.shape
    n_ctx = 0 if zc is None else zc.shape[1]
    width = heads * LANES

    def col(rows, s, name):
        base = s[name] * (HEADS // heads)
        return pl.BlockSpec((1, rows, width), lambda b, h: (b, 0, base + h))

    in_specs = [pl.BlockSpec(memory_space=pltpu.SMEM),
                col(n, sec, "rq"), col(n, sec, "rg"), col(n, sec, "rk"), col(n, sec, "rv")]
    args = [log_gamma, z, z, z, z]
    if n_ctx:
        in_specs += [col(n_ctx, sec_c, "rk"), col(n_ctx, sec_c, "rv")]
        args += [zc, zc]
    return pl.pallas_call(
        functools.partial(_ret_kernel, layer=layer, n=n, n_ctx=n_ctx, rc=rc, heads=heads),
        grid=(n_b, HEADS // heads),
        in_specs=in_specs,
        out_specs=pl.BlockSpec((1, n, width), lambda b, h: (b, 0, h)),
        out_shape=jax.ShapeDtypeStruct((n_b, n, BRANCH_W), BF16),
        scratch_shapes=[pltpu.VMEM((n // rc, HEAD_W, 2 * HEAD_W), F32),
                        pltpu.VMEM((n // rc, HEAD_W, 2 * HEAD_W), BF16)],
        compiler_params=pltpu.CompilerParams(vmem_limit_bytes=VMEM_LIMIT),
        name="retention_ctx_init" if n_ctx else "retention",
    )(*args)


def _split_maps(q, lane):
    zero = jnp.zeros_like(q)
    return jnp.concatenate([jnp.where(lane < DIFF_DQK, q, zero),
                            jnp.where(lane >= DIFF_DQK, q, zero)], axis=0)


def _diff_ctx_kernel(lam_ref, q_ref, g_ref, k_ref, v_ref, o_ref, *, layer, tq, post_scale):
    lane = lax.broadcasted_iota(jnp.int32, (tq, LANES), 1)
    for h in range(HEADS):
        cs = slice(h * LANES, (h + 1) * LANES)
        s = _dot_nt(_split_maps(q_ref[0, :, cs], lane), k_ref[0, :, cs])
        p = jnp.exp2(s - jnp.max(s, axis=-1, keepdims=True))
        o = _dot(p.astype(BF16), v_ref[0, :, cs]) / jnp.sum(p, axis=-1, keepdims=True)
        d = o[:tq] - lam_ref[layer] * o[tq:]
        o_ref[0, :, cs] = (_rms(d) * post_scale * g_ref[0, :, cs].astype(F32)).astype(BF16)


def _diff_attention_ctx(lam, layer, zc, sec, *, post_scale):
    n_b, n, _ = zc.shape

    def col(name):
        s = sec[name]
        return pl.BlockSpec((1, n, SEC), lambda b: (b, 0, s))

    return pl.pallas_call(
        functools.partial(_diff_ctx_kernel, layer=layer, tq=n, post_scale=post_scale),
        grid=(n_b,),
        in_specs=[pl.BlockSpec(memory_space=pltpu.SMEM),
                  col("dq"), col("dg"), col("dk"), col("dv")],
        out_specs=pl.BlockSpec((1, n, BRANCH_W), lambda b: (b, 0, 0)),
        out_shape=jax.ShapeDtypeStruct((n_b, n, BRANCH_W), BF16),
        compiler_params=pltpu.CompilerParams(vmem_limit_bytes=VMEM_LIMIT),
        name="diff_attention_ctx",
    )(lam, zc, zc, zc, zc)


def _diff_pipe_kernel(lam_ref, q_ref, g_ref, kc_ref, vc_ref, kl_ref, vl_ref, o_ref,
                      kall, vall, s0, s1, p0, p1, *, layer, n_sub, ts, post_scale):
    n_ctx = kc_ref.shape[1]
    nk = kall.shape[0]
    kall[0:n_ctx, :] = kc_ref[0]
    kall[n_ctx:, :] = kl_ref[0]
    vall[0:n_ctx, 0:LANES] = vc_ref[0]
    vall[n_ctx:, 0:LANES] = vl_ref[0]
    vall[:, LANES:] = jnp.ones((nk, LANES), BF16)
    lam = lam_ref[layer]
    lane = lax.broadcasted_iota(jnp.int32, (ts, LANES), 1)
    s_bufs, p_bufs = (s0, s1), (p0, p1)

    def stage_a(t):
        rows = slice(t * ts, (t + 1) * ts)
        s_bufs[t % 2][...] = _dot_nt(_split_maps(q_ref[0, rows, :], lane), kall[...])

    def stage_b(t):
        s = s_bufs[t % 2][...]
        p_bufs[t % 2][...] = jnp.exp2(s - jnp.max(s, axis=-1, keepdims=True)).astype(BF16)

    def stage_c(t):
        rows = slice(t * ts, (t + 1) * ts)
        oe = _dot(p_bufs[t % 2][...], vall[...])
        o = oe[:, :LANES] / oe[:, LANES:]
        d = o[:ts] - lam * o[ts:]
        o_ref[0, rows, :] = (_rms(d) * post_scale * g_ref[0, rows, :].astype(F32)).astype(BF16)

    for t in range(n_sub + 2):
        if t >= 2:
            stage_c(t - 2)
        if 1 <= t <= n_sub:
            stage_b(t - 1)
        if t < n_sub:
            stage_a(t)


def _diff_attention_pipelined(lam, layer, z, sec, zc, sec_c, *, ts, post_scale):
    n_b, n, _ = z.shape
    n_ctx = zc.shape[1]
    nk = n_ctx + n

    def col(arr_n, s, name):
        base = s[name] * HEADS
        return pl.BlockSpec((1, arr_n, LANES), lambda b, h: (b, 0, base + h))

    return pl.pallas_call(
        functools.partial(_diff_pipe_kernel, layer=layer, n_sub=n // ts, ts=ts,
                          post_scale=post_scale),
        grid=(n_b, HEADS),
        in_specs=[pl.BlockSpec(memory_space=pltpu.SMEM),
                  col(n, sec, "dq"), col(n, sec, "dg"),
                  col(n_ctx, sec_c, "dk"), col(n_ctx, sec_c, "dv"),
                  col(n, sec, "dk"), col(n, sec, "dv")],
        out_specs=pl.BlockSpec((1, n, LANES), lambda b, h: (b, 0, h)),
        out_shape=jax.ShapeDtypeStruct((n_b, n, BRANCH_W), BF16),
        scratch_shapes=[pltpu.VMEM((nk, LANES), BF16), pltpu.VMEM((nk, 2 * LANES), BF16),
                        pltpu.VMEM((2 * ts, nk), F32), pltpu.VMEM((2 * ts, nk), F32),
                        pltpu.VMEM((2 * ts, nk), BF16), pltpu.VMEM((2 * ts, nk), BF16)],
        compiler_params=pltpu.CompilerParams(vmem_limit_bytes=VMEM_LIMIT),
        name="diff_attention_pipelined",
    )(lam, z, z, zc, zc, z, z)


def _merge_kernel(x_ref, gate_ref, gpost_ref, oret_ref, odiff_ref, u_ref, vn_ref, mg_ref,
                  gr_ref, gm_ref, gd_ref, ws_ref, bs_ref, wbo_ref, wo_ref, out_ref, *, tm, sub):
    for r in range(tm // sub):
        rs = slice(r * sub, (r + 1) * sub)
        rows = []
        for c in range(sub // CHUNK):
            cs = slice(r * sub + c * CHUNK, r * sub + (c + 1) * CHUNK)
            cols = []
            for g in range(HEADS):
                blk = vn_ref[0, cs, g * LANES:(g + 1) * LANES]
                cols.append(_dot(ws_ref[g], blk) + bs_ref[g])
            rows.append(jnp.concatenate(cols, axis=1))
        sp = jnp.concatenate(rows, axis=0)
        o_mlp = (u_ref[0, rs, :].astype(F32) * sp * mg_ref[0, rs, :].astype(F32)).astype(BF16)
        t = (gr_ref[0, rs, :].astype(F32) * _dot(oret_ref[0, rs, :], wbo_ref[0])
             + gm_ref[0, rs, :].astype(F32) * _dot(o_mlp, wbo_ref[1])
             + gd_ref[0, rs, :].astype(F32) * _dot(odiff_ref[0, rs, :], wbo_ref[2]))
        y = _dot(t.astype(BF16), wo_ref[...])
        out_ref[0, rs, :] = x_ref[0, rs, :] + gate_ref[...] * (_rms(y) * gpost_ref[...])


def _merge(xs, l, mod4, row_of, g_post, o_ret, o_diff, z, sec, ws, bs, wbo, wo, *, tm, sub):
    n_b, n, _ = xs.shape

    def zsec(name):
        s = sec[name]
        return pl.BlockSpec((1, tm, SEC), lambda b, i: (b, i, s))

    def zgate(k):
        return pl.BlockSpec((1, tm, D_MODEL), lambda b, i: (b, i, k))

    in_specs = [
        pl.BlockSpec((1, tm, D_MODEL), lambda b, i: (b, i, 0)),
        _mod_spec(l, row_of, 2),
        _layer_spec(l, (1, D_MODEL)),
        pl.BlockSpec((1, tm, BRANCH_W), lambda b, i: (b, i, 0)),
        pl.BlockSpec((1, tm, BRANCH_W), lambda b, i: (b, i, 0)),
        zsec("mu"), zsec("mv"), zsec("mg"), zgate(0), zgate(1), zgate(2),
        _layer_spec(l, (HEADS, CHUNK, CHUNK)),
        _layer_spec(l, (HEADS, CHUNK, LANES)),
        _layer_spec(l, (N_BRANCH, BRANCH_W, D_MODEL)),
        _layer_spec(l, (D_MODEL, D_MODEL)),
    ]
    return pl.pallas_call(
        functools.partial(_merge_kernel, tm=tm, sub=sub),
        grid=(n_b, n // tm),
        in_specs=in_specs,
        out_specs=pl.BlockSpec((1, tm, D_MODEL), lambda b, i: (b, i, 0)),
        out_shape=jax.ShapeDtypeStruct(xs.shape, F32),
        compiler_params=pltpu.CompilerParams(vmem_limit_bytes=VMEM_LIMIT),
        name="merge",
    )(xs, mod4, g_post, o_ret, o_diff, z, z, z, z, z, z, ws, bs, wbo, wo)


def _rope_tables(n_lat):
    rows = n_lat // GRID_W
    row_pos = jnp.repeat(jnp.arange(rows, dtype=F32), GRID_W)
    col_pos = jnp.tile(jnp.arange(GRID_W, dtype=F32), rows)

    def angles(head_dim):
        n_freq = head_dim // 4
        inv = ROPE_BASE ** (-jnp.arange(n_freq, dtype=F32) / n_freq)
        ang = jnp.concatenate([row_pos[:, None] * inv, col_pos[:, None] * inv], axis=-1)
        return jnp.cos(ang), jnp.sin(ang)

    cos_r, sin_r = angles(HEAD_W)
    cos_d, sin_d = angles(DIFF_DQK)
    zeros = jnp.zeros_like(sin_d)
    c_r = jnp.concatenate([cos_r, cos_r], axis=-1)
    s_r = jnp.concatenate([-sin_r, sin_r], axis=-1)
    c_d = jnp.tile(jnp.concatenate([cos_d, cos_d], axis=-1), (1, 2))
    s_lo = jnp.tile(jnp.concatenate([-sin_d, zeros], axis=-1), (1, 2))
    s_hi = jnp.tile(jnp.concatenate([zeros, sin_d], axis=-1), (1, 2))
    return c_r, s_r, c_d, s_lo, s_hi


def _full_src_block(j):
    n_merge = MERGE_COLS // SEC
    return jnp.where(j < n_merge, j + (KV_COLS + 7 * BRANCH_W) // SEC, j - n_merge)


@jax.jit
def kernel(x, c, ctx, c_ctx, w_mod, b_mod, g_pre, g_post, w_in, ret_decay_logit, mlp_w_s,
           mlp_b_s, diff_lambda_q, diff_lambda_k, w_branch_out, w_out):
    n_b, n_lat, _ = x.shape
    n_ctx = ctx.shape[1]
    tables = _rope_tables(n_lat)

    cond_rows = 16
    ctx_row = n_b
    cc = jnp.zeros((cond_rows, D_MODEL), F32).at[:n_b].set(c).at[ctx_row].set(c_ctx)
    mod4 = _modulation(cc, w_mod, b_mod).reshape(DEPTH, cond_rows, 1, 3 * D_MODEL)
    lat_row = lambda b: b
    ctx_row_of = lambda b: ctx_row

    log_gamma = -jax.nn.softplus(-ret_decay_logit.astype(F32))
    lam_inits = [0.8 - 0.6 * math.exp(-0.3 * l) for l in range(DEPTH)]
    lam = (jnp.exp(jnp.sum(diff_lambda_q[:, 0] * diff_lambda_k[:, 0], axis=-1))
           - jnp.exp(jnp.sum(diff_lambda_q[:, 1] * diff_lambda_k[:, 1], axis=-1))
           + jnp.asarray(lam_inits, F32)).astype(F32)

    w_in_b = w_in.astype(BF16)
    ws = mlp_w_s.astype(BF16)
    bs = jnp.broadcast_to(mlp_b_s[..., None], (DEPTH, HEADS, CHUNK, LANES)).astype(F32)
    wbo = w_branch_out.astype(BF16)
    wo = w_out.astype(BF16)
    g_pre3 = g_pre.reshape(DEPTH, 1, D_MODEL)
    g_post3 = g_post.reshape(DEPTH, 1, D_MODEL)

    for l in range(DEPTH):
        last = l == DEPTH - 1
        post_scale = 1.0 - lam_inits[l]

        ctx_flat = ctx.reshape(1, n_b * n_ctx, D_MODEL)
        if last:
            zc_flat = _projection(ctx_flat, l, g_pre3, mod4, ctx_row_of, w_in_b, lambda j: j,
                                  KV_KINDS, None)
            sec_c = KV_SEC
        else:
            zc_flat = _projection(ctx_flat, l, g_pre3, mod4, ctx_row_of, w_in_b, _full_src_block,
                                  FULL_KINDS, None)
            sec_c = FULL_SEC
        zc = zc_flat.reshape(n_b, n_ctx, -1)
        if not last:
            co_ret = _retention(log_gamma, l, zc, sec_c, None, None, rc=RET_CHUNK, heads=HEADS)
            co_diff = _diff_attention_ctx(lam, l, zc, sec_c, post_scale=post_scale)
            ctx_next = _merge(ctx_flat, l, mod4, ctx_row_of, g_post3,
                              co_ret.reshape(1, n_b * n_ctx, BRANCH_W),
                              co_diff.reshape(1, n_b * n_ctx, BRANCH_W), zc_flat, sec_c,
                              ws, bs, wbo, wo, tm=MERGE_TM, sub=MERGE_SUB)
            ctx_next = ctx_next.reshape(n_b, n_ctx, D_MODEL)

        z = _projection(x, l, g_pre3, mod4, lat_row, w_in_b, _full_src_block, FULL_KINDS, tables)
        o_ret = _retention(log_gamma, l, z, FULL_SEC, zc, sec_c, rc=RET_CHUNK, heads=1)
        o_diff = _diff_attention_pipelined(lam, l, z, FULL_SEC, zc, sec_c, ts=DIFF_TS,
                                           post_scale=post_scale)
        x = _merge(x, l, mod4, lat_row, g_post3, o_ret, o_diff, z, FULL_SEC, ws, bs, wbo, wo,
                   tm=MERGE_TM, sub=MERGE_SUB)
        if not last:
            ctx = ctx_next
    return x
```

```python
import functools
import math

import jax
import jax.numpy as jnp
from jax import lax
from jax.experimental import pallas as pl
from jax.experimental.pallas import tpu as pltpu

F32 = jnp.float32
BF16 = jnp.bfloat16

D_MODEL = 1024
DEPTH = 2
GRID_W = 64
N_BRANCH = 3
BRANCH_W = D_MODEL // 2
HEADS = 4
HEAD_W = BRANCH_W // HEADS
DIFF_DQK = HEAD_W // 2
CHUNK = 128
ROPE_BASE = 10000.0
EPS = 1e-6
RET_SCALE = HEAD_W ** -0.5
DIFF_Q_SCALE = (DIFF_DQK ** -0.5) * math.log2(math.e)

LANES = 128
SEC = BRANCH_W
MERGE_COLS = N_BRANCH * D_MODEL
KV_COLS = 4 * BRANCH_W
VMEM_LIMIT = 56 * 1024 * 1024

PROJ_SUB = 128
DIFF_TS = 256
RET_CHUNK = 128
RET_HEADS = 4
MERGE_TM = 512
MERGE_SUB = 512

FULL_KINDS = ("sigmoid",) * 6 + ("rk", "plain", "dk", "plain", "rq", "silu", "dq", "silu",
                                 "plain", "ln", "silu")
FULL_SEC = dict(rk=6, rv=7, dk=8, dv=9, rq=10, rg=11, dq=12, dg=13, mu=14, mv=15, mg=16)
KV_KINDS = ("rk", "plain", "dk", "plain")
KV_SEC = dict(rk=0, rv=1, dk=2, dv=3)


def _dot(a, b):
    return jnp.dot(a, b, preferred_element_type=F32)


def _dot_nt(a, b):
    return lax.dot_general(a, b, (((1,), (1,)), ((), ())), preferred_element_type=F32)


def _rms(v):
    return v * lax.rsqrt(jnp.mean(v * v, axis=-1, keepdims=True) + EPS)


def _split_bf16(a):
    hi = a.astype(BF16)
    return hi, (a - hi.astype(F32)).astype(BF16)


def _mod_kernel(c_ref, w_ref, b_ref, o_ref):
    cv = c_ref[...]
    c_hi, c_lo = _split_bf16(cv * jax.nn.sigmoid(cv))
    w_hi, w_lo = _split_bf16(w_ref[0])
    o_ref[0] = _dot(c_hi, w_hi) + _dot(c_hi, w_lo) + _dot(c_lo, w_hi) + b_ref[0]


def _modulation(cc, w_mod, b_mod):
    rows = cc.shape[0]
    return pl.pallas_call(
        _mod_kernel,
        grid=(DEPTH, 3),
        in_specs=[
            pl.BlockSpec((rows, D_MODEL), lambda l, j: (0, 0)),
            pl.BlockSpec((1, D_MODEL, D_MODEL), lambda l, j: (l, 0, j)),
            pl.BlockSpec((1, 1, D_MODEL), lambda l, j: (l, 0, j)),
        ],
        out_specs=pl.BlockSpec((1, rows, D_MODEL), lambda l, j: (l, 0, j)),
        out_shape=jax.ShapeDtypeStruct((DEPTH, rows, 3 * D_MODEL), F32),
        compiler_params=pltpu.CompilerParams(vmem_limit_bytes=VMEM_LIMIT),
        name="modulation",
    )(cc, w_mod, b_mod.reshape(DEPTH, 1, 3 * D_MODEL))


def _rope_ret(a, c, s):
    outs = []
    for h in range(HEADS):
        ah = a[:, h * LANES:(h + 1) * LANES]
        outs.append(ah * c + pltpu.roll(ah, LANES // 2, 1) * s)
    return jnp.concatenate(outs, axis=1)


def _rope_diff(a, c, s_lo, s_hi):
    outs = []
    for h in range(HEADS):
        ah = a[:, h * LANES:(h + 1) * LANES]
        outs.append(ah * c + pltpu.roll(ah, LANES - DIFF_DQK // 2, 1) * s_lo
                    + pltpu.roll(ah, DIFF_DQK // 2, 1) * s_hi)
    return jnp.concatenate(outs, axis=1)


def _proj_kernel(*refs, kinds, src_blocks, rope, n_rows, sub):
    if rope:
        (x_ref, g_ref, shift_ref, scale_ref, w_ref,
         cr_ref, sr_ref, cd_ref, sdl_ref, sdh_ref, z_hbm, h_ref, stage, sems) = refs
    else:
        x_ref, g_ref, shift_ref, scale_ref, w_ref, z_hbm, h_ref, stage, sems = refs
    b = pl.program_id(0)
    n_sub = n_rows // sub
    n_slots = stage.shape[0]

    gs = g_ref[...] * (1.0 + scale_ref[...])
    sh = shift_ref[...]
    for r in range(n_sub):
        xs = x_ref[0, r * sub:(r + 1) * sub, :]
        h_ref[r * sub:(r + 1) * sub, :] = (_rms(xs) * gs + sh).astype(BF16)

    def epi_rk(a, rows):
        if rope:
            a = _rope_ret(a, cr_ref[rows, :], sr_ref[rows, :])
        return a * RET_SCALE

    def epi_rq(a, rows):
        return _rope_ret(a, cr_ref[rows, :], sr_ref[rows, :]) if rope else a

    def epi_dk(a, rows):
        return _rope_diff(a, cd_ref[rows, :], sdl_ref[rows, :], sdh_ref[rows, :]) if rope else a

    def epi_dq(a, rows):
        return epi_dk(a, rows) * DIFF_Q_SCALE

    def epi_ln(a, rows):
        mu = jnp.mean(a, axis=-1, keepdims=True)
        d = a - mu
        return d * lax.rsqrt(jnp.mean(d * d, axis=-1, keepdims=True) + EPS)

    epilogues = dict(
        plain=lambda a, rows: a,
        sigmoid=lambda a, rows: jax.nn.sigmoid(a),
        silu=lambda a, rows: a * jax.nn.sigmoid(a),
        rk=epi_rk, rq=epi_rq, dk=epi_dk, dq=epi_dq, ln=epi_ln)

    def writeback(j):
        slot = j % n_slots
        return pltpu.make_async_copy(stage.at[slot], z_hbm.at[b, :, pl.ds(j * SEC, SEC)],
                                     sems.at[slot])

    for j, kind in enumerate(kinds):
        slot = j % n_slots
        if j >= n_slots:
            writeback(j - n_slots).wait()
        wcols = slice(src_blocks[j] * SEC, (src_blocks[j] + 1) * SEC)
        for r in range(n_sub):
            rows = slice(r * sub, (r + 1) * sub)
            acc = _dot(h_ref[rows, :], w_ref[:, wcols])
            stage[slot, rows, :] = epilogues[kind](acc, rows).astype(BF16)
        writeback(j).start()
    for j in range(max(0, len(kinds) - n_slots), len(kinds)):
        writeback(j).wait()


def _mod_spec(l, row_of, k):
    return pl.BlockSpec((None, None, 1, D_MODEL), lambda *idx: (l, row_of(idx[0]), 0, k))


def _layer_spec(l, shape, **kwargs):
    zeros = (0,) * len(shape)
    return pl.BlockSpec((None,) + tuple(shape), lambda *idx: (l,) + zeros, **kwargs)


def _projection(xs, l, g_pre, mod4, row_of, w_all, src_blocks, kinds, tables):
    n_g, n_rows, _ = xs.shape
    n_sec = len(kinds)
    rope = tables is not None
    w_cols = (max(src_blocks) + 1) * SEC
    resident = dict(pipeline_mode=pl.Buffered(1))
    in_specs = [
        pl.BlockSpec((1, n_rows, D_MODEL), lambda b: (b, 0, 0)),
        _layer_spec(l, (1, D_MODEL)),
        _mod_spec(l, row_of, 0),
        _mod_spec(l, row_of, 1),
        _layer_spec(l, (D_MODEL, w_cols), **resident),
    ]
    args = [xs, g_pre, mod4, mod4, w_all]
    if rope:
        in_specs += [pl.BlockSpec((n_rows, LANES), lambda b: (0, 0), **resident)] * len(tables)
        args += list(tables)
    return pl.pallas_call(
        functools.partial(_proj_kernel, kinds=kinds, src_blocks=src_blocks, rope=rope,
                          n_rows=n_rows, sub=PROJ_SUB),
        grid=(n_g,),
        in_specs=in_specs,
        out_specs=pl.BlockSpec(memory_space=pl.ANY),
        out_shape=jax.ShapeDtypeStruct((n_g, n_rows, n_sec * SEC), BF16),
        scratch_shapes=[pltpu.VMEM((n_rows, D_MODEL), BF16),
                        pltpu.VMEM((2, n_rows, SEC), BF16),
                        pltpu.SemaphoreType.DMA((2,))],
        compiler_params=pltpu.CompilerParams(vmem_limit_bytes=VMEM_LIMIT),
        name="projection_rope" if rope else "projection",
    )(*args)


def _ret_kernel(*refs, layer, n, n_ctx, rc, heads):
    if n_ctx:
        lg_ref, q_ref, g_ref, k_ref, v_ref, kc_ref, vc_ref, o_ref, inc_ref, st_ref = refs
    else:
        lg_ref, q_ref, g_ref, k_ref, v_ref, o_ref, inc_ref, st_ref = refs
    nc = n // rc
    hs = range(heads)
    cols = [slice(hh * LANES, (hh + 1) * LANES) for hh in hs]
    lgf = [lg_ref[layer, 0, pl.program_id(1) * heads + hh] for hh in hs]
    lgb = [lg_ref[layer, 1, pl.program_id(1) * heads + hh] for hh in hs]
    pos = lax.broadcasted_iota(jnp.int32, (rc, LANES), 0).astype(F32)
    kdf = [jnp.exp(lgf[hh] * (rc - 1.0 - pos)).astype(BF16) for hh in hs]
    kdb = [jnp.exp(lgb[hh] * pos).astype(BF16) for hh in hs]
    cdf = [jnp.exp(jnp.full((1, LANES), lgf[hh] * rc, F32)) for hh in hs]
    cdb = [jnp.exp(jnp.full((1, LANES), lgb[hh] * rc, F32)) for hh in hs]

    def increment(kr, vr, nn, hh):
        rows = slice(nn * rc, (nn + 1) * rc)
        kk = kr[0, rows, cols[hh]]
        kcat = jnp.concatenate([kk * kdf[hh], kk * kdb[hh]], axis=1)
        vt = vr[0, rows, cols[hh]].astype(F32).T.astype(BF16)
        return _dot(vt, kcat)

    sf = [jnp.zeros((HEAD_W, HEAD_W), F32) for _ in hs]
    sb = [jnp.zeros((HEAD_W, HEAD_W), F32) for _ in hs]
    if n_ctx:
        incs = [[increment(kc_ref, vc_ref, nn, hh) for hh in hs] for nn in range(n_ctx // rc)]
        for inc in incs:
            sf = [cdf[hh] * sf[hh] + inc[hh][:, :HEAD_W] for hh in hs]
        for inc in reversed(incs):
            sb = [cdb[hh] * sb[hh] + inc[hh][:, HEAD_W:] for hh in hs]

    for nn in range(nc):
        for hh in hs:
            inc_ref[hh, nn] = increment(k_ref, v_ref, nn, hh)
    for nn in range(nc):
        for hh in hs:
            st_ref[hh, nn, :, 0:HEAD_W] = sf[hh].astype(BF16)
            sf[hh] = cdf[hh] * sf[hh] + inc_ref[hh, nn, :, 0:HEAD_W]
    for nn in reversed(range(nc)):
        for hh in hs:
            st_ref[hh, nn, :, HEAD_W:] = sb[hh].astype(BF16)
            sb[hh] = cdb[hh] * sb[hh] + inc_ref[hh, nn, :, HEAD_W:]

    ii = lax.broadcasted_iota(jnp.int32, (rc, rc), 0)
    jj = lax.broadcasted_iota(jnp.int32, (rc, rc), 1)
    dist = (ii - jj).astype(F32)
    dmat = [jnp.where(dist >= 0, jnp.exp(lgf[hh] * jnp.maximum(dist, 0.0)), 0.0)
            + jnp.where(dist <= 0, jnp.exp(lgb[hh] * jnp.maximum(-dist, 0.0)), 0.0) for hh in hs]
    qdf = [jnp.exp(lgf[hh] * (pos + 1.0)).astype(BF16) for hh in hs]
    qdb = [jnp.exp(lgb[hh] * (rc - pos)).astype(BF16) for hh in hs]
    for nn in range(nc):
        rows = slice(nn * rc, (nn + 1) * rc)
        for hh in hs:
            cs = cols[hh]
            q = q_ref[0, rows, cs]
            s = _dot_nt(q, k_ref[0, rows, cs]) * dmat[hh]
            inner = _dot(s.astype(BF16), v_ref[0, rows, cs])
            qcat = jnp.concatenate([q * qdf[hh], q * qdb[hh]], axis=1)
            out = inner + _dot_nt(qcat, st_ref[hh, nn])
            o_ref[0, rows, cs] = _rms(out).astype(BF16) * g_ref[0, rows, cs]


def _retention(log_gamma, layer, z, sec, zc, sec_c, *, rc, heads):
    n_b, n, _ = z.shape
    n_ctx = 0 if zc is None else zc.shape[1]
    width = heads * LANES

    def col(rows, s, name):
        base = s[name] * (HEADS // heads)
        return pl.BlockSpec((1, rows, width), lambda b, h: (b, 0, base + h))

    in_specs = [pl.BlockSpec(memory_space=pltpu.SMEM),
                col(n, sec, "rq"), col(n, sec, "rg"), col(n, sec, "rk"), col(n, sec, "rv")]
    args = [log_gamma, z, z, z, z]
    if n_ctx:
        in_specs += [col(n_ctx, sec_c, "rk"), col(n_ctx, sec_c, "rv")]
        args += [zc, zc]
    return pl.pallas_call(
        functools.partial(_ret_kernel, layer=layer, n=n, n_ctx=n_ctx, rc=rc, heads=heads),
        grid=(n_b, HEADS // heads),
        in_specs=in_specs,
        out_specs=pl.BlockSpec((1, n, width), lambda b, h: (b, 0, h)),
        out_shape=jax.ShapeDtypeStruct((n_b, n, BRANCH_W), BF16),
        scratch_shapes=[pltpu.VMEM((heads, n // rc, HEAD_W, 2 * HEAD_W), F32),
                        pltpu.VMEM((heads, n // rc, HEAD_W, 2 * HEAD_W), BF16)],
        compiler_params=pltpu.CompilerParams(vmem_limit_bytes=VMEM_LIMIT),
        name="retention_ctx_init" if n_ctx else "retention",
    )(*args)


def _split_maps(q, lane):
    zero = jnp.zeros_like(q)
    return jnp.concatenate([jnp.where(lane < DIFF_DQK, q, zero),
                            jnp.where(lane >= DIFF_DQK, q, zero)], axis=0)


def _diff_ctx_kernel(lam_ref, q_ref, g_ref, k_ref, v_ref, o_ref, *, layer, tq, post_scale):
    lane = lax.broadcasted_iota(jnp.int32, (tq, LANES), 1)
    for h in range(HEADS):
        cs = slice(h * LANES, (h + 1) * LANES)
        s = _dot_nt(_split_maps(q_ref[0, :, cs], lane), k_ref[0, :, cs])
        p = jnp.exp2(s - jnp.max(s, axis=-1, keepdims=True))
        o = _dot(p.astype(BF16), v_ref[0, :, cs]) / jnp.sum(p, axis=-1, keepdims=True)
        d = o[:tq] - lam_ref[layer] * o[tq:]
        o_ref[0, :, cs] = (_rms(d) * post_scale * g_ref[0, :, cs].astype(F32)).astype(BF16)


def _diff_attention_ctx(lam, layer, zc, sec, *, post_scale):
    n_b, n, _ = zc.shape

    def col(name):
        s = sec[name]
        return pl.BlockSpec((1, n, SEC), lambda b: (b, 0, s))

    return pl.pallas_call(
        functools.partial(_diff_ctx_kernel, layer=layer, tq=n, post_scale=post_scale),
        grid=(n_b,),
        in_specs=[pl.BlockSpec(memory_space=pltpu.SMEM),
                  col("dq"), col("dg"), col("dk"), col("dv")],
        out_specs=pl.BlockSpec((1, n, BRANCH_W), lambda b: (b, 0, 0)),
        out_shape=jax.ShapeDtypeStruct((n_b, n, BRANCH_W), BF16),
        compiler_params=pltpu.CompilerParams(vmem_limit_bytes=VMEM_LIMIT),
        name="diff_attention_ctx",
    )(lam, zc, zc, zc, zc)


def _diff_pipe_kernel(lam_ref, q_ref, g_ref, kc_ref, vc_ref, kl_ref, vl_ref, o_ref,
                      kall, vall, s0, s1, p0, p1, *, layer, n_sub, ts, post_scale):
    n_ctx = kc_ref.shape[1]
    nk = kall.shape[0]
    kall[0:n_ctx, :] = kc_ref[0]
    kall[n_ctx:, :] = kl_ref[0]
    vall[0:n_ctx, 0:LANES] = vc_ref[0]
    vall[n_ctx:, 0:LANES] = vl_ref[0]
    vall[:, LANES:] = jnp.ones((nk, LANES), BF16)
    lam = lam_ref[layer]
    lane = lax.broadcasted_iota(jnp.int32, (ts, LANES), 1)
    s_bufs, p_bufs = (s0, s1), (p0, p1)

    def stage_a(t):
        rows = slice(t * ts, (t + 1) * ts)
        s_bufs[t % 2][...] = _dot_nt(_split_maps(q_ref[0, rows, :], lane), kall[...])

    def stage_b(t):
        s = s_bufs[t % 2][...]
        p_bufs[t % 2][...] = jnp.exp2(s - jnp.max(s, axis=-1, keepdims=True)).astype(BF16)

    def stage_c(t):
        rows = slice(t * ts, (t + 1) * ts)
        oe = _dot(p_bufs[t % 2][...], vall[...])
        o = oe[:, :LANES] / oe[:, LANES:]
        d = o[:ts] - lam * o[ts:]
        o_ref[0, rows, :] = (_rms(d) * post_scale * g_ref[0, rows, :].astype(F32)).astype(BF16)

    for t in range(n_sub + 2):
        if t >= 2:
            stage_c(t - 2)
        if 1 <= t <= n_sub:
            stage_b(t - 1)
        if t < n_sub:
            stage_a(t)


def _diff_attention_pipelined(lam, layer, z, sec, zc, sec_c, *, ts, post_scale):
    n_b, n, _ = z.shape
    n_ctx = zc.shape[1]
    nk = n_ctx + n

    def col(arr_n, s, name):
        base = s[name] * HEADS
        return pl.BlockSpec((1, arr_n, LANES), lambda b, h: (b, 0, base + h))

    return pl.pallas_call(
        functools.partial(_diff_pipe_kernel, layer=layer, n_sub=n // ts, ts=ts,
                          post_scale=post_scale),
        grid=(n_b, HEADS),
        in_specs=[pl.BlockSpec(memory_space=pltpu.SMEM),
                  col(n, sec, "dq"), col(n, sec, "dg"),
                  col(n_ctx, sec_c, "dk"), col(n_ctx, sec_c, "dv"),
                  col(n, sec, "dk"), col(n, sec, "dv")],
        out_specs=pl.BlockSpec((1, n, LANES), lambda b, h: (b, 0, h)),
        out_shape=jax.ShapeDtypeStruct((n_b, n, BRANCH_W), BF16),
        scratch_shapes=[pltpu.VMEM((nk, LANES), BF16), pltpu.VMEM((nk, 2 * LANES), BF16),
                        pltpu.VMEM((2 * ts, nk), F32), pltpu.VMEM((2 * ts, nk), F32),
                        pltpu.VMEM((2 * ts, nk), BF16), pltpu.VMEM((2 * ts, nk), BF16)],
        compiler_params=pltpu.CompilerParams(vmem_limit_bytes=VMEM_LIMIT),
        name="diff_attention_pipelined",
    )(lam, z, z, zc, zc, z, z)


def _merge_kernel(x_ref, gate_ref, gpost_ref, oret_ref, odiff_ref, u_ref, vn_ref, mg_ref,
                  gr_ref, gm_ref, gd_ref, ws_ref, bs_ref, wbo_ref, wo_ref, out_ref, *, tm, sub):
    for r in range(tm // sub):
        rs = slice(r * sub, (r + 1) * sub)
        rows = []
        for c in range(sub // CHUNK):
            cs = slice(r * sub + c * CHUNK, r * sub + (c + 1) * CHUNK)
            cols = []
            for g in range(HEADS):
                blk = vn_ref[0, cs, g * LANES:(g + 1) * LANES]
                cols.append(_dot(ws_ref[g], blk) + bs_ref[g])
            rows.append(jnp.concatenate(cols, axis=1))
        sp = jnp.concatenate(rows, axis=0)
        o_mlp = (u_ref[0, rs, :].astype(F32) * sp * mg_ref[0, rs, :].astype(F32)).astype(BF16)
        t = (gr_ref[0, rs, :].astype(F32) * _dot(oret_ref[0, rs, :], wbo_ref[0])
             + gm_ref[0, rs, :].astype(F32) * _dot(o_mlp, wbo_ref[1])
             + gd_ref[0, rs, :].astype(F32) * _dot(odiff_ref[0, rs, :], wbo_ref[2]))
        y = _dot(t.astype(BF16), wo_ref[...])
        out_ref[0, rs, :] = x_ref[0, rs, :] + gate_ref[...] * (_rms(y) * gpost_ref[...])


def _merge(xs, l, mod4, row_of, g_post, o_ret, o_diff, z, sec, ws, bs, wbo, wo, *, tm, sub):
    n_b, n, _ = xs.shape

    def zsec(name):
        s = sec[name]
        return pl.BlockSpec((1, tm, SEC), lambda b, i: (b, i, s))

    def zgate(k):
        return pl.BlockSpec((1, tm, D_MODEL), lambda b, i: (b, i, k))

    in_specs = [
        pl.BlockSpec((1, tm, D_MODEL), lambda b, i: (b, i, 0)),
        _mod_spec(l, row_of, 2),
        _layer_spec(l, (1, D_MODEL)),
        pl.BlockSpec((1, tm, BRANCH_W), lambda b, i: (b, i, 0)),
        pl.BlockSpec((1, tm, BRANCH_W), lambda b, i: (b, i, 0)),
        zsec("mu"), zsec("mv"), zsec("mg"), zgate(0), zgate(1), zgate(2),
        _layer_spec(l, (HEADS, CHUNK, CHUNK)),
        _layer_spec(l, (HEADS, CHUNK, LANES)),
        _layer_spec(l, (N_BRANCH, BRANCH_W, D_MODEL)),
        _layer_spec(l, (D_MODEL, D_MODEL)),
    ]
    return pl.pallas_call(
        functools.partial(_merge_kernel, tm=tm, sub=sub),
        grid=(n_b, n // tm),
        in_specs=in_specs,
        out_specs=pl.BlockSpec((1, tm, D_MODEL), lambda b, i: (b, i, 0)),
        out_shape=jax.ShapeDtypeStruct(xs.shape, F32),
        compiler_params=pltpu.CompilerParams(vmem_limit_bytes=VMEM_LIMIT),
        name="merge",
    )(xs, mod4, g_post, o_ret, o_diff, z, z, z, z, z, z, ws, bs, wbo, wo)


def _rope_tables(n_lat):
    rows = n_lat // GRID_W
    row_pos = jnp.repeat(jnp.arange(rows, dtype=F32), GRID_W)
    col_pos = jnp.tile(jnp.arange(GRID_W, dtype=F32), rows)

    def angles(head_dim):
        n_freq = head_dim // 4
        inv = ROPE_BASE ** (-jnp.arange(n_freq, dtype=F32) / n_freq)
        ang = jnp.concatenate([row_pos[:, None] * inv, col_pos[:, None] * inv], axis=-1)
        return jnp.cos(ang), jnp.sin(ang)

    cos_r, sin_r = angles(HEAD_W)
    cos_d, sin_d = angles(DIFF_DQK)
    zeros = jnp.zeros_like(sin_d)
    c_r = jnp.concatenate([cos_r, cos_r], axis=-1)
    s_r = jnp.concatenate([-sin_r, sin_r], axis=-1)
    c_d = jnp.tile(jnp.concatenate([cos_d, cos_d], axis=-1), (1, 2))
    s_lo = jnp.tile(jnp.concatenate([-sin_d, zeros], axis=-1), (1, 2))
    s_hi = jnp.tile(jnp.concatenate([zeros, sin_d], axis=-1), (1, 2))
    return c_r, s_r, c_d, s_lo, s_hi


_N_MERGE_SEC = MERGE_COLS // SEC
_N_REST_SEC = (KV_COLS + 7 * BRANCH_W) // SEC
FULL_SRC = tuple(range(_N_REST_SEC, _N_REST_SEC + _N_MERGE_SEC)) + tuple(range(_N_REST_SEC))
KV_SRC = tuple(range(KV_COLS // SEC))


@jax.jit
def kernel(x, c, ctx, c_ctx, w_mod, b_mod, g_pre, g_post, w_in, ret_decay_logit, mlp_w_s,
           mlp_b_s, diff_lambda_q, diff_lambda_k, w_branch_out, w_out):
    n_b, n_lat, _ = x.shape
    n_ctx = ctx.shape[1]
    tables = _rope_tables(n_lat)

    cond_rows = 16
    ctx_row = n_b
    cc = jnp.zeros((cond_rows, D_MODEL), F32).at[:n_b].set(c).at[ctx_row].set(c_ctx)
    mod4 = _modulation(cc, w_mod, b_mod).reshape(DEPTH, cond_rows, 1, 3 * D_MODEL)
    lat_row = lambda b: b
    ctx_row_of = lambda b: ctx_row

    log_gamma = -jax.nn.softplus(-ret_decay_logit.astype(F32))
    lam_inits = [0.8 - 0.6 * math.exp(-0.3 * l) for l in range(DEPTH)]
    lam = (jnp.exp(jnp.sum(diff_lambda_q[:, 0] * diff_lambda_k[:, 0], axis=-1))
           - jnp.exp(jnp.sum(diff_lambda_q[:, 1] * diff_lambda_k[:, 1], axis=-1))
           + jnp.asarray(lam_inits, F32)).astype(F32)

    w_in_b = w_in.astype(BF16)
    ws = mlp_w_s.astype(BF16)
    bs = jnp.broadcast_to(mlp_b_s[..., None], (DEPTH, HEADS, CHUNK, LANES)).astype(F32)
    wbo = w_branch_out.astype(BF16)
    wo = w_out.astype(BF16)
    g_pre3 = g_pre.reshape(DEPTH, 1, D_MODEL)
    g_post3 = g_post.reshape(DEPTH, 1, D_MODEL)

    for l in range(DEPTH):
        last = l == DEPTH - 1
        post_scale = 1.0 - lam_inits[l]

        ctx_flat = ctx.reshape(1, n_b * n_ctx, D_MODEL)
        if last:
            zc_flat = _projection(ctx_flat, l, g_pre3, mod4, ctx_row_of, w_in_b, KV_SRC,
                                  KV_KINDS, None)
            sec_c = KV_SEC
        else:
            zc_flat = _projection(ctx_flat, l, g_pre3, mod4, ctx_row_of, w_in_b, FULL_SRC,
                                  FULL_KINDS, None)
            sec_c = FULL_SEC
        zc = zc_flat.reshape(n_b, n_ctx, -1)
        if not last:
            co_ret = _retention(log_gamma, l, zc, sec_c, None, None, rc=RET_CHUNK, heads=HEADS)
            co_diff = _diff_attention_ctx(lam, l, zc, sec_c, post_scale=post_scale)
            ctx_next = _merge(ctx_flat, l, mod4, ctx_row_of, g_post3,
                              co_ret.reshape(1, n_b * n_ctx, BRANCH_W),
                              co_diff.reshape(1, n_b * n_ctx, BRANCH_W), zc_flat, sec_c,
                              ws, bs, wbo, wo, tm=MERGE_TM, sub=MERGE_SUB)
            ctx_next = ctx_next.reshape(n_b, n_ctx, D_MODEL)

        z = _projection(x, l, g_pre3, mod4, lat_row, w_in_b, FULL_SRC, FULL_KINDS, tables)
        o_ret = _retention(log_gamma, l, z, FULL_SEC, zc, sec_c, rc=RET_CHUNK, heads=RET_HEADS)
        o_diff = _diff_attention_pipelined(lam, l, z, FULL_SEC, zc, sec_c, ts=DIFF_TS,
                                           post_scale=post_scale)
        x = _merge(x, l, mod4, lat_row, g_post3, o_ret, o_diff, z, FULL_SEC, ws, bs, wbo, wo,
                   tm=MERGE_TM, sub=MERGE_SUB)
        if not last:
            ctx = ctx_next
    return x
```

```python
import functools
import math

import jax
import jax.numpy as jnp
from jax import lax
from jax.experimental import pallas as pl
from jax.experimental.pallas import tpu as pltpu

F32 = jnp.float32
BF16 = jnp.bfloat16

D_MODEL = 1024
DEPTH = 2
GRID_W = 64
N_BRANCH = 3
BRANCH_W = D_MODEL // 2
HEADS = 4
HEAD_W = BRANCH_W // HEADS
DIFF_DQK = HEAD_W // 2
CHUNK = 128
ROPE_BASE = 10000.0
EPS = 1e-6
RET_SCALE = HEAD_W ** -0.5
DIFF_Q_SCALE = (DIFF_DQK ** -0.5) * math.log2(math.e)

LANES = 128
SEC = BRANCH_W
MERGE_COLS = N_BRANCH * D_MODEL
KV_COLS = 4 * BRANCH_W
IN_COLS = KV_COLS + 7 * BRANCH_W + MERGE_COLS
VMEM_LIMIT = 56 * 1024 * 1024

PROJ_SUB = 128
DIFF_TS = 256
RET_CHUNK = 128
RET_HEADS = 4
MERGE_TM = 512

FULL_KINDS = ("sigmoid",) * 6 + ("rk", "plain", "dk", "plain", "rq", "silu", "dq", "silu",
                                 "plain", "ln", "silu")
FULL_SEC = dict(rk=6, rv=7, dk=8, dv=9, rq=10, rg=11, dq=12, dg=13, mu=14, mv=15, mg=16)
KV_KINDS = ("rk", "plain", "dk", "plain")
KV_SEC = dict(rk=0, rv=1, dk=2, dv=3)
_N_MERGE_SEC = MERGE_COLS // SEC
_N_REST_SEC = (KV_COLS + 7 * BRANCH_W) // SEC


def _full_src(j):
    return jnp.where(j < _N_MERGE_SEC, j + _N_REST_SEC, j - _N_MERGE_SEC)


def _dot(a, b):
    return jnp.dot(a, b, preferred_element_type=F32)


def _dot_nt(a, b):
    return lax.dot_general(a, b, (((1,), (1,)), ((), ())), preferred_element_type=F32)


def _rms(v):
    return v * lax.rsqrt(jnp.mean(v * v, axis=-1, keepdims=True) + EPS)


def _mod_spec(l, row_of, k):
    return pl.BlockSpec((None, None, 1, D_MODEL), lambda *idx: (l, row_of(idx[0]), 0, k))


def _layer_spec(l, shape, **kwargs):
    zeros = (0,) * len(shape)
    return pl.BlockSpec((None,) + tuple(shape), lambda *idx: (l,) + zeros, **kwargs)


def _z_spec(sec_idx, rows, width, flat, col_of=lambda *idx: 0, row_of=None):
    if row_of is None:
        row_of = (lambda *idx: idx[0]) if flat else (lambda *idx: 0)
    group_of = (lambda *idx: 0) if flat else (lambda *idx: idx[0])
    return pl.BlockSpec((1, None, rows, width),
                        lambda *idx: (group_of(*idx), sec_idx, row_of(*idx), col_of(*idx)))


def _split_bf16(a):
    hi = a.astype(BF16)
    return hi, (a - hi.astype(F32)).astype(BF16)


def _mod_kernel(c_ref, w_ref, b_ref, o_ref):
    cv = c_ref[...]
    c_hi, c_lo = _split_bf16(cv * jax.nn.sigmoid(cv))
    w_hi, w_lo = _split_bf16(w_ref[0])
    o_ref[0] = _dot(c_hi, w_hi) + _dot(c_hi, w_lo) + _dot(c_lo, w_hi) + b_ref[0]


def _modulation(cc, w_mod, b_mod):
    rows = cc.shape[0]
    return pl.pallas_call(
        _mod_kernel,
        grid=(DEPTH, 3),
        in_specs=[
            pl.BlockSpec((rows, D_MODEL), lambda l, j: (0, 0)),
            pl.BlockSpec((1, D_MODEL, D_MODEL), lambda l, j: (l, 0, j)),
            pl.BlockSpec((1, 1, D_MODEL), lambda l, j: (l, 0, j)),
        ],
        out_specs=pl.BlockSpec((1, rows, D_MODEL), lambda l, j: (l, 0, j)),
        out_shape=jax.ShapeDtypeStruct((DEPTH, rows, 3 * D_MODEL), F32),
        compiler_params=pltpu.CompilerParams(vmem_limit_bytes=VMEM_LIMIT),
        name="modulation",
    )(cc, w_mod, b_mod.reshape(DEPTH, 1, 3 * D_MODEL))


def _rope_ret(a, c, s):
    outs = []
    for h in range(HEADS):
        ah = a[:, h * LANES:(h + 1) * LANES]
        outs.append(ah * c + pltpu.roll(ah, LANES // 2, 1) * s)
    return jnp.concatenate(outs, axis=1)


def _rope_diff(a, c, s_lo, s_hi):
    outs = []
    for h in range(HEADS):
        ah = a[:, h * LANES:(h + 1) * LANES]
        outs.append(ah * c + pltpu.roll(ah, LANES - DIFF_DQK // 2, 1) * s_lo
                    + pltpu.roll(ah, DIFF_DQK // 2, 1) * s_hi)
    return jnp.concatenate(outs, axis=1)


def _proj_kernel(*refs, kinds, src_of, rope, n_rows, sub):
    if rope:
        (x_ref, g_ref, shift_ref, scale_ref, w_ref,
         cr_ref, sr_ref, cd_ref, sdl_ref, sdh_ref, z_hbm, h_ref, stage, sems) = refs
    else:
        x_ref, g_ref, shift_ref, scale_ref, w_ref, z_hbm, h_ref, stage, sems = refs
    b = pl.program_id(0)
    n_sub = n_rows // sub
    n_sec = len(kinds)
    n_slots = stage.shape[0]

    gs = g_ref[...] * (1.0 + scale_ref[...])
    sh = shift_ref[...]
    for r in range(n_sub):
        xs = x_ref[0, r * sub:(r + 1) * sub, :]
        h_ref[r * sub:(r + 1) * sub, :] = (_rms(xs) * gs + sh).astype(BF16)

    def epi_rk(a, rows):
        if rope:
            a = _rope_ret(a, cr_ref[rows, :], sr_ref[rows, :])
        return a * RET_SCALE

    def epi_rq(a, rows):
        return _rope_ret(a, cr_ref[rows, :], sr_ref[rows, :]) if rope else a

    def epi_dk(a, rows):
        return _rope_diff(a, cd_ref[rows, :], sdl_ref[rows, :], sdh_ref[rows, :]) if rope else a

    def epi_dq(a, rows):
        return epi_dk(a, rows) * DIFF_Q_SCALE

    def epi_ln(a, rows):
        mu = jnp.mean(a, axis=-1, keepdims=True)
        d = a - mu
        return d * lax.rsqrt(jnp.mean(d * d, axis=-1, keepdims=True) + EPS)

    epilogues = dict(
        plain=lambda a, rows: a,
        sigmoid=lambda a, rows: jax.nn.sigmoid(a),
        silu=lambda a, rows: a * jax.nn.sigmoid(a),
        rk=epi_rk, rq=epi_rq, dk=epi_dk, dq=epi_dq, ln=epi_ln)

    def writeback(j):
        slot = j % n_slots
        return pltpu.make_async_copy(stage.at[slot], z_hbm.at[b, j], sems.at[slot])

    def run(epilogue, j):
        w_sec = w_ref.at[src_of(j)]
        out = stage.at[j % n_slots]
        for r in range(n_sub):
            rows = slice(r * sub, (r + 1) * sub)
            out[rows, :] = epilogue(_dot(h_ref[rows, :], w_sec[...]), rows).astype(BF16)

    def section(j, carry):
        @pl.when(j >= n_slots)
        def _():
            writeback(j - n_slots).wait()

        for kind in sorted(set(kinds)):
            idx = [i for i, k in enumerate(kinds) if k == kind]
            cond = j == idx[0]
            for i in idx[1:]:
                cond = cond | (j == i)
            pl.when(cond)(functools.partial(run, epilogues[kind], j))
        writeback(j).start()
        return carry

    lax.fori_loop(0, n_sec, section, 0)
    for j in range(max(0, n_sec - n_slots), n_sec):
        writeback(j).wait()


def _projection(xs, l, g_pre, mod4, row_of, w_sec, src_of, kinds, tables):
    n_g, n_rows, _ = xs.shape
    n_sec = len(kinds)
    rope = tables is not None
    n_src = w_sec.shape[1] if src_of is _full_src else n_sec
    resident = dict(pipeline_mode=pl.Buffered(1))
    in_specs = [
        pl.BlockSpec((1, n_rows, D_MODEL), lambda b: (b, 0, 0)),
        _layer_spec(l, (1, D_MODEL)),
        _mod_spec(l, row_of, 0),
        _mod_spec(l, row_of, 1),
        _layer_spec(l, (n_src, D_MODEL, SEC), **resident),
    ]
    args = [xs, g_pre, mod4, mod4, w_sec]
    if rope:
        in_specs += [pl.BlockSpec((n_rows, LANES), lambda b: (0, 0), **resident)] * len(tables)
        args += list(tables)
    return pl.pallas_call(
        functools.partial(_proj_kernel, kinds=kinds, src_of=src_of, rope=rope,
                          n_rows=n_rows, sub=PROJ_SUB),
        grid=(n_g,),
        in_specs=in_specs,
        out_specs=pl.BlockSpec(memory_space=pl.ANY),
        out_shape=jax.ShapeDtypeStruct((n_g, n_sec, n_rows, SEC), BF16),
        scratch_shapes=[pltpu.VMEM((n_rows, D_MODEL), BF16),
                        pltpu.VMEM((2, n_rows, SEC), BF16),
                        pltpu.SemaphoreType.DMA((2,))],
        compiler_params=pltpu.CompilerParams(vmem_limit_bytes=VMEM_LIMIT),
        name="projection_rope" if rope else "projection",
    )(*args)


def _ret_kernel(*refs, layer, n, n_ctx, rc, heads):
    if n_ctx:
        lg_ref, q_ref, g_ref, k_ref, v_ref, kc_ref, vc_ref, o_ref, inc_ref, st_ref = refs
    else:
        lg_ref, q_ref, g_ref, k_ref, v_ref, o_ref, inc_ref, st_ref = refs
    nc = n // rc
    hs = range(heads)
    cols = [slice(hh * LANES, (hh + 1) * LANES) for hh in hs]
    lgf = [lg_ref[layer, 0, pl.program_id(1) * heads + hh] for hh in hs]
    lgb = [lg_ref[layer, 1, pl.program_id(1) * heads + hh] for hh in hs]
    pos = lax.broadcasted_iota(jnp.int32, (rc, LANES), 0).astype(F32)
    kdf = [jnp.exp(lgf[hh] * (rc - 1.0 - pos)).astype(BF16) for hh in hs]
    kdb = [jnp.exp(lgb[hh] * pos).astype(BF16) for hh in hs]
    cdf = [jnp.exp(jnp.full((1, LANES), lgf[hh] * rc, F32)) for hh in hs]
    cdb = [jnp.exp(jnp.full((1, LANES), lgb[hh] * rc, F32)) for hh in hs]

    def increment(kr, vr, nn, hh):
        rows = slice(nn * rc, (nn + 1) * rc)
        kk = kr[0, rows, cols[hh]]
        kcat = jnp.concatenate([kk * kdf[hh], kk * kdb[hh]], axis=1)
        vt = vr[0, rows, cols[hh]].astype(F32).T.astype(BF16)
        return _dot(vt, kcat)

    sf = [jnp.zeros((HEAD_W, HEAD_W), F32) for _ in hs]
    sb = [jnp.zeros((HEAD_W, HEAD_W), F32) for _ in hs]
    if n_ctx:
        incs = [[increment(kc_ref, vc_ref, nn, hh) for hh in hs] for nn in range(n_ctx // rc)]
        for inc in incs:
            sf = [cdf[hh] * sf[hh] + inc[hh][:, :HEAD_W] for hh in hs]
        for inc in reversed(incs):
            sb = [cdb[hh] * sb[hh] + inc[hh][:, HEAD_W:] for hh in hs]

    for nn in range(nc):
        for hh in hs:
            inc_ref[hh, nn] = increment(k_ref, v_ref, nn, hh)
    for nn in range(nc):
        for hh in hs:
            st_ref[hh, nn, :, 0:HEAD_W] = sf[hh].astype(BF16)
            sf[hh] = cdf[hh] * sf[hh] + inc_ref[hh, nn, :, 0:HEAD_W]
    for nn in reversed(range(nc)):
        for hh in hs:
            st_ref[hh, nn, :, HEAD_W:] = sb[hh].astype(BF16)
            sb[hh] = cdb[hh] * sb[hh] + inc_ref[hh, nn, :, HEAD_W:]

    ii = lax.broadcasted_iota(jnp.int32, (rc, rc), 0)
    jj = lax.broadcasted_iota(jnp.int32, (rc, rc), 1)
    dist = (ii - jj).astype(F32)
    dmat = [jnp.where(dist >= 0, jnp.exp(lgf[hh] * jnp.maximum(dist, 0.0)), 0.0)
            + jnp.where(dist <= 0, jnp.exp(lgb[hh] * jnp.maximum(-dist, 0.0)), 0.0) for hh in hs]
    qdf = [jnp.exp(lgf[hh] * (pos + 1.0)).astype(BF16) for hh in hs]
    qdb = [jnp.exp(lgb[hh] * (rc - pos)).astype(BF16) for hh in hs]
    for nn in range(nc):
        rows = slice(nn * rc, (nn + 1) * rc)
        for hh in hs:
            cs = cols[hh]
            q = q_ref[0, rows, cs]
            s = _dot_nt(q, k_ref[0, rows, cs]) * dmat[hh]
            inner = _dot(s.astype(BF16), v_ref[0, rows, cs])
            qcat = jnp.concatenate([q * qdf[hh], q * qdb[hh]], axis=1)
            out = inner + _dot_nt(qcat, st_ref[hh, nn])
            o_ref[0, rows, cs] = _rms(out).astype(BF16) * g_ref[0, rows, cs]


def _retention(log_gamma, layer, z, sec, flat, n_b, n, zc, sec_c, n_ctx, *, rc, heads):
    width = heads * LANES
    head_block = lambda b, h: h

    in_specs = [pl.BlockSpec(memory_space=pltpu.SMEM)]
    in_specs += [_z_spec(sec[name], n, width, flat, col_of=head_block)
                 for name in ("rq", "rg", "rk", "rv")]
    args = [log_gamma, z, z, z, z]
    if n_ctx:
        in_specs += [_z_spec(sec_c[name], n_ctx, width, True, col_of=head_block)
                     for name in ("rk", "rv")]
        args += [zc, zc]
    return pl.pallas_call(
        functools.partial(_ret_kernel, layer=layer, n=n, n_ctx=n_ctx, rc=rc, heads=heads),
        grid=(n_b, HEADS // heads),
        in_specs=in_specs,
        out_specs=pl.BlockSpec((1, n, width), lambda b, h: (b, 0, h)),
        out_shape=jax.ShapeDtypeStruct((n_b, n, BRANCH_W), BF16),
        scratch_shapes=[pltpu.VMEM((heads, n // rc, HEAD_W, 2 * HEAD_W), F32),
                        pltpu.VMEM((heads, n // rc, HEAD_W, 2 * HEAD_W), BF16)],
        compiler_params=pltpu.CompilerParams(vmem_limit_bytes=VMEM_LIMIT),
        name="retention_ctx_init" if n_ctx else "retention",
    )(*args)


def _split_maps(q, lane):
    zero = jnp.zeros_like(q)
    return jnp.concatenate([jnp.where(lane < DIFF_DQK, q, zero),
                            jnp.where(lane >= DIFF_DQK, q, zero)], axis=0)


def _diff_ctx_kernel(lam_ref, q_ref, g_ref, k_ref, v_ref, o_ref, *, layer, tq, post_scale):
    lane = lax.broadcasted_iota(jnp.int32, (tq, LANES), 1)
    for h in range(HEADS):
        cs = slice(h * LANES, (h + 1) * LANES)
        s = _dot_nt(_split_maps(q_ref[0, :, cs], lane), k_ref[0, :, cs])
        p = jnp.exp2(s - jnp.max(s, axis=-1, keepdims=True))
        o = _dot(p.astype(BF16), v_ref[0, :, cs]) / jnp.sum(p, axis=-1, keepdims=True)
        d = o[:tq] - lam_ref[layer] * o[tq:]
        o_ref[0, :, cs] = (_rms(d) * post_scale * g_ref[0, :, cs].astype(F32)).astype(BF16)


def _diff_attention_ctx(lam, layer, zc, sec, n_b, n, *, post_scale):
    return pl.pallas_call(
        functools.partial(_diff_ctx_kernel, layer=layer, tq=n, post_scale=post_scale),
        grid=(n_b,),
        in_specs=[pl.BlockSpec(memory_space=pltpu.SMEM)]
        + [_z_spec(sec[name], n, SEC, True) for name in ("dq", "dg", "dk", "dv")],
        out_specs=pl.BlockSpec((1, n, BRANCH_W), lambda b: (b, 0, 0)),
        out_shape=jax.ShapeDtypeStruct((n_b, n, BRANCH_W), BF16),
        compiler_params=pltpu.CompilerParams(vmem_limit_bytes=VMEM_LIMIT),
        name="diff_attention_ctx",
    )(lam, zc, zc, zc, zc)


def _diff_pipe_kernel(lam_ref, q_ref, g_ref, kc_ref, vc_ref, kl_ref, vl_ref, o_ref,
                      kall, vall, s0, s1, p0, p1, *, layer, n_sub, ts, post_scale):
    n_ctx = kc_ref.shape[1]
    nk = kall.shape[0]
    kall[0:n_ctx, :] = kc_ref[0]
    kall[n_ctx:, :] = kl_ref[0]
    vall[0:n_ctx, 0:LANES] = vc_ref[0]
    vall[n_ctx:, 0:LANES] = vl_ref[0]
    vall[:, LANES:] = jnp.ones((nk, LANES), BF16)
    lam = lam_ref[layer]
    lane = lax.broadcasted_iota(jnp.int32, (ts, LANES), 1)
    s_bufs, p_bufs = (s0, s1), (p0, p1)

    def stage_a(t):
        rows = slice(t * ts, (t + 1) * ts)
        s_bufs[t % 2][...] = _dot_nt(_split_maps(q_ref[0, rows, :], lane), kall[...])

    def stage_b(t):
        s = s_bufs[t % 2][...]
        p_bufs[t % 2][...] = jnp.exp2(s - jnp.max(s, axis=-1, keepdims=True)).astype(BF16)

    def stage_c(t):
        rows = slice(t * ts, (t + 1) * ts)
        oe = _dot(p_bufs[t % 2][...], vall[...])
        o = oe[:, :LANES] / oe[:, LANES:]
        d = o[:ts] - lam * o[ts:]
        o_ref[0, rows, :] = (_rms(d) * post_scale * g_ref[0, rows, :].astype(F32)).astype(BF16)

    for t in range(n_sub + 2):
        if t >= 2:
            stage_c(t - 2)
        if 1 <= t <= n_sub:
            stage_b(t - 1)
        if t < n_sub:
            stage_a(t)


def _diff_attention_pipelined(lam, layer, z, sec, n_b, n, zc, sec_c, n_ctx, *, ts, post_scale):
    nk = n_ctx + n
    head_block = lambda b, h: h
    lat = lambda name: _z_spec(sec[name], n, LANES, False, col_of=head_block)
    cx = lambda name: _z_spec(sec_c[name], n_ctx, LANES, True, col_of=head_block)
    return pl.pallas_call(
        functools.partial(_diff_pipe_kernel, layer=layer, n_sub=n // ts, ts=ts,
                          post_scale=post_scale),
        grid=(n_b, HEADS),
        in_specs=[pl.BlockSpec(memory_space=pltpu.SMEM),
                  lat("dq"), lat("dg"), cx("dk"), cx("dv"), lat("dk"), lat("dv")],
        out_specs=pl.BlockSpec((1, n, LANES), lambda b, h: (b, 0, h)),
        out_shape=jax.ShapeDtypeStruct((n_b, n, BRANCH_W), BF16),
        scratch_shapes=[pltpu.VMEM((nk, LANES), BF16), pltpu.VMEM((nk, 2 * LANES), BF16),
                        pltpu.VMEM((2 * ts, nk), F32), pltpu.VMEM((2 * ts, nk), F32),
                        pltpu.VMEM((2 * ts, nk), BF16), pltpu.VMEM((2 * ts, nk), BF16)],
        compiler_params=pltpu.CompilerParams(vmem_limit_bytes=VMEM_LIMIT),
        name="diff_attention_pipelined",
    )(lam, z, z, zc, zc, z, z)


def _merge_kernel(x_ref, gate_ref, gpost_ref, oret_ref, odiff_ref, u_ref, vn_ref, mg_ref,
                  gr_ref, gm_ref, gd_ref, ws_ref, bs_ref, wbo_ref, wo_ref, out_ref, *, tm):
    rows = []
    for c in range(tm // CHUNK):
        cols = []
        for g in range(HEADS):
            blk = vn_ref[0, c * CHUNK:(c + 1) * CHUNK, g * LANES:(g + 1) * LANES]
            cols.append(_dot(ws_ref[g], blk) + bs_ref[g])
        rows.append(jnp.concatenate(cols, axis=1))
    sp = jnp.concatenate(rows, axis=0)
    o_mlp = (u_ref[0].astype(F32) * sp * mg_ref[0].astype(F32)).astype(BF16)

    def gated(gate2_ref, o, w):
        gate = jnp.concatenate([gate2_ref[0, 0], gate2_ref[0, 1]], axis=1).astype(F32)
        return gate * _dot(o, w)

    t = (gated(gr_ref, oret_ref[0], wbo_ref[0]) + gated(gm_ref, o_mlp, wbo_ref[1])
         + gated(gd_ref, odiff_ref[0], wbo_ref[2]))
    y = _dot(t.astype(BF16), wo_ref[...])
    out_ref[0] = x_ref[0] + gate_ref[...] * (_rms(y) * gpost_ref[...])


def _merge(xs, l, mod4, row_of, g_post, o_ret, o_diff, z, sec, ws, bs, wbo, wo, *, tm):
    n_b, n, _ = xs.shape
    row_block = lambda b, i: i

    def zsec(name):
        return _z_spec(sec[name], tm, SEC, False, row_of=row_block)

    def zgate(k):
        return pl.BlockSpec((1, 2, tm, SEC), lambda b, i: (b, k, i, 0))

    in_specs = [
        pl.BlockSpec((1, tm, D_MODEL), lambda b, i: (b, i, 0)),
        _mod_spec(l, row_of, 2),
        _layer_spec(l, (1, D_MODEL)),
        pl.BlockSpec((1, tm, BRANCH_W), lambda b, i: (b, i, 0)),
        pl.BlockSpec((1, tm, BRANCH_W), lambda b, i: (b, i, 0)),
        zsec("mu"), zsec("mv"), zsec("mg"), zgate(0), zgate(1), zgate(2),
        _layer_spec(l, (HEADS, CHUNK, CHUNK)),
        _layer_spec(l, (HEADS, CHUNK, LANES)),
        _layer_spec(l, (N_BRANCH, BRANCH_W, D_MODEL)),
        _layer_spec(l, (D_MODEL, D_MODEL)),
    ]
    return pl.pallas_call(
        functools.partial(_merge_kernel, tm=tm),
        grid=(n_b, n // tm),
        in_specs=in_specs,
        out_specs=pl.BlockSpec((1, tm, D_MODEL), lambda b, i: (b, i, 0)),
        out_shape=jax.ShapeDtypeStruct(xs.shape, F32),
        compiler_params=pltpu.CompilerParams(vmem_limit_bytes=VMEM_LIMIT),
        name="merge",
    )(xs, mod4, g_post, o_ret, o_diff, z, z, z, z, z, z, ws, bs, wbo, wo)


def _rope_tables(n_lat):
    rows = n_lat // GRID_W
    row_pos = jnp.repeat(jnp.arange(rows, dtype=F32), GRID_W)
    col_pos = jnp.tile(jnp.arange(GRID_W, dtype=F32), rows)

    def angles(head_dim):
        n_freq = head_dim // 4
        inv = ROPE_BASE ** (-jnp.arange(n_freq, dtype=F32) / n_freq)
        ang = jnp.concatenate([row_pos[:, None] * inv, col_pos[:, None] * inv], axis=-1)
        return jnp.cos(ang), jnp.sin(ang)

    cos_r, sin_r = angles(HEAD_W)
    cos_d, sin_d = angles(DIFF_DQK)
    zeros = jnp.zeros_like(sin_d)
    c_r = jnp.concatenate([cos_r, cos_r], axis=-1)
    s_r = jnp.concatenate([-sin_r, sin_r], axis=-1)
    c_d = jnp.tile(jnp.concatenate([cos_d, cos_d], axis=-1), (1, 2))
    s_lo = jnp.tile(jnp.concatenate([-sin_d, zeros], axis=-1), (1, 2))
    s_hi = jnp.tile(jnp.concatenate([zeros, sin_d], axis=-1), (1, 2))
    return c_r, s_r, c_d, s_lo, s_hi


@jax.jit
def kernel(x, c, ctx, c_ctx, w_mod, b_mod, g_pre, g_post, w_in, ret_decay_logit, mlp_w_s,
           mlp_b_s, diff_lambda_q, diff_lambda_k, w_branch_out, w_out):
    n_b, n_lat, _ = x.shape
    n_ctx = ctx.shape[1]
    tables = _rope_tables(n_lat)

    cond_rows = 16
    ctx_row = n_b
    cc = jnp.zeros((cond_rows, D_MODEL), F32).at[:n_b].set(c).at[ctx_row].set(c_ctx)
    mod4 = _modulation(cc, w_mod, b_mod).reshape(DEPTH, cond_rows, 1, 3 * D_MODEL)
    lat_row = lambda b: b
    ctx_row_of = lambda b: ctx_row

    log_gamma = -jax.nn.softplus(-ret_decay_logit.astype(F32))
    lam_inits = [0.8 - 0.6 * math.exp(-0.3 * l) for l in range(DEPTH)]
    lam = (jnp.exp(jnp.sum(diff_lambda_q[:, 0] * diff_lambda_k[:, 0], axis=-1))
           - jnp.exp(jnp.sum(diff_lambda_q[:, 1] * diff_lambda_k[:, 1], axis=-1))
           + jnp.asarray(lam_inits, F32)).astype(F32)

    w_sec = w_in.astype(BF16).reshape(DEPTH, D_MODEL, IN_COLS // SEC, SEC).transpose(0, 2, 1, 3)
    ws = mlp_w_s.astype(BF16)
    bs = jnp.broadcast_to(mlp_b_s[..., None], (DEPTH, HEADS, CHUNK, LANES)).astype(F32)
    wbo = w_branch_out.astype(BF16)
    wo = w_out.astype(BF16)
    g_pre3 = g_pre.reshape(DEPTH, 1, D_MODEL)
    g_post3 = g_post.reshape(DEPTH, 1, D_MODEL)

    for l in range(DEPTH):
        last = l == DEPTH - 1
        post_scale = 1.0 - lam_inits[l]

        ctx_flat = ctx.reshape(1, n_b * n_ctx, D_MODEL)
        if last:
            zc = _projection(ctx_flat, l, g_pre3, mod4, ctx_row_of, w_sec, lambda j: j,
                             KV_KINDS, None)
            sec_c = KV_SEC
        else:
            zc = _projection(ctx_flat, l, g_pre3, mod4, ctx_row_of, w_sec, _full_src,
                             FULL_KINDS, None)
            sec_c = FULL_SEC
            co_ret = _retention(log_gamma, l, zc, sec_c, True, n_b, n_ctx, None, None, 0,
                                rc=RET_CHUNK, heads=HEADS)
            co_diff = _diff_attention_ctx(lam, l, zc, sec_c, n_b, n_ctx, post_scale=post_scale)
            ctx_next = _merge(ctx_flat, l, mod4, ctx_row_of, g_post3,
                              co_ret.reshape(1, n_b * n_ctx, BRANCH_W),
                              co_diff.reshape(1, n_b * n_ctx, BRANCH_W), zc, sec_c,
                              ws, bs, wbo, wo, tm=MERGE_TM)
            ctx_next = ctx_next.reshape(n_b, n_ctx, D_MODEL)

        z = _projection(x, l, g_pre3, mod4, lat_row, w_sec, _full_src, FULL_KINDS, tables)
        o_ret = _retention(log_gamma, l, z, FULL_SEC, False, n_b, n_lat, zc, sec_c, n_ctx,
                           rc=RET_CHUNK, heads=RET_HEADS)
        o_diff = _diff_attention_pipelined(lam, l, z, FULL_SEC, n_b, n_lat, zc, sec_c, n_ctx,
                                           ts=DIFF_TS, post_scale=post_scale)
        x = _merge(x, l, mod4, lat_row, g_post3, o_ret, o_diff, z, FULL_SEC, ws, bs, wbo, wo,
                   tm=MERGE_TM)
        if not last:
            ctx = ctx_next
    return x
```

```python
import functools
import math

import jax
import jax.numpy as jnp
from jax import lax
from jax.experimental import pallas as pl
from jax.experimental.pallas import tpu as pltpu

F32 = jnp.float32
BF16 = jnp.bfloat16

D_MODEL = 1024
DEPTH = 2
GRID_W = 64
N_BRANCH = 3
BRANCH_W = D_MODEL // 2
HEADS = 4
HEAD_W = BRANCH_W // HEADS
DIFF_DQK = HEAD_W // 2
CHUNK = 128
ROPE_BASE = 10000.0
EPS = 1e-6
RET_SCALE = HEAD_W ** -0.5
DIFF_Q_SCALE = (DIFF_DQK ** -0.5) * math.log2(math.e)

LANES = 128
SEC = BRANCH_W
MERGE_COLS = N_BRANCH * D_MODEL
KV_COLS = 4 * BRANCH_W
IN_COLS = KV_COLS + 7 * BRANCH_W + MERGE_COLS
VMEM_LIMIT = 56 * 1024 * 1024

PROJ_SUB = 128
DIFF_TS = 256
RET_CHUNK = 128
RET_HEADS = 4
MERGE_TM = 512

FULL_KINDS = ("sigmoid",) * 6 + ("rk", "plain", "dk", "plain", "rq", "silu", "dq", "silu",
                                 "plain", "ln", "silu")
FULL_SEC = dict(rk=6, rv=7, dk=8, dv=9, rq=10, rg=11, dq=12, dg=13, mu=14, mv=15, mg=16)
KV_KINDS = ("rk", "plain", "dk", "plain")
KV_SEC = dict(rk=0, rv=1, dk=2, dv=3)
_N_MERGE_SEC = MERGE_COLS // SEC
_N_REST_SEC = (KV_COLS + 7 * BRANCH_W) // SEC


def _full_src(j):
    return jnp.where(j < _N_MERGE_SEC, j + _N_REST_SEC, j - _N_MERGE_SEC)


def _dot(a, b):
    return jnp.dot(a, b, preferred_element_type=F32)


def _dot_nt(a, b):
    return lax.dot_general(a, b, (((1,), (1,)), ((), ())), preferred_element_type=F32)


def _rms(v):
    return v * lax.rsqrt(jnp.mean(v * v, axis=-1, keepdims=True) + EPS)


def _mod_spec(l, row_of, k):
    return pl.BlockSpec((None, None, 1, D_MODEL), lambda *idx: (l, row_of(idx[0]), 0, k))


def _layer_spec(l, shape, **kwargs):
    zeros = (0,) * len(shape)
    return pl.BlockSpec((None,) + tuple(shape), lambda *idx: (l,) + zeros, **kwargs)


def _z_spec(sec_idx, rows, width, flat, col_of=lambda *idx: 0, row_of=None):
    if row_of is None:
        row_of = (lambda *idx: idx[0]) if flat else (lambda *idx: 0)
    group_of = (lambda *idx: 0) if flat else (lambda *idx: idx[0])
    return pl.BlockSpec((1, None, rows, width),
                        lambda *idx: (group_of(*idx), sec_idx, row_of(*idx), col_of(*idx)))


def _split_bf16(a):
    hi = a.astype(BF16)
    return hi, (a - hi.astype(F32)).astype(BF16)


def _mod_kernel(c_ref, w_ref, b_ref, o_ref):
    cv = c_ref[...]
    c_hi, c_lo = _split_bf16(cv * jax.nn.sigmoid(cv))
    w_hi, w_lo = _split_bf16(w_ref[0])
    o_ref[0] = _dot(c_hi, w_hi) + _dot(c_hi, w_lo) + _dot(c_lo, w_hi) + b_ref[0]


def _modulation(cc, w_mod, b_mod):
    rows = cc.shape[0]
    return pl.pallas_call(
        _mod_kernel,
        grid=(DEPTH, 3),
        in_specs=[
            pl.BlockSpec((rows, D_MODEL), lambda l, j: (0, 0)),
            pl.BlockSpec((1, D_MODEL, D_MODEL), lambda l, j: (l, 0, j)),
            pl.BlockSpec((1, 1, D_MODEL), lambda l, j: (l, 0, j)),
        ],
        out_specs=pl.BlockSpec((1, rows, D_MODEL), lambda l, j: (l, 0, j)),
        out_shape=jax.ShapeDtypeStruct((DEPTH, rows, 3 * D_MODEL), F32),
        compiler_params=pltpu.CompilerParams(vmem_limit_bytes=VMEM_LIMIT),
        name="modulation",
    )(cc, w_mod, b_mod.reshape(DEPTH, 1, 3 * D_MODEL))


def _rope_ret(a, c, s):
    outs = []
    for h in range(HEADS):
        ah = a[:, h * LANES:(h + 1) * LANES]
        outs.append(ah * c + pltpu.roll(ah, LANES // 2, 1) * s)
    return jnp.concatenate(outs, axis=1)


def _rope_diff(a, c, s_lo, s_hi):
    outs = []
    for h in range(HEADS):
        ah = a[:, h * LANES:(h + 1) * LANES]
        outs.append(ah * c + pltpu.roll(ah, LANES - DIFF_DQK // 2, 1) * s_lo
                    + pltpu.roll(ah, DIFF_DQK // 2, 1) * s_hi)
    return jnp.concatenate(outs, axis=1)


def _proj_kernel(*refs, layer, kinds, src_of, n_src, rope, n_rows, sub):
    if rope:
        (x_ref, g_ref, shift_ref, scale_ref, w_hbm, tab_hbm, z_hbm,
         h_ref, stage, w_vmem, sems_out, sems_w, tab_vmem, sem_tab) = refs
        cr_ref, sr_ref, cd_ref, sdl_ref, sdh_ref = [tab_vmem.at[i] for i in range(5)]
    else:
        (x_ref, g_ref, shift_ref, scale_ref, w_hbm, z_hbm,
         h_ref, stage, w_vmem, sems_out, sems_w) = refs
    b = pl.program_id(0)
    first = b == 0
    n_sub = n_rows // sub
    n_sec = len(kinds)
    n_slots = stage.shape[0]

    def weight_copy(s):
        cols = pl.ds(s * SEC if isinstance(s, int) else pl.multiple_of(s * SEC, SEC), SEC)
        return pltpu.make_async_copy(w_hbm.at[layer, :, cols], w_vmem.at[s], sems_w.at[s])

    def table_copy():
        return pltpu.make_async_copy(tab_hbm, tab_vmem, sem_tab.at[0])

    @pl.when(first)
    def _():
        for s in range(n_src):
            weight_copy(s).start()
        if rope:
            table_copy().start()

    gs = g_ref[...] * (1.0 + scale_ref[...])
    sh = shift_ref[...]
    for r in range(n_sub):
        xs = x_ref[0, r * sub:(r + 1) * sub, :]
        h_ref[r * sub:(r + 1) * sub, :] = (_rms(xs) * gs + sh).astype(BF16)

    if rope:
        pl.when(first)(lambda: table_copy().wait())

    def epi_rk(a, rows):
        if rope:
            a = _rope_ret(a, cr_ref[rows, :], sr_ref[rows, :])
        return a * RET_SCALE

    def epi_rq(a, rows):
        return _rope_ret(a, cr_ref[rows, :], sr_ref[rows, :]) if rope else a

    def epi_dk(a, rows):
        return _rope_diff(a, cd_ref[rows, :], sdl_ref[rows, :], sdh_ref[rows, :]) if rope else a

    def epi_dq(a, rows):
        return epi_dk(a, rows) * DIFF_Q_SCALE

    def epi_ln(a, rows):
        mu = jnp.mean(a, axis=-1, keepdims=True)
        d = a - mu
        return d * lax.rsqrt(jnp.mean(d * d, axis=-1, keepdims=True) + EPS)

    epilogues = dict(
        plain=lambda a, rows: a,
        sigmoid=lambda a, rows: jax.nn.sigmoid(a),
        silu=lambda a, rows: a * jax.nn.sigmoid(a),
        rk=epi_rk, rq=epi_rq, dk=epi_dk, dq=epi_dq, ln=epi_ln)

    def writeback(j):
        slot = j % n_slots
        return pltpu.make_async_copy(stage.at[slot], z_hbm.at[b, j], sems_out.at[slot])

    def run(epilogue, j):
        w_sec = w_vmem.at[src_of(j)]
        out = stage.at[j % n_slots]
        for r in range(n_sub):
            rows = slice(r * sub, (r + 1) * sub)
            out[rows, :] = epilogue(_dot(h_ref[rows, :], w_sec[...]), rows).astype(BF16)

    def section(j, carry):
        pl.when(first)(lambda: weight_copy(src_of(j)).wait())

        @pl.when(j >= n_slots)
        def _():
            writeback(j - n_slots).wait()

        for kind in sorted(set(kinds)):
            idx = [i for i, k in enumerate(kinds) if k == kind]
            cond = j == idx[0]
            for i in idx[1:]:
                cond = cond | (j == i)
            pl.when(cond)(functools.partial(run, epilogues[kind], j))
        writeback(j).start()
        return carry

    lax.fori_loop(0, n_sec, section, 0)
    for j in range(max(0, n_sec - n_slots), n_sec):
        writeback(j).wait()


def _projection(xs, l, g_pre, mod4, row_of, w_all, src_of, kinds, tables):
    n_g, n_rows, _ = xs.shape
    n_sec = len(kinds)
    rope = tables is not None
    n_src = w_all.shape[2] // SEC if src_of is _full_src else n_sec
    in_specs = [
        pl.BlockSpec((1, n_rows, D_MODEL), lambda b: (b, 0, 0)),
        _layer_spec(l, (1, D_MODEL)),
        _mod_spec(l, row_of, 0),
        _mod_spec(l, row_of, 1),
        pl.BlockSpec(memory_space=pl.ANY),
    ]
    args = [xs, g_pre, mod4, mod4, w_all]
    scratch = [pltpu.VMEM((n_rows, D_MODEL), BF16),
               pltpu.VMEM((2, n_rows, SEC), BF16),
               pltpu.VMEM((n_src, D_MODEL, SEC), BF16),
               pltpu.SemaphoreType.DMA((2,)),
               pltpu.SemaphoreType.DMA((n_src,))]
    if rope:
        in_specs.append(pl.BlockSpec(memory_space=pl.ANY))
        args.append(tables)
        scratch += [pltpu.VMEM(tables.shape, F32), pltpu.SemaphoreType.DMA((1,))]
    return pl.pallas_call(
        functools.partial(_proj_kernel, layer=l, kinds=kinds, src_of=src_of, n_src=n_src,
                          rope=rope, n_rows=n_rows, sub=PROJ_SUB),
        grid=(n_g,),
        in_specs=in_specs,
        out_specs=pl.BlockSpec(memory_space=pl.ANY),
        out_shape=jax.ShapeDtypeStruct((n_g, n_sec, n_rows, SEC), BF16),
        scratch_shapes=scratch,
        compiler_params=pltpu.CompilerParams(vmem_limit_bytes=VMEM_LIMIT),
        name="projection_rope" if rope else "projection",
    )(*args)


def _ret_kernel(*refs, layer, n, n_ctx, rc, heads):
    if n_ctx:
        lg_ref, q_ref, g_ref, k_ref, v_ref, kc_ref, vc_ref, o_ref, inc_ref, st_ref = refs
    else:
        lg_ref, q_ref, g_ref, k_ref, v_ref, o_ref, inc_ref, st_ref = refs
    nc = n // rc
    hs = range(heads)
    cols = [slice(hh * LANES, (hh + 1) * LANES) for hh in hs]
    lgf = [lg_ref[layer, 0, pl.program_id(1) * heads + hh] for hh in hs]
    lgb = [lg_ref[layer, 1, pl.program_id(1) * heads + hh] for hh in hs]
    pos = lax.broadcasted_iota(jnp.int32, (rc, LANES), 0).astype(F32)
    kdf = [jnp.exp(lgf[hh] * (rc - 1.0 - pos)).astype(BF16) for hh in hs]
    kdb = [jnp.exp(lgb[hh] * pos).astype(BF16) for hh in hs]
    cdf = [jnp.exp(jnp.full((1, LANES), lgf[hh] * rc, F32)) for hh in hs]
    cdb = [jnp.exp(jnp.full((1, LANES), lgb[hh] * rc, F32)) for hh in hs]

    def increment(kr, vr, nn, hh):
        rows = slice(nn * rc, (nn + 1) * rc)
        kk = kr[0, rows, cols[hh]]
        kcat = jnp.concatenate([kk * kdf[hh], kk * kdb[hh]], axis=1)
        vt = vr[0, rows, cols[hh]].astype(F32).T.astype(BF16)
        return _dot(vt, kcat)

    sf = [jnp.zeros((HEAD_W, HEAD_W), F32) for _ in hs]
    sb = [jnp.zeros((HEAD_W, HEAD_W), F32) for _ in hs]
    if n_ctx:
        incs = [[increment(kc_ref, vc_ref, nn, hh) for hh in hs] for nn in range(n_ctx // rc)]
        for inc in incs:
            sf = [cdf[hh] * sf[hh] + inc[hh][:, :HEAD_W] for hh in hs]
        for inc in reversed(incs):
            sb = [cdb[hh] * sb[hh] + inc[hh][:, HEAD_W:] for hh in hs]

    for nn in range(nc):
        for hh in hs:
            inc_ref[hh, nn] = increment(k_ref, v_ref, nn, hh)
    for nn in range(nc):
        for hh in hs:
            st_ref[hh, nn, :, 0:HEAD_W] = sf[hh].astype(BF16)
            sf[hh] = cdf[hh] * sf[hh] + inc_ref[hh, nn, :, 0:HEAD_W]
    for nn in reversed(range(nc)):
        for hh in hs:
            st_ref[hh, nn, :, HEAD_W:] = sb[hh].astype(BF16)
            sb[hh] = cdb[hh] * sb[hh] + inc_ref[hh, nn, :, HEAD_W:]

    ii = lax.broadcasted_iota(jnp.int32, (rc, rc), 0)
    jj = lax.broadcasted_iota(jnp.int32, (rc, rc), 1)
    dist = (ii - jj).astype(F32)
    dmat = [jnp.where(dist >= 0, jnp.exp(lgf[hh] * jnp.maximum(dist, 0.0)), 0.0)
            + jnp.where(dist <= 0, jnp.exp(lgb[hh] * jnp.maximum(-dist, 0.0)), 0.0) for hh in hs]
    qdf = [jnp.exp(lgf[hh] * (pos + 1.0)).astype(BF16) for hh in hs]
    qdb = [jnp.exp(lgb[hh] * (rc - pos)).astype(BF16) for hh in hs]
    for nn in range(nc):
        rows = slice(nn * rc, (nn + 1) * rc)
        for hh in hs:
            cs = cols[hh]
            q = q_ref[0, rows, cs]
            s = _dot_nt(q, k_ref[0, rows, cs]) * dmat[hh]
            inner = _dot(s.astype(BF16), v_ref[0, rows, cs])
            qcat = jnp.concatenate([q * qdf[hh], q * qdb[hh]], axis=1)
            out = inner + _dot_nt(qcat, st_ref[hh, nn])
            o_ref[0, rows, cs] = _rms(out).astype(BF16) * g_ref[0, rows, cs]


def _retention(log_gamma, layer, z, sec, flat, n_b, n, zc, sec_c, n_ctx, *, rc, heads):
    width = heads * LANES
    head_block = lambda b, h: h

    in_specs = [pl.BlockSpec(memory_space=pltpu.SMEM)]
    in_specs += [_z_spec(sec[name], n, width, flat, col_of=head_block)
                 for name in ("rq", "rg", "rk", "rv")]
    args = [log_gamma, z, z, z, z]
    if n_ctx:
        in_specs += [_z_spec(sec_c[name], n_ctx, width, True, col_of=head_block)
                     for name in ("rk", "rv")]
        args += [zc, zc]
    return pl.pallas_call(
        functools.partial(_ret_kernel, layer=layer, n=n, n_ctx=n_ctx, rc=rc, heads=heads),
        grid=(n_b, HEADS // heads),
        in_specs=in_specs,
        out_specs=pl.BlockSpec((1, n, width), lambda b, h: (b, 0, h)),
        out_shape=jax.ShapeDtypeStruct((n_b, n, BRANCH_W), BF16),
        scratch_shapes=[pltpu.VMEM((heads, n // rc, HEAD_W, 2 * HEAD_W), F32),
                        pltpu.VMEM((heads, n // rc, HEAD_W, 2 * HEAD_W), BF16)],
        compiler_params=pltpu.CompilerParams(vmem_limit_bytes=VMEM_LIMIT),
        name="retention_ctx_init" if n_ctx else "retention",
    )(*args)


def _split_maps(q, lane):
    zero = jnp.zeros_like(q)
    return jnp.concatenate([jnp.where(lane < DIFF_DQK, q, zero),
                            jnp.where(lane >= DIFF_DQK, q, zero)], axis=0)


def _diff_ctx_kernel(lam_ref, q_ref, g_ref, k_ref, v_ref, o_ref, *, layer, tq, post_scale):
    lane = lax.broadcasted_iota(jnp.int32, (tq, LANES), 1)
    for h in range(HEADS):
        cs = slice(h * LANES, (h + 1) * LANES)
        s = _dot_nt(_split_maps(q_ref[0, :, cs], lane), k_ref[0, :, cs])
        p = jnp.exp2(s - jnp.max(s, axis=-1, keepdims=True))
        o = _dot(p.astype(BF16), v_ref[0, :, cs]) / jnp.sum(p, axis=-1, keepdims=True)
        d = o[:tq] - lam_ref[layer] * o[tq:]
        o_ref[0, :, cs] = (_rms(d) * post_scale * g_ref[0, :, cs].astype(F32)).astype(BF16)


def _diff_attention_ctx(lam, layer, zc, sec, n_b, n, *, post_scale):
    return pl.pallas_call(
        functools.partial(_diff_ctx_kernel, layer=layer, tq=n, post_scale=post_scale),
        grid=(n_b,),
        in_specs=[pl.BlockSpec(memory_space=pltpu.SMEM)]
        + [_z_spec(sec[name], n, SEC, True) for name in ("dq", "dg", "dk", "dv")],
        out_specs=pl.BlockSpec((1, n, BRANCH_W), lambda b: (b, 0, 0)),
        out_shape=jax.ShapeDtypeStruct((n_b, n, BRANCH_W), BF16),
        compiler_params=pltpu.CompilerParams(vmem_limit_bytes=VMEM_LIMIT),
        name="diff_attention_ctx",
    )(lam, zc, zc, zc, zc)


def _diff_pipe_kernel(lam_ref, q_ref, g_ref, kc_ref, vc_ref, kl_ref, vl_ref, o_ref,
                      kall, vall, s0, s1, p0, p1, *, layer, n_sub, ts, post_scale):
    n_ctx = kc_ref.shape[1]
    nk = kall.shape[0]
    kall[0:n_ctx, :] = kc_ref[0]
    kall[n_ctx:, :] = kl_ref[0]
    vall[0:n_ctx, 0:LANES] = vc_ref[0]
    vall[n_ctx:, 0:LANES] = vl_ref[0]
    vall[:, LANES:] = jnp.ones((nk, LANES), BF16)
    lam = lam_ref[layer]
    lane = lax.broadcasted_iota(jnp.int32, (ts, LANES), 1)
    s_bufs, p_bufs = (s0, s1), (p0, p1)

    def stage_a(t):
        rows = slice(t * ts, (t + 1) * ts)
        s_bufs[t % 2][...] = _dot_nt(_split_maps(q_ref[0, rows, :], lane), kall[...])

    def stage_b(t):
        s = s_bufs[t % 2][...]
        p_bufs[t % 2][...] = jnp.exp2(s - jnp.max(s, axis=-1, keepdims=True)).astype(BF16)

    def stage_c(t):
        rows = slice(t * ts, (t + 1) * ts)
        oe = _dot(p_bufs[t % 2][...], vall[...])
        o = oe[:, :LANES] / oe[:, LANES:]
        d = o[:ts] - lam * o[ts:]
        o_ref[0, rows, :] = (_rms(d) * post_scale * g_ref[0, rows, :].astype(F32)).astype(BF16)

    for t in range(n_sub + 2):
        if t >= 2:
            stage_c(t - 2)
        if 1 <= t <= n_sub:
            stage_b(t - 1)
        if t < n_sub:
            stage_a(t)


def _diff_attention_pipelined(lam, layer, z, sec, n_b, n, zc, sec_c, n_ctx, *, ts, post_scale):
    nk = n_ctx + n
    head_block = lambda b, h: h
    lat = lambda name: _z_spec(sec[name], n, LANES, False, col_of=head_block)
    cx = lambda name: _z_spec(sec_c[name], n_ctx, LANES, True, col_of=head_block)
    return pl.pallas_call(
        functools.partial(_diff_pipe_kernel, layer=layer, n_sub=n // ts, ts=ts,
                          post_scale=post_scale),
        grid=(n_b, HEADS),
        in_specs=[pl.BlockSpec(memory_space=pltpu.SMEM),
                  lat("dq"), lat("dg"), cx("dk"), cx("dv"), lat("dk"), lat("dv")],
        out_specs=pl.BlockSpec((1, n, LANES), lambda b, h: (b, 0, h)),
        out_shape=jax.ShapeDtypeStruct((n_b, n, BRANCH_W), BF16),
        scratch_shapes=[pltpu.VMEM((nk, LANES), BF16), pltpu.VMEM((nk, 2 * LANES), BF16),
                        pltpu.VMEM((2 * ts, nk), F32), pltpu.VMEM((2 * ts, nk), F32),
                        pltpu.VMEM((2 * ts, nk), BF16), pltpu.VMEM((2 * ts, nk), BF16)],
        compiler_params=pltpu.CompilerParams(vmem_limit_bytes=VMEM_LIMIT),
        name="diff_attention_pipelined",
    )(lam, z, z, zc, zc, z, z)


def _merge_kernel(x_ref, gate_ref, gpost_ref, oret_ref, odiff_ref, u_ref, vn_ref, mg_ref,
                  gr_ref, gm_ref, gd_ref, ws_ref, bs_ref, wbo_ref, wo_ref, out_ref, *, tm):
    rows = []
    for c in range(tm // CHUNK):
        cols = []
        for g in range(HEADS):
            blk = vn_ref[0, c * CHUNK:(c + 1) * CHUNK, g * LANES:(g + 1) * LANES]
            cols.append(_dot(ws_ref[g], blk) + bs_ref[g])
        rows.append(jnp.concatenate(cols, axis=1))
    sp = jnp.concatenate(rows, axis=0)
    o_mlp = (u_ref[0].astype(F32) * sp * mg_ref[0].astype(F32)).astype(BF16)

    def gated(gate2_ref, o, w):
        gate = jnp.concatenate([gate2_ref[0, 0], gate2_ref[0, 1]], axis=1).astype(F32)
        return gate * _dot(o, w)

    t = (gated(gr_ref, oret_ref[0], wbo_ref[0]) + gated(gm_ref, o_mlp, wbo_ref[1])
         + gated(gd_ref, odiff_ref[0], wbo_ref[2]))
    y = _dot(t.astype(BF16), wo_ref[...])
    out_ref[0] = x_ref[0] + gate_ref[...] * (_rms(y) * gpost_ref[...])


def _merge(xs, l, mod4, row_of, g_post, o_ret, o_diff, z, sec, ws, bs, wbo, wo, *, tm):
    n_b, n, _ = xs.shape
    row_block = lambda b, i: i

    def zsec(name):
        return _z_spec(sec[name], tm, SEC, False, row_of=row_block)

    def zgate(k):
        return pl.BlockSpec((1, 2, tm, SEC), lambda b, i: (b, k, i, 0))

    in_specs = [
        pl.BlockSpec((1, tm, D_MODEL), lambda b, i: (b, i, 0)),
        _mod_spec(l, row_of, 2),
        _layer_spec(l, (1, D_MODEL)),
        pl.BlockSpec((1, tm, BRANCH_W), lambda b, i: (b, i, 0)),
        pl.BlockSpec((1, tm, BRANCH_W), lambda b, i: (b, i, 0)),
        zsec("mu"), zsec("mv"), zsec("mg"), zgate(0), zgate(1), zgate(2),
        _layer_spec(l, (HEADS, CHUNK, CHUNK)),
        _layer_spec(l, (HEADS, CHUNK, LANES)),
        _layer_spec(l, (N_BRANCH, BRANCH_W, D_MODEL)),
        _layer_spec(l, (D_MODEL, D_MODEL)),
    ]
    return pl.pallas_call(
        functools.partial(_merge_kernel, tm=tm),
        grid=(n_b, n // tm),
        in_specs=in_specs,
        out_specs=pl.BlockSpec((1, tm, D_MODEL), lambda b, i: (b, i, 0)),
        out_shape=jax.ShapeDtypeStruct(xs.shape, F32),
        compiler_params=pltpu.CompilerParams(vmem_limit_bytes=VMEM_LIMIT),
        name="merge",
    )(xs, mod4, g_post, o_ret, o_diff, z, z, z, z, z, z, ws, bs, wbo, wo)


def _rope_tables(n_lat):
    rows = n_lat // GRID_W
    row_pos = jnp.repeat(jnp.arange(rows, dtype=F32), GRID_W)
    col_pos = jnp.tile(jnp.arange(GRID_W, dtype=F32), rows)

    def angles(head_dim):
        n_freq = head_dim // 4
        inv = ROPE_BASE ** (-jnp.arange(n_freq, dtype=F32) / n_freq)
        ang = jnp.concatenate([row_pos[:, None] * inv, col_pos[:, None] * inv], axis=-1)
        return jnp.cos(ang), jnp.sin(ang)

    cos_r, sin_r = angles(HEAD_W)
    cos_d, sin_d = angles(DIFF_DQK)
    zeros = jnp.zeros_like(sin_d)
    c_r = jnp.concatenate([cos_r, cos_r], axis=-1)
    s_r = jnp.concatenate([-sin_r, sin_r], axis=-1)
    c_d = jnp.tile(jnp.concatenate([cos_d, cos_d], axis=-1), (1, 2))
    s_lo = jnp.tile(jnp.concatenate([-sin_d, zeros], axis=-1), (1, 2))
    s_hi = jnp.tile(jnp.concatenate([zeros, sin_d], axis=-1), (1, 2))
    return c_r, s_r, c_d, s_lo, s_hi


@jax.jit
def kernel(x, c, ctx, c_ctx, w_mod, b_mod, g_pre, g_post, w_in, ret_decay_logit, mlp_w_s,
           mlp_b_s, diff_lambda_q, diff_lambda_k, w_branch_out, w_out):
    n_b, n_lat, _ = x.shape
    n_ctx = ctx.shape[1]
    tables = jnp.stack(_rope_tables(n_lat))

    cond_rows = 16
    ctx_row = n_b
    cc = jnp.zeros((cond_rows, D_MODEL), F32).at[:n_b].set(c).at[ctx_row].set(c_ctx)
    mod4 = _modulation(cc, w_mod, b_mod).reshape(DEPTH, cond_rows, 1, 3 * D_MODEL)
    lat_row = lambda b: b
    ctx_row_of = lambda b: ctx_row

    log_gamma = -jax.nn.softplus(-ret_decay_logit.astype(F32))
    lam_inits = [0.8 - 0.6 * math.exp(-0.3 * l) for l in range(DEPTH)]
    lam = (jnp.exp(jnp.sum(diff_lambda_q[:, 0] * diff_lambda_k[:, 0], axis=-1))
           - jnp.exp(jnp.sum(diff_lambda_q[:, 1] * diff_lambda_k[:, 1], axis=-1))
           + jnp.asarray(lam_inits, F32)).astype(F32)

    w_sec = w_in.astype(BF16)
    ws = mlp_w_s.astype(BF16)
    bs = jnp.broadcast_to(mlp_b_s[..., None], (DEPTH, HEADS, CHUNK, LANES)).astype(F32)
    wbo = w_branch_out.astype(BF16)
    wo = w_out.astype(BF16)
    g_pre3 = g_pre.reshape(DEPTH, 1, D_MODEL)
    g_post3 = g_post.reshape(DEPTH, 1, D_MODEL)

    for l in range(DEPTH):
        last = l == DEPTH - 1
        post_scale = 1.0 - lam_inits[l]

        ctx_flat = ctx.reshape(1, n_b * n_ctx, D_MODEL)
        if last:
            zc = _projection(ctx_flat, l, g_pre3, mod4, ctx_row_of, w_sec, lambda j: j,
                             KV_KINDS, None)
            sec_c = KV_SEC
        else:
            zc = _projection(ctx_flat, l, g_pre3, mod4, ctx_row_of, w_sec, _full_src,
                             FULL_KINDS, None)
            sec_c = FULL_SEC
            co_ret = _retention(log_gamma, l, zc, sec_c, True, n_b, n_ctx, None, None, 0,
                                rc=RET_CHUNK, heads=HEADS)
            co_diff = _diff_attention_ctx(lam, l, zc, sec_c, n_b, n_ctx, post_scale=post_scale)
            ctx_next = _merge(ctx_flat, l, mod4, ctx_row_of, g_post3,
                              co_ret.reshape(1, n_b * n_ctx, BRANCH_W),
                              co_diff.reshape(1, n_b * n_ctx, BRANCH_W), zc, sec_c,
                              ws, bs, wbo, wo, tm=MERGE_TM)
            ctx_next = ctx_next.reshape(n_b, n_ctx, D_MODEL)

        z = _projection(x, l, g_pre3, mod4, lat_row, w_sec, _full_src, FULL_KINDS, tables)
        o_ret = _retention(log_gamma, l, z, FULL_SEC, False, n_b, n_lat, zc, sec_c, n_ctx,
                           rc=RET_CHUNK, heads=RET_HEADS)
        o_diff = _diff_attention_pipelined(lam, l, z, FULL_SEC, n_b, n_lat, zc, sec_c, n_ctx,
                                           ts=DIFF_TS, post_scale=post_scale)
        x = _merge(x, l, mod4, lat_row, g_post3, o_ret, o_diff, z, FULL_SEC, ws, bs, wbo, wo,
                   tm=MERGE_TM)
        if not last:
            ctx = ctx_next
    return x
```

```python
import functools
import math

import jax
import jax.numpy as jnp
from jax import lax
from jax.experimental import pallas as pl
from jax.experimental.pallas import tpu as pltpu

F32 = jnp.float32
BF16 = jnp.bfloat16

D_MODEL = 1024
DEPTH = 2
GRID_W = 64
N_BRANCH = 3
BRANCH_W = D_MODEL // 2
HEADS = 4
HEAD_W = BRANCH_W // HEADS
DIFF_DQK = HEAD_W // 2
CHUNK = 128
ROPE_BASE = 10000.0
EPS = 1e-6
RET_SCALE = HEAD_W ** -0.5
DIFF_Q_SCALE = (DIFF_DQK ** -0.5) * math.log2(math.e)

LANES = 128
SEC = BRANCH_W
MERGE_COLS = N_BRANCH * D_MODEL
KV_COLS = 4 * BRANCH_W
IN_COLS = KV_COLS + 7 * BRANCH_W + MERGE_COLS
VMEM_LIMIT = 56 * 1024 * 1024

PROJ_SUB = 128
DIFF_TS = 256
RET_CHUNK = 128
RET_HEADS = 4
MERGE_TM = 512

FULL_KINDS = ("sigmoid",) * 6 + ("rk", "plain", "dk", "plain", "rq", "silu", "dq", "silu",
                                 "plain", "ln", "silu")
FULL_SEC = dict(rk=6, rv=7, dk=8, dv=9, rq=10, rg=11, dq=12, dg=13, mu=14, mv=15, mg=16)
KV_KINDS = ("rk", "plain", "dk", "plain")
KV_SEC = dict(rk=0, rv=1, dk=2, dv=3)
_N_MERGE_SEC = MERGE_COLS // SEC
_N_REST_SEC = (KV_COLS + 7 * BRANCH_W) // SEC


def _full_src(j):
    return jnp.where(j < _N_MERGE_SEC, j + _N_REST_SEC, j - _N_MERGE_SEC)


def _dot(a, b):
    return jnp.dot(a, b, preferred_element_type=F32)


def _dot_nt(a, b):
    return lax.dot_general(a, b, (((1,), (1,)), ((), ())), preferred_element_type=F32)


def _rms(v):
    return v * lax.rsqrt(jnp.mean(v * v, axis=-1, keepdims=True) + EPS)


def _mod_spec(l, row_of, k):
    return pl.BlockSpec((None, None, 1, D_MODEL), lambda *idx: (l, row_of(idx[0]), 0, k))


def _layer_spec(l, shape, **kwargs):
    zeros = (0,) * len(shape)
    return pl.BlockSpec((None,) + tuple(shape), lambda *idx: (l,) + zeros, **kwargs)


def _z_spec(sec_idx, rows, width, flat, col_of=lambda *idx: 0, row_of=None):
    if row_of is None:
        row_of = (lambda *idx: idx[0]) if flat else (lambda *idx: 0)
    group_of = (lambda *idx: 0) if flat else (lambda *idx: idx[0])
    return pl.BlockSpec((1, None, rows, width),
                        lambda *idx: (group_of(*idx), sec_idx, row_of(*idx), col_of(*idx)))


def _split_bf16(a):
    hi = a.astype(BF16)
    return hi, (a - hi.astype(F32)).astype(BF16)


def _mod_kernel(c_ref, w_ref, b_ref, o_ref):
    cv = c_ref[...]
    c_hi, c_lo = _split_bf16(cv * jax.nn.sigmoid(cv))
    w_hi, w_lo = _split_bf16(w_ref[0])
    o_ref[0] = _dot(c_hi, w_hi) + _dot(c_hi, w_lo) + _dot(c_lo, w_hi) + b_ref[0]


def _modulation(cc, w_mod, b_mod):
    rows = cc.shape[0]
    return pl.pallas_call(
        _mod_kernel,
        grid=(DEPTH, 3),
        in_specs=[
            pl.BlockSpec((rows, D_MODEL), lambda l, j: (0, 0)),
            pl.BlockSpec((1, D_MODEL, D_MODEL), lambda l, j: (l, 0, j)),
            pl.BlockSpec((1, 1, D_MODEL), lambda l, j: (l, 0, j)),
        ],
        out_specs=pl.BlockSpec((1, rows, D_MODEL), lambda l, j: (l, 0, j)),
        out_shape=jax.ShapeDtypeStruct((DEPTH, rows, 3 * D_MODEL), F32),
        compiler_params=pltpu.CompilerParams(vmem_limit_bytes=VMEM_LIMIT),
        name="modulation",
    )(cc, w_mod, b_mod.reshape(DEPTH, 1, 3 * D_MODEL))


def _rope_ret(a, c, s):
    outs = []
    for h in range(HEADS):
        ah = a[:, h * LANES:(h + 1) * LANES]
        outs.append(ah * c + pltpu.roll(ah, LANES // 2, 1) * s)
    return jnp.concatenate(outs, axis=1)


def _rope_diff(a, c, s_lo, s_hi):
    outs = []
    for h in range(HEADS):
        ah = a[:, h * LANES:(h + 1) * LANES]
        outs.append(ah * c + pltpu.roll(ah, LANES - DIFF_DQK // 2, 1) * s_lo
                    + pltpu.roll(ah, DIFF_DQK // 2, 1) * s_hi)
    return jnp.concatenate(outs, axis=1)


def _proj_kernel(*refs, layer, kinds, src_of, n_src, rope, n_rows, sub):
    if rope:
        (x_ref, g_ref, shift_ref, scale_ref, w_hbm, tab_hbm, z_hbm,
         h_ref, stage, w_vmem, w_stage, sems_out, sems_w, tab_vmem, sem_tab) = refs
        cr_ref, sr_ref, cd_ref, sdl_ref, sdh_ref = [tab_vmem.at[i] for i in range(5)]
    else:
        (x_ref, g_ref, shift_ref, scale_ref, w_hbm, z_hbm,
         h_ref, stage, w_vmem, w_stage, sems_out, sems_w) = refs
    b = pl.program_id(0)
    first = b == 0
    n_sub = n_rows // sub
    n_sec = len(kinds)
    n_slots = stage.shape[0]

    n_wslots = w_stage.shape[0]

    def weight_copy(j):
        start = src_of(j) * SEC
        cols = pl.ds(start if isinstance(start, int) else pl.multiple_of(start, SEC), SEC)
        slot = j % n_wslots
        return pltpu.make_async_copy(w_hbm.at[layer, :, cols], w_stage.at[slot], sems_w.at[slot])

    def load_weights(j):
        weight_copy(j).wait()
        src = w_stage.at[j % n_wslots]
        dst = w_vmem.at[src_of(j)]
        for r in range(D_MODEL // LANES):
            rows = slice(r * LANES, (r + 1) * LANES)
            dst[rows, :] = src[rows, :].astype(BF16)

        @pl.when(j + n_wslots < n_sec)
        def _():
            weight_copy(j + n_wslots).start()

    def table_copy():
        return pltpu.make_async_copy(tab_hbm, tab_vmem, sem_tab.at[0])

    @pl.when(first)
    def _():
        for j in range(min(n_wslots, n_sec)):
            weight_copy(j).start()
        if rope:
            table_copy().start()

    gs = g_ref[...] * (1.0 + scale_ref[...])
    sh = shift_ref[...]
    for r in range(n_sub):
        xs = x_ref[0, r * sub:(r + 1) * sub, :]
        h_ref[r * sub:(r + 1) * sub, :] = (_rms(xs) * gs + sh).astype(BF16)

    if rope:
        pl.when(first)(lambda: table_copy().wait())

    def epi_rk(a, rows):
        if rope:
            a = _rope_ret(a, cr_ref[rows, :], sr_ref[rows, :])
        return a * RET_SCALE

    def epi_rq(a, rows):
        return _rope_ret(a, cr_ref[rows, :], sr_ref[rows, :]) if rope else a

    def epi_dk(a, rows):
        return _rope_diff(a, cd_ref[rows, :], sdl_ref[rows, :], sdh_ref[rows, :]) if rope else a

    def epi_dq(a, rows):
        return epi_dk(a, rows) * DIFF_Q_SCALE

    def epi_ln(a, rows):
        mu = jnp.mean(a, axis=-1, keepdims=True)
        d = a - mu
        return d * lax.rsqrt(jnp.mean(d * d, axis=-1, keepdims=True) + EPS)

    epilogues = dict(
        plain=lambda a, rows: a,
        sigmoid=lambda a, rows: jax.nn.sigmoid(a),
        silu=lambda a, rows: a * jax.nn.sigmoid(a),
        rk=epi_rk, rq=epi_rq, dk=epi_dk, dq=epi_dq, ln=epi_ln)

    def writeback(j):
        slot = j % n_slots
        return pltpu.make_async_copy(stage.at[slot], z_hbm.at[b, j], sems_out.at[slot])

    def run(epilogue, j):
        w_sec = w_vmem.at[src_of(j)]
        out = stage.at[j % n_slots]
        for r in range(n_sub):
            rows = slice(r * sub, (r + 1) * sub)
            out[rows, :] = epilogue(_dot(h_ref[rows, :], w_sec[...]), rows).astype(BF16)

    def section(j, carry):
        pl.when(first)(functools.partial(load_weights, j))

        @pl.when(j >= n_slots)
        def _():
            writeback(j - n_slots).wait()

        for kind in sorted(set(kinds)):
            idx = [i for i, k in enumerate(kinds) if k == kind]
            cond = j == idx[0]
            for i in idx[1:]:
                cond = cond | (j == i)
            pl.when(cond)(functools.partial(run, epilogues[kind], j))
        writeback(j).start()
        return carry

    lax.fori_loop(0, n_sec, section, 0)
    for j in range(max(0, n_sec - n_slots), n_sec):
        writeback(j).wait()


def _projection(xs, l, g_pre, mod4, row_of, w_all, src_of, kinds, tables):
    n_g, n_rows, _ = xs.shape
    n_sec = len(kinds)
    rope = tables is not None
    n_src = w_all.shape[2] // SEC if src_of is _full_src else n_sec
    in_specs = [
        pl.BlockSpec((1, n_rows, D_MODEL), lambda b: (b, 0, 0)),
        _layer_spec(l, (1, D_MODEL)),
        _mod_spec(l, row_of, 0),
        _mod_spec(l, row_of, 1),
        pl.BlockSpec(memory_space=pl.ANY),
    ]
    args = [xs, g_pre, mod4, mod4, w_all]
    scratch = [pltpu.VMEM((n_rows, D_MODEL), BF16),
               pltpu.VMEM((2, n_rows, SEC), BF16),
               pltpu.VMEM((n_src, D_MODEL, SEC), BF16),
               pltpu.VMEM((2, D_MODEL, SEC), F32),
               pltpu.SemaphoreType.DMA((2,)),
               pltpu.SemaphoreType.DMA((2,))]
    if rope:
        in_specs.append(pl.BlockSpec(memory_space=pl.ANY))
        args.append(tables)
        scratch += [pltpu.VMEM(tables.shape, F32), pltpu.SemaphoreType.DMA((1,))]
    return pl.pallas_call(
        functools.partial(_proj_kernel, layer=l, kinds=kinds, src_of=src_of, n_src=n_src,
                          rope=rope, n_rows=n_rows, sub=PROJ_SUB),
        grid=(n_g,),
        in_specs=in_specs,
        out_specs=pl.BlockSpec(memory_space=pl.ANY),
        out_shape=jax.ShapeDtypeStruct((n_g, n_sec, n_rows, SEC), BF16),
        scratch_shapes=scratch,
        compiler_params=pltpu.CompilerParams(vmem_limit_bytes=VMEM_LIMIT),
        name="projection_rope" if rope else "projection",
    )(*args)


def _ret_kernel(*refs, layer, n, n_ctx, rc, heads):
    if n_ctx:
        lg_ref, q_ref, g_ref, k_ref, v_ref, kc_ref, vc_ref, o_ref, inc_ref, st_ref = refs
    else:
        lg_ref, q_ref, g_ref, k_ref, v_ref, o_ref, inc_ref, st_ref = refs
    nc = n // rc
    hs = range(heads)
    cols = [slice(hh * LANES, (hh + 1) * LANES) for hh in hs]
    lgf = [lg_ref[layer, 0, pl.program_id(1) * heads + hh] for hh in hs]
    lgb = [lg_ref[layer, 1, pl.program_id(1) * heads + hh] for hh in hs]
    pos = lax.broadcasted_iota(jnp.int32, (rc, LANES), 0).astype(F32)
    kdf = [jnp.exp(lgf[hh] * (rc - 1.0 - pos)).astype(BF16) for hh in hs]
    kdb = [jnp.exp(lgb[hh] * pos).astype(BF16) for hh in hs]
    cdf = [jnp.exp(jnp.full((1, LANES), lgf[hh] * rc, F32)) for hh in hs]
    cdb = [jnp.exp(jnp.full((1, LANES), lgb[hh] * rc, F32)) for hh in hs]

    def increment(kr, vr, nn, hh):
        rows = slice(nn * rc, (nn + 1) * rc)
        kk = kr[0, rows, cols[hh]]
        kcat = jnp.concatenate([kk * kdf[hh], kk * kdb[hh]], axis=1)
        vt = vr[0, rows, cols[hh]].astype(F32).T.astype(BF16)
        return _dot(vt, kcat)

    sf = [jnp.zeros((HEAD_W, HEAD_W), F32) for _ in hs]
    sb = [jnp.zeros((HEAD_W, HEAD_W), F32) for _ in hs]
    if n_ctx:
        incs = [[increment(kc_ref, vc_ref, nn, hh) for hh in hs] for nn in range(n_ctx // rc)]
        for inc in incs:
            sf = [cdf[hh] * sf[hh] + inc[hh][:, :HEAD_W] for hh in hs]
        for inc in reversed(incs):
            sb = [cdb[hh] * sb[hh] + inc[hh][:, HEAD_W:] for hh in hs]

    for nn in range(nc):
        for hh in hs:
            inc_ref[hh, nn] = increment(k_ref, v_ref, nn, hh)
    for nn in range(nc):
        for hh in hs:
            st_ref[hh, nn, :, 0:HEAD_W] = sf[hh].astype(BF16)
            sf[hh] = cdf[hh] * sf[hh] + inc_ref[hh, nn, :, 0:HEAD_W]
    for nn in reversed(range(nc)):
        for hh in hs:
            st_ref[hh, nn, :, HEAD_W:] = sb[hh].astype(BF16)
            sb[hh] = cdb[hh] * sb[hh] + inc_ref[hh, nn, :, HEAD_W:]

    ii = lax.broadcasted_iota(jnp.int32, (rc, rc), 0)
    jj = lax.broadcasted_iota(jnp.int32, (rc, rc), 1)
    dist = (ii - jj).astype(F32)
    dmat = [jnp.where(dist >= 0, jnp.exp(lgf[hh] * jnp.maximum(dist, 0.0)), 0.0)
            + jnp.where(dist <= 0, jnp.exp(lgb[hh] * jnp.maximum(-dist, 0.0)), 0.0) for hh in hs]
    qdf = [jnp.exp(lgf[hh] * (pos + 1.0)).astype(BF16) for hh in hs]
    qdb = [jnp.exp(lgb[hh] * (rc - pos)).astype(BF16) for hh in hs]
    for nn in range(nc):
        rows = slice(nn * rc, (nn + 1) * rc)
        for hh in hs:
            cs = cols[hh]
            q = q_ref[0, rows, cs]
            s = _dot_nt(q, k_ref[0, rows, cs]) * dmat[hh]
            inner = _dot(s.astype(BF16), v_ref[0, rows, cs])
            qcat = jnp.concatenate([q * qdf[hh], q * qdb[hh]], axis=1)
            out = inner + _dot_nt(qcat, st_ref[hh, nn])
            o_ref[0, rows, cs] = _rms(out).astype(BF16) * g_ref[0, rows, cs]


def _retention(log_gamma, layer, z, sec, flat, n_b, n, zc, sec_c, n_ctx, *, rc, heads):
    width = heads * LANES
    head_block = lambda b, h: h

    in_specs = [pl.BlockSpec(memory_space=pltpu.SMEM)]
    in_specs += [_z_spec(sec[name], n, width, flat, col_of=head_block)
                 for name in ("rq", "rg", "rk", "rv")]
    args = [log_gamma, z, z, z, z]
    if n_ctx:
        in_specs += [_z_spec(sec_c[name], n_ctx, width, True, col_of=head_block)
                     for name in ("rk", "rv")]
        args += [zc, zc]
    return pl.pallas_call(
        functools.partial(_ret_kernel, layer=layer, n=n, n_ctx=n_ctx, rc=rc, heads=heads),
        grid=(n_b, HEADS // heads),
        in_specs=in_specs,
        out_specs=pl.BlockSpec((1, n, width), lambda b, h: (b, 0, h)),
        out_shape=jax.ShapeDtypeStruct((n_b, n, BRANCH_W), BF16),
        scratch_shapes=[pltpu.VMEM((heads, n // rc, HEAD_W, 2 * HEAD_W), F32),
                        pltpu.VMEM((heads, n // rc, HEAD_W, 2 * HEAD_W), BF16)],
        compiler_params=pltpu.CompilerParams(vmem_limit_bytes=VMEM_LIMIT),
        name="retention_ctx_init" if n_ctx else "retention",
    )(*args)


def _split_maps(q, lane):
    zero = jnp.zeros_like(q)
    return jnp.concatenate([jnp.where(lane < DIFF_DQK, q, zero),
                            jnp.where(lane >= DIFF_DQK, q, zero)], axis=0)


def _diff_ctx_kernel(lam_ref, q_ref, g_ref, k_ref, v_ref, o_ref, *, layer, tq, post_scale):
    lane = lax.broadcasted_iota(jnp.int32, (tq, LANES), 1)
    for h in range(HEADS):
        cs = slice(h * LANES, (h + 1) * LANES)
        s = _dot_nt(_split_maps(q_ref[0, :, cs], lane), k_ref[0, :, cs])
        p = jnp.exp2(s - jnp.max(s, axis=-1, keepdims=True))
        o = _dot(p.astype(BF16), v_ref[0, :, cs]) / jnp.sum(p, axis=-1, keepdims=True)
        d = o[:tq] - lam_ref[layer] * o[tq:]
        o_ref[0, :, cs] = (_rms(d) * post_scale * g_ref[0, :, cs].astype(F32)).astype(BF16)


def _diff_attention_ctx(lam, layer, zc, sec, n_b, n, *, post_scale):
    return pl.pallas_call(
        functools.partial(_diff_ctx_kernel, layer=layer, tq=n, post_scale=post_scale),
        grid=(n_b,),
        in_specs=[pl.BlockSpec(memory_space=pltpu.SMEM)]
        + [_z_spec(sec[name], n, SEC, True) for name in ("dq", "dg", "dk", "dv")],
        out_specs=pl.BlockSpec((1, n, BRANCH_W), lambda b: (b, 0, 0)),
        out_shape=jax.ShapeDtypeStruct((n_b, n, BRANCH_W), BF16),
        compiler_params=pltpu.CompilerParams(vmem_limit_bytes=VMEM_LIMIT),
        name="diff_attention_ctx",
    )(lam, zc, zc, zc, zc)


def _diff_pipe_kernel(lam_ref, q_ref, g_ref, kc_ref, vc_ref, kl_ref, vl_ref, o_ref,
                      kall, vall, s0, s1, p0, p1, *, layer, n_sub, ts, post_scale):
    n_ctx = kc_ref.shape[1]
    nk = kall.shape[0]
    kall[0:n_ctx, :] = kc_ref[0]
    kall[n_ctx:, :] = kl_ref[0]
    vall[0:n_ctx, 0:LANES] = vc_ref[0]
    vall[n_ctx:, 0:LANES] = vl_ref[0]
    vall[:, LANES:] = jnp.ones((nk, LANES), BF16)
    lam = lam_ref[layer]
    lane = lax.broadcasted_iota(jnp.int32, (ts, LANES), 1)
    s_bufs, p_bufs = (s0, s1), (p0, p1)

    def stage_a(t):
        rows = slice(t * ts, (t + 1) * ts)
        s_bufs[t % 2][...] = _dot_nt(_split_maps(q_ref[0, rows, :], lane), kall[...])

    def stage_b(t):
        s = s_bufs[t % 2][...]
        p_bufs[t % 2][...] = jnp.exp2(s - jnp.max(s, axis=-1, keepdims=True)).astype(BF16)

    def stage_c(t):
        rows = slice(t * ts, (t + 1) * ts)
        oe = _dot(p_bufs[t % 2][...], vall[...])
        o = oe[:, :LANES] / oe[:, LANES:]
        d = o[:ts] - lam * o[ts:]
        o_ref[0, rows, :] = (_rms(d) * post_scale * g_ref[0, rows, :].astype(F32)).astype(BF16)

    for t in range(n_sub + 2):
        if t >= 2:
            stage_c(t - 2)
        if 1 <= t <= n_sub:
            stage_b(t - 1)
        if t < n_sub:
            stage_a(t)


def _diff_attention_pipelined(lam, layer, z, sec, n_b, n, zc, sec_c, n_ctx, *, ts, post_scale):
    nk = n_ctx + n
    head_block = lambda b, h: h
    lat = lambda name: _z_spec(sec[name], n, LANES, False, col_of=head_block)
    cx = lambda name: _z_spec(sec_c[name], n_ctx, LANES, True, col_of=head_block)
    return pl.pallas_call(
        functools.partial(_diff_pipe_kernel, layer=layer, n_sub=n // ts, ts=ts,
                          post_scale=post_scale),
        grid=(n_b, HEADS),
        in_specs=[pl.BlockSpec(memory_space=pltpu.SMEM),
                  lat("dq"), lat("dg"), cx("dk"), cx("dv"), lat("dk"), lat("dv")],
        out_specs=pl.BlockSpec((1, n, LANES), lambda b, h: (b, 0, h)),
        out_shape=jax.ShapeDtypeStruct((n_b, n, BRANCH_W), BF16),
        scratch_shapes=[pltpu.VMEM((nk, LANES), BF16), pltpu.VMEM((nk, 2 * LANES), BF16),
                        pltpu.VMEM((2 * ts, nk), F32), pltpu.VMEM((2 * ts, nk), F32),
                        pltpu.VMEM((2 * ts, nk), BF16), pltpu.VMEM((2 * ts, nk), BF16)],
        compiler_params=pltpu.CompilerParams(vmem_limit_bytes=VMEM_LIMIT),
        name="diff_attention_pipelined",
    )(lam, z, z, zc, zc, z, z)


def _merge_kernel(x_ref, gate_ref, gpost_ref, oret_ref, odiff_ref, u_ref, vn_ref, mg_ref,
                  gr_ref, gm_ref, gd_ref, ws_ref, bs_ref, wbo_ref, wo_ref, out_ref, *, tm):
    rows = []
    for c in range(tm // CHUNK):
        cols = []
        for g in range(HEADS):
            blk = vn_ref[0, c * CHUNK:(c + 1) * CHUNK, g * LANES:(g + 1) * LANES]
            cols.append(_dot(ws_ref[g], blk) + bs_ref[g])
        rows.append(jnp.concatenate(cols, axis=1))
    sp = jnp.concatenate(rows, axis=0)
    o_mlp = (u_ref[0].astype(F32) * sp * mg_ref[0].astype(F32)).astype(BF16)

    def gated(gate2_ref, o, w):
        gate = jnp.concatenate([gate2_ref[0, 0], gate2_ref[0, 1]], axis=1).astype(F32)
        return gate * _dot(o, w)

    t = (gated(gr_ref, oret_ref[0], wbo_ref[0]) + gated(gm_ref, o_mlp, wbo_ref[1])
         + gated(gd_ref, odiff_ref[0], wbo_ref[2]))
    y = _dot(t.astype(BF16), wo_ref[...])
    out_ref[0] = x_ref[0] + gate_ref[...] * (_rms(y) * gpost_ref[...])


def _merge(xs, l, mod4, row_of, g_post, o_ret, o_diff, z, sec, ws, bs, wbo, wo, *, tm):
    n_b, n, _ = xs.shape
    row_block = lambda b, i: i

    def zsec(name):
        return _z_spec(sec[name], tm, SEC, False, row_of=row_block)

    def zgate(k):
        return pl.BlockSpec((1, 2, tm, SEC), lambda b, i: (b, k, i, 0))

    in_specs = [
        pl.BlockSpec((1, tm, D_MODEL), lambda b, i: (b, i, 0)),
        _mod_spec(l, row_of, 2),
        _layer_spec(l, (1, D_MODEL)),
        pl.BlockSpec((1, tm, BRANCH_W), lambda b, i: (b, i, 0)),
        pl.BlockSpec((1, tm, BRANCH_W), lambda b, i: (b, i, 0)),
        zsec("mu"), zsec("mv"), zsec("mg"), zgate(0), zgate(1), zgate(2),
        _layer_spec(l, (HEADS, CHUNK, CHUNK)),
        _layer_spec(l, (HEADS, CHUNK, LANES)),
        _layer_spec(l, (N_BRANCH, BRANCH_W, D_MODEL)),
        _layer_spec(l, (D_MODEL, D_MODEL)),
    ]
    return pl.pallas_call(
        functools.partial(_merge_kernel, tm=tm),
        grid=(n_b, n // tm),
        in_specs=in_specs,
        out_specs=pl.BlockSpec((1, tm, D_MODEL), lambda b, i: (b, i, 0)),
        out_shape=jax.ShapeDtypeStruct(xs.shape, F32),
        compiler_params=pltpu.CompilerParams(vmem_limit_bytes=VMEM_LIMIT),
        name="merge",
    )(xs, mod4, g_post, o_ret, o_diff, z, z, z, z, z, z, ws, bs, wbo, wo)


def _rope_tables(n_lat):
    rows = n_lat // GRID_W
    row_pos = jnp.repeat(jnp.arange(rows, dtype=F32), GRID_W)
    col_pos = jnp.tile(jnp.arange(GRID_W, dtype=F32), rows)

    def angles(head_dim):
        n_freq = head_dim // 4
        inv = ROPE_BASE ** (-jnp.arange(n_freq, dtype=F32) / n_freq)
        ang = jnp.concatenate([row_pos[:, None] * inv, col_pos[:, None] * inv], axis=-1)
        return jnp.cos(ang), jnp.sin(ang)

    cos_r, sin_r = angles(HEAD_W)
    cos_d, sin_d = angles(DIFF_DQK)
    zeros = jnp.zeros_like(sin_d)
    c_r = jnp.concatenate([cos_r, cos_r], axis=-1)
    s_r = jnp.concatenate([-sin_r, sin_r], axis=-1)
    c_d = jnp.tile(jnp.concatenate([cos_d, cos_d], axis=-1), (1, 2))
    s_lo = jnp.tile(jnp.concatenate([-sin_d, zeros], axis=-1), (1, 2))
    s_hi = jnp.tile(jnp.concatenate([zeros, sin_d], axis=-1), (1, 2))
    return c_r, s_r, c_d, s_lo, s_hi


@jax.jit
def kernel(x, c, ctx, c_ctx, w_mod, b_mod, g_pre, g_post, w_in, ret_decay_logit, mlp_w_s,
           mlp_b_s, diff_lambda_q, diff_lambda_k, w_branch_out, w_out):
    n_b, n_lat, _ = x.shape
    n_ctx = ctx.shape[1]
    tables = jnp.stack(_rope_tables(n_lat))

    cond_rows = 16
    ctx_row = n_b
    cc = jnp.zeros((cond_rows, D_MODEL), F32).at[:n_b].set(c).at[ctx_row].set(c_ctx)
    mod4 = _modulation(cc, w_mod, b_mod).reshape(DEPTH, cond_rows, 1, 3 * D_MODEL)
    lat_row = lambda b: b
    ctx_row_of = lambda b: ctx_row

    log_gamma = -jax.nn.softplus(-ret_decay_logit.astype(F32))
    lam_inits = [0.8 - 0.6 * math.exp(-0.3 * l) for l in range(DEPTH)]
    lam = (jnp.exp(jnp.sum(diff_lambda_q[:, 0] * diff_lambda_k[:, 0], axis=-1))
           - jnp.exp(jnp.sum(diff_lambda_q[:, 1] * diff_lambda_k[:, 1], axis=-1))
           + jnp.asarray(lam_inits, F32)).astype(F32)

    w_sec = w_in
    ws = mlp_w_s.astype(BF16)
    bs = jnp.broadcast_to(mlp_b_s[..., None], (DEPTH, HEADS, CHUNK, LANES)).astype(F32)
    wbo = w_branch_out.astype(BF16)
    wo = w_out.astype(BF16)
    g_pre3 = g_pre.reshape(DEPTH, 1, D_MODEL)
    g_post3 = g_post.reshape(DEPTH, 1, D_MODEL)

    for l in range(DEPTH):
        last = l == DEPTH - 1
        post_scale = 1.0 - lam_inits[l]

        ctx_flat = ctx.reshape(1, n_b * n_ctx, D_MODEL)
        if last:
            zc = _projection(ctx_flat, l, g_pre3, mod4, ctx_row_of, w_sec, lambda j: j,
                             KV_KINDS, None)
            sec_c = KV_SEC
        else:
            zc = _projection(ctx_flat, l, g_pre3, mod4, ctx_row_of, w_sec, _full_src,
                             FULL_KINDS, None)
            sec_c = FULL_SEC
            co_ret = _retention(log_gamma, l, zc, sec_c, True, n_b, n_ctx, None, None, 0,
                                rc=RET_CHUNK, heads=HEADS)
            co_diff = _diff_attention_ctx(lam, l, zc, sec_c, n_b, n_ctx, post_scale=post_scale)
            ctx_next = _merge(ctx_flat, l, mod4, ctx_row_of, g_post3,
                              co_ret.reshape(1, n_b * n_ctx, BRANCH_W),
                              co_diff.reshape(1, n_b * n_ctx, BRANCH_W), zc, sec_c,
                              ws, bs, wbo, wo, tm=MERGE_TM)
            ctx_next = ctx_next.reshape(n_b, n_ctx, D_MODEL)

        z = _projection(x, l, g_pre3, mod4, lat_row, w_sec, _full_src, FULL_KINDS, tables)
        o_ret = _retention(log_gamma, l, z, FULL_SEC, False, n_b, n_lat, zc, sec_c, n_ctx,
                           rc=RET_CHUNK, heads=RET_HEADS)
        o_diff = _diff_attention_pipelined(lam, l, z, FULL_SEC, n_b, n_lat, zc, sec_c, n_ctx,
                                           ts=DIFF_TS, post_scale=post_scale)
        x = _merge(x, l, mod4, lat_row, g_post3, o_ret, o_diff, z, FULL_SEC, ws, bs, wbo, wo,
                   tm=MERGE_TM)
        if not last:
            ctx = ctx_next
    return x
```

```python
import functools
import math

import jax
import jax.numpy as jnp
from jax import lax
from jax.experimental import pallas as pl
from jax.experimental.pallas import tpu as pltpu

F32 = jnp.float32
BF16 = jnp.bfloat16

D_MODEL = 1024
DEPTH = 2
GRID_W = 64
N_BRANCH = 3
BRANCH_W = D_MODEL // 2
HEADS = 4
HEAD_W = BRANCH_W // HEADS
DIFF_DQK = HEAD_W // 2
CHUNK = 128
ROPE_BASE = 10000.0
EPS = 1e-6
RET_SCALE = HEAD_W ** -0.5
DIFF_Q_SCALE = (DIFF_DQK ** -0.5) * math.log2(math.e)

LANES = 128
SEC = BRANCH_W
MERGE_COLS = N_BRANCH * D_MODEL
KV_COLS = 4 * BRANCH_W
IN_COLS = KV_COLS + 7 * BRANCH_W + MERGE_COLS
VMEM_LIMIT = 56 * 1024 * 1024

PROJ_SUB = 128
DIFF_TS = 256
RET_CHUNK = 256
RET_HEADS = 4
MERGE_TM = 512
MERGE_OUT_SUB = 512

FULL_KINDS = ("sigmoid",) * 6 + ("rk", "plain", "dk", "plain", "rq", "silu", "dq", "silu",
                                 "plain", "ln", "silu")
FULL_SEC = dict(rk=6, rv=7, dk=8, dv=9, rq=10, rg=11, dq=12, dg=13, mu=14, mv=15, mg=16)
KV_KINDS = ("rk", "plain", "dk", "plain")
KV_SEC = dict(rk=0, rv=1, dk=2, dv=3)
_N_MERGE_SEC = MERGE_COLS // SEC
_N_REST_SEC = (KV_COLS + 7 * BRANCH_W) // SEC


def _full_src(j):
    return jnp.where(j < _N_MERGE_SEC, j + _N_REST_SEC, j - _N_MERGE_SEC)


def _dot(a, b):
    return jnp.dot(a, b, preferred_element_type=F32)


def _dot_nt(a, b):
    return lax.dot_general(a, b, (((1,), (1,)), ((), ())), preferred_element_type=F32)


def _rms(v):
    return v * lax.rsqrt(jnp.mean(v * v, axis=-1, keepdims=True) + EPS)


def _mod_spec(l, row_of, k):
    return pl.BlockSpec((None, None, 1, D_MODEL), lambda *idx: (l, row_of(idx[0]), 0, k))


def _layer_spec(l, shape, **kwargs):
    zeros = (0,) * len(shape)
    return pl.BlockSpec((None,) + tuple(shape), lambda *idx: (l,) + zeros, **kwargs)


def _z_spec(sec_idx, rows, width, flat, col_of=lambda *idx: 0, row_of=None):
    if row_of is None:
        row_of = (lambda *idx: idx[0]) if flat else (lambda *idx: 0)
    group_of = (lambda *idx: 0) if flat else (lambda *idx: idx[0])
    return pl.BlockSpec((1, None, rows, width),
                        lambda *idx: (group_of(*idx), sec_idx, row_of(*idx), col_of(*idx)))


def _split_bf16(a):
    hi = a.astype(BF16)
    return hi, (a - hi.astype(F32)).astype(BF16)


def _mod_kernel(c_ref, w_ref, b_ref, o_ref):
    cv = c_ref[...]
    c_hi, c_lo = _split_bf16(cv * jax.nn.sigmoid(cv))
    w_hi, w_lo = _split_bf16(w_ref[0])
    o_ref[0] = _dot(c_hi, w_hi) + _dot(c_hi, w_lo) + _dot(c_lo, w_hi) + b_ref[0]


def _modulation(cc, w_mod, b_mod):
    rows = cc.shape[0]
    return pl.pallas_call(
        _mod_kernel,
        grid=(DEPTH, 3),
        in_specs=[
            pl.BlockSpec((rows, D_MODEL), lambda l, j: (0, 0)),
            pl.BlockSpec((1, D_MODEL, D_MODEL), lambda l, j: (l, 0, j)),
            pl.BlockSpec((1, 1, D_MODEL), lambda l, j: (l, 0, j)),
        ],
        out_specs=pl.BlockSpec((1, rows, D_MODEL), lambda l, j: (l, 0, j)),
        out_shape=jax.ShapeDtypeStruct((DEPTH, rows, 3 * D_MODEL), F32),
        compiler_params=pltpu.CompilerParams(vmem_limit_bytes=VMEM_LIMIT),
        name="modulation",
    )(cc, w_mod, b_mod.reshape(DEPTH, 1, 3 * D_MODEL))


def _rope_ret(a, c, s):
    outs = []
    for h in range(HEADS):
        ah = a[:, h * LANES:(h + 1) * LANES]
        outs.append(ah * c + pltpu.roll(ah, LANES // 2, 1) * s)
    return jnp.concatenate(outs, axis=1)


def _rope_diff(a, c, s_lo, s_hi):
    outs = []
    for h in range(HEADS):
        ah = a[:, h * LANES:(h + 1) * LANES]
        outs.append(ah * c + pltpu.roll(ah, LANES - DIFF_DQK // 2, 1) * s_lo
                    + pltpu.roll(ah, DIFF_DQK // 2, 1) * s_hi)
    return jnp.concatenate(outs, axis=1)


def _proj_kernel(*refs, layer, kinds, src_of, n_src, rope, n_rows, sub):
    if rope:
        (x_ref, g_ref, shift_ref, scale_ref, w_hbm, tab_hbm, z_hbm,
         h_ref, stage, w_vmem, w_stage, sems_out, sems_w, tab_vmem, sem_tab) = refs
        cr_ref, sr_ref, cd_ref, sdl_ref, sdh_ref = [tab_vmem.at[i] for i in range(5)]
    else:
        (x_ref, g_ref, shift_ref, scale_ref, w_hbm, z_hbm,
         h_ref, stage, w_vmem, w_stage, sems_out, sems_w) = refs
    b = pl.program_id(0)
    first = b == 0
    n_sub = n_rows // sub
    n_sec = len(kinds)
    n_slots = stage.shape[0]

    n_wslots = w_stage.shape[0]

    def weight_copy(j):
        start = src_of(j) * SEC
        cols = pl.ds(start if isinstance(start, int) else pl.multiple_of(start, SEC), SEC)
        slot = j % n_wslots
        return pltpu.make_async_copy(w_hbm.at[layer, :, cols], w_stage.at[slot], sems_w.at[slot])

    def load_weights(j):
        weight_copy(j).wait()
        src = w_stage.at[j % n_wslots]
        dst = w_vmem.at[src_of(j)]
        for r in range(D_MODEL // LANES):
            rows = slice(r * LANES, (r + 1) * LANES)
            dst[rows, :] = src[rows, :].astype(BF16)

        @pl.when(j + n_wslots < n_sec)
        def _():
            weight_copy(j + n_wslots).start()

    def table_copy():
        return pltpu.make_async_copy(tab_hbm, tab_vmem, sem_tab.at[0])

    @pl.when(first)
    def _():
        for j in range(min(n_wslots, n_sec)):
            weight_copy(j).start()
        if rope:
            table_copy().start()

    gs = g_ref[...] * (1.0 + scale_ref[...])
    sh = shift_ref[...]
    for r in range(n_sub):
        xs = x_ref[0, r * sub:(r + 1) * sub, :]
        h_ref[r * sub:(r + 1) * sub, :] = (_rms(xs) * gs + sh).astype(BF16)

    if rope:
        pl.when(first)(lambda: table_copy().wait())

    def epi_rk(a, rows):
        if rope:
            a = _rope_ret(a, cr_ref[rows, :], sr_ref[rows, :])
        return a * RET_SCALE

    def epi_rq(a, rows):
        return _rope_ret(a, cr_ref[rows, :], sr_ref[rows, :]) if rope else a

    def epi_dk(a, rows):
        return _rope_diff(a, cd_ref[rows, :], sdl_ref[rows, :], sdh_ref[rows, :]) if rope else a

    def epi_dq(a, rows):
        return epi_dk(a, rows) * DIFF_Q_SCALE

    def epi_ln(a, rows):
        mu = jnp.mean(a, axis=-1, keepdims=True)
        d = a - mu
        return d * lax.rsqrt(jnp.mean(d * d, axis=-1, keepdims=True) + EPS)

    epilogues = dict(
        plain=lambda a, rows: a,
        sigmoid=lambda a, rows: jax.nn.sigmoid(a),
        silu=lambda a, rows: a * jax.nn.sigmoid(a),
        rk=epi_rk, rq=epi_rq, dk=epi_dk, dq=epi_dq, ln=epi_ln)

    def writeback(j):
        slot = j % n_slots
        return pltpu.make_async_copy(stage.at[slot], z_hbm.at[b, j], sems_out.at[slot])

    def run(epilogue, j):
        w_sec = w_vmem.at[src_of(j)]
        out = stage.at[j % n_slots]
        for r in range(n_sub):
            rows = slice(r * sub, (r + 1) * sub)
            out[rows, :] = epilogue(_dot(h_ref[rows, :], w_sec[...]), rows).astype(BF16)

    def section(j, carry):
        pl.when(first)(functools.partial(load_weights, j))

        @pl.when(j >= n_slots)
        def _():
            writeback(j - n_slots).wait()

        for kind in sorted(set(kinds)):
            idx = [i for i, k in enumerate(kinds) if k == kind]
            cond = j == idx[0]
            for i in idx[1:]:
                cond = cond | (j == i)
            pl.when(cond)(functools.partial(run, epilogues[kind], j))
        writeback(j).start()
        return carry

    lax.fori_loop(0, n_sec, section, 0)
    for j in range(max(0, n_sec - n_slots), n_sec):
        writeback(j).wait()


def _projection(xs, l, g_pre, mod4, row_of, w_all, src_of, kinds, tables):
    n_g, n_rows, _ = xs.shape
    n_sec = len(kinds)
    rope = tables is not None
    n_src = w_all.shape[2] // SEC if src_of is _full_src else n_sec
    in_specs = [
        pl.BlockSpec((1, n_rows, D_MODEL), lambda b: (b, 0, 0)),
        _layer_spec(l, (1, D_MODEL)),
        _mod_spec(l, row_of, 0),
        _mod_spec(l, row_of, 1),
        pl.BlockSpec(memory_space=pl.ANY),
    ]
    args = [xs, g_pre, mod4, mod4, w_all]
    scratch = [pltpu.VMEM((n_rows, D_MODEL), BF16),
               pltpu.VMEM((2, n_rows, SEC), BF16),
               pltpu.VMEM((n_src, D_MODEL, SEC), BF16),
               pltpu.VMEM((2, D_MODEL, SEC), F32),
               pltpu.SemaphoreType.DMA((2,)),
               pltpu.SemaphoreType.DMA((2,))]
    if rope:
        in_specs.append(pl.BlockSpec(memory_space=pl.ANY))
        args.append(tables)
        scratch += [pltpu.VMEM(tables.shape, F32), pltpu.SemaphoreType.DMA((1,))]
    return pl.pallas_call(
        functools.partial(_proj_kernel, layer=l, kinds=kinds, src_of=src_of, n_src=n_src,
                          rope=rope, n_rows=n_rows, sub=PROJ_SUB),
        grid=(n_g,),
        in_specs=in_specs,
        out_specs=pl.BlockSpec(memory_space=pl.ANY),
        out_shape=jax.ShapeDtypeStruct((n_g, n_sec, n_rows, SEC), BF16),
        scratch_shapes=scratch,
        compiler_params=pltpu.CompilerParams(vmem_limit_bytes=VMEM_LIMIT),
        name="projection_rope" if rope else "projection",
    )(*args)


def _ret_kernel(*refs, layer, n, n_ctx, rc, heads):
    if n_ctx:
        lg_ref, q_ref, g_ref, k_ref, v_ref, kc_ref, vc_ref, o_ref, inc_ref, st_ref = refs
    else:
        lg_ref, q_ref, g_ref, k_ref, v_ref, o_ref, inc_ref, st_ref = refs
    nc = n // rc
    hs = range(heads)
    cols = [slice(hh * LANES, (hh + 1) * LANES) for hh in hs]
    lgf = [lg_ref[layer, 0, pl.program_id(1) * heads + hh] for hh in hs]
    lgb = [lg_ref[layer, 1, pl.program_id(1) * heads + hh] for hh in hs]
    pos = lax.broadcasted_iota(jnp.int32, (rc, LANES), 0).astype(F32)
    kdf = [jnp.exp(lgf[hh] * (rc - 1.0 - pos)).astype(BF16) for hh in hs]
    kdb = [jnp.exp(lgb[hh] * pos).astype(BF16) for hh in hs]
    cdf = [jnp.exp(jnp.full((1, LANES), lgf[hh] * rc, F32)) for hh in hs]
    cdb = [jnp.exp(jnp.full((1, LANES), lgb[hh] * rc, F32)) for hh in hs]

    def increment(kr, vr, nn, hh):
        rows = slice(nn * rc, (nn + 1) * rc)
        kk = kr[0, rows, cols[hh]]
        kcat = jnp.concatenate([kk * kdf[hh], kk * kdb[hh]], axis=1)
        vt = vr[0, rows, cols[hh]].astype(F32).T.astype(BF16)
        return _dot(vt, kcat)

    sf = [jnp.zeros((HEAD_W, HEAD_W), F32) for _ in hs]
    sb = [jnp.zeros((HEAD_W, HEAD_W), F32) for _ in hs]
    if n_ctx:
        incs = [[increment(kc_ref, vc_ref, nn, hh) for hh in hs] for nn in range(n_ctx // rc)]
        for inc in incs:
            sf = [cdf[hh] * sf[hh] + inc[hh][:, :HEAD_W] for hh in hs]
        for inc in reversed(incs):
            sb = [cdb[hh] * sb[hh] + inc[hh][:, HEAD_W:] for hh in hs]

    for nn in range(nc):
        for hh in hs:
            inc_ref[hh, nn] = increment(k_ref, v_ref, nn, hh)
    for nn in range(nc):
        for hh in hs:
            st_ref[hh, nn, :, 0:HEAD_W] = sf[hh].astype(BF16)
            sf[hh] = cdf[hh] * sf[hh] + inc_ref[hh, nn, :, 0:HEAD_W]
    for nn in reversed(range(nc)):
        for hh in hs:
            st_ref[hh, nn, :, HEAD_W:] = sb[hh].astype(BF16)
            sb[hh] = cdb[hh] * sb[hh] + inc_ref[hh, nn, :, HEAD_W:]

    ii = lax.broadcasted_iota(jnp.int32, (rc, rc), 0)
    jj = lax.broadcasted_iota(jnp.int32, (rc, rc), 1)
    dist = (ii - jj).astype(F32)
    dmat = [jnp.where(dist >= 0, jnp.exp(lgf[hh] * jnp.maximum(dist, 0.0)), 0.0)
            + jnp.where(dist <= 0, jnp.exp(lgb[hh] * jnp.maximum(-dist, 0.0)), 0.0) for hh in hs]
    qdf = [jnp.exp(lgf[hh] * (pos + 1.0)).astype(BF16) for hh in hs]
    qdb = [jnp.exp(lgb[hh] * (rc - pos)).astype(BF16) for hh in hs]
    for nn in range(nc):
        rows = slice(nn * rc, (nn + 1) * rc)
        for hh in hs:
            cs = cols[hh]
            q = q_ref[0, rows, cs]
            s = _dot_nt(q, k_ref[0, rows, cs]) * dmat[hh]
            inner = _dot(s.astype(BF16), v_ref[0, rows, cs])
            qcat = jnp.concatenate([q * qdf[hh], q * qdb[hh]], axis=1)
            out = inner + _dot_nt(qcat, st_ref[hh, nn])
            o_ref[0, rows, cs] = _rms(out).astype(BF16) * g_ref[0, rows, cs]


def _retention(log_gamma, layer, z, sec, flat, n_b, n, zc, sec_c, n_ctx, *, rc, heads):
    width = heads * LANES
    head_block = lambda b, h: h

    in_specs = [pl.BlockSpec(memory_space=pltpu.SMEM)]
    in_specs += [_z_spec(sec[name], n, width, flat, col_of=head_block)
                 for name in ("rq", "rg", "rk", "rv")]
    args = [log_gamma, z, z, z, z]
    if n_ctx:
        in_specs += [_z_spec(sec_c[name], n_ctx, width, True, col_of=head_block)
                     for name in ("rk", "rv")]
        args += [zc, zc]
    return pl.pallas_call(
        functools.partial(_ret_kernel, layer=layer, n=n, n_ctx=n_ctx, rc=rc, heads=heads),
        grid=(n_b, HEADS // heads),
        in_specs=in_specs,
        out_specs=pl.BlockSpec((1, n, width), lambda b, h: (b, 0, h)),
        out_shape=jax.ShapeDtypeStruct((n_b, n, BRANCH_W), BF16),
        scratch_shapes=[pltpu.VMEM((heads, n // rc, HEAD_W, 2 * HEAD_W), F32),
                        pltpu.VMEM((heads, n // rc, HEAD_W, 2 * HEAD_W), BF16)],
        compiler_params=pltpu.CompilerParams(vmem_limit_bytes=VMEM_LIMIT),
        name="retention_ctx_init" if n_ctx else "retention",
    )(*args)


def _split_maps(q, lane):
    zero = jnp.zeros_like(q)
    return jnp.concatenate([jnp.where(lane < DIFF_DQK, q, zero),
                            jnp.where(lane >= DIFF_DQK, q, zero)], axis=0)


def _diff_ctx_kernel(lam_ref, q_ref, g_ref, k_ref, v_ref, o_ref, *, layer, tq, post_scale):
    lane = lax.broadcasted_iota(jnp.int32, (tq, LANES), 1)
    for h in range(HEADS):
        cs = slice(h * LANES, (h + 1) * LANES)
        s = _dot_nt(_split_maps(q_ref[0, :, cs], lane), k_ref[0, :, cs])
        p = jnp.exp2(s - jnp.max(s, axis=-1, keepdims=True))
        o = _dot(p.astype(BF16), v_ref[0, :, cs]) / jnp.sum(p, axis=-1, keepdims=True)
        d = o[:tq] - lam_ref[layer] * o[tq:]
        o_ref[0, :, cs] = (_rms(d) * post_scale * g_ref[0, :, cs].astype(F32)).astype(BF16)


def _diff_attention_ctx(lam, layer, zc, sec, n_b, n, *, post_scale):
    return pl.pallas_call(
        functools.partial(_diff_ctx_kernel, layer=layer, tq=n, post_scale=post_scale),
        grid=(n_b,),
        in_specs=[pl.BlockSpec(memory_space=pltpu.SMEM)]
        + [_z_spec(sec[name], n, SEC, True) for name in ("dq", "dg", "dk", "dv")],
        out_specs=pl.BlockSpec((1, n, BRANCH_W), lambda b: (b, 0, 0)),
        out_shape=jax.ShapeDtypeStruct((n_b, n, BRANCH_W), BF16),
        compiler_params=pltpu.CompilerParams(vmem_limit_bytes=VMEM_LIMIT),
        name="diff_attention_ctx",
    )(lam, zc, zc, zc, zc)


def _diff_pipe_kernel(lam_ref, q_ref, g_ref, kc_ref, vc_ref, kl_ref, vl_ref, o_ref,
                      kall, vall, s0, s1, p0, p1, *, layer, n_sub, ts, post_scale):
    n_ctx = kc_ref.shape[1]
    nk = kall.shape[0]
    kall[0:n_ctx, :] = kc_ref[0]
    kall[n_ctx:, :] = kl_ref[0]
    vall[0:n_ctx, 0:LANES] = vc_ref[0]
    vall[n_ctx:, 0:LANES] = vl_ref[0]
    vall[:, LANES:] = jnp.ones((nk, LANES), BF16)
    lam = lam_ref[layer]
    lane = lax.broadcasted_iota(jnp.int32, (ts, LANES), 1)
    s_bufs, p_bufs = (s0, s1), (p0, p1)

    def stage_a(t):
        rows = slice(t * ts, (t + 1) * ts)
        s_bufs[t % 2][...] = _dot_nt(_split_maps(q_ref[0, rows, :], lane), kall[...])

    def stage_b(t):
        s = s_bufs[t % 2][...]
        p_bufs[t % 2][...] = jnp.exp2(s - jnp.max(s, axis=-1, keepdims=True)).astype(BF16)

    def stage_c(t):
        rows = slice(t * ts, (t + 1) * ts)
        oe = _dot(p_bufs[t % 2][...], vall[...])
        o = oe[:, :LANES] / oe[:, LANES:]
        d = o[:ts] - lam * o[ts:]
        o_ref[0, rows, :] = (_rms(d) * post_scale * g_ref[0, rows, :].astype(F32)).astype(BF16)

    for t in range(n_sub + 2):
        if t >= 2:
            stage_c(t - 2)
        if 1 <= t <= n_sub:
            stage_b(t - 1)
        if t < n_sub:
            stage_a(t)


def _diff_attention_pipelined(lam, layer, z, sec, n_b, n, zc, sec_c, n_ctx, *, ts, post_scale):
    nk = n_ctx + n
    head_block = lambda b, h: h
    lat = lambda name: _z_spec(sec[name], n, LANES, False, col_of=head_block)
    cx = lambda name: _z_spec(sec_c[name], n_ctx, LANES, True, col_of=head_block)
    return pl.pallas_call(
        functools.partial(_diff_pipe_kernel, layer=layer, n_sub=n // ts, ts=ts,
                          post_scale=post_scale),
        grid=(n_b, HEADS),
        in_specs=[pl.BlockSpec(memory_space=pltpu.SMEM),
                  lat("dq"), lat("dg"), cx("dk"), cx("dv"), lat("dk"), lat("dv")],
        out_specs=pl.BlockSpec((1, n, LANES), lambda b, h: (b, 0, h)),
        out_shape=jax.ShapeDtypeStruct((n_b, n, BRANCH_W), BF16),
        scratch_shapes=[pltpu.VMEM((nk, LANES), BF16), pltpu.VMEM((nk, 2 * LANES), BF16),
                        pltpu.VMEM((2 * ts, nk), F32), pltpu.VMEM((2 * ts, nk), F32),
                        pltpu.VMEM((2 * ts, nk), BF16), pltpu.VMEM((2 * ts, nk), BF16)],
        compiler_params=pltpu.CompilerParams(vmem_limit_bytes=VMEM_LIMIT),
        name="diff_attention_pipelined",
    )(lam, z, z, zc, zc, z, z)


def _merge_kernel(x_ref, gate_ref, gpost_ref, oret_ref, odiff_ref, u_ref, vn_ref, mg_ref,
                  gr_ref, gm_ref, gd_ref, ws_ref, bs_ref, wbo_ref, wo_ref, out_ref, *, tm, out_sub):
    rows = []
    for c in range(tm // CHUNK):
        cols = []
        for g in range(HEADS):
            blk = vn_ref[0, c * CHUNK:(c + 1) * CHUNK, g * LANES:(g + 1) * LANES]
            cols.append(_dot(ws_ref[g], blk) + bs_ref[g])
        rows.append(jnp.concatenate(cols, axis=1))
    sp = jnp.concatenate(rows, axis=0)
    o_mlp = (u_ref[0].astype(F32) * sp * mg_ref[0].astype(F32)).astype(BF16)

    def gated(gate2_ref, o, w):
        gate = jnp.concatenate([gate2_ref[0, 0], gate2_ref[0, 1]], axis=1).astype(F32)
        return gate * _dot(o, w)

    t = (gated(gr_ref, oret_ref[0], wbo_ref[0]) + gated(gm_ref, o_mlp, wbo_ref[1])
         + gated(gd_ref, odiff_ref[0], wbo_ref[2]))
    tb = t.astype(BF16)
    for r in range(tm // out_sub):
        rs = slice(r * out_sub, (r + 1) * out_sub)
        y = _dot(tb[rs, :], wo_ref[...])
        out_ref[0, rs, :] = x_ref[0, rs, :] + gate_ref[...] * (_rms(y) * gpost_ref[...])


def _merge(xs, l, mod4, row_of, g_post, o_ret, o_diff, z, sec, ws, bs, wbo, wo, *, tm):
    n_b, n, _ = xs.shape
    row_block = lambda b, i: i

    def zsec(name):
        return _z_spec(sec[name], tm, SEC, False, row_of=row_block)

    def zgate(k):
        return pl.BlockSpec((1, 2, tm, SEC), lambda b, i: (b, k, i, 0))

    in_specs = [
        pl.BlockSpec((1, tm, D_MODEL), lambda b, i: (b, i, 0)),
        _mod_spec(l, row_of, 2),
        _layer_spec(l, (1, D_MODEL)),
        pl.BlockSpec((1, tm, BRANCH_W), lambda b, i: (b, i, 0)),
        pl.BlockSpec((1, tm, BRANCH_W), lambda b, i: (b, i, 0)),
        zsec("mu"), zsec("mv"), zsec("mg"), zgate(0), zgate(1), zgate(2),
        _layer_spec(l, (HEADS, CHUNK, CHUNK)),
        _layer_spec(l, (HEADS, CHUNK, LANES)),
        _layer_spec(l, (N_BRANCH, BRANCH_W, D_MODEL)),
        _layer_spec(l, (D_MODEL, D_MODEL)),
    ]
    return pl.pallas_call(
        functools.partial(_merge_kernel, tm=tm, out_sub=MERGE_OUT_SUB),
        grid=(n_b, n // tm),
        in_specs=in_specs,
        out_specs=pl.BlockSpec((1, tm, D_MODEL), lambda b, i: (b, i, 0)),
        out_shape=jax.ShapeDtypeStruct(xs.shape, F32),
        compiler_params=pltpu.CompilerParams(vmem_limit_bytes=VMEM_LIMIT),
        name="merge",
    )(xs, mod4, g_post, o_ret, o_diff, z, z, z, z, z, z, ws, bs, wbo, wo)


def _rope_tables(n_lat):
    rows = n_lat // GRID_W
    row_pos = jnp.repeat(jnp.arange(rows, dtype=F32), GRID_W)
    col_pos = jnp.tile(jnp.arange(GRID_W, dtype=F32), rows)

    def angles(head_dim):
        n_freq = head_dim // 4
        inv = ROPE_BASE ** (-jnp.arange(n_freq, dtype=F32) / n_freq)
        ang = jnp.concatenate([row_pos[:, None] * inv, col_pos[:, None] * inv], axis=-1)
        return jnp.cos(ang), jnp.sin(ang)

    cos_r, sin_r = angles(HEAD_W)
    cos_d, sin_d = angles(DIFF_DQK)
    zeros = jnp.zeros_like(sin_d)
    c_r = jnp.concatenate([cos_r, cos_r], axis=-1)
    s_r = jnp.concatenate([-sin_r, sin_r], axis=-1)
    c_d = jnp.tile(jnp.concatenate([cos_d, cos_d], axis=-1), (1, 2))
    s_lo = jnp.tile(jnp.concatenate([-sin_d, zeros], axis=-1), (1, 2))
    s_hi = jnp.tile(jnp.concatenate([zeros, sin_d], axis=-1), (1, 2))
    return c_r, s_r, c_d, s_lo, s_hi


@jax.jit
def kernel(x, c, ctx, c_ctx, w_mod, b_mod, g_pre, g_post, w_in, ret_decay_logit, mlp_w_s,
           mlp_b_s, diff_lambda_q, diff_lambda_k, w_branch_out, w_out):
    n_b, n_lat, _ = x.shape
    n_ctx = ctx.shape[1]
    tables = jnp.stack(_rope_tables(n_lat))

    cond_rows = 16
    ctx_row = n_b
    cc = jnp.zeros((cond_rows, D_MODEL), F32).at[:n_b].set(c).at[ctx_row].set(c_ctx)
    mod4 = _modulation(cc, w_mod, b_mod).reshape(DEPTH, cond_rows, 1, 3 * D_MODEL)
    lat_row = lambda b: b
    ctx_row_of = lambda b: ctx_row

    log_gamma = -jax.nn.softplus(-ret_decay_logit.astype(F32))
    lam_inits = [0.8 - 0.6 * math.exp(-0.3 * l) for l in range(DEPTH)]
    lam = (jnp.exp(jnp.sum(diff_lambda_q[:, 0] * diff_lambda_k[:, 0], axis=-1))
           - jnp.exp(jnp.sum(diff_lambda_q[:, 1] * diff_lambda_k[:, 1], axis=-1))
           + jnp.asarray(lam_inits, F32)).astype(F32)

    w_sec = w_in
    ws = mlp_w_s.astype(BF16)
    bs = jnp.broadcast_to(mlp_b_s[..., None], (DEPTH, HEADS, CHUNK, LANES)).astype(F32)
    wbo = w_branch_out.astype(BF16)
    wo = w_out.astype(BF16)
    g_pre3 = g_pre.reshape(DEPTH, 1, D_MODEL)
    g_post3 = g_post.reshape(DEPTH, 1, D_MODEL)

    for l in range(DEPTH):
        last = l == DEPTH - 1
        post_scale = 1.0 - lam_inits[l]

        ctx_flat = ctx.reshape(1, n_b * n_ctx, D_MODEL)
        if last:
            zc = _projection(ctx_flat, l, g_pre3, mod4, ctx_row_of, w_sec, lambda j: j,
                             KV_KINDS, None)
            sec_c = KV_SEC
        else:
            zc = _projection(ctx_flat, l, g_pre3, mod4, ctx_row_of, w_sec, _full_src,
                             FULL_KINDS, None)
            sec_c = FULL_SEC
            co_ret = _retention(log_gamma, l, zc, sec_c, True, n_b, n_ctx, None, None, 0,
                                rc=RET_CHUNK, heads=HEADS)
            co_diff = _diff_attention_ctx(lam, l, zc, sec_c, n_b, n_ctx, post_scale=post_scale)
            ctx_next = _merge(ctx_flat, l, mod4, ctx_row_of, g_post3,
                              co_ret.reshape(1, n_b * n_ctx, BRANCH_W),
                              co_diff.reshape(1, n_b * n_ctx, BRANCH_W), zc, sec_c,
                              ws, bs, wbo, wo, tm=MERGE_TM)
            ctx_next = ctx_next.reshape(n_b, n_ctx, D_MODEL)

        z = _projection(x, l, g_pre3, mod4, lat_row, w_sec, _full_src, FULL_KINDS, tables)
        o_ret = _retention(log_gamma, l, z, FULL_SEC, False, n_b, n_lat, zc, sec_c, n_ctx,
                           rc=RET_CHUNK, heads=RET_HEADS)
        o_diff = _diff_attention_pipelined(lam, l, z, FULL_SEC, n_b, n_lat, zc, sec_c, n_ctx,
                                           ts=DIFF_TS, post_scale=post_scale)
        x = _merge(x, l, mod4, lat_row, g_post3, o_ret, o_diff, z, FULL_SEC, ws, bs, wbo, wo,
                   tm=MERGE_TM)
        if not last:
            ctx = ctx_next
    return x
```

```python
import functools
import math

import jax
import jax.numpy as jnp
from jax import lax
from jax.experimental import pallas as pl
from jax.experimental.pallas import tpu as pltpu

F32 = jnp.float32
BF16 = jnp.bfloat16

D_MODEL = 1024
DEPTH = 2
GRID_W = 64
N_BRANCH = 3
BRANCH_W = D_MODEL // 2
HEADS = 4
HEAD_W = BRANCH_W // HEADS
DIFF_DQK = HEAD_W // 2
CHUNK = 128
ROPE_BASE = 10000.0
EPS = 1e-6
RET_SCALE = HEAD_W ** -0.5
DIFF_Q_SCALE = (DIFF_DQK ** -0.5) * math.log2(math.e)

LANES = 128
SEC = BRANCH_W
MERGE_COLS = N_BRANCH * D_MODEL
KV_COLS = 4 * BRANCH_W
IN_COLS = KV_COLS + 7 * BRANCH_W + MERGE_COLS
VMEM_LIMIT = 56 * 1024 * 1024

PROJ_SUB = 128
DIFF_TS = 256
RET_CHUNK = 256
RET_HEADS = 4
MERGE_TM = 512
MERGE_OUT_SUB = 512

FULL_KINDS = ("sigmoid",) * 6 + ("rk", "plain", "dk", "plain", "rq", "silu", "dq", "silu",
                                 "plain", "ln", "silu")
FULL_SEC = dict(rk=6, rv=7, dk=8, dv=9, rq=10, rg=11, dq=12, dg=13, mu=14, mv=15, mg=16)
KV_KINDS = ("rk", "plain", "dk", "plain")
KV_SEC = dict(rk=0, rv=1, dk=2, dv=3)
_N_MERGE_SEC = MERGE_COLS // SEC
_N_REST_SEC = (KV_COLS + 7 * BRANCH_W) // SEC


def _full_src(j):
    if isinstance(j, int):
        return j + _N_REST_SEC if j < _N_MERGE_SEC else j - _N_MERGE_SEC
    return jnp.where(j < _N_MERGE_SEC, j + _N_REST_SEC, j - _N_MERGE_SEC)


def _dot(a, b):
    return jnp.dot(a, b, preferred_element_type=F32)


def _dot_nt(a, b):
    return lax.dot_general(a, b, (((1,), (1,)), ((), ())), preferred_element_type=F32)


def _rms(v):
    return v * lax.rsqrt(jnp.mean(v * v, axis=-1, keepdims=True) + EPS)


def _mod_spec(l, row_of, k):
    return pl.BlockSpec((None, None, 1, D_MODEL), lambda *idx: (l, row_of(idx[0]), 0, k))


def _layer_spec(l, shape, **kwargs):
    zeros = (0,) * len(shape)
    return pl.BlockSpec((None,) + tuple(shape), lambda *idx: (l,) + zeros, **kwargs)


def _z_spec(sec_idx, rows, width, flat, col_of=lambda *idx: 0, row_of=None):
    if row_of is None:
        row_of = (lambda *idx: idx[0]) if flat else (lambda *idx: 0)
    group_of = (lambda *idx: 0) if flat else (lambda *idx: idx[0])
    return pl.BlockSpec((1, None, rows, width),
                        lambda *idx: (group_of(*idx), sec_idx, row_of(*idx), col_of(*idx)))


def _split_bf16(a):
    hi = a.astype(BF16)
    return hi, (a - hi.astype(F32)).astype(BF16)


def _mod_kernel(c_ref, w_ref, b_ref, o_ref):
    cv = c_ref[...]
    c_hi, c_lo = _split_bf16(cv * jax.nn.sigmoid(cv))
    w_hi, w_lo = _split_bf16(w_ref[0])
    o_ref[0] = _dot(c_hi, w_hi) + _dot(c_hi, w_lo) + _dot(c_lo, w_hi) + b_ref[0]


def _modulation(cc, w_mod, b_mod):
    rows = cc.shape[0]
    return pl.pallas_call(
        _mod_kernel,
        grid=(DEPTH, 3),
        in_specs=[
            pl.BlockSpec((rows, D_MODEL), lambda l, j: (0, 0)),
            pl.BlockSpec((1, D_MODEL, D_MODEL), lambda l, j: (l, 0, j)),
            pl.BlockSpec((1, 1, D_MODEL), lambda l, j: (l, 0, j)),
        ],
        out_specs=pl.BlockSpec((1, rows, D_MODEL), lambda l, j: (l, 0, j)),
        out_shape=jax.ShapeDtypeStruct((DEPTH, rows, 3 * D_MODEL), F32),
        compiler_params=pltpu.CompilerParams(vmem_limit_bytes=VMEM_LIMIT),
        name="modulation",
    )(cc, w_mod, b_mod.reshape(DEPTH, 1, 3 * D_MODEL))


def _rope_ret(a, c, s):
    outs = []
    for h in range(HEADS):
        ah = a[:, h * LANES:(h + 1) * LANES]
        outs.append(ah * c + pltpu.roll(ah, LANES // 2, 1) * s)
    return jnp.concatenate(outs, axis=1)


def _rope_diff(a, c, s_lo, s_hi):
    outs = []
    for h in range(HEADS):
        ah = a[:, h * LANES:(h + 1) * LANES]
        outs.append(ah * c + pltpu.roll(ah, LANES - DIFF_DQK // 2, 1) * s_lo
                    + pltpu.roll(ah, DIFF_DQK // 2, 1) * s_hi)
    return jnp.concatenate(outs, axis=1)


def _proj_kernel(*refs, layer, kinds, src_of, n_src, rope, n_rows, sub, unroll_sections):
    if rope:
        (x_ref, g_ref, shift_ref, scale_ref, w_hbm, tab_hbm, z_hbm,
         h_ref, stage, w_vmem, w_stage, sems_out, sems_w, tab_vmem, sem_tab) = refs
        cr_ref, sr_ref, cd_ref, sdl_ref, sdh_ref = [tab_vmem.at[i] for i in range(5)]
    else:
        (x_ref, g_ref, shift_ref, scale_ref, w_hbm, z_hbm,
         h_ref, stage, w_vmem, w_stage, sems_out, sems_w) = refs
    b = pl.program_id(0)
    first = b == 0
    n_sub = n_rows // sub
    n_sec = len(kinds)
    n_slots = stage.shape[0]

    n_wslots = w_stage.shape[0]

    def weight_copy(j):
        start = src_of(j) * SEC
        cols = pl.ds(start if isinstance(start, int) else pl.multiple_of(start, SEC), SEC)
        slot = j % n_wslots
        return pltpu.make_async_copy(w_hbm.at[layer, :, cols], w_stage.at[slot], sems_w.at[slot])

    def load_weights(j):
        weight_copy(j).wait()
        src = w_stage.at[j % n_wslots]
        dst = w_vmem.at[src_of(j)]
        for r in range(D_MODEL // LANES):
            rows = slice(r * LANES, (r + 1) * LANES)
            dst[rows, :] = src[rows, :].astype(BF16)

        if isinstance(j, int):
            if j + n_wslots < n_sec:
                weight_copy(j + n_wslots).start()
        else:
            pl.when(j + n_wslots < n_sec)(lambda: weight_copy(j + n_wslots).start())

    def table_copy():
        return pltpu.make_async_copy(tab_hbm, tab_vmem, sem_tab.at[0])

    @pl.when(first)
    def _():
        for j in range(min(n_wslots, n_sec)):
            weight_copy(j).start()
        if rope:
            table_copy().start()

    gs = g_ref[...] * (1.0 + scale_ref[...])
    sh = shift_ref[...]
    for r in range(n_sub):
        xs = x_ref[0, r * sub:(r + 1) * sub, :]
        h_ref[r * sub:(r + 1) * sub, :] = (_rms(xs) * gs + sh).astype(BF16)

    if rope:
        pl.when(first)(lambda: table_copy().wait())

    def epi_rk(a, rows):
        if rope:
            a = _rope_ret(a, cr_ref[rows, :], sr_ref[rows, :])
        return a * RET_SCALE

    def epi_rq(a, rows):
        return _rope_ret(a, cr_ref[rows, :], sr_ref[rows, :]) if rope else a

    def epi_dk(a, rows):
        return _rope_diff(a, cd_ref[rows, :], sdl_ref[rows, :], sdh_ref[rows, :]) if rope else a

    def epi_dq(a, rows):
        return epi_dk(a, rows) * DIFF_Q_SCALE

    def epi_ln(a, rows):
        mu = jnp.mean(a, axis=-1, keepdims=True)
        d = a - mu
        return d * lax.rsqrt(jnp.mean(d * d, axis=-1, keepdims=True) + EPS)

    epilogues = dict(
        plain=lambda a, rows: a,
        sigmoid=lambda a, rows: jax.nn.sigmoid(a),
        silu=lambda a, rows: a * jax.nn.sigmoid(a),
        rk=epi_rk, rq=epi_rq, dk=epi_dk, dq=epi_dq, ln=epi_ln)

    def writeback(j):
        slot = j % n_slots
        return pltpu.make_async_copy(stage.at[slot], z_hbm.at[b, j], sems_out.at[slot])

    def run(epilogue, j):
        w_sec = w_vmem.at[src_of(j)]
        out = stage.at[j % n_slots]
        for r in range(n_sub):
            rows = slice(r * sub, (r + 1) * sub)
            out[rows, :] = epilogue(_dot(h_ref[rows, :], w_sec[...]), rows).astype(BF16)

    def section(j, carry):
        pl.when(first)(functools.partial(load_weights, j))

        @pl.when(j >= n_slots)
        def _():
            writeback(j - n_slots).wait()

        for kind in sorted(set(kinds)):
            idx = [i for i, k in enumerate(kinds) if k == kind]
            cond = j == idx[0]
            for i in idx[1:]:
                cond = cond | (j == i)
            pl.when(cond)(functools.partial(run, epilogues[kind], j))
        writeback(j).start()
        return carry

    if unroll_sections:
        @pl.when(first)
        def _():
            for j in range(n_sec):
                load_weights(j)

        for j, kind in enumerate(kinds):
            if j >= n_slots:
                writeback(j - n_slots).wait()
            run(epilogues[kind], j)
            writeback(j).start()
    else:
        lax.fori_loop(0, n_sec, section, 0)
    for j in range(max(0, n_sec - n_slots), n_sec):
        writeback(j).wait()


def _projection(xs, l, g_pre, mod4, row_of, w_all, src_of, kinds, tables):
    n_g, n_rows, _ = xs.shape
    n_sec = len(kinds)
    rope = tables is not None
    n_src = w_all.shape[2] // SEC if src_of is _full_src else n_sec
    in_specs = [
        pl.BlockSpec((1, n_rows, D_MODEL), lambda b: (b, 0, 0)),
        _layer_spec(l, (1, D_MODEL)),
        _mod_spec(l, row_of, 0),
        _mod_spec(l, row_of, 1),
        pl.BlockSpec(memory_space=pl.ANY),
    ]
    args = [xs, g_pre, mod4, mod4, w_all]
    scratch = [pltpu.VMEM((n_rows, D_MODEL), BF16),
               pltpu.VMEM((2, n_rows, SEC), BF16),
               pltpu.VMEM((n_src, D_MODEL, SEC), BF16),
               pltpu.VMEM((2, D_MODEL, SEC), F32),
               pltpu.SemaphoreType.DMA((2,)),
               pltpu.SemaphoreType.DMA((2,))]
    if rope:
        in_specs.append(pl.BlockSpec(memory_space=pl.ANY))
        args.append(tables)
        scratch += [pltpu.VMEM(tables.shape, F32), pltpu.SemaphoreType.DMA((1,))]
    return pl.pallas_call(
        functools.partial(_proj_kernel, layer=l, kinds=kinds, src_of=src_of, n_src=n_src,
                          rope=rope, n_rows=n_rows, sub=PROJ_SUB,
                          unroll_sections=n_g > 1),
        grid=(n_g,),
        in_specs=in_specs,
        out_specs=pl.BlockSpec(memory_space=pl.ANY),
        out_shape=jax.ShapeDtypeStruct((n_g, n_sec, n_rows, SEC), BF16),
        scratch_shapes=scratch,
        compiler_params=pltpu.CompilerParams(vmem_limit_bytes=VMEM_LIMIT),
        name="projection_rope" if rope else "projection",
    )(*args)


def _ret_kernel(*refs, layer, n, n_ctx, rc, heads):
    if n_ctx:
        lg_ref, q_ref, g_ref, k_ref, v_ref, kc_ref, vc_ref, o_ref, inc_ref, st_ref = refs
    else:
        lg_ref, q_ref, g_ref, k_ref, v_ref, o_ref, inc_ref, st_ref = refs
    nc = n // rc
    hs = range(heads)
    cols = [slice(hh * LANES, (hh + 1) * LANES) for hh in hs]
    lgf = [lg_ref[layer, 0, pl.program_id(1) * heads + hh] for hh in hs]
    lgb = [lg_ref[layer, 1, pl.program_id(1) * heads + hh] for hh in hs]
    pos = lax.broadcasted_iota(jnp.int32, (rc, LANES), 0).astype(F32)
    kdf = [jnp.exp(lgf[hh] * (rc - 1.0 - pos)).astype(BF16) for hh in hs]
    kdb = [jnp.exp(lgb[hh] * pos).astype(BF16) for hh in hs]
    cdf = [jnp.exp(jnp.full((1, LANES), lgf[hh] * rc, F32)) for hh in hs]
    cdb = [jnp.exp(jnp.full((1, LANES), lgb[hh] * rc, F32)) for hh in hs]

    def increment(kr, vr, nn, hh):
        rows = slice(nn * rc, (nn + 1) * rc)
        kk = kr[0, rows, cols[hh]]
        kcat = jnp.concatenate([kk * kdf[hh], kk * kdb[hh]], axis=1)
        vt = vr[0, rows, cols[hh]].astype(F32).T.astype(BF16)
        return _dot(vt, kcat)

    sf = [jnp.zeros((HEAD_W, HEAD_W), F32) for _ in hs]
    sb = [jnp.zeros((HEAD_W, HEAD_W), F32) for _ in hs]
    if n_ctx:
        incs = [[increment(kc_ref, vc_ref, nn, hh) for hh in hs] for nn in range(n_ctx // rc)]
        for inc in incs:
            sf = [cdf[hh] * sf[hh] + inc[hh][:, :HEAD_W] for hh in hs]
        for inc in reversed(incs):
            sb = [cdb[hh] * sb[hh] + inc[hh][:, HEAD_W:] for hh in hs]

    for nn in range(nc):
        for hh in hs:
            inc_ref[hh, nn] = increment(k_ref, v_ref, nn, hh)
    for nn in range(nc):
        for hh in hs:
            st_ref[hh, nn, :, 0:HEAD_W] = sf[hh].astype(BF16)
            sf[hh] = cdf[hh] * sf[hh] + inc_ref[hh, nn, :, 0:HEAD_W]
    for nn in reversed(range(nc)):
        for hh in hs:
            st_ref[hh, nn, :, HEAD_W:] = sb[hh].astype(BF16)
            sb[hh] = cdb[hh] * sb[hh] + inc_ref[hh, nn, :, HEAD_W:]

    ii = lax.broadcasted_iota(jnp.int32, (rc, rc), 0)
    jj = lax.broadcasted_iota(jnp.int32, (rc, rc), 1)
    dist = (ii - jj).astype(F32)
    dmat = [jnp.where(dist >= 0, jnp.exp(lgf[hh] * jnp.maximum(dist, 0.0)), 0.0)
            + jnp.where(dist <= 0, jnp.exp(lgb[hh] * jnp.maximum(-dist, 0.0)), 0.0) for hh in hs]
    qdf = [jnp.exp(lgf[hh] * (pos + 1.0)).astype(BF16) for hh in hs]
    qdb = [jnp.exp(lgb[hh] * (rc - pos)).astype(BF16) for hh in hs]
    for nn in range(nc):
        rows = slice(nn * rc, (nn + 1) * rc)
        for hh in hs:
            cs = cols[hh]
            q = q_ref[0, rows, cs]
            s = _dot_nt(q, k_ref[0, rows, cs]) * dmat[hh]
            inner = _dot(s.astype(BF16), v_ref[0, rows, cs])
            qcat = jnp.concatenate([q * qdf[hh], q * qdb[hh]], axis=1)
            out = inner + _dot_nt(qcat, st_ref[hh, nn])
            o_ref[0, rows, cs] = _rms(out).astype(BF16) * g_ref[0, rows, cs]


def _retention(log_gamma, layer, z, sec, flat, n_b, n, zc, sec_c, n_ctx, *, rc, heads):
    width = heads * LANES
    head_block = lambda b, h: h

    in_specs = [pl.BlockSpec(memory_space=pltpu.SMEM)]
    in_specs += [_z_spec(sec[name], n, width, flat, col_of=head_block)
                 for name in ("rq", "rg", "rk", "rv")]
    args = [log_gamma, z, z, z, z]
    if n_ctx:
        in_specs += [_z_spec(sec_c[name], n_ctx, width, True, col_of=head_block)
                     for name in ("rk", "rv")]
        args += [zc, zc]
    return pl.pallas_call(
        functools.partial(_ret_kernel, layer=layer, n=n, n_ctx=n_ctx, rc=rc, heads=heads),
        grid=(n_b, HEADS // heads),
        in_specs=in_specs,
        out_specs=pl.BlockSpec((1, n, width), lambda b, h: (b, 0, h)),
        out_shape=jax.ShapeDtypeStruct((n_b, n, BRANCH_W), BF16),
        scratch_shapes=[pltpu.VMEM((heads, n // rc, HEAD_W, 2 * HEAD_W), F32),
                        pltpu.VMEM((heads, n // rc, HEAD_W, 2 * HEAD_W), BF16)],
        compiler_params=pltpu.CompilerParams(vmem_limit_bytes=VMEM_LIMIT),
        name="retention_ctx_init" if n_ctx else "retention",
    )(*args)


def _split_maps(q, lane):
    zero = jnp.zeros_like(q)
    return jnp.concatenate([jnp.where(lane < DIFF_DQK, q, zero),
                            jnp.where(lane >= DIFF_DQK, q, zero)], axis=0)


def _diff_ctx_kernel(lam_ref, q_ref, g_ref, k_ref, v_ref, o_ref, *, layer, tq, post_scale):
    lane = lax.broadcasted_iota(jnp.int32, (tq, LANES), 1)
    for h in range(HEADS):
        cs = slice(h * LANES, (h + 1) * LANES)
        s = _dot_nt(_split_maps(q_ref[0, :, cs], lane), k_ref[0, :, cs])
        p = jnp.exp2(s - jnp.max(s, axis=-1, keepdims=True))
        o = _dot(p.astype(BF16), v_ref[0, :, cs]) / jnp.sum(p, axis=-1, keepdims=True)
        d = o[:tq] - lam_ref[layer] * o[tq:]
        o_ref[0, :, cs] = (_rms(d) * post_scale * g_ref[0, :, cs].astype(F32)).astype(BF16)


def _diff_attention_ctx(lam, layer, zc, sec, n_b, n, *, post_scale):
    return pl.pallas_call(
        functools.partial(_diff_ctx_kernel, layer=layer, tq=n, post_scale=post_scale),
        grid=(n_b,),
        in_specs=[pl.BlockSpec(memory_space=pltpu.SMEM)]
        + [_z_spec(sec[name], n, SEC, True) for name in ("dq", "dg", "dk", "dv")],
        out_specs=pl.BlockSpec((1, n, BRANCH_W), lambda b: (b, 0, 0)),
        out_shape=jax.ShapeDtypeStruct((n_b, n, BRANCH_W), BF16),
        compiler_params=pltpu.CompilerParams(vmem_limit_bytes=VMEM_LIMIT),
        name="diff_attention_ctx",
    )(lam, zc, zc, zc, zc)


def _diff_pipe_kernel(lam_ref, q_ref, g_ref, kc_ref, vc_ref, kl_ref, vl_ref, o_ref,
                      kall, vall, s0, s1, p0, p1, *, layer, n_sub, ts, post_scale):
    n_ctx = kc_ref.shape[1]
    nk = kall.shape[0]
    kall[0:n_ctx, :] = kc_ref[0]
    kall[n_ctx:, :] = kl_ref[0]
    vall[0:n_ctx, 0:LANES] = vc_ref[0]
    vall[n_ctx:, 0:LANES] = vl_ref[0]
    vall[:, LANES:] = jnp.ones((nk, LANES), BF16)
    lam = lam_ref[layer]
    lane = lax.broadcasted_iota(jnp.int32, (ts, LANES), 1)
    s_bufs, p_bufs = (s0, s1), (p0, p1)

    def stage_a(t):
        rows = slice(t * ts, (t + 1) * ts)
        s_bufs[t % 2][...] = _dot_nt(_split_maps(q_ref[0, rows, :], lane), kall[...])

    def stage_b(t):
        s = s_bufs[t % 2][...]
        p_bufs[t % 2][...] = jnp.exp2(s - jnp.max(s, axis=-1, keepdims=True)).astype(BF16)

    def stage_c(t):
        rows = slice(t * ts, (t + 1) * ts)
        oe = _dot(p_bufs[t % 2][...], vall[...])
        o = oe[:, :LANES] / oe[:, LANES:]
        d = o[:ts] - lam * o[ts:]
        o_ref[0, rows, :] = (_rms(d) * post_scale * g_ref[0, rows, :].astype(F32)).astype(BF16)

    for t in range(n_sub + 2):
        if t >= 2:
            stage_c(t - 2)
        if 1 <= t <= n_sub:
            stage_b(t - 1)
        if t < n_sub:
            stage_a(t)


def _diff_attention_pipelined(lam, layer, z, sec, n_b, n, zc, sec_c, n_ctx, *, ts, post_scale):
    nk = n_ctx + n
    head_block = lambda b, h: h
    lat = lambda name: _z_spec(sec[name], n, LANES, False, col_of=head_block)
    cx = lambda name: _z_spec(sec_c[name], n_ctx, LANES, True, col_of=head_block)
    return pl.pallas_call(
        functools.partial(_diff_pipe_kernel, layer=layer, n_sub=n // ts, ts=ts,
                          post_scale=post_scale),
        grid=(n_b, HEADS),
        in_specs=[pl.BlockSpec(memory_space=pltpu.SMEM),
                  lat("dq"), lat("dg"), cx("dk"), cx("dv"), lat("dk"), lat("dv")],
        out_specs=pl.BlockSpec((1, n, LANES), lambda b, h: (b, 0, h)),
        out_shape=jax.ShapeDtypeStruct((n_b, n, BRANCH_W), BF16),
        scratch_shapes=[pltpu.VMEM((nk, LANES), BF16), pltpu.VMEM((nk, 2 * LANES), BF16),
                        pltpu.VMEM((2 * ts, nk), F32), pltpu.VMEM((2 * ts, nk), F32),
                        pltpu.VMEM((2 * ts, nk), BF16), pltpu.VMEM((2 * ts, nk), BF16)],
        compiler_params=pltpu.CompilerParams(vmem_limit_bytes=VMEM_LIMIT),
        name="diff_attention_pipelined",
    )(lam, z, z, zc, zc, z, z)


def _merge_kernel(x_ref, gate_ref, gpost_ref, oret_ref, odiff_ref, u_ref, vn_ref, mg_ref,
                  gr_ref, gm_ref, gd_ref, ws_ref, bs_ref, wbo_ref, wo_ref, out_ref, *, tm, out_sub):
    rows = []
    for c in range(tm // CHUNK):
        cols = []
        for g in range(HEADS):
            blk = vn_ref[0, c * CHUNK:(c + 1) * CHUNK, g * LANES:(g + 1) * LANES]
            cols.append(_dot(ws_ref[g], blk) + bs_ref[g])
        rows.append(jnp.concatenate(cols, axis=1))
    sp = jnp.concatenate(rows, axis=0)
    o_mlp = (u_ref[0].astype(F32) * sp * mg_ref[0].astype(F32)).astype(BF16)

    def gated(gate2_ref, o, w):
        gate = jnp.concatenate([gate2_ref[0, 0], gate2_ref[0, 1]], axis=1).astype(F32)
        return gate * _dot(o, w)

    t = (gated(gr_ref, oret_ref[0], wbo_ref[0]) + gated(gm_ref, o_mlp, wbo_ref[1])
         + gated(gd_ref, odiff_ref[0], wbo_ref[2]))
    tb = t.astype(BF16)
    for r in range(tm // out_sub):
        rs = slice(r * out_sub, (r + 1) * out_sub)
        y = _dot(tb[rs, :], wo_ref[...])
        out_ref[0, rs, :] = x_ref[0, rs, :] + gate_ref[...] * (_rms(y) * gpost_ref[...])


def _merge(xs, l, mod4, row_of, g_post, o_ret, o_diff, z, sec, ws, bs, wbo, wo, *, tm):
    n_b, n, _ = xs.shape
    row_block = lambda b, i: i

    def zsec(name):
        return _z_spec(sec[name], tm, SEC, False, row_of=row_block)

    def zgate(k):
        return pl.BlockSpec((1, 2, tm, SEC), lambda b, i: (b, k, i, 0))

    in_specs = [
        pl.BlockSpec((1, tm, D_MODEL), lambda b, i: (b, i, 0)),
        _mod_spec(l, row_of, 2),
        _layer_spec(l, (1, D_MODEL)),
        pl.BlockSpec((1, tm, BRANCH_W), lambda b, i: (b, i, 0)),
        pl.BlockSpec((1, tm, BRANCH_W), lambda b, i: (b, i, 0)),
        zsec("mu"), zsec("mv"), zsec("mg"), zgate(0), zgate(1), zgate(2),
        _layer_spec(l, (HEADS, CHUNK, CHUNK)),
        _layer_spec(l, (HEADS, CHUNK, LANES)),
        _layer_spec(l, (N_BRANCH, BRANCH_W, D_MODEL)),
        _layer_spec(l, (D_MODEL, D_MODEL)),
    ]
    return pl.pallas_call(
        functools.partial(_merge_kernel, tm=tm, out_sub=MERGE_OUT_SUB),
        grid=(n_b, n // tm),
        in_specs=in_specs,
        out_specs=pl.BlockSpec((1, tm, D_MODEL), lambda b, i: (b, i, 0)),
        out_shape=jax.ShapeDtypeStruct(xs.shape, F32),
        compiler_params=pltpu.CompilerParams(vmem_limit_bytes=VMEM_LIMIT),
        name="merge",
    )(xs, mod4, g_post, o_ret, o_diff, z, z, z, z, z, z, ws, bs, wbo, wo)


def _rope_tables(n_lat):
    rows = n_lat // GRID_W
    row_pos = jnp.repeat(jnp.arange(rows, dtype=F32), GRID_W)
    col_pos = jnp.tile(jnp.arange(GRID_W, dtype=F32), rows)

    def angles(head_dim):
        n_freq = head_dim // 4
        inv = ROPE_BASE ** (-jnp.arange(n_freq, dtype=F32) / n_freq)
        ang = jnp.concatenate([row_pos[:, None] * inv, col_pos[:, None] * inv], axis=-1)
        return jnp.cos(ang), jnp.sin(ang)

    cos_r, sin_r = angles(HEAD_W)
    cos_d, sin_d = angles(DIFF_DQK)
    zeros = jnp.zeros_like(sin_d)
    c_r = jnp.concatenate([cos_r, cos_r], axis=-1)
    s_r = jnp.concatenate([-sin_r, sin_r], axis=-1)
    c_d = jnp.tile(jnp.concatenate([cos_d, cos_d], axis=-1), (1, 2))
    s_lo = jnp.tile(jnp.concatenate([-sin_d, zeros], axis=-1), (1, 2))
    s_hi = jnp.tile(jnp.concatenate([zeros, sin_d], axis=-1), (1, 2))
    return c_r, s_r, c_d, s_lo, s_hi


@jax.jit
def kernel(x, c, ctx, c_ctx, w_mod, b_mod, g_pre, g_post, w_in, ret_decay_logit, mlp_w_s,
           mlp_b_s, diff_lambda_q, diff_lambda_k, w_branch_out, w_out):
    n_b, n_lat, _ = x.shape
    n_ctx = ctx.shape[1]
    tables = jnp.stack(_rope_tables(n_lat))

    cond_rows = 16
    ctx_row = n_b
    cc = jnp.zeros((cond_rows, D_MODEL), F32).at[:n_b].set(c).at[ctx_row].set(c_ctx)
    mod4 = _modulation(cc, w_mod, b_mod).reshape(DEPTH, cond_rows, 1, 3 * D_MODEL)
    lat_row = lambda b: b
    ctx_row_of = lambda b: ctx_row

    log_gamma = -jax.nn.softplus(-ret_decay_logit.astype(F32))
    lam_inits = [0.8 - 0.6 * math.exp(-0.3 * l) for l in range(DEPTH)]
    lam = (jnp.exp(jnp.sum(diff_lambda_q[:, 0] * diff_lambda_k[:, 0], axis=-1))
           - jnp.exp(jnp.sum(diff_lambda_q[:, 1] * diff_lambda_k[:, 1], axis=-1))
           + jnp.asarray(lam_inits, F32)).astype(F32)

    w_sec = w_in
    ws = mlp_w_s.astype(BF16)
    bs = jnp.broadcast_to(mlp_b_s[..., None], (DEPTH, HEADS, CHUNK, LANES)).astype(F32)
    wbo = w_branch_out.astype(BF16)
    wo = w_out.astype(BF16)
    g_pre3 = g_pre.reshape(DEPTH, 1, D_MODEL)
    g_post3 = g_post.reshape(DEPTH, 1, D_MODEL)

    for l in range(DEPTH):
        last = l == DEPTH - 1
        post_scale = 1.0 - lam_inits[l]

        ctx_flat = ctx.reshape(1, n_b * n_ctx, D_MODEL)
        if last:
            zc = _projection(ctx_flat, l, g_pre3, mod4, ctx_row_of, w_sec, lambda j: j,
                             KV_KINDS, None)
            sec_c = KV_SEC
        else:
            zc = _projection(ctx_flat, l, g_pre3, mod4, ctx_row_of, w_sec, _full_src,
                             FULL_KINDS, None)
            sec_c = FULL_SEC
            co_ret = _retention(log_gamma, l, zc, sec_c, True, n_b, n_ctx, None, None, 0,
                                rc=RET_CHUNK, heads=HEADS)
            co_diff = _diff_attention_ctx(lam, l, zc, sec_c, n_b, n_ctx, post_scale=post_scale)
            ctx_next = _merge(ctx_flat, l, mod4, ctx_row_of, g_post3,
                              co_ret.reshape(1, n_b * n_ctx, BRANCH_W),
                              co_diff.reshape(1, n_b * n_ctx, BRANCH_W), zc, sec_c,
                              ws, bs, wbo, wo, tm=MERGE_TM)
            ctx_next = ctx_next.reshape(n_b, n_ctx, D_MODEL)

        z = _projection(x, l, g_pre3, mod4, lat_row, w_sec, _full_src, FULL_KINDS, tables)
        o_ret = _retention(log_gamma, l, z, FULL_SEC, False, n_b, n_lat, zc, sec_c, n_ctx,
                           rc=RET_CHUNK, heads=RET_HEADS)
        o_diff = _diff_attention_pipelined(lam, l, z, FULL_SEC, n_b, n_lat, zc, sec_c, n_ctx,
                                           ts=DIFF_TS, post_scale=post_scale)
        x = _merge(x, l, mod4, lat_row, g_post3, o_ret, o_diff, z, FULL_SEC, ws, bs, wbo, wo,
                   tm=MERGE_TM)
        if not last:
            ctx = ctx_next
    return x
```

```python
import functools
import math

import jax
import jax.numpy as jnp
from jax import lax
from jax.experimental import pallas as pl
from jax.experimental.pallas import tpu as pltpu

F32 = jnp.float32
BF16 = jnp.bfloat16

D_MODEL = 1024
DEPTH = 2
GRID_W = 64
N_BRANCH = 3
BRANCH_W = D_MODEL // 2
HEADS = 4
HEAD_W = BRANCH_W // HEADS
DIFF_DQK = HEAD_W // 2
CHUNK = 128
ROPE_BASE = 10000.0
EPS = 1e-6
RET_SCALE = HEAD_W ** -0.5
DIFF_Q_SCALE = (DIFF_DQK ** -0.5) * math.log2(math.e)

LANES = 128
SEC = BRANCH_W
MERGE_COLS = N_BRANCH * D_MODEL
KV_COLS = 4 * BRANCH_W
IN_COLS = KV_COLS + 7 * BRANCH_W + MERGE_COLS
VMEM_LIMIT = 56 * 1024 * 1024

PROJ_SUB = 128
PROJ_OUT_SLOTS = 3
DIFF_TS = 256
RET_CHUNK = 256
RET_HEADS = 4
MERGE_TM = 512
MERGE_OUT_SUB = 512

FULL_KINDS = ("sigmoid",) * 6 + ("rk", "plain", "dk", "plain", "rq", "silu", "dq", "silu",
                                 "plain", "ln", "silu")
FULL_SEC = dict(rk=6, rv=7, dk=8, dv=9, rq=10, rg=11, dq=12, dg=13, mu=14, mv=15, mg=16)
KV_KINDS = ("rk", "plain", "dk", "plain")
KV_SEC = dict(rk=0, rv=1, dk=2, dv=3)
_N_MERGE_SEC = MERGE_COLS // SEC
_N_REST_SEC = (KV_COLS + 7 * BRANCH_W) // SEC


def _full_src(j):
    return jnp.where(j < _N_MERGE_SEC, j + _N_REST_SEC, j - _N_MERGE_SEC)


def _dot(a, b):
    return jnp.dot(a, b, preferred_element_type=F32)


def _dot_nt(a, b):
    return lax.dot_general(a, b, (((1,), (1,)), ((), ())), preferred_element_type=F32)


def _rms(v):
    return v * lax.rsqrt(jnp.mean(v * v, axis=-1, keepdims=True) + EPS)


def _mod_spec(l, row_of, k):
    return pl.BlockSpec((None, None, 1, D_MODEL), lambda *idx: (l, row_of(idx[0]), 0, k))


def _layer_spec(l, shape, **kwargs):
    zeros = (0,) * len(shape)
    return pl.BlockSpec((None,) + tuple(shape), lambda *idx: (l,) + zeros, **kwargs)


def _z_spec(sec_idx, rows, width, flat, col_of=lambda *idx: 0, row_of=None):
    if row_of is None:
        row_of = (lambda *idx: idx[0]) if flat else (lambda *idx: 0)
    group_of = (lambda *idx: 0) if flat else (lambda *idx: idx[0])
    return pl.BlockSpec((1, None, rows, width),
                        lambda *idx: (group_of(*idx), sec_idx, row_of(*idx), col_of(*idx)))


def _split_bf16(a):
    hi = a.astype(BF16)
    return hi, (a - hi.astype(F32)).astype(BF16)


def _mod_kernel(c_ref, w_ref, b_ref, o_ref):
    cv = c_ref[...]
    c_hi, c_lo = _split_bf16(cv * jax.nn.sigmoid(cv))
    w_hi, w_lo = _split_bf16(w_ref[0])
    o_ref[0] = _dot(c_hi, w_hi) + _dot(c_hi, w_lo) + _dot(c_lo, w_hi) + b_ref[0]


def _modulation(cc, w_mod, b_mod):
    rows = cc.shape[0]
    return pl.pallas_call(
        _mod_kernel,
        grid=(DEPTH, 3),
        in_specs=[
            pl.BlockSpec((rows, D_MODEL), lambda l, j: (0, 0)),
            pl.BlockSpec((1, D_MODEL, D_MODEL), lambda l, j: (l, 0, j)),
            pl.BlockSpec((1, 1, D_MODEL), lambda l, j: (l, 0, j)),
        ],
        out_specs=pl.BlockSpec((1, rows, D_MODEL), lambda l, j: (l, 0, j)),
        out_shape=jax.ShapeDtypeStruct((DEPTH, rows, 3 * D_MODEL), F32),
        compiler_params=pltpu.CompilerParams(vmem_limit_bytes=VMEM_LIMIT),
        name="modulation",
    )(cc, w_mod, b_mod.reshape(DEPTH, 1, 3 * D_MODEL))


def _rope_ret(a, c, s):
    outs = []
    for h in range(HEADS):
        ah = a[:, h * LANES:(h + 1) * LANES]
        outs.append(ah * c + pltpu.roll(ah, LANES // 2, 1) * s)
    return jnp.concatenate(outs, axis=1)


def _rope_diff(a, c, s_lo, s_hi):
    outs = []
    for h in range(HEADS):
        ah = a[:, h * LANES:(h + 1) * LANES]
        outs.append(ah * c + pltpu.roll(ah, LANES - DIFF_DQK // 2, 1) * s_lo
                    + pltpu.roll(ah, DIFF_DQK // 2, 1) * s_hi)
    return jnp.concatenate(outs, axis=1)


def _proj_kernel(*refs, layer, kinds, src_of, n_src, rope, n_rows, sub):
    if rope:
        (x_ref, g_ref, shift_ref, scale_ref, w_hbm, tab_hbm, z_hbm,
         h_ref, stage, w_vmem, w_stage, sems_out, sems_w, tab_vmem, sem_tab) = refs
        cr_ref, sr_ref, cd_ref, sdl_ref, sdh_ref = [tab_vmem.at[i] for i in range(5)]
    else:
        (x_ref, g_ref, shift_ref, scale_ref, w_hbm, z_hbm,
         h_ref, stage, w_vmem, w_stage, sems_out, sems_w) = refs
    b = pl.program_id(0)
    first = b == 0
    n_sub = n_rows // sub
    n_sec = len(kinds)
    n_slots = stage.shape[0]

    n_wslots = w_stage.shape[0]

    def weight_copy(j):
        start = src_of(j) * SEC
        cols = pl.ds(start if isinstance(start, int) else pl.multiple_of(start, SEC), SEC)
        slot = j % n_wslots
        return pltpu.make_async_copy(w_hbm.at[layer, :, cols], w_stage.at[slot], sems_w.at[slot])

    def load_weights(j):
        weight_copy(j).wait()
        src = w_stage.at[j % n_wslots]
        dst = w_vmem.at[src_of(j)]
        for r in range(D_MODEL // LANES):
            rows = slice(r * LANES, (r + 1) * LANES)
            dst[rows, :] = src[rows, :].astype(BF16)

        @pl.when(j + n_wslots < n_sec)
        def _():
            weight_copy(j + n_wslots).start()

    def table_copy():
        return pltpu.make_async_copy(tab_hbm, tab_vmem, sem_tab.at[0])

    @pl.when(first)
    def _():
        for j in range(min(n_wslots, n_sec)):
            weight_copy(j).start()
        if rope:
            table_copy().start()

    gs = g_ref[...] * (1.0 + scale_ref[...])
    sh = shift_ref[...]
    for r in range(n_sub):
        xs = x_ref[0, r * sub:(r + 1) * sub, :]
        h_ref[r * sub:(r + 1) * sub, :] = (_rms(xs) * gs + sh).astype(BF16)

    if rope:
        pl.when(first)(lambda: table_copy().wait())

    def epi_rk(a, rows):
        if rope:
            a = _rope_ret(a, cr_ref[rows, :], sr_ref[rows, :])
        return a * RET_SCALE

    def epi_rq(a, rows):
        return _rope_ret(a, cr_ref[rows, :], sr_ref[rows, :]) if rope else a

    def epi_dk(a, rows):
        return _rope_diff(a, cd_ref[rows, :], sdl_ref[rows, :], sdh_ref[rows, :]) if rope else a

    def epi_dq(a, rows):
        return epi_dk(a, rows) * DIFF_Q_SCALE

    def epi_ln(a, rows):
        mu = jnp.mean(a, axis=-1, keepdims=True)
        d = a - mu
        return d * lax.rsqrt(jnp.mean(d * d, axis=-1, keepdims=True) + EPS)

    epilogues = dict(
        plain=lambda a, rows: a,
        sigmoid=lambda a, rows: jax.nn.sigmoid(a),
        silu=lambda a, rows: a * jax.nn.sigmoid(a),
        rk=epi_rk, rq=epi_rq, dk=epi_dk, dq=epi_dq, ln=epi_ln)

    def writeback(j):
        slot = j % n_slots
        return pltpu.make_async_copy(stage.at[slot], z_hbm.at[b, j], sems_out.at[slot])

    def run(epilogue, j):
        w_sec = w_vmem.at[src_of(j)]
        out = stage.at[j % n_slots]
        for r in range(n_sub):
            rows = slice(r * sub, (r + 1) * sub)
            out[rows, :] = epilogue(_dot(h_ref[rows, :], w_sec[...]), rows).astype(BF16)

    def section(j, carry):
        pl.when(first)(functools.partial(load_weights, j))

        @pl.when(j >= n_slots)
        def _():
            writeback(j - n_slots).wait()

        for kind in sorted(set(kinds)):
            idx = [i for i, k in enumerate(kinds) if k == kind]
            cond = j == idx[0]
            for i in idx[1:]:
                cond = cond | (j == i)
            pl.when(cond)(functools.partial(run, epilogues[kind], j))
        writeback(j).start()
        return carry

    lax.fori_loop(0, n_sec, section, 0)
    for j in range(max(0, n_sec - n_slots), n_sec):
        writeback(j).wait()


def _projection(xs, l, g_pre, mod4, row_of, w_all, src_of, kinds, tables):
    n_g, n_rows, _ = xs.shape
    n_sec = len(kinds)
    rope = tables is not None
    n_src = w_all.shape[2] // SEC if src_of is _full_src else n_sec
    in_specs = [
        pl.BlockSpec((1, n_rows, D_MODEL), lambda b: (b, 0, 0)),
        _layer_spec(l, (1, D_MODEL)),
        _mod_spec(l, row_of, 0),
        _mod_spec(l, row_of, 1),
        pl.BlockSpec(memory_space=pl.ANY),
    ]
    args = [xs, g_pre, mod4, mod4, w_all]
    scratch = [pltpu.VMEM((n_rows, D_MODEL), BF16),
               pltpu.VMEM((PROJ_OUT_SLOTS, n_rows, SEC), BF16),
               pltpu.VMEM((n_src, D_MODEL, SEC), BF16),
               pltpu.VMEM((2, D_MODEL, SEC), F32),
               pltpu.SemaphoreType.DMA((PROJ_OUT_SLOTS,)),
               pltpu.SemaphoreType.DMA((2,))]
    if rope:
        in_specs.append(pl.BlockSpec(memory_space=pl.ANY))
        args.append(tables)
        scratch += [pltpu.VMEM(tables.shape, F32), pltpu.SemaphoreType.DMA((1,))]
    return pl.pallas_call(
        functools.partial(_proj_kernel, layer=l, kinds=kinds, src_of=src_of, n_src=n_src,
                          rope=rope, n_rows=n_rows, sub=PROJ_SUB),
        grid=(n_g,),
        in_specs=in_specs,
        out_specs=pl.BlockSpec(memory_space=pl.ANY),
        out_shape=jax.ShapeDtypeStruct((n_g, n_sec, n_rows, SEC), BF16),
        scratch_shapes=scratch,
        compiler_params=pltpu.CompilerParams(vmem_limit_bytes=VMEM_LIMIT),
        name="projection_rope" if rope else "projection",
    )(*args)


def _ret_kernel(*refs, layer, n, n_ctx, rc, heads):
    if n_ctx:
        lg_ref, q_ref, g_ref, k_ref, v_ref, kc_ref, vc_ref, o_ref, inc_ref, st_ref = refs
    else:
        lg_ref, q_ref, g_ref, k_ref, v_ref, o_ref, inc_ref, st_ref = refs
    nc = n // rc
    hs = range(heads)
    cols = [slice(hh * LANES, (hh + 1) * LANES) for hh in hs]
    lgf = [lg_ref[layer, 0, pl.program_id(1) * heads + hh] for hh in hs]
    lgb = [lg_ref[layer, 1, pl.program_id(1) * heads + hh] for hh in hs]
    pos = lax.broadcasted_iota(jnp.int32, (rc, LANES), 0).astype(F32)
    kdf = [jnp.exp(lgf[hh] * (rc - 1.0 - pos)).astype(BF16) for hh in hs]
    kdb = [jnp.exp(lgb[hh] * pos).astype(BF16) for hh in hs]
    cdf = [jnp.exp(jnp.full((1, LANES), lgf[hh] * rc, F32)) for hh in hs]
    cdb = [jnp.exp(jnp.full((1, LANES), lgb[hh] * rc, F32)) for hh in hs]

    def increment(kr, vr, nn, hh):
        rows = slice(nn * rc, (nn + 1) * rc)
        kk = kr[0, rows, cols[hh]]
        kcat = jnp.concatenate([kk * kdf[hh], kk * kdb[hh]], axis=1)
        vt = vr[0, rows, cols[hh]].astype(F32).T.astype(BF16)
        return _dot(vt, kcat)

    sf = [jnp.zeros((HEAD_W, HEAD_W), F32) for _ in hs]
    sb = [jnp.zeros((HEAD_W, HEAD_W), F32) for _ in hs]
    if n_ctx:
        incs = [[increment(kc_ref, vc_ref, nn, hh) for hh in hs] for nn in range(n_ctx // rc)]
        for inc in incs:
            sf = [cdf[hh] * sf[hh] + inc[hh][:, :HEAD_W] for hh in hs]
        for inc in reversed(incs):
            sb = [cdb[hh] * sb[hh] + inc[hh][:, HEAD_W:] for hh in hs]

    for nn in range(nc):
        for hh in hs:
            inc_ref[hh, nn] = increment(k_ref, v_ref, nn, hh)
    for nn in range(nc):
        for hh in hs:
            st_ref[hh, nn, :, 0:HEAD_W] = sf[hh].astype(BF16)
            sf[hh] = cdf[hh] * sf[hh] + inc_ref[hh, nn, :, 0:HEAD_W]
    for nn in reversed(range(nc)):
        for hh in hs:
            st_ref[hh, nn, :, HEAD_W:] = sb[hh].astype(BF16)
            sb[hh] = cdb[hh] * sb[hh] + inc_ref[hh, nn, :, HEAD_W:]

    ii = lax.broadcasted_iota(jnp.int32, (rc, rc), 0)
    jj = lax.broadcasted_iota(jnp.int32, (rc, rc), 1)
    dist = (ii - jj).astype(F32)
    dmat = [jnp.where(dist >= 0, jnp.exp(lgf[hh] * jnp.maximum(dist, 0.0)), 0.0)
            + jnp.where(dist <= 0, jnp.exp(lgb[hh] * jnp.maximum(-dist, 0.0)), 0.0) for hh in hs]
    qdf = [jnp.exp(lgf[hh] * (pos + 1.0)).astype(BF16) for hh in hs]
    qdb = [jnp.exp(lgb[hh] * (rc - pos)).astype(BF16) for hh in hs]
    for nn in range(nc):
        rows = slice(nn * rc, (nn + 1) * rc)
        for hh in hs:
            cs = cols[hh]
            q = q_ref[0, rows, cs]
            s = _dot_nt(q, k_ref[0, rows, cs]) * dmat[hh]
            inner = _dot(s.astype(BF16), v_ref[0, rows, cs])
            qcat = jnp.concatenate([q * qdf[hh], q * qdb[hh]], axis=1)
            out = inner + _dot_nt(qcat, st_ref[hh, nn])
            o_ref[0, rows, cs] = _rms(out).astype(BF16) * g_ref[0, rows, cs]


def _retention(log_gamma, layer, z, sec, flat, n_b, n, zc, sec_c, n_ctx, *, rc, heads):
    width = heads * LANES
    head_block = lambda b, h: h

    in_specs = [pl.BlockSpec(memory_space=pltpu.SMEM)]
    in_specs += [_z_spec(sec[name], n, width, flat, col_of=head_block)
                 for name in ("rq", "rg", "rk", "rv")]
    args = [log_gamma, z, z, z, z]
    if n_ctx:
        in_specs += [_z_spec(sec_c[name], n_ctx, width, True, col_of=head_block)
                     for name in ("rk", "rv")]
        args += [zc, zc]
    return pl.pallas_call(
        functools.partial(_ret_kernel, layer=layer, n=n, n_ctx=n_ctx, rc=rc, heads=heads),
        grid=(n_b, HEADS // heads),
        in_specs=in_specs,
        out_specs=pl.BlockSpec((1, n, width), lambda b, h: (b, 0, h)),
        out_shape=jax.ShapeDtypeStruct((n_b, n, BRANCH_W), BF16),
        scratch_shapes=[pltpu.VMEM((heads, n // rc, HEAD_W, 2 * HEAD_W), F32),
                        pltpu.VMEM((heads, n // rc, HEAD_W, 2 * HEAD_W), BF16)],
        compiler_params=pltpu.CompilerParams(vmem_limit_bytes=VMEM_LIMIT),
        name="retention_ctx_init" if n_ctx else "retention",
    )(*args)


def _split_maps(q, lane):
    zero = jnp.zeros_like(q)
    return jnp.concatenate([jnp.where(lane < DIFF_DQK, q, zero),
                            jnp.where(lane >= DIFF_DQK, q, zero)], axis=0)


def _diff_ctx_kernel(lam_ref, q_ref, g_ref, k_ref, v_ref, o_ref, *, layer, tq, post_scale):
    lane = lax.broadcasted_iota(jnp.int32, (tq, LANES), 1)
    for h in range(HEADS):
        cs = slice(h * LANES, (h + 1) * LANES)
        s = _dot_nt(_split_maps(q_ref[0, :, cs], lane), k_ref[0, :, cs])
        p = jnp.exp2(s - jnp.max(s, axis=-1, keepdims=True))
        o = _dot(p.astype(BF16), v_ref[0, :, cs]) / jnp.sum(p, axis=-1, keepdims=True)
        d = o[:tq] - lam_ref[layer] * o[tq:]
        o_ref[0, :, cs] = (_rms(d) * post_scale * g_ref[0, :, cs].astype(F32)).astype(BF16)


def _diff_attention_ctx(lam, layer, zc, sec, n_b, n, *, post_scale):
    return pl.pallas_call(
        functools.partial(_diff_ctx_kernel, layer=layer, tq=n, post_scale=post_scale),
        grid=(n_b,),
        in_specs=[pl.BlockSpec(memory_space=pltpu.SMEM)]
        + [_z_spec(sec[name], n, SEC, True) for name in ("dq", "dg", "dk", "dv")],
        out_specs=pl.BlockSpec((1, n, BRANCH_W), lambda b: (b, 0, 0)),
        out_shape=jax.ShapeDtypeStruct((n_b, n, BRANCH_W), BF16),
        compiler_params=pltpu.CompilerParams(vmem_limit_bytes=VMEM_LIMIT),
        name="diff_attention_ctx",
    )(lam, zc, zc, zc, zc)


def _diff_pipe_kernel(lam_ref, q_ref, g_ref, kc_ref, vc_ref, kl_ref, vl_ref, o_ref,
                      kall, vall, s0, s1, p0, p1, *, layer, n_sub, ts, post_scale):
    n_ctx = kc_ref.shape[1]
    nk = kall.shape[0]
    kall[0:n_ctx, :] = kc_ref[0]
    kall[n_ctx:, :] = kl_ref[0]
    vall[0:n_ctx, 0:LANES] = vc_ref[0]
    vall[n_ctx:, 0:LANES] = vl_ref[0]
    vall[:, LANES:] = jnp.ones((nk, LANES), BF16)
    lam = lam_ref[layer]
    lane = lax.broadcasted_iota(jnp.int32, (ts, LANES), 1)
    s_bufs, p_bufs = (s0, s1), (p0, p1)

    def stage_a(t):
        rows = slice(t * ts, (t + 1) * ts)
        s_bufs[t % 2][...] = _dot_nt(_split_maps(q_ref[0, rows, :], lane), kall[...])

    def stage_b(t):
        s = s_bufs[t % 2][...]
        p_bufs[t % 2][...] = jnp.exp2(s - jnp.max(s, axis=-1, keepdims=True)).astype(BF16)

    def stage_c(t):
        rows = slice(t * ts, (t + 1) * ts)
        oe = _dot(p_bufs[t % 2][...], vall[...])
        o = oe[:, :LANES] / oe[:, LANES:]
        d = o[:ts] - lam * o[ts:]
        o_ref[0, rows, :] = (_rms(d) * post_scale * g_ref[0, rows, :].astype(F32)).astype(BF16)

    for t in range(n_sub + 2):
        if t >= 2:
            stage_c(t - 2)
        if 1 <= t <= n_sub:
            stage_b(t - 1)
        if t < n_sub:
            stage_a(t)


def _diff_attention_pipelined(lam, layer, z, sec, n_b, n, zc, sec_c, n_ctx, *, ts, post_scale):
    nk = n_ctx + n
    head_block = lambda b, h: h
    lat = lambda name: _z_spec(sec[name], n, LANES, False, col_of=head_block)
    cx = lambda name: _z_spec(sec_c[name], n_ctx, LANES, True, col_of=head_block)
    return pl.pallas_call(
        functools.partial(_diff_pipe_kernel, layer=layer, n_sub=n // ts, ts=ts,
                          post_scale=post_scale),
        grid=(n_b, HEADS),
        in_specs=[pl.BlockSpec(memory_space=pltpu.SMEM),
                  lat("dq"), lat("dg"), cx("dk"), cx("dv"), lat("dk"), lat("dv")],
        out_specs=pl.BlockSpec((1, n, LANES), lambda b, h: (b, 0, h)),
        out_shape=jax.ShapeDtypeStruct((n_b, n, BRANCH_W), BF16),
        scratch_shapes=[pltpu.VMEM((nk, LANES), BF16), pltpu.VMEM((nk, 2 * LANES), BF16),
                        pltpu.VMEM((2 * ts, nk), F32), pltpu.VMEM((2 * ts, nk), F32),
                        pltpu.VMEM((2 * ts, nk), BF16), pltpu.VMEM((2 * ts, nk), BF16)],
        compiler_params=pltpu.CompilerParams(vmem_limit_bytes=VMEM_LIMIT),
        name="diff_attention_pipelined",
    )(lam, z, z, zc, zc, z, z)


def _merge_kernel(x_ref, gate_ref, gpost_ref, oret_ref, odiff_ref, u_ref, vn_ref, mg_ref,
                  gr_ref, gm_ref, gd_ref, ws_ref, bs_ref, wbo_ref, wo_ref, out_ref, *, tm, out_sub):
    rows = []
    for c in range(tm // CHUNK):
        cols = []
        for g in range(HEADS):
            blk = vn_ref[0, c * CHUNK:(c + 1) * CHUNK, g * LANES:(g + 1) * LANES]
            cols.append(_dot(ws_ref[g], blk) + bs_ref[g])
        rows.append(jnp.concatenate(cols, axis=1))
    sp = jnp.concatenate(rows, axis=0)
    o_mlp = (u_ref[0].astype(F32) * sp * mg_ref[0].astype(F32)).astype(BF16)

    def gated(gate2_ref, o, w):
        gate = jnp.concatenate([gate2_ref[0, 0], gate2_ref[0, 1]], axis=1).astype(F32)
        return gate * _dot(o, w)

    t = (gated(gr_ref, oret_ref[0], wbo_ref[0]) + gated(gm_ref, o_mlp, wbo_ref[1])
         + gated(gd_ref, odiff_ref[0], wbo_ref[2]))
    tb = t.astype(BF16)
    for r in range(tm // out_sub):
        rs = slice(r * out_sub, (r + 1) * out_sub)
        y = _dot(tb[rs, :], wo_ref[...])
        out_ref[0, rs, :] = x_ref[0, rs, :] + gate_ref[...] * (_rms(y) * gpost_ref[...])


def _merge(xs, l, mod4, row_of, g_post, o_ret, o_diff, z, sec, ws, bs, wbo, wo, *, tm):
    n_b, n, _ = xs.shape
    row_block = lambda b, i: i

    def zsec(name):
        return _z_spec(sec[name], tm, SEC, False, row_of=row_block)

    def zgate(k):
        return pl.BlockSpec((1, 2, tm, SEC), lambda b, i: (b, k, i, 0))

    in_specs = [
        pl.BlockSpec((1, tm, D_MODEL), lambda b, i: (b, i, 0)),
        _mod_spec(l, row_of, 2),
        _layer_spec(l, (1, D_MODEL)),
        pl.BlockSpec((1, tm, BRANCH_W), lambda b, i: (b, i, 0)),
        pl.BlockSpec((1, tm, BRANCH_W), lambda b, i: (b, i, 0)),
        zsec("mu"), zsec("mv"), zsec("mg"), zgate(0), zgate(1), zgate(2),
        _layer_spec(l, (HEADS, CHUNK, CHUNK)),
        _layer_spec(l, (HEADS, CHUNK, LANES)),
        _layer_spec(l, (N_BRANCH, BRANCH_W, D_MODEL)),
        _layer_spec(l, (D_MODEL, D_MODEL)),
    ]
    return pl.pallas_call(
        functools.partial(_merge_kernel, tm=tm, out_sub=MERGE_OUT_SUB),
        grid=(n_b, n // tm),
        in_specs=in_specs,
        out_specs=pl.BlockSpec((1, tm, D_MODEL), lambda b, i: (b, i, 0)),
        out_shape=jax.ShapeDtypeStruct(xs.shape, F32),
        compiler_params=pltpu.CompilerParams(vmem_limit_bytes=VMEM_LIMIT),
        name="merge",
    )(xs, mod4, g_post, o_ret, o_diff, z, z, z, z, z, z, ws, bs, wbo, wo)


def _rope_tables(n_lat):
    rows = n_lat // GRID_W
    row_pos = jnp.repeat(jnp.arange(rows, dtype=F32), GRID_W)
    col_pos = jnp.tile(jnp.arange(GRID_W, dtype=F32), rows)

    def angles(head_dim):
        n_freq = head_dim // 4
        inv = ROPE_BASE ** (-jnp.arange(n_freq, dtype=F32) / n_freq)
        ang = jnp.concatenate([row_pos[:, None] * inv, col_pos[:, None] * inv], axis=-1)
        return jnp.cos(ang), jnp.sin(ang)

    cos_r, sin_r = angles(HEAD_W)
    cos_d, sin_d = angles(DIFF_DQK)
    zeros = jnp.zeros_like(sin_d)
    c_r = jnp.concatenate([cos_r, cos_r], axis=-1)
    s_r = jnp.concatenate([-sin_r, sin_r], axis=-1)
    c_d = jnp.tile(jnp.concatenate([cos_d, cos_d], axis=-1), (1, 2))
    s_lo = jnp.tile(jnp.concatenate([-sin_d, zeros], axis=-1), (1, 2))
    s_hi = jnp.tile(jnp.concatenate([zeros, sin_d], axis=-1), (1, 2))
    return c_r, s_r, c_d, s_lo, s_hi


@jax.jit
def kernel(x, c, ctx, c_ctx, w_mod, b_mod, g_pre, g_post, w_in, ret_decay_logit, mlp_w_s,
           mlp_b_s, diff_lambda_q, diff_lambda_k, w_branch_out, w_out):
    n_b, n_lat, _ = x.shape
    n_ctx = ctx.shape[1]
    tables = jnp.stack(_rope_tables(n_lat))

    cond_rows = 16
    ctx_row = n_b
    cc = jnp.zeros((cond_rows, D_MODEL), F32).at[:n_b].set(c).at[ctx_row].set(c_ctx)
    mod4 = _modulation(cc, w_mod, b_mod).reshape(DEPTH, cond_rows, 1, 3 * D_MODEL)
    lat_row = lambda b: b
    ctx_row_of = lambda b: ctx_row

    log_gamma = -jax.nn.softplus(-ret_decay_logit.astype(F32))
    lam_inits = [0.8 - 0.6 * math.exp(-0.3 * l) for l in range(DEPTH)]
    lam = (jnp.exp(jnp.sum(diff_lambda_q[:, 0] * diff_lambda_k[:, 0], axis=-1))
           - jnp.exp(jnp.sum(diff_lambda_q[:, 1] * diff_lambda_k[:, 1], axis=-1))
           + jnp.asarray(lam_inits, F32)).astype(F32)

    w_sec = w_in
    ws = mlp_w_s.astype(BF16)
    bs = jnp.broadcast_to(mlp_b_s[..., None], (DEPTH, HEADS, CHUNK, LANES)).astype(F32)
    wbo = w_branch_out.astype(BF16)
    wo = w_out.astype(BF16)
    g_pre3 = g_pre.reshape(DEPTH, 1, D_MODEL)
    g_post3 = g_post.reshape(DEPTH, 1, D_MODEL)

    for l in range(DEPTH):
        last = l == DEPTH - 1
        post_scale = 1.0 - lam_inits[l]

        ctx_flat = ctx.reshape(1, n_b * n_ctx, D_MODEL)
        if last:
            zc = _projection(ctx_flat, l, g_pre3, mod4, ctx_row_of, w_sec, lambda j: j,
                             KV_KINDS, None)
            sec_c = KV_SEC
        else:
            zc = _projection(ctx_flat, l, g_pre3, mod4, ctx_row_of, w_sec, _full_src,
                             FULL_KINDS, None)
            sec_c = FULL_SEC
            co_ret = _retention(log_gamma, l, zc, sec_c, True, n_b, n_ctx, None, None, 0,
                                rc=RET_CHUNK, heads=HEADS)
            co_diff = _diff_attention_ctx(lam, l, zc, sec_c, n_b, n_ctx, post_scale=post_scale)
            ctx_next = _merge(ctx_flat, l, mod4, ctx_row_of, g_post3,
                              co_ret.reshape(1, n_b * n_ctx, BRANCH_W),
                              co_diff.reshape(1, n_b * n_ctx, BRANCH_W), zc, sec_c,
                              ws, bs, wbo, wo, tm=MERGE_TM)
            ctx_next = ctx_next.reshape(n_b, n_ctx, D_MODEL)

        z = _projection(x, l, g_pre3, mod4, lat_row, w_sec, _full_src, FULL_KINDS, tables)
        o_ret = _retention(log_gamma, l, z, FULL_SEC, False, n_b, n_lat, zc, sec_c, n_ctx,
                           rc=RET_CHUNK, heads=RET_HEADS)
        o_diff = _diff_attention_pipelined(lam, l, z, FULL_SEC, n_b, n_lat, zc, sec_c, n_ctx,
                                           ts=DIFF_TS, post_scale=post_scale)
        x = _merge(x, l, mod4, lat_row, g_post3, o_ret, o_diff, z, FULL_SEC, ws, bs, wbo, wo,
                   tm=MERGE_TM)
        if not last:
            ctx = ctx_next
    return x
```

```python
import functools
import math

import jax
import jax.numpy as jnp
from jax import lax
from jax.experimental import pallas as pl
from jax.experimental.pallas import tpu as pltpu

F32 = jnp.float32
BF16 = jnp.bfloat16

D_MODEL = 1024
DEPTH = 2
GRID_W = 64
N_BRANCH = 3
BRANCH_W = D_MODEL // 2
HEADS = 4
HEAD_W = BRANCH_W // HEADS
DIFF_DQK = HEAD_W // 2
CHUNK = 128
ROPE_BASE = 10000.0
EPS = 1e-6
RET_SCALE = HEAD_W ** -0.5
DIFF_Q_SCALE = (DIFF_DQK ** -0.5) * math.log2(math.e)

LANES = 128
SEC = BRANCH_W
MERGE_COLS = N_BRANCH * D_MODEL
KV_COLS = 4 * BRANCH_W
IN_COLS = KV_COLS + 7 * BRANCH_W + MERGE_COLS
VMEM_LIMIT = 56 * 1024 * 1024

PROJ_SUB = 128
PROJ_OUT_SLOTS = 2
DIFF_TS = 256
RET_CHUNK = 256
MERGE_TM = 512
MERGE_OUT_SUB = 512

FULL_KINDS = ("sigmoid",) * 6 + ("rk", "plain", "dk", "plain", "rq", "silu", "dq", "silu",
                                 "plain", "ln", "silu")
FULL_SEC = dict(rk=6, rv=7, dk=8, dv=9, rq=10, rg=11, dq=12, dg=13, mu=14, mv=15, mg=16)
KV_KINDS = ("rk", "plain", "dk", "plain")
KV_SEC = dict(rk=0, rv=1, dk=2, dv=3)
_N_MERGE_SEC = MERGE_COLS // SEC
_N_REST_SEC = (KV_COLS + 7 * BRANCH_W) // SEC


def _full_src(j):
    return jnp.where(j < _N_MERGE_SEC, j + _N_REST_SEC, j - _N_MERGE_SEC)


def _dot(a, b):
    return jnp.dot(a, b, preferred_element_type=F32)


def _dot_nt(a, b):
    return lax.dot_general(a, b, (((1,), (1,)), ((), ())), preferred_element_type=F32)


def _rms(v):
    return v * lax.rsqrt(jnp.mean(v * v, axis=-1, keepdims=True) + EPS)


def _mod_spec(l, row_of, k):
    return pl.BlockSpec((None, None, 1, D_MODEL), lambda *idx: (l, row_of(idx[0]), 0, k))


def _layer_spec(l, shape, **kwargs):
    zeros = (0,) * len(shape)
    return pl.BlockSpec((None,) + tuple(shape), lambda *idx: (l,) + zeros, **kwargs)


def _z_spec(sec_idx, rows, width, flat, col_of=lambda *idx: 0, row_of=None):
    if row_of is None:
        row_of = (lambda *idx: idx[0]) if flat else (lambda *idx: 0)
    group_of = (lambda *idx: 0) if flat else (lambda *idx: idx[0])
    return pl.BlockSpec((1, None, rows, width),
                        lambda *idx: (group_of(*idx), sec_idx, row_of(*idx), col_of(*idx)))


def _split_bf16(a):
    hi = a.astype(BF16)
    return hi, (a - hi.astype(F32)).astype(BF16)


def _mod_kernel(c_ref, w_ref, b_ref, o_ref):
    cv = c_ref[...]
    c_hi, c_lo = _split_bf16(cv * jax.nn.sigmoid(cv))
    w_hi, w_lo = _split_bf16(w_ref[0])
    o_ref[0] = _dot(c_hi, w_hi) + _dot(c_hi, w_lo) + _dot(c_lo, w_hi) + b_ref[0]


def _modulation(cc, w_mod, b_mod):
    rows = cc.shape[0]
    return pl.pallas_call(
        _mod_kernel,
        grid=(DEPTH, 3),
        in_specs=[
            pl.BlockSpec((rows, D_MODEL), lambda l, j: (0, 0)),
            pl.BlockSpec((1, D_MODEL, D_MODEL), lambda l, j: (l, 0, j)),
            pl.BlockSpec((1, 1, D_MODEL), lambda l, j: (l, 0, j)),
        ],
        out_specs=pl.BlockSpec((1, rows, D_MODEL), lambda l, j: (l, 0, j)),
        out_shape=jax.ShapeDtypeStruct((DEPTH, rows, 3 * D_MODEL), F32),
        compiler_params=pltpu.CompilerParams(vmem_limit_bytes=VMEM_LIMIT),
        name="modulation",
    )(cc, w_mod, b_mod.reshape(DEPTH, 1, 3 * D_MODEL))


def _rope_ret(a, c, s):
    outs = []
    for h in range(HEADS):
        ah = a[:, h * LANES:(h + 1) * LANES]
        outs.append(ah * c + pltpu.roll(ah, LANES // 2, 1) * s)
    return jnp.concatenate(outs, axis=1)


def _rope_diff(a, c, s_lo, s_hi):
    outs = []
    for h in range(HEADS):
        ah = a[:, h * LANES:(h + 1) * LANES]
        outs.append(ah * c + pltpu.roll(ah, LANES - DIFF_DQK // 2, 1) * s_lo
                    + pltpu.roll(ah, DIFF_DQK // 2, 1) * s_hi)
    return jnp.concatenate(outs, axis=1)


def _proj_kernel(*refs, layer, kinds, src_of, n_src, rope, n_rows, sub):
    if rope:
        (x_ref, g_ref, shift_ref, scale_ref, w_hbm, tab_hbm, z_hbm,
         h_ref, stage, w_vmem, w_stage, sems_out, sems_w, tab_vmem, sem_tab) = refs
        cr_ref, sr_ref, cd_ref, sdl_ref, sdh_ref = [tab_vmem.at[i] for i in range(5)]
    else:
        (x_ref, g_ref, shift_ref, scale_ref, w_hbm, z_hbm,
         h_ref, stage, w_vmem, w_stage, sems_out, sems_w) = refs
    b = pl.program_id(0)
    first = b == 0
    n_sub = n_rows // sub
    n_sec = len(kinds)
    n_slots = stage.shape[0]

    n_wslots = w_stage.shape[0]

    def weight_copy(j):
        start = src_of(j) * SEC
        cols = pl.ds(start if isinstance(start, int) else pl.multiple_of(start, SEC), SEC)
        slot = j % n_wslots
        return pltpu.make_async_copy(w_hbm.at[layer, :, cols], w_stage.at[slot], sems_w.at[slot])

    def load_weights(j):
        weight_copy(j).wait()
        src = w_stage.at[j % n_wslots]
        dst = w_vmem.at[src_of(j)]
        for r in range(D_MODEL // LANES):
            rows = slice(r * LANES, (r + 1) * LANES)
            dst[rows, :] = src[rows, :].astype(BF16)

        @pl.when(j + n_wslots < n_sec)
        def _():
            weight_copy(j + n_wslots).start()

    def table_copy():
        return pltpu.make_async_copy(tab_hbm, tab_vmem, sem_tab.at[0])

    @pl.when(first)
    def _():
        for j in range(min(n_wslots, n_sec)):
            weight_copy(j).start()
        if rope:
            table_copy().start()

    gs = g_ref[...] * (1.0 + scale_ref[...])
    sh = shift_ref[...]
    for r in range(n_sub):
        xs = x_ref[0, r * sub:(r + 1) * sub, :]
        h_ref[r * sub:(r + 1) * sub, :] = (_rms(xs) * gs + sh).astype(BF16)

    if rope:
        pl.when(first)(lambda: table_copy().wait())

    def epi_rk(a, rows):
        if rope:
            a = _rope_ret(a, cr_ref[rows, :], sr_ref[rows, :])
        return a * RET_SCALE

    def epi_rq(a, rows):
        return _rope_ret(a, cr_ref[rows, :], sr_ref[rows, :]) if rope else a

    def epi_dk(a, rows):
        return _rope_diff(a, cd_ref[rows, :], sdl_ref[rows, :], sdh_ref[rows, :]) if rope else a

    def epi_dq(a, rows):
        return epi_dk(a, rows) * DIFF_Q_SCALE

    def epi_ln(a, rows):
        mu = jnp.mean(a, axis=-1, keepdims=True)
        d = a - mu
        return d * lax.rsqrt(jnp.mean(d * d, axis=-1, keepdims=True) + EPS)

    epilogues = dict(
        plain=lambda a, rows: a,
        sigmoid=lambda a, rows: jax.nn.sigmoid(a),
        silu=lambda a, rows: a * jax.nn.sigmoid(a),
        rk=epi_rk, rq=epi_rq, dk=epi_dk, dq=epi_dq, ln=epi_ln)

    def writeback(j):
        slot = j % n_slots
        return pltpu.make_async_copy(stage.at[slot], z_hbm.at[b, j], sems_out.at[slot])

    def run(epilogue, j):
        w_sec = w_vmem.at[src_of(j)]
        out = stage.at[j % n_slots]
        for r in range(n_sub):
            rows = slice(r * sub, (r + 1) * sub)
            out[rows, :] = epilogue(_dot(h_ref[rows, :], w_sec[...]), rows).astype(BF16)

    def section(j, carry):
        pl.when(first)(functools.partial(load_weights, j))

        @pl.when(j >= n_slots)
        def _():
            writeback(j - n_slots).wait()

        for kind in sorted(set(kinds)):
            idx = [i for i, k in enumerate(kinds) if k == kind]
            cond = j == idx[0]
            for i in idx[1:]:
                cond = cond | (j == i)
            pl.when(cond)(functools.partial(run, epilogues[kind], j))
        writeback(j).start()
        return carry

    lax.fori_loop(0, n_sec, section, 0)
    for j in range(max(0, n_sec - n_slots), n_sec):
        writeback(j).wait()


def _projection(xs, l, g_pre, mod4, row_of, w_all, src_of, kinds, tables):
    n_g, n_rows, _ = xs.shape
    n_sec = len(kinds)
    rope = tables is not None
    n_src = w_all.shape[2] // SEC if src_of is _full_src else n_sec
    in_specs = [
        pl.BlockSpec((1, n_rows, D_MODEL), lambda b: (b, 0, 0)),
        _layer_spec(l, (1, D_MODEL)),
        _mod_spec(l, row_of, 0),
        _mod_spec(l, row_of, 1),
        pl.BlockSpec(memory_space=pl.ANY),
    ]
    args = [xs, g_pre, mod4, mod4, w_all]
    scratch = [pltpu.VMEM((n_rows, D_MODEL), BF16),
               pltpu.VMEM((PROJ_OUT_SLOTS, n_rows, SEC), BF16),
               pltpu.VMEM((n_src, D_MODEL, SEC), BF16),
               pltpu.VMEM((2, D_MODEL, SEC), F32),
               pltpu.SemaphoreType.DMA((PROJ_OUT_SLOTS,)),
               pltpu.SemaphoreType.DMA((2,))]
    if rope:
        in_specs.append(pl.BlockSpec(memory_space=pl.ANY))
        args.append(tables)
        scratch += [pltpu.VMEM(tables.shape, F32), pltpu.SemaphoreType.DMA((1,))]
    return pl.pallas_call(
        functools.partial(_proj_kernel, layer=l, kinds=kinds, src_of=src_of, n_src=n_src,
                          rope=rope, n_rows=n_rows, sub=PROJ_SUB),
        grid=(n_g,),
        in_specs=in_specs,
        out_specs=pl.BlockSpec(memory_space=pl.ANY),
        out_shape=jax.ShapeDtypeStruct((n_g, n_sec, n_rows, SEC), BF16),
        scratch_shapes=scratch,
        compiler_params=pltpu.CompilerParams(vmem_limit_bytes=VMEM_LIMIT),
        name="projection_rope" if rope else "projection",
    )(*args)


def _retention_tasks(lgf, lgb, cs, q_ref, g_ref, k_ref, v_ref, kc_ref, vc_ref, o_ref, inc_ref,
                     st_ref, *, n, n_ctx, rc):
    nc = n // rc
    pos = lax.broadcasted_iota(jnp.int32, (rc, LANES), 0).astype(F32)
    kdf = jnp.exp(lgf * (rc - 1.0 - pos)).astype(BF16)
    kdb = jnp.exp(lgb * pos).astype(BF16)
    cdf = jnp.exp(jnp.full((1, LANES), lgf * rc, F32))
    cdb = jnp.exp(jnp.full((1, LANES), lgb * rc, F32))
    qdf = jnp.exp(lgf * (pos + 1.0)).astype(BF16)
    qdb = jnp.exp(lgb * (rc - pos)).astype(BF16)
    ii = lax.broadcasted_iota(jnp.int32, (rc, rc), 0)
    jj = lax.broadcasted_iota(jnp.int32, (rc, rc), 1)
    dist = (ii - jj).astype(F32)
    dmat = (jnp.where(dist >= 0, jnp.exp(lgf * jnp.maximum(dist, 0.0)), 0.0)
            + jnp.where(dist <= 0, jnp.exp(lgb * jnp.maximum(-dist, 0.0)), 0.0))
    state = {}

    def increment(kr, vr, nn):
        rows = slice(nn * rc, (nn + 1) * rc)
        kk = kr[0, rows, cs]
        kcat = jnp.concatenate([kk * kdf, kk * kdb], axis=1)
        vt = vr[0, rows, cs].astype(F32).T.astype(BF16)
        return _dot(vt, kcat)

    def init_task():
        sf = jnp.zeros((HEAD_W, HEAD_W), F32)
        sb = jnp.zeros((HEAD_W, HEAD_W), F32)
        if n_ctx:
            incs = [increment(kc_ref, vc_ref, nn) for nn in range(n_ctx // rc)]
            for inc in incs:
                sf = cdf * sf + inc[:, :HEAD_W]
            for inc in reversed(incs):
                sb = cdb * sb + inc[:, HEAD_W:]
        state["sf"], state["sb"] = sf, sb

    def inc_task(nn):
        inc_ref[nn] = increment(k_ref, v_ref, nn)

    def scan_task():
        sf, sb = state["sf"], state["sb"]
        for nn in range(nc):
            st_ref[nn, :, 0:HEAD_W] = sf.astype(BF16)
            sf = cdf * sf + inc_ref[nn, :, 0:HEAD_W]
        for nn in reversed(range(nc)):
            st_ref[nn, :, HEAD_W:] = sb.astype(BF16)
            sb = cdb * sb + inc_ref[nn, :, HEAD_W:]

    def out_task(nn):
        rows = slice(nn * rc, (nn + 1) * rc)
        q = q_ref[0, rows, cs]
        s = _dot_nt(q, k_ref[0, rows, cs]) * dmat
        inner = _dot(s.astype(BF16), v_ref[0, rows, cs])
        qcat = jnp.concatenate([q * qdf, q * qdb], axis=1)
        out = inner + _dot_nt(qcat, st_ref[nn])
        o_ref[0, rows, cs] = _rms(out).astype(BF16) * g_ref[0, rows, cs]

    inc_tasks = [init_task] + [functools.partial(inc_task, nn) for nn in range(nc)]
    return inc_tasks, scan_task, [functools.partial(out_task, nn) for nn in range(nc)]


def _ret_kernel(lg_ref, q_ref, g_ref, k_ref, v_ref, o_ref, inc_ref, st_ref, *, layer, n, rc,
                heads):
    tasks = []
    for hh in range(heads):
        head = pl.program_id(1) * heads + hh
        tasks.append(_retention_tasks(
            lg_ref[layer, 0, head], lg_ref[layer, 1, head], slice(hh * LANES, (hh + 1) * LANES),
            q_ref, g_ref, k_ref, v_ref, None, None, o_ref, inc_ref.at[hh], st_ref.at[hh],
            n=n, n_ctx=0, rc=rc))
    for i in range(len(tasks[0][0])):
        for inc_tasks, _, _ in tasks:
            inc_tasks[i]()
    for _, scan_task, _ in tasks:
        scan_task()
    for i in range(len(tasks[0][2])):
        for _, _, out_tasks in tasks:
            out_tasks[i]()


def _retention_ctx(log_gamma, layer, zc, sec, n_b, n, *, rc, heads):
    width = heads * LANES
    head_block = lambda b, h: h
    return pl.pallas_call(
        functools.partial(_ret_kernel, layer=layer, n=n, rc=rc, heads=heads),
        grid=(n_b, HEADS // heads),
        in_specs=[pl.BlockSpec(memory_space=pltpu.SMEM)]
        + [_z_spec(sec[name], n, width, True, col_of=head_block)
           for name in ("rq", "rg", "rk", "rv")],
        out_specs=pl.BlockSpec((1, n, width), lambda b, h: (b, 0, h)),
        out_shape=jax.ShapeDtypeStruct((n_b, n, BRANCH_W), BF16),
        scratch_shapes=[pltpu.VMEM((heads, n // rc, HEAD_W, 2 * HEAD_W), F32),
                        pltpu.VMEM((heads, n // rc, HEAD_W, 2 * HEAD_W), BF16)],
        compiler_params=pltpu.CompilerParams(vmem_limit_bytes=VMEM_LIMIT),
        name="retention_ctx",
    )(log_gamma, zc, zc, zc, zc)


def _split_maps(q, lane):
    zero = jnp.zeros_like(q)
    return jnp.concatenate([jnp.where(lane < DIFF_DQK, q, zero),
                            jnp.where(lane >= DIFF_DQK, q, zero)], axis=0)


def _diff_ctx_kernel(lam_ref, q_ref, g_ref, k_ref, v_ref, o_ref, *, layer, tq, post_scale):
    lane = lax.broadcasted_iota(jnp.int32, (tq, LANES), 1)
    for h in range(HEADS):
        cs = slice(h * LANES, (h + 1) * LANES)
        s = _dot_nt(_split_maps(q_ref[0, :, cs], lane), k_ref[0, :, cs])
        p = jnp.exp2(s - jnp.max(s, axis=-1, keepdims=True))
        o = _dot(p.astype(BF16), v_ref[0, :, cs]) / jnp.sum(p, axis=-1, keepdims=True)
        d = o[:tq] - lam_ref[layer] * o[tq:]
        o_ref[0, :, cs] = (_rms(d) * post_scale * g_ref[0, :, cs].astype(F32)).astype(BF16)


def _diff_attention_ctx(lam, layer, zc, sec, n_b, n, *, post_scale):
    return pl.pallas_call(
        functools.partial(_diff_ctx_kernel, layer=layer, tq=n, post_scale=post_scale),
        grid=(n_b,),
        in_specs=[pl.BlockSpec(memory_space=pltpu.SMEM)]
        + [_z_spec(sec[name], n, SEC, True) for name in ("dq", "dg", "dk", "dv")],
        out_specs=pl.BlockSpec((1, n, BRANCH_W), lambda b: (b, 0, 0)),
        out_shape=jax.ShapeDtypeStruct((n_b, n, BRANCH_W), BF16),
        compiler_params=pltpu.CompilerParams(vmem_limit_bytes=VMEM_LIMIT),
        name="diff_attention_ctx",
    )(lam, zc, zc, zc, zc)


def _mixer_kernel(lam_ref, lg_ref, q_ref, g_ref, kc_ref, vc_ref, kl_ref, vl_ref,
                  rq_ref, rg_ref, rk_ref, rv_ref, rkc_ref, rvc_ref, od_ref, or_ref,
                  kall, vall, s0, s1, p0, p1, inc_ref, st_ref, *, layer, n, n_ctx, ts, rc,
                  post_scale):
    n_sub = n // ts
    nk = kall.shape[0]
    kall[0:n_ctx, :] = kc_ref[0]
    kall[n_ctx:, :] = kl_ref[0]
    vall[0:n_ctx, 0:LANES] = vc_ref[0]
    vall[n_ctx:, 0:LANES] = vl_ref[0]
    vall[:, LANES:] = jnp.ones((nk, LANES), BF16)
    lam = lam_ref[layer]
    lane = lax.broadcasted_iota(jnp.int32, (ts, LANES), 1)
    s_bufs, p_bufs = (s0, s1), (p0, p1)

    def stage_a(t):
        rows = slice(t * ts, (t + 1) * ts)
        s_bufs[t % 2][...] = _dot_nt(_split_maps(q_ref[0, rows, :], lane), kall[...])

    def stage_b(t):
        s = s_bufs[t % 2][...]
        p_bufs[t % 2][...] = jnp.exp2(s - jnp.max(s, axis=-1, keepdims=True)).astype(BF16)

    def stage_c(t):
        rows = slice(t * ts, (t + 1) * ts)
        oe = _dot(p_bufs[t % 2][...], vall[...])
        o = oe[:, :LANES] / oe[:, LANES:]
        d = o[:ts] - lam * o[ts:]
        od_ref[0, rows, :] = (_rms(d) * post_scale * g_ref[0, rows, :].astype(F32)).astype(BF16)

    head = pl.program_id(1)
    inc_tasks, scan_task, out_tasks = _retention_tasks(
        lg_ref[layer, 0, head], lg_ref[layer, 1, head], slice(0, LANES),
        rq_ref, rg_ref, rk_ref, rv_ref, rkc_ref, rvc_ref, or_ref, inc_ref, st_ref,
        n=n, n_ctx=n_ctx, rc=rc)
    n_steps = n_sub + 2
    extra = [[] for _ in range(n_steps)]
    half = (len(inc_tasks) + 1) // 2
    extra[0] += inc_tasks[:half]
    extra[1] += inc_tasks[half:]
    extra[2].append(scan_task)
    for i, task in enumerate(out_tasks):
        extra[2 + (i * (n_steps - 2)) // len(out_tasks)].append(task)

    for t in range(n_steps):
        if t >= 2:
            stage_c(t - 2)
        if 1 <= t <= n_sub:
            stage_b(t - 1)
        if t < n_sub:
            stage_a(t)
        for task in extra[t]:
            task()


def _latent_mixers(lam, log_gamma, layer, z, sec, n_b, n, zc, sec_c, n_ctx, *, ts, rc,
                   post_scale):
    nk = n_ctx + n
    head_block = lambda b, h: h
    lat = lambda name: _z_spec(sec[name], n, LANES, False, col_of=head_block)
    cx = lambda name: _z_spec(sec_c[name], n_ctx, LANES, True, col_of=head_block)
    out_spec = pl.BlockSpec((1, n, LANES), lambda b, h: (b, 0, h))
    out_shape = jax.ShapeDtypeStruct((n_b, n, BRANCH_W), BF16)
    o_diff, o_ret = pl.pallas_call(
        functools.partial(_mixer_kernel, layer=layer, n=n, n_ctx=n_ctx, ts=ts, rc=rc,
                          post_scale=post_scale),
        grid=(n_b, HEADS),
        in_specs=[pl.BlockSpec(memory_space=pltpu.SMEM), pl.BlockSpec(memory_space=pltpu.SMEM),
                  lat("dq"), lat("dg"), cx("dk"), cx("dv"), lat("dk"), lat("dv"),
                  lat("rq"), lat("rg"), lat("rk"), lat("rv"), cx("rk"), cx("rv")],
        out_specs=[out_spec, out_spec],
        out_shape=[out_shape, out_shape],
        scratch_shapes=[pltpu.VMEM((nk, LANES), BF16), pltpu.VMEM((nk, 2 * LANES), BF16),
                        pltpu.VMEM((2 * ts, nk), F32), pltpu.VMEM((2 * ts, nk), F32),
                        pltpu.VMEM((2 * ts, nk), BF16), pltpu.VMEM((2 * ts, nk), BF16),
                        pltpu.VMEM((n // rc, HEAD_W, 2 * HEAD_W), F32),
                        pltpu.VMEM((n // rc, HEAD_W, 2 * HEAD_W), BF16)],
        compiler_params=pltpu.CompilerParams(vmem_limit_bytes=VMEM_LIMIT),
        name="latent_mixers",
    )(lam, log_gamma, z, z, zc, zc, z, z, z, z, z, z, zc, zc)
    return o_ret, o_diff


def _merge_kernel(x_ref, gate_ref, gpost_ref, oret_ref, odiff_ref, u_ref, vn_ref, mg_ref,
                  gr_ref, gm_ref, gd_ref, ws_ref, bs_ref, wbo_ref, wo_ref, out_ref, *, tm, out_sub):
    rows = []
    for c in range(tm // CHUNK):
        cols = []
        for g in range(HEADS):
            blk = vn_ref[0, c * CHUNK:(c + 1) * CHUNK, g * LANES:(g + 1) * LANES]
            cols.append(_dot(ws_ref[g], blk) + bs_ref[g])
        rows.append(jnp.concatenate(cols, axis=1))
    sp = jnp.concatenate(rows, axis=0)
    o_mlp = (u_ref[0].astype(F32) * sp * mg_ref[0].astype(F32)).astype(BF16)

    def gated(gate2_ref, o, w):
        gate = jnp.concatenate([gate2_ref[0, 0], gate2_ref[0, 1]], axis=1).astype(F32)
        return gate * _dot(o, w)

    t = (gated(gr_ref, oret_ref[0], wbo_ref[0]) + gated(gm_ref, o_mlp, wbo_ref[1])
         + gated(gd_ref, odiff_ref[0], wbo_ref[2]))
    tb = t.astype(BF16)
    for r in range(tm // out_sub):
        rs = slice(r * out_sub, (r + 1) * out_sub)
        y = _dot(tb[rs, :], wo_ref[...])
        out_ref[0, rs, :] = x_ref[0, rs, :] + gate_ref[...] * (_rms(y) * gpost_ref[...])


def _merge(xs, l, mod4, row_of, g_post, o_ret, o_diff, z, sec, ws, bs, wbo, wo, *, tm):
    n_b, n, _ = xs.shape
    row_block = lambda b, i: i

    def zsec(name):
        return _z_spec(sec[name], tm, SEC, False, row_of=row_block)

    def zgate(k):
        return pl.BlockSpec((1, 2, tm, SEC), lambda b, i: (b, k, i, 0))

    in_specs = [
        pl.BlockSpec((1, tm, D_MODEL), lambda b, i: (b, i, 0)),
        _mod_spec(l, row_of, 2),
        _layer_spec(l, (1, D_MODEL)),
        pl.BlockSpec((1, tm, BRANCH_W), lambda b, i: (b, i, 0)),
        pl.BlockSpec((1, tm, BRANCH_W), lambda b, i: (b, i, 0)),
        zsec("mu"), zsec("mv"), zsec("mg"), zgate(0), zgate(1), zgate(2),
        _layer_spec(l, (HEADS, CHUNK, CHUNK)),
        _layer_spec(l, (HEADS, CHUNK, LANES)),
        _layer_spec(l, (N_BRANCH, BRANCH_W, D_MODEL)),
        _layer_spec(l, (D_MODEL, D_MODEL)),
    ]
    return pl.pallas_call(
        functools.partial(_merge_kernel, tm=tm, out_sub=MERGE_OUT_SUB),
        grid=(n_b, n // tm),
        in_specs=in_specs,
        out_specs=pl.BlockSpec((1, tm, D_MODEL), lambda b, i: (b, i, 0)),
        out_shape=jax.ShapeDtypeStruct(xs.shape, F32),
        compiler_params=pltpu.CompilerParams(vmem_limit_bytes=VMEM_LIMIT),
        name="merge",
    )(xs, mod4, g_post, o_ret, o_diff, z, z, z, z, z, z, ws, bs, wbo, wo)


def _rope_tables(n_lat):
    rows = n_lat // GRID_W
    row_pos = jnp.repeat(jnp.arange(rows, dtype=F32), GRID_W)
    col_pos = jnp.tile(jnp.arange(GRID_W, dtype=F32), rows)

    def angles(head_dim):
        n_freq = head_dim // 4
        inv = ROPE_BASE ** (-jnp.arange(n_freq, dtype=F32) / n_freq)
        ang = jnp.concatenate([row_pos[:, None] * inv, col_pos[:, None] * inv], axis=-1)
        return jnp.cos(ang), jnp.sin(ang)

    cos_r, sin_r = angles(HEAD_W)
    cos_d, sin_d = angles(DIFF_DQK)
    zeros = jnp.zeros_like(sin_d)
    c_r = jnp.concatenate([cos_r, cos_r], axis=-1)
    s_r = jnp.concatenate([-sin_r, sin_r], axis=-1)
    c_d = jnp.tile(jnp.concatenate([cos_d, cos_d], axis=-1), (1, 2))
    s_lo = jnp.tile(jnp.concatenate([-sin_d, zeros], axis=-1), (1, 2))
    s_hi = jnp.tile(jnp.concatenate([zeros, sin_d], axis=-1), (1, 2))
    return c_r, s_r, c_d, s_lo, s_hi


@jax.jit
def kernel(x, c, ctx, c_ctx, w_mod, b_mod, g_pre, g_post, w_in, ret_decay_logit, mlp_w_s,
           mlp_b_s, diff_lambda_q, diff_lambda_k, w_branch_out, w_out):
    n_b, n_lat, _ = x.shape
    n_ctx = ctx.shape[1]
    tables = jnp.stack(_rope_tables(n_lat))

    cond_rows = 16
    ctx_row = n_b
    cc = jnp.zeros((cond_rows, D_MODEL), F32).at[:n_b].set(c).at[ctx_row].set(c_ctx)
    mod4 = _modulation(cc, w_mod, b_mod).reshape(DEPTH, cond_rows, 1, 3 * D_MODEL)
    lat_row = lambda b: b
    ctx_row_of = lambda b: ctx_row

    log_gamma = -jax.nn.softplus(-ret_decay_logit.astype(F32))
    lam_inits = [0.8 - 0.6 * math.exp(-0.3 * l) for l in range(DEPTH)]
    lam = (jnp.exp(jnp.sum(diff_lambda_q[:, 0] * diff_lambda_k[:, 0], axis=-1))
           - jnp.exp(jnp.sum(diff_lambda_q[:, 1] * diff_lambda_k[:, 1], axis=-1))
           + jnp.asarray(lam_inits, F32)).astype(F32)

    w_sec = w_in
    ws = mlp_w_s.astype(BF16)
    bs = jnp.broadcast_to(mlp_b_s[..., None], (DEPTH, HEADS, CHUNK, LANES)).astype(F32)
    wbo = w_branch_out.astype(BF16)
    wo = w_out.astype(BF16)
    g_pre3 = g_pre.reshape(DEPTH, 1, D_MODEL)
    g_post3 = g_post.reshape(DEPTH, 1, D_MODEL)

    for l in range(DEPTH):
        last = l == DEPTH - 1
        post_scale = 1.0 - lam_inits[l]

        ctx_flat = ctx.reshape(1, n_b * n_ctx, D_MODEL)
        if last:
            zc = _projection(ctx_flat, l, g_pre3, mod4, ctx_row_of, w_sec, lambda j: j,
                             KV_KINDS, None)
            sec_c = KV_SEC
        else:
            zc = _projection(ctx_flat, l, g_pre3, mod4, ctx_row_of, w_sec, _full_src,
                             FULL_KINDS, None)
            sec_c = FULL_SEC
            co_ret = _retention_ctx(log_gamma, l, zc, sec_c, n_b, n_ctx, rc=RET_CHUNK,
                                    heads=HEADS)
            co_diff = _diff_attention_ctx(lam, l, zc, sec_c, n_b, n_ctx, post_scale=post_scale)
            ctx_next = _merge(ctx_flat, l, mod4, ctx_row_of, g_post3,
                              co_ret.reshape(1, n_b * n_ctx, BRANCH_W),
                              co_diff.reshape(1, n_b * n_ctx, BRANCH_W), zc, sec_c,
                              ws, bs, wbo, wo, tm=MERGE_TM)
            ctx_next = ctx_next.reshape(n_b, n_ctx, D_MODEL)

        z = _projection(x, l, g_pre3, mod4, lat_row, w_sec, _full_src, FULL_KINDS, tables)
        o_ret, o_diff = _latent_mixers(lam, log_gamma, l, z, FULL_SEC, n_b, n_lat, zc, sec_c,
                                       n_ctx, ts=DIFF_TS, rc=RET_CHUNK, post_scale=post_scale)
        x = _merge(x, l, mod4, lat_row, g_post3, o_ret, o_diff, z, FULL_SEC, ws, bs, wbo, wo,
                   tm=MERGE_TM)
        if not last:
            ctx = ctx_next
    return x
```

```python
import functools
import math

import jax
import jax.numpy as jnp
from jax import lax
from jax.experimental import pallas as pl
from jax.experimental.pallas import tpu as pltpu

F32 = jnp.float32
BF16 = jnp.bfloat16

D_MODEL = 1024
DEPTH = 2
GRID_W = 64
N_BRANCH = 3
BRANCH_W = D_MODEL // 2
HEADS = 4
HEAD_W = BRANCH_W // HEADS
DIFF_DQK = HEAD_W // 2
CHUNK = 128
ROPE_BASE = 10000.0
EPS = 1e-6
RET_SCALE = HEAD_W ** -0.5
DIFF_Q_SCALE = (DIFF_DQK ** -0.5) * math.log2(math.e)

LANES = 128
SEC = BRANCH_W
MERGE_COLS = N_BRANCH * D_MODEL
KV_COLS = 4 * BRANCH_W
IN_COLS = KV_COLS + 7 * BRANCH_W + MERGE_COLS
VMEM_LIMIT = 56 * 1024 * 1024

PROJ_SUB = 256
PROJ_OUT_SLOTS = 2
DIFF_TS = 256
RET_CHUNK = 256
MERGE_TM = 512
MERGE_OUT_SUB = 512

FULL_KINDS = ("sigmoid",) * 6 + ("rk", "plain", "dk", "plain", "rq", "silu", "dq", "silu",
                                 "plain", "ln", "silu")
FULL_SEC = dict(rk=6, rv=7, dk=8, dv=9, rq=10, rg=11, dq=12, dg=13, mu=14, mv=15, mg=16)
KV_KINDS = ("rk", "plain", "dk", "plain")
KV_SEC = dict(rk=0, rv=1, dk=2, dv=3)
_N_MERGE_SEC = MERGE_COLS // SEC
_N_REST_SEC = (KV_COLS + 7 * BRANCH_W) // SEC


def _full_src(j):
    return jnp.where(j < _N_MERGE_SEC, j + _N_REST_SEC, j - _N_MERGE_SEC)


def _dot(a, b):
    return jnp.dot(a, b, preferred_element_type=F32)


def _dot_nt(a, b):
    return lax.dot_general(a, b, (((1,), (1,)), ((), ())), preferred_element_type=F32)


def _rms(v):
    return v * lax.rsqrt(jnp.mean(v * v, axis=-1, keepdims=True) + EPS)


def _mod_spec(l, row_of, k):
    return pl.BlockSpec((None, None, 1, D_MODEL), lambda *idx: (l, row_of(idx[0]), 0, k))


def _layer_spec(l, shape, **kwargs):
    zeros = (0,) * len(shape)
    return pl.BlockSpec((None,) + tuple(shape), lambda *idx: (l,) + zeros, **kwargs)


def _z_spec(sec_idx, rows, width, flat, col_of=lambda *idx: 0, row_of=None):
    if row_of is None:
        row_of = (lambda *idx: idx[0]) if flat else (lambda *idx: 0)
    group_of = (lambda *idx: 0) if flat else (lambda *idx: idx[0])
    return pl.BlockSpec((1, None, rows, width),
                        lambda *idx: (group_of(*idx), sec_idx, row_of(*idx), col_of(*idx)))


def _split_bf16(a):
    hi = a.astype(BF16)
    return hi, (a - hi.astype(F32)).astype(BF16)


def _mod_kernel(c_ref, w_ref, b_ref, o_ref):
    cv = c_ref[...]
    c_hi, c_lo = _split_bf16(cv * jax.nn.sigmoid(cv))
    w_hi, w_lo = _split_bf16(w_ref[0])
    o_ref[0] = _dot(c_hi, w_hi) + _dot(c_hi, w_lo) + _dot(c_lo, w_hi) + b_ref[0]


def _modulation(cc, w_mod, b_mod):
    rows = cc.shape[0]
    return pl.pallas_call(
        _mod_kernel,
        grid=(DEPTH, 3),
        in_specs=[
            pl.BlockSpec((rows, D_MODEL), lambda l, j: (0, 0)),
            pl.BlockSpec((1, D_MODEL, D_MODEL), lambda l, j: (l, 0, j)),
            pl.BlockSpec((1, 1, D_MODEL), lambda l, j: (l, 0, j)),
        ],
        out_specs=pl.BlockSpec((1, rows, D_MODEL), lambda l, j: (l, 0, j)),
        out_shape=jax.ShapeDtypeStruct((DEPTH, rows, 3 * D_MODEL), F32),
        compiler_params=pltpu.CompilerParams(vmem_limit_bytes=VMEM_LIMIT),
        name="modulation",
    )(cc, w_mod, b_mod.reshape(DEPTH, 1, 3 * D_MODEL))


def _rope_ret(a, c, s):
    outs = []
    for h in range(HEADS):
        ah = a[:, h * LANES:(h + 1) * LANES]
        outs.append(ah * c + pltpu.roll(ah, LANES // 2, 1) * s)
    return jnp.concatenate(outs, axis=1)


def _rope_diff(a, c, s_lo, s_hi):
    outs = []
    for h in range(HEADS):
        ah = a[:, h * LANES:(h + 1) * LANES]
        outs.append(ah * c + pltpu.roll(ah, LANES - DIFF_DQK // 2, 1) * s_lo
                    + pltpu.roll(ah, DIFF_DQK // 2, 1) * s_hi)
    return jnp.concatenate(outs, axis=1)


def _proj_kernel(*refs, layer, kinds, src_of, n_src, rope, n_rows, sub):
    if rope:
        (x_ref, g_ref, shift_ref, scale_ref, w_hbm, tab_hbm, z_hbm,
         h_ref, stage, w_vmem, w_stage, sems_out, sems_w, tab_vmem, sem_tab) = refs
        cr_ref, sr_ref, cd_ref, sdl_ref, sdh_ref = [tab_vmem.at[i] for i in range(5)]
    else:
        (x_ref, g_ref, shift_ref, scale_ref, w_hbm, z_hbm,
         h_ref, stage, w_vmem, w_stage, sems_out, sems_w) = refs
    b = pl.program_id(0)
    first = b == 0
    n_sub = n_rows // sub
    n_sec = len(kinds)
    n_slots = stage.shape[0]

    n_wslots = w_stage.shape[0]

    def weight_copy(j):
        start = src_of(j) * SEC
        cols = pl.ds(start if isinstance(start, int) else pl.multiple_of(start, SEC), SEC)
        slot = j % n_wslots
        return pltpu.make_async_copy(w_hbm.at[layer, :, cols], w_stage.at[slot], sems_w.at[slot])

    def load_weights(j):
        weight_copy(j).wait()
        src = w_stage.at[j % n_wslots]
        dst = w_vmem.at[src_of(j)]
        for r in range(D_MODEL // LANES):
            rows = slice(r * LANES, (r + 1) * LANES)
            dst[rows, :] = src[rows, :].astype(BF16)

        @pl.when(j + n_wslots < n_sec)
        def _():
            weight_copy(j + n_wslots).start()

    def table_copy():
        return pltpu.make_async_copy(tab_hbm, tab_vmem, sem_tab.at[0])

    @pl.when(first)
    def _():
        for j in range(min(n_wslots, n_sec)):
            weight_copy(j).start()
        if rope:
            table_copy().start()

    gs = g_ref[...] * (1.0 + scale_ref[...])
    sh = shift_ref[...]
    for r in range(n_sub):
        xs = x_ref[0, r * sub:(r + 1) * sub, :]
        h_ref[r * sub:(r + 1) * sub, :] = (_rms(xs) * gs + sh).astype(BF16)

    if rope:
        pl.when(first)(lambda: table_copy().wait())

    def epi_rk(a, rows):
        if rope:
            a = _rope_ret(a, cr_ref[rows, :], sr_ref[rows, :])
        return a * RET_SCALE

    def epi_rq(a, rows):
        return _rope_ret(a, cr_ref[rows, :], sr_ref[rows, :]) if rope else a

    def epi_dk(a, rows):
        return _rope_diff(a, cd_ref[rows, :], sdl_ref[rows, :], sdh_ref[rows, :]) if rope else a

    def epi_dq(a, rows):
        return epi_dk(a, rows) * DIFF_Q_SCALE

    def epi_ln(a, rows):
        mu = jnp.mean(a, axis=-1, keepdims=True)
        d = a - mu
        return d * lax.rsqrt(jnp.mean(d * d, axis=-1, keepdims=True) + EPS)

    epilogues = dict(
        plain=lambda a, rows: a,
        sigmoid=lambda a, rows: jax.nn.sigmoid(a),
        silu=lambda a, rows: a * jax.nn.sigmoid(a),
        rk=epi_rk, rq=epi_rq, dk=epi_dk, dq=epi_dq, ln=epi_ln)

    def writeback(j):
        slot = j % n_slots
        return pltpu.make_async_copy(stage.at[slot], z_hbm.at[b, j], sems_out.at[slot])

    def run(epilogue, j):
        w_sec = w_vmem.at[src_of(j)]
        out = stage.at[j % n_slots]
        for r in range(n_sub):
            rows = slice(r * sub, (r + 1) * sub)
            out[rows, :] = epilogue(_dot(h_ref[rows, :], w_sec[...]), rows).astype(BF16)

    def section(j, carry):
        pl.when(first)(functools.partial(load_weights, j))

        @pl.when(j >= n_slots)
        def _():
            writeback(j - n_slots).wait()

        for kind in sorted(set(kinds)):
            idx = [i for i, k in enumerate(kinds) if k == kind]
            cond = j == idx[0]
            for i in idx[1:]:
                cond = cond | (j == i)
            pl.when(cond)(functools.partial(run, epilogues[kind], j))
        writeback(j).start()
        return carry

    lax.fori_loop(0, n_sec, section, 0)
    for j in range(max(0, n_sec - n_slots), n_sec):
        writeback(j).wait()


def _projection(xs, l, g_pre, mod4, row_of, w_all, src_of, kinds, tables):
    n_g, n_rows, _ = xs.shape
    n_sec = len(kinds)
    rope = tables is not None
    n_src = w_all.shape[2] // SEC if src_of is _full_src else n_sec
    in_specs = [
        pl.BlockSpec((1, n_rows, D_MODEL), lambda b: (b, 0, 0)),
        _layer_spec(l, (1, D_MODEL)),
        _mod_spec(l, row_of, 0),
        _mod_spec(l, row_of, 1),
        pl.BlockSpec(memory_space=pl.ANY),
    ]
    args = [xs, g_pre, mod4, mod4, w_all]
    scratch = [pltpu.VMEM((n_rows, D_MODEL), BF16),
               pltpu.VMEM((PROJ_OUT_SLOTS, n_rows, SEC), BF16),
               pltpu.VMEM((n_src, D_MODEL, SEC), BF16),
               pltpu.VMEM((2, D_MODEL, SEC), F32),
               pltpu.SemaphoreType.DMA((PROJ_OUT_SLOTS,)),
               pltpu.SemaphoreType.DMA((2,))]
    if rope:
        in_specs.append(pl.BlockSpec(memory_space=pl.ANY))
        args.append(tables)
        scratch += [pltpu.VMEM(tables.shape, F32), pltpu.SemaphoreType.DMA((1,))]
    return pl.pallas_call(
        functools.partial(_proj_kernel, layer=l, kinds=kinds, src_of=src_of, n_src=n_src,
                          rope=rope, n_rows=n_rows, sub=PROJ_SUB),
        grid=(n_g,),
        in_specs=in_specs,
        out_specs=pl.BlockSpec(memory_space=pl.ANY),
        out_shape=jax.ShapeDtypeStruct((n_g, n_sec, n_rows, SEC), BF16),
        scratch_shapes=scratch,
        compiler_params=pltpu.CompilerParams(vmem_limit_bytes=VMEM_LIMIT),
        name="projection_rope" if rope else "projection",
    )(*args)


def _retention_tasks(lgf, lgb, cs, q_ref, g_ref, k_ref, v_ref, kc_ref, vc_ref, o_ref, inc_ref,
                     st_ref, *, n, n_ctx, rc):
    nc = n // rc
    pos = lax.broadcasted_iota(jnp.int32, (rc, LANES), 0).astype(F32)
    kdf = jnp.exp(lgf * (rc - 1.0 - pos)).astype(BF16)
    kdb = jnp.exp(lgb * pos).astype(BF16)
    cdf = jnp.exp(jnp.full((1, LANES), lgf * rc, F32))
    cdb = jnp.exp(jnp.full((1, LANES), lgb * rc, F32))
    qdf = jnp.exp(lgf * (pos + 1.0)).astype(BF16)
    qdb = jnp.exp(lgb * (rc - pos)).astype(BF16)
    ii = lax.broadcasted_iota(jnp.int32, (rc, rc), 0)
    jj = lax.broadcasted_iota(jnp.int32, (rc, rc), 1)
    dist = (ii - jj).astype(F32)
    dmat = (jnp.where(dist >= 0, jnp.exp(lgf * jnp.maximum(dist, 0.0)), 0.0)
            + jnp.where(dist <= 0, jnp.exp(lgb * jnp.maximum(-dist, 0.0)), 0.0))
    state = {}

    def increment(kr, vr, nn):
        rows = slice(nn * rc, (nn + 1) * rc)
        kk = kr[0, rows, cs]
        kcat = jnp.concatenate([kk * kdf, kk * kdb], axis=1)
        vt = vr[0, rows, cs].astype(F32).T.astype(BF16)
        return _dot(vt, kcat)

    def init_task():
        sf = jnp.zeros((HEAD_W, HEAD_W), F32)
        sb = jnp.zeros((HEAD_W, HEAD_W), F32)
        if n_ctx:
            incs = [increment(kc_ref, vc_ref, nn) for nn in range(n_ctx // rc)]
            for inc in incs:
                sf = cdf * sf + inc[:, :HEAD_W]
            for inc in reversed(incs):
                sb = cdb * sb + inc[:, HEAD_W:]
        state["sf"], state["sb"] = sf, sb

    def inc_task(nn):
        inc_ref[nn] = increment(k_ref, v_ref, nn)

    def scan_task():
        sf, sb = state["sf"], state["sb"]
        for nn in range(nc):
            st_ref[nn, :, 0:HEAD_W] = sf.astype(BF16)
            sf = cdf * sf + inc_ref[nn, :, 0:HEAD_W]
        for nn in reversed(range(nc)):
            st_ref[nn, :, HEAD_W:] = sb.astype(BF16)
            sb = cdb * sb + inc_ref[nn, :, HEAD_W:]

    def out_task(nn):
        rows = slice(nn * rc, (nn + 1) * rc)
        q = q_ref[0, rows, cs]
        s = _dot_nt(q, k_ref[0, rows, cs]) * dmat
        inner = _dot(s.astype(BF16), v_ref[0, rows, cs])
        qcat = jnp.concatenate([q * qdf, q * qdb], axis=1)
        out = inner + _dot_nt(qcat, st_ref[nn])
        o_ref[0, rows, cs] = _rms(out).astype(BF16) * g_ref[0, rows, cs]

    inc_tasks = [init_task] + [functools.partial(inc_task, nn) for nn in range(nc)]
    return inc_tasks, scan_task, [functools.partial(out_task, nn) for nn in range(nc)]


def _ret_kernel(lg_ref, q_ref, g_ref, k_ref, v_ref, o_ref, inc_ref, st_ref, *, layer, n, rc,
                heads):
    tasks = []
    for hh in range(heads):
        head = pl.program_id(1) * heads + hh
        tasks.append(_retention_tasks(
            lg_ref[layer, 0, head], lg_ref[layer, 1, head], slice(hh * LANES, (hh + 1) * LANES),
            q_ref, g_ref, k_ref, v_ref, None, None, o_ref, inc_ref.at[hh], st_ref.at[hh],
            n=n, n_ctx=0, rc=rc))
    for i in range(len(tasks[0][0])):
        for inc_tasks, _, _ in tasks:
            inc_tasks[i]()
    for _, scan_task, _ in tasks:
        scan_task()
    for i in range(len(tasks[0][2])):
        for _, _, out_tasks in tasks:
            out_tasks[i]()


def _retention_ctx(log_gamma, layer, zc, sec, n_b, n, *, rc, heads):
    width = heads * LANES
    head_block = lambda b, h: h
    return pl.pallas_call(
        functools.partial(_ret_kernel, layer=layer, n=n, rc=rc, heads=heads),
        grid=(n_b, HEADS // heads),
        in_specs=[pl.BlockSpec(memory_space=pltpu.SMEM)]
        + [_z_spec(sec[name], n, width, True, col_of=head_block)
           for name in ("rq", "rg", "rk", "rv")],
        out_specs=pl.BlockSpec((1, n, width), lambda b, h: (b, 0, h)),
        out_shape=jax.ShapeDtypeStruct((n_b, n, BRANCH_W), BF16),
        scratch_shapes=[pltpu.VMEM((heads, n // rc, HEAD_W, 2 * HEAD_W), F32),
                        pltpu.VMEM((heads, n // rc, HEAD_W, 2 * HEAD_W), BF16)],
        compiler_params=pltpu.CompilerParams(vmem_limit_bytes=VMEM_LIMIT),
        name="retention_ctx",
    )(log_gamma, zc, zc, zc, zc)


def _split_maps(q, lane):
    zero = jnp.zeros_like(q)
    return jnp.concatenate([jnp.where(lane < DIFF_DQK, q, zero),
                            jnp.where(lane >= DIFF_DQK, q, zero)], axis=0)


def _diff_ctx_kernel(lam_ref, q_ref, g_ref, k_ref, v_ref, o_ref, *, layer, tq, post_scale):
    lane = lax.broadcasted_iota(jnp.int32, (tq, LANES), 1)
    for h in range(HEADS):
        cs = slice(h * LANES, (h + 1) * LANES)
        s = _dot_nt(_split_maps(q_ref[0, :, cs], lane), k_ref[0, :, cs])
        p = jnp.exp2(s - jnp.max(s, axis=-1, keepdims=True))
        o = _dot(p.astype(BF16), v_ref[0, :, cs]) / jnp.sum(p, axis=-1, keepdims=True)
        d = o[:tq] - lam_ref[layer] * o[tq:]
        o_ref[0, :, cs] = (_rms(d) * post_scale * g_ref[0, :, cs].astype(F32)).astype(BF16)


def _diff_attention_ctx(lam, layer, zc, sec, n_b, n, *, post_scale):
    return pl.pallas_call(
        functools.partial(_diff_ctx_kernel, layer=layer, tq=n, post_scale=post_scale),
        grid=(n_b,),
        in_specs=[pl.BlockSpec(memory_space=pltpu.SMEM)]
        + [_z_spec(sec[name], n, SEC, True) for name in ("dq", "dg", "dk", "dv")],
        out_specs=pl.BlockSpec((1, n, BRANCH_W), lambda b: (b, 0, 0)),
        out_shape=jax.ShapeDtypeStruct((n_b, n, BRANCH_W), BF16),
        compiler_params=pltpu.CompilerParams(vmem_limit_bytes=VMEM_LIMIT),
        name="diff_attention_ctx",
    )(lam, zc, zc, zc, zc)


def _mixer_kernel(lam_ref, lg_ref, q_ref, g_ref, kc_ref, vc_ref, kl_ref, vl_ref,
                  rq_ref, rg_ref, rk_ref, rv_ref, rkc_ref, rvc_ref, od_ref, or_ref,
                  kall, vall, s0, s1, p0, p1, inc_ref, st_ref, *, layer, n, n_ctx, ts, rc,
                  post_scale):
    n_sub = n // ts
    nk = kall.shape[0]
    kall[0:n_ctx, :] = kc_ref[0]
    kall[n_ctx:, :] = kl_ref[0]
    vall[0:n_ctx, 0:LANES] = vc_ref[0]
    vall[n_ctx:, 0:LANES] = vl_ref[0]
    vall[:, LANES:] = jnp.ones((nk, LANES), BF16)
    lam = lam_ref[layer]
    lane = lax.broadcasted_iota(jnp.int32, (ts, LANES), 1)
    s_bufs, p_bufs = (s0, s1), (p0, p1)

    def stage_a(t):
        rows = slice(t * ts, (t + 1) * ts)
        s_bufs[t % 2][...] = _dot_nt(_split_maps(q_ref[0, rows, :], lane), kall[...])

    def stage_b(t):
        s = s_bufs[t % 2][...]
        p_bufs[t % 2][...] = jnp.exp2(s - jnp.max(s, axis=-1, keepdims=True)).astype(BF16)

    def stage_c(t):
        rows = slice(t * ts, (t + 1) * ts)
        oe = _dot(p_bufs[t % 2][...], vall[...])
        o = oe[:, :LANES] / oe[:, LANES:]
        d = o[:ts] - lam * o[ts:]
        od_ref[0, rows, :] = (_rms(d) * post_scale * g_ref[0, rows, :].astype(F32)).astype(BF16)

    head = pl.program_id(1)
    inc_tasks, scan_task, out_tasks = _retention_tasks(
        lg_ref[layer, 0, head], lg_ref[layer, 1, head], slice(0, LANES),
        rq_ref, rg_ref, rk_ref, rv_ref, rkc_ref, rvc_ref, or_ref, inc_ref, st_ref,
        n=n, n_ctx=n_ctx, rc=rc)
    n_steps = n_sub + 2
    extra = [[] for _ in range(n_steps)]
    half = (len(inc_tasks) + 1) // 2
    extra[0] += inc_tasks[:half]
    extra[1] += inc_tasks[half:]
    extra[2].append(scan_task)
    for i, task in enumerate(out_tasks):
        extra[2 + (i * (n_steps - 2)) // len(out_tasks)].append(task)

    for t in range(n_steps):
        if t >= 2:
            stage_c(t - 2)
        if 1 <= t <= n_sub:
            stage_b(t - 1)
        if t < n_sub:
            stage_a(t)
        for task in extra[t]:
            task()


def _latent_mixers(lam, log_gamma, layer, z, sec, n_b, n, zc, sec_c, n_ctx, *, ts, rc,
                   post_scale):
    nk = n_ctx + n
    head_block = lambda b, h: h
    lat = lambda name: _z_spec(sec[name], n, LANES, False, col_of=head_block)
    cx = lambda name: _z_spec(sec_c[name], n_ctx, LANES, True, col_of=head_block)
    out_spec = pl.BlockSpec((1, n, LANES), lambda b, h: (b, 0, h))
    out_shape = jax.ShapeDtypeStruct((n_b, n, BRANCH_W), BF16)
    o_diff, o_ret = pl.pallas_call(
        functools.partial(_mixer_kernel, layer=layer, n=n, n_ctx=n_ctx, ts=ts, rc=rc,
                          post_scale=post_scale),
        grid=(n_b, HEADS),
        in_specs=[pl.BlockSpec(memory_space=pltpu.SMEM), pl.BlockSpec(memory_space=pltpu.SMEM),
                  lat("dq"), lat("dg"), cx("dk"), cx("dv"), lat("dk"), lat("dv"),
                  lat("rq"), lat("rg"), lat("rk"), lat("rv"), cx("rk"), cx("rv")],
        out_specs=[out_spec, out_spec],
        out_shape=[out_shape, out_shape],
        scratch_shapes=[pltpu.VMEM((nk, LANES), BF16), pltpu.VMEM((nk, 2 * LANES), BF16),
                        pltpu.VMEM((2 * ts, nk), F32), pltpu.VMEM((2 * ts, nk), F32),
                        pltpu.VMEM((2 * ts, nk), BF16), pltpu.VMEM((2 * ts, nk), BF16),
                        pltpu.VMEM((n // rc, HEAD_W, 2 * HEAD_W), F32),
                        pltpu.VMEM((n // rc, HEAD_W, 2 * HEAD_W), BF16)],
        compiler_params=pltpu.CompilerParams(vmem_limit_bytes=VMEM_LIMIT),
        name="latent_mixers",
    )(lam, log_gamma, z, z, zc, zc, z, z, z, z, z, z, zc, zc)
    return o_ret, o_diff


def _merge_kernel(x_ref, gate_ref, gpost_ref, oret_ref, odiff_ref, u_ref, vn_ref, mg_ref,
                  gr_ref, gm_ref, gd_ref, ws_ref, bs_ref, wbo_ref, wo_ref, out_ref, *, tm, out_sub):
    rows = []
    for c in range(tm // CHUNK):
        cols = []
        for g in range(HEADS):
            blk = vn_ref[0, c * CHUNK:(c + 1) * CHUNK, g * LANES:(g + 1) * LANES]
            cols.append(_dot(ws_ref[g], blk) + bs_ref[g])
        rows.append(jnp.concatenate(cols, axis=1))
    sp = jnp.concatenate(rows, axis=0)
    o_mlp = (u_ref[0].astype(F32) * sp * mg_ref[0].astype(F32)).astype(BF16)

    def gated(gate2_ref, o, w):
        gate = jnp.concatenate([gate2_ref[0, 0], gate2_ref[0, 1]], axis=1).astype(F32)
        return gate * _dot(o, w)

    t = (gated(gr_ref, oret_ref[0], wbo_ref[0]) + gated(gm_ref, o_mlp, wbo_ref[1])
         + gated(gd_ref, odiff_ref[0], wbo_ref[2]))
    tb = t.astype(BF16)
    for r in range(tm // out_sub):
        rs = slice(r * out_sub, (r + 1) * out_sub)
        y = _dot(tb[rs, :], wo_ref[...])
        out_ref[0, rs, :] = x_ref[0, rs, :] + gate_ref[...] * (_rms(y) * gpost_ref[...])


def _merge(xs, l, mod4, row_of, g_post, o_ret, o_diff, z, sec, ws, bs, wbo, wo, *, tm):
    n_b, n, _ = xs.shape
    row_block = lambda b, i: i

    def zsec(name):
        return _z_spec(sec[name], tm, SEC, False, row_of=row_block)

    def zgate(k):
        return pl.BlockSpec((1, 2, tm, SEC), lambda b, i: (b, k, i, 0))

    in_specs = [
        pl.BlockSpec((1, tm, D_MODEL), lambda b, i: (b, i, 0)),
        _mod_spec(l, row_of, 2),
        _layer_spec(l, (1, D_MODEL)),
        pl.BlockSpec((1, tm, BRANCH_W), lambda b, i: (b, i, 0)),
        pl.BlockSpec((1, tm, BRANCH_W), lambda b, i: (b, i, 0)),
        zsec("mu"), zsec("mv"), zsec("mg"), zgate(0), zgate(1), zgate(2),
        _layer_spec(l, (HEADS, CHUNK, CHUNK)),
        _layer_spec(l, (HEADS, CHUNK, LANES)),
        _layer_spec(l, (N_BRANCH, BRANCH_W, D_MODEL)),
        _layer_spec(l, (D_MODEL, D_MODEL)),
    ]
    return pl.pallas_call(
        functools.partial(_merge_kernel, tm=tm, out_sub=MERGE_OUT_SUB),
        grid=(n_b, n // tm),
        in_specs=in_specs,
        out_specs=pl.BlockSpec((1, tm, D_MODEL), lambda b, i: (b, i, 0)),
        out_shape=jax.ShapeDtypeStruct(xs.shape, F32),
        compiler_params=pltpu.CompilerParams(vmem_limit_bytes=VMEM_LIMIT),
        name="merge",
    )(xs, mod4, g_post, o_ret, o_diff, z, z, z, z, z, z, ws, bs, wbo, wo)


def _rope_tables(n_lat):
    rows = n_lat // GRID_W
    row_pos = jnp.repeat(jnp.arange(rows, dtype=F32), GRID_W)
    col_pos = jnp.tile(jnp.arange(GRID_W, dtype=F32), rows)

    def angles(head_dim):
        n_freq = head_dim // 4
        inv = ROPE_BASE ** (-jnp.arange(n_freq, dtype=F32) / n_freq)
        ang = jnp.concatenate([row_pos[:, None] * inv, col_pos[:, None] * inv], axis=-1)
        return jnp.cos(ang), jnp.sin(ang)

    cos_r, sin_r = angles(HEAD_W)
    cos_d, sin_d = angles(DIFF_DQK)
    zeros = jnp.zeros_like(sin_d)
    c_r = jnp.concatenate([cos_r, cos_r], axis=-1)
    s_r = jnp.concatenate([-sin_r, sin_r], axis=-1)
    c_d = jnp.tile(jnp.concatenate([cos_d, cos_d], axis=-1), (1, 2))
    s_lo = jnp.tile(jnp.concatenate([-sin_d, zeros], axis=-1), (1, 2))
    s_hi = jnp.tile(jnp.concatenate([zeros, sin_d], axis=-1), (1, 2))
    return c_r, s_r, c_d, s_lo, s_hi


@jax.jit
def kernel(x, c, ctx, c_ctx, w_mod, b_mod, g_pre, g_post, w_in, ret_decay_logit, mlp_w_s,
           mlp_b_s, diff_lambda_q, diff_lambda_k, w_branch_out, w_out):
    n_b, n_lat, _ = x.shape
    n_ctx = ctx.shape[1]
    tables = jnp.stack(_rope_tables(n_lat))

    cond_rows = 16
    ctx_row = n_b
    cc = jnp.zeros((cond_rows, D_MODEL), F32).at[:n_b].set(c).at[ctx_row].set(c_ctx)
    mod4 = _modulation(cc, w_mod, b_mod).reshape(DEPTH, cond_rows, 1, 3 * D_MODEL)
    lat_row = lambda b: b
    ctx_row_of = lambda b: ctx_row

    log_gamma = -jax.nn.softplus(-ret_decay_logit.astype(F32))
    lam_inits = [0.8 - 0.6 * math.exp(-0.3 * l) for l in range(DEPTH)]
    lam = (jnp.exp(jnp.sum(diff_lambda_q[:, 0] * diff_lambda_k[:, 0], axis=-1))
           - jnp.exp(jnp.sum(diff_lambda_q[:, 1] * diff_lambda_k[:, 1], axis=-1))
           + jnp.asarray(lam_inits, F32)).astype(F32)

    w_sec = w_in
    ws = mlp_w_s.astype(BF16)
    bs = jnp.broadcast_to(mlp_b_s[..., None], (DEPTH, HEADS, CHUNK, LANES)).astype(F32)
    wbo = w_branch_out.astype(BF16)
    wo = w_out.astype(BF16)
    g_pre3 = g_pre.reshape(DEPTH, 1, D_MODEL)
    g_post3 = g_post.reshape(DEPTH, 1, D_MODEL)

    for l in range(DEPTH):
        last = l == DEPTH - 1
        post_scale = 1.0 - lam_inits[l]

        ctx_flat = ctx.reshape(1, n_b * n_ctx, D_MODEL)
        if last:
            zc = _projection(ctx_flat, l, g_pre3, mod4, ctx_row_of, w_sec, lambda j: j,
                             KV_KINDS, None)
            sec_c = KV_SEC
        else:
            zc = _projection(ctx_flat, l, g_pre3, mod4, ctx_row_of, w_sec, _full_src,
                             FULL_KINDS, None)
            sec_c = FULL_SEC
            co_ret = _retention_ctx(log_gamma, l, zc, sec_c, n_b, n_ctx, rc=RET_CHUNK,
                                    heads=HEADS)
            co_diff = _diff_attention_ctx(lam, l, zc, sec_c, n_b, n_ctx, post_scale=post_scale)
            ctx_next = _merge(ctx_flat, l, mod4, ctx_row_of, g_post3,
                              co_ret.reshape(1, n_b * n_ctx, BRANCH_W),
                              co_diff.reshape(1, n_b * n_ctx, BRANCH_W), zc, sec_c,
                              ws, bs, wbo, wo, tm=MERGE_TM)
            ctx_next = ctx_next.reshape(n_b, n_ctx, D_MODEL)

        z = _projection(x, l, g_pre3, mod4, lat_row, w_sec, _full_src, FULL_KINDS, tables)
        o_ret, o_diff = _latent_mixers(lam, log_gamma, l, z, FULL_SEC, n_b, n_lat, zc, sec_c,
                                       n_ctx, ts=DIFF_TS, rc=RET_CHUNK, post_scale=post_scale)
        x = _merge(x, l, mod4, lat_row, g_post3, o_ret, o_diff, z, FULL_SEC, ws, bs, wbo, wo,
                   tm=MERGE_TM)
        if not last:
            ctx = ctx_next
    return x
```

```python
import functools
import math

import jax
import jax.numpy as jnp
from jax import lax
from jax.experimental import pallas as pl
from jax.experimental.pallas import tpu as pltpu

F32 = jnp.float32
BF16 = jnp.bfloat16

D_MODEL = 1024
DEPTH = 2
GRID_W = 64
N_BRANCH = 3
BRANCH_W = D_MODEL // 2
HEADS = 4
HEAD_W = BRANCH_W // HEADS
DIFF_DQK = HEAD_W // 2
CHUNK = 128
ROPE_BASE = 10000.0
EPS = 1e-6
RET_SCALE = HEAD_W ** -0.5
DIFF_Q_SCALE = (DIFF_DQK ** -0.5) * math.log2(math.e)

LANES = 128
SEC = BRANCH_W
MERGE_COLS = N_BRANCH * D_MODEL
KV_COLS = 4 * BRANCH_W
IN_COLS = KV_COLS + 7 * BRANCH_W + MERGE_COLS
VMEM_LIMIT = 56 * 1024 * 1024

PROJ_SUB = 512
PROJ_OUT_SLOTS = 2
DIFF_TS = 256
RET_CHUNK = 256
MERGE_TM = 512
MERGE_OUT_SUB = 512

FULL_KINDS = ("sigmoid",) * 6 + ("rk", "plain", "dk", "plain", "rq", "silu", "dq", "silu",
                                 "plain", "ln", "silu")
FULL_SEC = dict(rk=6, rv=7, dk=8, dv=9, rq=10, rg=11, dq=12, dg=13, mu=14, mv=15, mg=16)
KV_KINDS = ("rk", "plain", "dk", "plain")
KV_SEC = dict(rk=0, rv=1, dk=2, dv=3)
_N_MERGE_SEC = MERGE_COLS // SEC
_N_REST_SEC = (KV_COLS + 7 * BRANCH_W) // SEC


def _full_src(j):
    return jnp.where(j < _N_MERGE_SEC, j + _N_REST_SEC, j - _N_MERGE_SEC)


def _dot(a, b):
    return jnp.dot(a, b, preferred_element_type=F32)


def _dot_nt(a, b):
    return lax.dot_general(a, b, (((1,), (1,)), ((), ())), preferred_element_type=F32)


def _rms(v):
    return v * lax.rsqrt(jnp.mean(v * v, axis=-1, keepdims=True) + EPS)


def _mod_spec(l, row_of, k):
    return pl.BlockSpec((None, None, 1, D_MODEL), lambda *idx: (l, row_of(idx[0]), 0, k))


def _layer_spec(l, shape, **kwargs):
    zeros = (0,) * len(shape)
    return pl.BlockSpec((None,) + tuple(shape), lambda *idx: (l,) + zeros, **kwargs)


def _z_spec(sec_idx, rows, width, flat, col_of=lambda *idx: 0, row_of=None):
    if row_of is None:
        row_of = (lambda *idx: idx[0]) if flat else (lambda *idx: 0)
    group_of = (lambda *idx: 0) if flat else (lambda *idx: idx[0])
    return pl.BlockSpec((1, None, rows, width),
                        lambda *idx: (group_of(*idx), sec_idx, row_of(*idx), col_of(*idx)))


def _split_bf16(a):
    hi = a.astype(BF16)
    return hi, (a - hi.astype(F32)).astype(BF16)


def _mod_kernel(c_ref, w_ref, b_ref, o_ref):
    cv = c_ref[...]
    c_hi, c_lo = _split_bf16(cv * jax.nn.sigmoid(cv))
    w_hi, w_lo = _split_bf16(w_ref[0])
    o_ref[0] = _dot(c_hi, w_hi) + _dot(c_hi, w_lo) + _dot(c_lo, w_hi) + b_ref[0]


def _modulation(cc, w_mod, b_mod):
    rows = cc.shape[0]
    return pl.pallas_call(
        _mod_kernel,
        grid=(DEPTH, 3),
        in_specs=[
            pl.BlockSpec((rows, D_MODEL), lambda l, j: (0, 0)),
            pl.BlockSpec((1, D_MODEL, D_MODEL), lambda l, j: (l, 0, j)),
            pl.BlockSpec((1, 1, D_MODEL), lambda l, j: (l, 0, j)),
        ],
        out_specs=pl.BlockSpec((1, rows, D_MODEL), lambda l, j: (l, 0, j)),
        out_shape=jax.ShapeDtypeStruct((DEPTH, rows, 3 * D_MODEL), F32),
        compiler_params=pltpu.CompilerParams(vmem_limit_bytes=VMEM_LIMIT),
        name="modulation",
    )(cc, w_mod, b_mod.reshape(DEPTH, 1, 3 * D_MODEL))


def _rope_ret(a, c, s):
    outs = []
    for h in range(HEADS):
        ah = a[:, h * LANES:(h + 1) * LANES]
        outs.append(ah * c + pltpu.roll(ah, LANES // 2, 1) * s)
    return jnp.concatenate(outs, axis=1)


def _rope_diff(a, c, s_lo, s_hi):
    outs = []
    for h in range(HEADS):
        ah = a[:, h * LANES:(h + 1) * LANES]
        outs.append(ah * c + pltpu.roll(ah, LANES - DIFF_DQK // 2, 1) * s_lo
                    + pltpu.roll(ah, DIFF_DQK // 2, 1) * s_hi)
    return jnp.concatenate(outs, axis=1)


def _proj_kernel(*refs, layer, kinds, src_of, n_src, rope, n_rows, sub):
    if rope:
        (x_ref, g_ref, shift_ref, scale_ref, w_hbm, tab_hbm, z_hbm,
         h_ref, stage, w_vmem, w_stage, sems_out, sems_w, tab_vmem, sem_tab) = refs
        cr_ref, sr_ref, cd_ref, sdl_ref, sdh_ref = [tab_vmem.at[i] for i in range(5)]
    else:
        (x_ref, g_ref, shift_ref, scale_ref, w_hbm, z_hbm,
         h_ref, stage, w_vmem, w_stage, sems_out, sems_w) = refs
    b = pl.program_id(0)
    first = b == 0
    n_sub = n_rows // sub
    n_sec = len(kinds)
    n_slots = stage.shape[0]

    n_wslots = w_stage.shape[0]

    def weight_copy(j):
        start = src_of(j) * SEC
        cols = pl.ds(start if isinstance(start, int) else pl.multiple_of(start, SEC), SEC)
        slot = j % n_wslots
        return pltpu.make_async_copy(w_hbm.at[layer, :, cols], w_stage.at[slot], sems_w.at[slot])

    def load_weights(j):
        weight_copy(j).wait()
        src = w_stage.at[j % n_wslots]
        dst = w_vmem.at[src_of(j)]
        for r in range(D_MODEL // LANES):
            rows = slice(r * LANES, (r + 1) * LANES)
            dst[rows, :] = src[rows, :].astype(BF16)

        @pl.when(j + n_wslots < n_sec)
        def _():
            weight_copy(j + n_wslots).start()

    def table_copy():
        return pltpu.make_async_copy(tab_hbm, tab_vmem, sem_tab.at[0])

    @pl.when(first)
    def _():
        for j in range(min(n_wslots, n_sec)):
            weight_copy(j).start()
        if rope:
            table_copy().start()

    gs = g_ref[...] * (1.0 + scale_ref[...])
    sh = shift_ref[...]
    for r in range(n_sub):
        xs = x_ref[0, r * sub:(r + 1) * sub, :]
        h_ref[r * sub:(r + 1) * sub, :] = (_rms(xs) * gs + sh).astype(BF16)

    if rope:
        pl.when(first)(lambda: table_copy().wait())

    def epi_rk(a, rows):
        if rope:
            a = _rope_ret(a, cr_ref[rows, :], sr_ref[rows, :])
        return a * RET_SCALE

    def epi_rq(a, rows):
        return _rope_ret(a, cr_ref[rows, :], sr_ref[rows, :]) if rope else a

    def epi_dk(a, rows):
        return _rope_diff(a, cd_ref[rows, :], sdl_ref[rows, :], sdh_ref[rows, :]) if rope else a

    def epi_dq(a, rows):
        return epi_dk(a, rows) * DIFF_Q_SCALE

    def epi_ln(a, rows):
        mu = jnp.mean(a, axis=-1, keepdims=True)
        d = a - mu
        return d * lax.rsqrt(jnp.mean(d * d, axis=-1, keepdims=True) + EPS)

    epilogues = dict(
        plain=lambda a, rows: a,
        sigmoid=lambda a, rows: jax.nn.sigmoid(a),
        silu=lambda a, rows: a * jax.nn.sigmoid(a),
        rk=epi_rk, rq=epi_rq, dk=epi_dk, dq=epi_dq, ln=epi_ln)

    def writeback(j):
        slot = j % n_slots
        return pltpu.make_async_copy(stage.at[slot], z_hbm.at[b, j], sems_out.at[slot])

    def run(epilogue, j):
        w_sec = w_vmem.at[src_of(j)]
        out = stage.at[j % n_slots]
        for r in range(n_sub):
            rows = slice(r * sub, (r + 1) * sub)
            out[rows, :] = epilogue(_dot(h_ref[rows, :], w_sec[...]), rows).astype(BF16)

    def section(j, carry):
        pl.when(first)(functools.partial(load_weights, j))

        @pl.when(j >= n_slots)
        def _():
            writeback(j - n_slots).wait()

        for kind in sorted(set(kinds)):
            idx = [i for i, k in enumerate(kinds) if k == kind]
            cond = j == idx[0]
            for i in idx[1:]:
                cond = cond | (j == i)
            pl.when(cond)(functools.partial(run, epilogues[kind], j))
        writeback(j).start()
        return carry

    lax.fori_loop(0, n_sec, section, 0)
    for j in range(max(0, n_sec - n_slots), n_sec):
        writeback(j).wait()


def _projection(xs, l, g_pre, mod4, row_of, w_all, src_of, kinds, tables):
    n_g, n_rows, _ = xs.shape
    n_sec = len(kinds)
    rope = tables is not None
    n_src = w_all.shape[2] // SEC if src_of is _full_src else n_sec
    in_specs = [
        pl.BlockSpec((1, n_rows, D_MODEL), lambda b: (b, 0, 0)),
        _layer_spec(l, (1, D_MODEL)),
        _mod_spec(l, row_of, 0),
        _mod_spec(l, row_of, 1),
        pl.BlockSpec(memory_space=pl.ANY),
    ]
    args = [xs, g_pre, mod4, mod4, w_all]
    scratch = [pltpu.VMEM((n_rows, D_MODEL), BF16),
               pltpu.VMEM((PROJ_OUT_SLOTS, n_rows, SEC), BF16),
               pltpu.VMEM((n_src, D_MODEL, SEC), BF16),
               pltpu.VMEM((2, D_MODEL, SEC), F32),
               pltpu.SemaphoreType.DMA((PROJ_OUT_SLOTS,)),
               pltpu.SemaphoreType.DMA((2,))]
    if rope:
        in_specs.append(pl.BlockSpec(memory_space=pl.ANY))
        args.append(tables)
        scratch += [pltpu.VMEM(tables.shape, F32), pltpu.SemaphoreType.DMA((1,))]
    return pl.pallas_call(
        functools.partial(_proj_kernel, layer=l, kinds=kinds, src_of=src_of, n_src=n_src,
                          rope=rope, n_rows=n_rows, sub=PROJ_SUB),
        grid=(n_g,),
        in_specs=in_specs,
        out_specs=pl.BlockSpec(memory_space=pl.ANY),
        out_shape=jax.ShapeDtypeStruct((n_g, n_sec, n_rows, SEC), BF16),
        scratch_shapes=scratch,
        compiler_params=pltpu.CompilerParams(vmem_limit_bytes=VMEM_LIMIT),
        name="projection_rope" if rope else "projection",
    )(*args)


def _retention_tasks(lgf, lgb, cs, q_ref, g_ref, k_ref, v_ref, kc_ref, vc_ref, o_ref, inc_ref,
                     st_ref, *, n, n_ctx, rc):
    nc = n // rc
    pos = lax.broadcasted_iota(jnp.int32, (rc, LANES), 0).astype(F32)
    kdf = jnp.exp(lgf * (rc - 1.0 - pos)).astype(BF16)
    kdb = jnp.exp(lgb * pos).astype(BF16)
    cdf = jnp.exp(jnp.full((1, LANES), lgf * rc, F32))
    cdb = jnp.exp(jnp.full((1, LANES), lgb * rc, F32))
    qdf = jnp.exp(lgf * (pos + 1.0)).astype(BF16)
    qdb = jnp.exp(lgb * (rc - pos)).astype(BF16)
    ii = lax.broadcasted_iota(jnp.int32, (rc, rc), 0)
    jj = lax.broadcasted_iota(jnp.int32, (rc, rc), 1)
    dist = (ii - jj).astype(F32)
    dmat = (jnp.where(dist >= 0, jnp.exp(lgf * jnp.maximum(dist, 0.0)), 0.0)
            + jnp.where(dist <= 0, jnp.exp(lgb * jnp.maximum(-dist, 0.0)), 0.0))
    state = {}

    def increment(kr, vr, nn):
        rows = slice(nn * rc, (nn + 1) * rc)
        kk = kr[0, rows, cs]
        kcat = jnp.concatenate([kk * kdf, kk * kdb], axis=1)
        vt = vr[0, rows, cs].astype(F32).T.astype(BF16)
        return _dot(vt, kcat)

    def init_task():
        sf = jnp.zeros((HEAD_W, HEAD_W), F32)
        sb = jnp.zeros((HEAD_W, HEAD_W), F32)
        if n_ctx:
            incs = [increment(kc_ref, vc_ref, nn) for nn in range(n_ctx // rc)]
            for inc in incs:
                sf = cdf * sf + inc[:, :HEAD_W]
            for inc in reversed(incs):
                sb = cdb * sb + inc[:, HEAD_W:]
        state["sf"], state["sb"] = sf, sb

    def inc_task(nn):
        inc_ref[nn] = increment(k_ref, v_ref, nn)

    def scan_task():
        sf, sb = state["sf"], state["sb"]
        for nn in range(nc):
            st_ref[nn, :, 0:HEAD_W] = sf.astype(BF16)
            sf = cdf * sf + inc_ref[nn, :, 0:HEAD_W]
        for nn in reversed(range(nc)):
            st_ref[nn, :, HEAD_W:] = sb.astype(BF16)
            sb = cdb * sb + inc_ref[nn, :, HEAD_W:]

    def out_task(nn):
        rows = slice(nn * rc, (nn + 1) * rc)
        q = q_ref[0, rows, cs]
        s = _dot_nt(q, k_ref[0, rows, cs]) * dmat
        inner = _dot(s.astype(BF16), v_ref[0, rows, cs])
        qcat = jnp.concatenate([q * qdf, q * qdb], axis=1)
        out = inner + _dot_nt(qcat, st_ref[nn])
        o_ref[0, rows, cs] = _rms(out).astype(BF16) * g_ref[0, rows, cs]

    inc_tasks = [init_task] + [functools.partial(inc_task, nn) for nn in range(nc)]
    return inc_tasks, scan_task, [functools.partial(out_task, nn) for nn in range(nc)]


def _ret_kernel(lg_ref, q_ref, g_ref, k_ref, v_ref, o_ref, inc_ref, st_ref, *, layer, n, rc,
                heads):
    tasks = []
    for hh in range(heads):
        head = pl.program_id(1) * heads + hh
        tasks.append(_retention_tasks(
            lg_ref[layer, 0, head], lg_ref[layer, 1, head], slice(hh * LANES, (hh + 1) * LANES),
            q_ref, g_ref, k_ref, v_ref, None, None, o_ref, inc_ref.at[hh], st_ref.at[hh],
            n=n, n_ctx=0, rc=rc))
    for i in range(len(tasks[0][0])):
        for inc_tasks, _, _ in tasks:
            inc_tasks[i]()
    for _, scan_task, _ in tasks:
        scan_task()
    for i in range(len(tasks[0][2])):
        for _, _, out_tasks in tasks:
            out_tasks[i]()


def _retention_ctx(log_gamma, layer, zc, sec, n_b, n, *, rc, heads):
    width = heads * LANES
    head_block = lambda b, h: h
    return pl.pallas_call(
        functools.partial(_ret_kernel, layer=layer, n=n, rc=rc, heads=heads),
        grid=(n_b, HEADS // heads),
        in_specs=[pl.BlockSpec(memory_space=pltpu.SMEM)]
        + [_z_spec(sec[name], n, width, True, col_of=head_block)
           for name in ("rq", "rg", "rk", "rv")],
        out_specs=pl.BlockSpec((1, n, width), lambda b, h: (b, 0, h)),
        out_shape=jax.ShapeDtypeStruct((n_b, n, BRANCH_W), BF16),
        scratch_shapes=[pltpu.VMEM((heads, n // rc, HEAD_W, 2 * HEAD_W), F32),
                        pltpu.VMEM((heads, n // rc, HEAD_W, 2 * HEAD_W), BF16)],
        compiler_params=pltpu.CompilerParams(vmem_limit_bytes=VMEM_LIMIT),
        name="retention_ctx",
    )(log_gamma, zc, zc, zc, zc)


def _split_maps(q, lane):
    zero = jnp.zeros_like(q)
    return jnp.concatenate([jnp.where(lane < DIFF_DQK, q, zero),
                            jnp.where(lane >= DIFF_DQK, q, zero)], axis=0)


def _diff_ctx_kernel(lam_ref, q_ref, g_ref, k_ref, v_ref, o_ref, *, layer, tq, post_scale):
    lane = lax.broadcasted_iota(jnp.int32, (tq, LANES), 1)
    for h in range(HEADS):
        cs = slice(h * LANES, (h + 1) * LANES)
        s = _dot_nt(_split_maps(q_ref[0, :, cs], lane), k_ref[0, :, cs])
        p = jnp.exp2(s - jnp.max(s, axis=-1, keepdims=True))
        o = _dot(p.astype(BF16), v_ref[0, :, cs]) / jnp.sum(p, axis=-1, keepdims=True)
        d = o[:tq] - lam_ref[layer] * o[tq:]
        o_ref[0, :, cs] = (_rms(d) * post_scale * g_ref[0, :, cs].astype(F32)).astype(BF16)


def _diff_attention_ctx(lam, layer, zc, sec, n_b, n, *, post_scale):
    return pl.pallas_call(
        functools.partial(_diff_ctx_kernel, layer=layer, tq=n, post_scale=post_scale),
        grid=(n_b,),
        in_specs=[pl.BlockSpec(memory_space=pltpu.SMEM)]
        + [_z_spec(sec[name], n, SEC, True) for name in ("dq", "dg", "dk", "dv")],
        out_specs=pl.BlockSpec((1, n, BRANCH_W), lambda b: (b, 0, 0)),
        out_shape=jax.ShapeDtypeStruct((n_b, n, BRANCH_W), BF16),
        compiler_params=pltpu.CompilerParams(vmem_limit_bytes=VMEM_LIMIT),
        name="diff_attention_ctx",
    )(lam, zc, zc, zc, zc)


def _mixer_kernel(lam_ref, lg_ref, q_ref, g_ref, kc_ref, vc_ref, kl_ref, vl_ref,
                  rq_ref, rg_ref, rk_ref, rv_ref, rkc_ref, rvc_ref, od_ref, or_ref,
                  kall, vall, s0, s1, p0, p1, inc_ref, st_ref, *, layer, n, n_ctx, ts, rc,
                  post_scale):
    n_sub = n // ts
    nk = kall.shape[0]
    kall[0:n_ctx, :] = kc_ref[0]
    kall[n_ctx:, :] = kl_ref[0]
    vall[0:n_ctx, 0:LANES] = vc_ref[0]
    vall[n_ctx:, 0:LANES] = vl_ref[0]
    vall[:, LANES:] = jnp.ones((nk, LANES), BF16)
    lam = lam_ref[layer]
    lane = lax.broadcasted_iota(jnp.int32, (ts, LANES), 1)
    s_bufs, p_bufs = (s0, s1), (p0, p1)

    def stage_a(t):
        rows = slice(t * ts, (t + 1) * ts)
        s_bufs[t % 2][...] = _dot_nt(_split_maps(q_ref[0, rows, :], lane), kall[...])

    def stage_b(t):
        s = s_bufs[t % 2][...]
        p_bufs[t % 2][...] = jnp.exp2(s - jnp.max(s, axis=-1, keepdims=True)).astype(BF16)

    def stage_c(t):
        rows = slice(t * ts, (t + 1) * ts)
        oe = _dot(p_bufs[t % 2][...], vall[...])
        o = oe[:, :LANES] / oe[:, LANES:]
        d = o[:ts] - lam * o[ts:]
        od_ref[0, rows, :] = (_rms(d) * post_scale * g_ref[0, rows, :].astype(F32)).astype(BF16)

    head = pl.program_id(1)
    inc_tasks, scan_task, out_tasks = _retention_tasks(
        lg_ref[layer, 0, head], lg_ref[layer, 1, head], slice(0, LANES),
        rq_ref, rg_ref, rk_ref, rv_ref, rkc_ref, rvc_ref, or_ref, inc_ref, st_ref,
        n=n, n_ctx=n_ctx, rc=rc)
    n_steps = n_sub + 2
    extra = [[] for _ in range(n_steps)]
    half = (len(inc_tasks) + 1) // 2
    extra[0] += inc_tasks[:half]
    extra[1] += inc_tasks[half:]
    extra[2].append(scan_task)
    for i, task in enumerate(out_tasks):
        extra[2 + (i * (n_steps - 2)) // len(out_tasks)].append(task)

    for t in range(n_steps):
        if t >= 2:
            stage_c(t - 2)
        if 1 <= t <= n_sub:
            stage_b(t - 1)
        if t < n_sub:
            stage_a(t)
        for task in extra[t]:
            task()


def _latent_mixers(lam, log_gamma, layer, z, sec, n_b, n, zc, sec_c, n_ctx, *, ts, rc,
                   post_scale):
    nk = n_ctx + n
    head_block = lambda b, h: h
    lat = lambda name: _z_spec(sec[name], n, LANES, False, col_of=head_block)
    cx = lambda name: _z_spec(sec_c[name], n_ctx, LANES, True, col_of=head_block)
    out_spec = pl.BlockSpec((1, n, LANES), lambda b, h: (b, 0, h))
    out_shape = jax.ShapeDtypeStruct((n_b, n, BRANCH_W), BF16)
    o_diff, o_ret = pl.pallas_call(
        functools.partial(_mixer_kernel, layer=layer, n=n, n_ctx=n_ctx, ts=ts, rc=rc,
                          post_scale=post_scale),
        grid=(n_b, HEADS),
        in_specs=[pl.BlockSpec(memory_space=pltpu.SMEM), pl.BlockSpec(memory_space=pltpu.SMEM),
                  lat("dq"), lat("dg"), cx("dk"), cx("dv"), lat("dk"), lat("dv"),
                  lat("rq"), lat("rg"), lat("rk"), lat("rv"), cx("rk"), cx("rv")],
        out_specs=[out_spec, out_spec],
        out_shape=[out_shape, out_shape],
        scratch_shapes=[pltpu.VMEM((nk, LANES), BF16), pltpu.VMEM((nk, 2 * LANES), BF16),
                        pltpu.VMEM((2 * ts, nk), F32), pltpu.VMEM((2 * ts, nk), F32),
                        pltpu.VMEM((2 * ts, nk), BF16), pltpu.VMEM((2 * ts, nk), BF16),
                        pltpu.VMEM((n // rc, HEAD_W, 2 * HEAD_W), F32),
                        pltpu.VMEM((n // rc, HEAD_W, 2 * HEAD_W), BF16)],
        compiler_params=pltpu.CompilerParams(vmem_limit_bytes=VMEM_LIMIT),
        name="latent_mixers",
    )(lam, log_gamma, z, z, zc, zc, z, z, z, z, z, z, zc, zc)
    return o_ret, o_diff


def _merge_kernel(x_ref, gate_ref, gpost_ref, oret_ref, odiff_ref, u_ref, vn_ref, mg_ref,
                  gr_ref, gm_ref, gd_ref, ws_ref, bs_ref, wbo_ref, wo_ref, out_ref, *, tm, out_sub):
    rows = []
    for c in range(tm // CHUNK):
        cols = []
        for g in range(HEADS):
            blk = vn_ref[0, c * CHUNK:(c + 1) * CHUNK, g * LANES:(g + 1) * LANES]
            cols.append(_dot(ws_ref[g], blk) + bs_ref[g])
        rows.append(jnp.concatenate(cols, axis=1))
    sp = jnp.concatenate(rows, axis=0)
    o_mlp = (u_ref[0].astype(F32) * sp * mg_ref[0].astype(F32)).astype(BF16)

    def gated(gate2_ref, o, w):
        gate = jnp.concatenate([gate2_ref[0, 0], gate2_ref[0, 1]], axis=1).astype(F32)
        return gate * _dot(o, w)

    t = (gated(gr_ref, oret_ref[0], wbo_ref[0]) + gated(gm_ref, o_mlp, wbo_ref[1])
         + gated(gd_ref, odiff_ref[0], wbo_ref[2]))
    tb = t.astype(BF16)
    for r in range(tm // out_sub):
        rs = slice(r * out_sub, (r + 1) * out_sub)
        y = _dot(tb[rs, :], wo_ref[...])
        out_ref[0, rs, :] = x_ref[0, rs, :] + gate_ref[...] * (_rms(y) * gpost_ref[...])


def _merge(xs, l, mod4, row_of, g_post, o_ret, o_diff, z, sec, ws, bs, wbo, wo, *, tm):
    n_b, n, _ = xs.shape
    row_block = lambda b, i: i

    def zsec(name):
        return _z_spec(sec[name], tm, SEC, False, row_of=row_block)

    def zgate(k):
        return pl.BlockSpec((1, 2, tm, SEC), lambda b, i: (b, k, i, 0))

    in_specs = [
        pl.BlockSpec((1, tm, D_MODEL), lambda b, i: (b, i, 0)),
        _mod_spec(l, row_of, 2),
        _layer_spec(l, (1, D_MODEL)),
        pl.BlockSpec((1, tm, BRANCH_W), lambda b, i: (b, i, 0)),
        pl.BlockSpec((1, tm, BRANCH_W), lambda b, i: (b, i, 0)),
        zsec("mu"), zsec("mv"), zsec("mg"), zgate(0), zgate(1), zgate(2),
        _layer_spec(l, (HEADS, CHUNK, CHUNK)),
        _layer_spec(l, (HEADS, CHUNK, LANES)),
        _layer_spec(l, (N_BRANCH, BRANCH_W, D_MODEL)),
        _layer_spec(l, (D_MODEL, D_MODEL)),
    ]
    return pl.pallas_call(
        functools.partial(_merge_kernel, tm=tm, out_sub=MERGE_OUT_SUB),
        grid=(n_b, n // tm),
        in_specs=in_specs,
        out_specs=pl.BlockSpec((1, tm, D_MODEL), lambda b, i: (b, i, 0)),
        out_shape=jax.ShapeDtypeStruct(xs.shape, F32),
        compiler_params=pltpu.CompilerParams(vmem_limit_bytes=VMEM_LIMIT),
        name="merge",
    )(xs, mod4, g_post, o_ret, o_diff, z, z, z, z, z, z, ws, bs, wbo, wo)


def _rope_tables(n_lat):
    rows = n_lat // GRID_W
    row_pos = jnp.repeat(jnp.arange(rows, dtype=F32), GRID_W)
    col_pos = jnp.tile(jnp.arange(GRID_W, dtype=F32), rows)

    def angles(head_dim):
        n_freq = head_dim // 4
        inv = ROPE_BASE ** (-jnp.arange(n_freq, dtype=F32) / n_freq)
        ang = jnp.concatenate([row_pos[:, None] * inv, col_pos[:, None] * inv], axis=-1)
        return jnp.cos(ang), jnp.sin(ang)

    cos_r, sin_r = angles(HEAD_W)
    cos_d, sin_d = angles(DIFF_DQK)
    zeros = jnp.zeros_like(sin_d)
    c_r = jnp.concatenate([cos_r, cos_r], axis=-1)
    s_r = jnp.concatenate([-sin_r, sin_r], axis=-1)
    c_d = jnp.tile(jnp.concatenate([cos_d, cos_d], axis=-1), (1, 2))
    s_lo = jnp.tile(jnp.concatenate([-sin_d, zeros], axis=-1), (1, 2))
    s_hi = jnp.tile(jnp.concatenate([zeros, sin_d], axis=-1), (1, 2))
    return c_r, s_r, c_d, s_lo, s_hi


@jax.jit
def kernel(x, c, ctx, c_ctx, w_mod, b_mod, g_pre, g_post, w_in, ret_decay_logit, mlp_w_s,
           mlp_b_s, diff_lambda_q, diff_lambda_k, w_branch_out, w_out):
    n_b, n_lat, _ = x.shape
    n_ctx = ctx.shape[1]
    tables = jnp.stack(_rope_tables(n_lat))

    cond_rows = 16
    ctx_row = n_b
    cc = jnp.zeros((cond_rows, D_MODEL), F32).at[:n_b].set(c).at[ctx_row].set(c_ctx)
    mod4 = _modulation(cc, w_mod, b_mod).reshape(DEPTH, cond_rows, 1, 3 * D_MODEL)
    lat_row = lambda b: b
    ctx_row_of = lambda b: ctx_row

    log_gamma = -jax.nn.softplus(-ret_decay_logit.astype(F32))
    lam_inits = [0.8 - 0.6 * math.exp(-0.3 * l) for l in range(DEPTH)]
    lam = (jnp.exp(jnp.sum(diff_lambda_q[:, 0] * diff_lambda_k[:, 0], axis=-1))
           - jnp.exp(jnp.sum(diff_lambda_q[:, 1] * diff_lambda_k[:, 1], axis=-1))
           + jnp.asarray(lam_inits, F32)).astype(F32)

    w_sec = w_in
    ws = mlp_w_s.astype(BF16)
    bs = jnp.broadcast_to(mlp_b_s[..., None], (DEPTH, HEADS, CHUNK, LANES)).astype(F32)
    wbo = w_branch_out.astype(BF16)
    wo = w_out.astype(BF16)
    g_pre3 = g_pre.reshape(DEPTH, 1, D_MODEL)
    g_post3 = g_post.reshape(DEPTH, 1, D_MODEL)

    for l in range(DEPTH):
        last = l == DEPTH - 1
        post_scale = 1.0 - lam_inits[l]

        ctx_flat = ctx.reshape(1, n_b * n_ctx, D_MODEL)
        if last:
            zc = _projection(ctx_flat, l, g_pre3, mod4, ctx_row_of, w_sec, lambda j: j,
                             KV_KINDS, None)
            sec_c = KV_SEC
        else:
            zc = _projection(ctx_flat, l, g_pre3, mod4, ctx_row_of, w_sec, _full_src,
                             FULL_KINDS, None)
            sec_c = FULL_SEC
            co_ret = _retention_ctx(log_gamma, l, zc, sec_c, n_b, n_ctx, rc=RET_CHUNK,
                                    heads=HEADS)
            co_diff = _diff_attention_ctx(lam, l, zc, sec_c, n_b, n_ctx, post_scale=post_scale)
            ctx_next = _merge(ctx_flat, l, mod4, ctx_row_of, g_post3,
                              co_ret.reshape(1, n_b * n_ctx, BRANCH_W),
                              co_diff.reshape(1, n_b * n_ctx, BRANCH_W), zc, sec_c,
                              ws, bs, wbo, wo, tm=MERGE_TM)
            ctx_next = ctx_next.reshape(n_b, n_ctx, D_MODEL)

        z = _projection(x, l, g_pre3, mod4, lat_row, w_sec, _full_src, FULL_KINDS, tables)
        o_ret, o_diff = _latent_mixers(lam, log_gamma, l, z, FULL_SEC, n_b, n_lat, zc, sec_c,
                                       n_ctx, ts=DIFF_TS, rc=RET_CHUNK, post_scale=post_scale)
        x = _merge(x, l, mod4, lat_row, g_post3, o_ret, o_diff, z, FULL_SEC, ws, bs, wbo, wo,
                   tm=MERGE_TM)
        if not last:
            ctx = ctx_next
    return x
```

```python
import functools
import math

import jax
import jax.numpy as jnp
from jax import lax
from jax.experimental import pallas as pl
from jax.experimental.pallas import tpu as pltpu

F32 = jnp.float32
BF16 = jnp.bfloat16

D_MODEL = 1024
DEPTH = 2
GRID_W = 64
N_BRANCH = 3
BRANCH_W = D_MODEL // 2
HEADS = 4
HEAD_W = BRANCH_W // HEADS
DIFF_DQK = HEAD_W // 2
CHUNK = 128
ROPE_BASE = 10000.0
EPS = 1e-6
RET_SCALE = HEAD_W ** -0.5
DIFF_Q_SCALE = (DIFF_DQK ** -0.5) * math.log2(math.e)

LANES = 128
SEC = BRANCH_W
MERGE_COLS = N_BRANCH * D_MODEL
KV_COLS = 4 * BRANCH_W
IN_COLS = KV_COLS + 7 * BRANCH_W + MERGE_COLS
VMEM_LIMIT = 56 * 1024 * 1024

PROJ_SUB = 256
PROJ_OUT_SLOTS = 2
DIFF_TS = 128
RET_CHUNK = 256
MERGE_TM = 512
MERGE_OUT_SUB = 512

FULL_KINDS = ("sigmoid",) * 6 + ("rk", "plain", "dk", "plain", "rq", "silu", "dq", "silu",
                                 "plain", "ln", "silu")
FULL_SEC = dict(rk=6, rv=7, dk=8, dv=9, rq=10, rg=11, dq=12, dg=13, mu=14, mv=15, mg=16)
KV_KINDS = ("rk", "plain", "dk", "plain")
KV_SEC = dict(rk=0, rv=1, dk=2, dv=3)
_N_MERGE_SEC = MERGE_COLS // SEC
_N_REST_SEC = (KV_COLS + 7 * BRANCH_W) // SEC


def _full_src(j):
    return jnp.where(j < _N_MERGE_SEC, j + _N_REST_SEC, j - _N_MERGE_SEC)


def _dot(a, b):
    return jnp.dot(a, b, preferred_element_type=F32)


def _dot_nt(a, b):
    return lax.dot_general(a, b, (((1,), (1,)), ((), ())), preferred_element_type=F32)


def _rms(v):
    return v * lax.rsqrt(jnp.mean(v * v, axis=-1, keepdims=True) + EPS)


def _mod_spec(l, row_of, k):
    return pl.BlockSpec((None, None, 1, D_MODEL), lambda *idx: (l, row_of(idx[0]), 0, k))


def _layer_spec(l, shape, **kwargs):
    zeros = (0,) * len(shape)
    return pl.BlockSpec((None,) + tuple(shape), lambda *idx: (l,) + zeros, **kwargs)


def _z_spec(sec_idx, rows, width, flat, col_of=lambda *idx: 0, row_of=None):
    if row_of is None:
        row_of = (lambda *idx: idx[0]) if flat else (lambda *idx: 0)
    group_of = (lambda *idx: 0) if flat else (lambda *idx: idx[0])
    return pl.BlockSpec((1, None, rows, width),
                        lambda *idx: (group_of(*idx), sec_idx, row_of(*idx), col_of(*idx)))


def _split_bf16(a):
    hi = a.astype(BF16)
    return hi, (a - hi.astype(F32)).astype(BF16)


def _mod_kernel(c_ref, w_ref, b_ref, o_ref):
    cv = c_ref[...]
    c_hi, c_lo = _split_bf16(cv * jax.nn.sigmoid(cv))
    w_hi, w_lo = _split_bf16(w_ref[0])
    o_ref[0] = _dot(c_hi, w_hi) + _dot(c_hi, w_lo) + _dot(c_lo, w_hi) + b_ref[0]


def _modulation(cc, w_mod, b_mod):
    rows = cc.shape[0]
    return pl.pallas_call(
        _mod_kernel,
        grid=(DEPTH, 3),
        in_specs=[
            pl.BlockSpec((rows, D_MODEL), lambda l, j: (0, 0)),
            pl.BlockSpec((1, D_MODEL, D_MODEL), lambda l, j: (l, 0, j)),
            pl.BlockSpec((1, 1, D_MODEL), lambda l, j: (l, 0, j)),
        ],
        out_specs=pl.BlockSpec((1, rows, D_MODEL), lambda l, j: (l, 0, j)),
        out_shape=jax.ShapeDtypeStruct((DEPTH, rows, 3 * D_MODEL), F32),
        compiler_params=pltpu.CompilerParams(vmem_limit_bytes=VMEM_LIMIT),
        name="modulation",
    )(cc, w_mod, b_mod.reshape(DEPTH, 1, 3 * D_MODEL))


def _rope_ret(a, c, s):
    outs = []
    for h in range(HEADS):
        ah = a[:, h * LANES:(h + 1) * LANES]
        outs.append(ah * c + pltpu.roll(ah, LANES // 2, 1) * s)
    return jnp.concatenate(outs, axis=1)


def _rope_diff(a, c, s_lo, s_hi):
    outs = []
    for h in range(HEADS):
        ah = a[:, h * LANES:(h + 1) * LANES]
        outs.append(ah * c + pltpu.roll(ah, LANES - DIFF_DQK // 2, 1) * s_lo
                    + pltpu.roll(ah, DIFF_DQK // 2, 1) * s_hi)
    return jnp.concatenate(outs, axis=1)


def _proj_kernel(*refs, layer, kinds, src_of, n_src, rope, n_rows, sub):
    if rope:
        (x_ref, g_ref, shift_ref, scale_ref, w_hbm, tab_hbm, z_hbm,
         h_ref, stage, w_vmem, w_stage, sems_out, sems_w, tab_vmem, sem_tab) = refs
        cr_ref, sr_ref, cd_ref, sdl_ref, sdh_ref = [tab_vmem.at[i] for i in range(5)]
    else:
        (x_ref, g_ref, shift_ref, scale_ref, w_hbm, z_hbm,
         h_ref, stage, w_vmem, w_stage, sems_out, sems_w) = refs
    b = pl.program_id(0)
    first = b == 0
    n_sub = n_rows // sub
    n_sec = len(kinds)
    n_slots = stage.shape[0]

    n_wslots = w_stage.shape[0]

    def weight_copy(j):
        start = src_of(j) * SEC
        cols = pl.ds(start if isinstance(start, int) else pl.multiple_of(start, SEC), SEC)
        slot = j % n_wslots
        return pltpu.make_async_copy(w_hbm.at[layer, :, cols], w_stage.at[slot], sems_w.at[slot])

    def load_weights(j):
        weight_copy(j).wait()
        src = w_stage.at[j % n_wslots]
        dst = w_vmem.at[src_of(j)]
        for r in range(D_MODEL // LANES):
            rows = slice(r * LANES, (r + 1) * LANES)
            dst[rows, :] = src[rows, :].astype(BF16)

        @pl.when(j + n_wslots < n_sec)
        def _():
            weight_copy(j + n_wslots).start()

    def table_copy():
        return pltpu.make_async_copy(tab_hbm, tab_vmem, sem_tab.at[0])

    @pl.when(first)
    def _():
        for j in range(min(n_wslots, n_sec)):
            weight_copy(j).start()
        if rope:
            table_copy().start()

    gs = g_ref[...] * (1.0 + scale_ref[...])
    sh = shift_ref[...]
    for r in range(n_sub):
        xs = x_ref[0, r * sub:(r + 1) * sub, :]
        h_ref[r * sub:(r + 1) * sub, :] = (_rms(xs) * gs + sh).astype(BF16)

    if rope:
        pl.when(first)(lambda: table_copy().wait())

    def epi_rk(a, rows):
        if rope:
            a = _rope_ret(a, cr_ref[rows, :], sr_ref[rows, :])
        return a * RET_SCALE

    def epi_rq(a, rows):
        return _rope_ret(a, cr_ref[rows, :], sr_ref[rows, :]) if rope else a

    def epi_dk(a, rows):
        return _rope_diff(a, cd_ref[rows, :], sdl_ref[rows, :], sdh_ref[rows, :]) if rope else a

    def epi_dq(a, rows):
        return epi_dk(a, rows) * DIFF_Q_SCALE

    def epi_ln(a, rows):
        mu = jnp.mean(a, axis=-1, keepdims=True)
        d = a - mu
        return d * lax.rsqrt(jnp.mean(d * d, axis=-1, keepdims=True) + EPS)

    epilogues = dict(
        plain=lambda a, rows: a,
        sigmoid=lambda a, rows: jax.nn.sigmoid(a),
        silu=lambda a, rows: a * jax.nn.sigmoid(a),
        rk=epi_rk, rq=epi_rq, dk=epi_dk, dq=epi_dq, ln=epi_ln)

    def writeback(j):
        slot = j % n_slots
        return pltpu.make_async_copy(stage.at[slot], z_hbm.at[b, j], sems_out.at[slot])

    def run(epilogue, j):
        w_sec = w_vmem.at[src_of(j)]
        out = stage.at[j % n_slots]
        for r in range(n_sub):
            rows = slice(r * sub, (r + 1) * sub)
            out[rows, :] = epilogue(_dot(h_ref[rows, :], w_sec[...]), rows).astype(BF16)

    def section(j, carry):
        pl.when(first)(functools.partial(load_weights, j))

        @pl.when(j >= n_slots)
        def _():
            writeback(j - n_slots).wait()

        for kind in sorted(set(kinds)):
            idx = [i for i, k in enumerate(kinds) if k == kind]
            cond = j == idx[0]
            for i in idx[1:]:
                cond = cond | (j == i)
            pl.when(cond)(functools.partial(run, epilogues[kind], j))
        writeback(j).start()
        return carry

    lax.fori_loop(0, n_sec, section, 0)
    for j in range(max(0, n_sec - n_slots), n_sec):
        writeback(j).wait()


def _projection(xs, l, g_pre, mod4, row_of, w_all, src_of, kinds, tables):
    n_g, n_rows, _ = xs.shape
    n_sec = len(kinds)
    rope = tables is not None
    n_src = w_all.shape[2] // SEC if src_of is _full_src else n_sec
    in_specs = [
        pl.BlockSpec((1, n_rows, D_MODEL), lambda b: (b, 0, 0)),
        _layer_spec(l, (1, D_MODEL)),
        _mod_spec(l, row_of, 0),
        _mod_spec(l, row_of, 1),
        pl.BlockSpec(memory_space=pl.ANY),
    ]
    args = [xs, g_pre, mod4, mod4, w_all]
    scratch = [pltpu.VMEM((n_rows, D_MODEL), BF16),
               pltpu.VMEM((PROJ_OUT_SLOTS, n_rows, SEC), BF16),
               pltpu.VMEM((n_src, D_MODEL, SEC), BF16),
               pltpu.VMEM((2, D_MODEL, SEC), F32),
               pltpu.SemaphoreType.DMA((PROJ_OUT_SLOTS,)),
               pltpu.SemaphoreType.DMA((2,))]
    if rope:
        in_specs.append(pl.BlockSpec(memory_space=pl.ANY))
        args.append(tables)
        scratch += [pltpu.VMEM(tables.shape, F32), pltpu.SemaphoreType.DMA((1,))]
    return pl.pallas_call(
        functools.partial(_proj_kernel, layer=l, kinds=kinds, src_of=src_of, n_src=n_src,
                          rope=rope, n_rows=n_rows, sub=PROJ_SUB),
        grid=(n_g,),
        in_specs=in_specs,
        out_specs=pl.BlockSpec(memory_space=pl.ANY),
        out_shape=jax.ShapeDtypeStruct((n_g, n_sec, n_rows, SEC), BF16),
        scratch_shapes=scratch,
        compiler_params=pltpu.CompilerParams(vmem_limit_bytes=VMEM_LIMIT),
        name="projection_rope" if rope else "projection",
    )(*args)


def _retention_tasks(lgf, lgb, cs, q_ref, g_ref, k_ref, v_ref, kc_ref, vc_ref, o_ref, inc_ref,
                     st_ref, *, n, n_ctx, rc):
    nc = n // rc
    pos = lax.broadcasted_iota(jnp.int32, (rc, LANES), 0).astype(F32)
    kdf = jnp.exp(lgf * (rc - 1.0 - pos)).astype(BF16)
    kdb = jnp.exp(lgb * pos).astype(BF16)
    cdf = jnp.exp(jnp.full((1, LANES), lgf * rc, F32))
    cdb = jnp.exp(jnp.full((1, LANES), lgb * rc, F32))
    qdf = jnp.exp(lgf * (pos + 1.0)).astype(BF16)
    qdb = jnp.exp(lgb * (rc - pos)).astype(BF16)
    ii = lax.broadcasted_iota(jnp.int32, (rc, rc), 0)
    jj = lax.broadcasted_iota(jnp.int32, (rc, rc), 1)
    dist = (ii - jj).astype(F32)
    dmat = (jnp.where(dist >= 0, jnp.exp(lgf * jnp.maximum(dist, 0.0)), 0.0)
            + jnp.where(dist <= 0, jnp.exp(lgb * jnp.maximum(-dist, 0.0)), 0.0))
    state = {}

    def increment(kr, vr, nn):
        rows = slice(nn * rc, (nn + 1) * rc)
        kk = kr[0, rows, cs]
        kcat = jnp.concatenate([kk * kdf, kk * kdb], axis=1)
        vt = vr[0, rows, cs].astype(F32).T.astype(BF16)
        return _dot(vt, kcat)

    def init_task():
        sf = jnp.zeros((HEAD_W, HEAD_W), F32)
        sb = jnp.zeros((HEAD_W, HEAD_W), F32)
        if n_ctx:
            incs = [increment(kc_ref, vc_ref, nn) for nn in range(n_ctx // rc)]
            for inc in incs:
                sf = cdf * sf + inc[:, :HEAD_W]
            for inc in reversed(incs):
                sb = cdb * sb + inc[:, HEAD_W:]
        state["sf"], state["sb"] = sf, sb

    def inc_task(nn):
        inc_ref[nn] = increment(k_ref, v_ref, nn)

    def scan_task():
        sf, sb = state["sf"], state["sb"]
        for nn in range(nc):
            st_ref[nn, :, 0:HEAD_W] = sf.astype(BF16)
            sf = cdf * sf + inc_ref[nn, :, 0:HEAD_W]
        for nn in reversed(range(nc)):
            st_ref[nn, :, HEAD_W:] = sb.astype(BF16)
            sb = cdb * sb + inc_ref[nn, :, HEAD_W:]

    def out_task(nn):
        rows = slice(nn * rc, (nn + 1) * rc)
        q = q_ref[0, rows, cs]
        s = _dot_nt(q, k_ref[0, rows, cs]) * dmat
        inner = _dot(s.astype(BF16), v_ref[0, rows, cs])
        qcat = jnp.concatenate([q * qdf, q * qdb], axis=1)
        out = inner + _dot_nt(qcat, st_ref[nn])
        o_ref[0, rows, cs] = _rms(out).astype(BF16) * g_ref[0, rows, cs]

    inc_tasks = [init_task] + [functools.partial(inc_task, nn) for nn in range(nc)]
    return inc_tasks, scan_task, [functools.partial(out_task, nn) for nn in range(nc)]


def _ret_kernel(lg_ref, q_ref, g_ref, k_ref, v_ref, o_ref, inc_ref, st_ref, *, layer, n, rc,
                heads):
    tasks = []
    for hh in range(heads):
        head = pl.program_id(1) * heads + hh
        tasks.append(_retention_tasks(
            lg_ref[layer, 0, head], lg_ref[layer, 1, head], slice(hh * LANES, (hh + 1) * LANES),
            q_ref, g_ref, k_ref, v_ref, None, None, o_ref, inc_ref.at[hh], st_ref.at[hh],
            n=n, n_ctx=0, rc=rc))
    for i in range(len(tasks[0][0])):
        for inc_tasks, _, _ in tasks:
            inc_tasks[i]()
    for _, scan_task, _ in tasks:
        scan_task()
    for i in range(len(tasks[0][2])):
        for _, _, out_tasks in tasks:
            out_tasks[i]()


def _retention_ctx(log_gamma, layer, zc, sec, n_b, n, *, rc, heads):
    width = heads * LANES
    head_block = lambda b, h: h
    return pl.pallas_call(
        functools.partial(_ret_kernel, layer=layer, n=n, rc=rc, heads=heads),
        grid=(n_b, HEADS // heads),
        in_specs=[pl.BlockSpec(memory_space=pltpu.SMEM)]
        + [_z_spec(sec[name], n, width, True, col_of=head_block)
           for name in ("rq", "rg", "rk", "rv")],
        out_specs=pl.BlockSpec((1, n, width), lambda b, h: (b, 0, h)),
        out_shape=jax.ShapeDtypeStruct((n_b, n, BRANCH_W), BF16),
        scratch_shapes=[pltpu.VMEM((heads, n // rc, HEAD_W, 2 * HEAD_W), F32),
                        pltpu.VMEM((heads, n // rc, HEAD_W, 2 * HEAD_W), BF16)],
        compiler_params=pltpu.CompilerParams(vmem_limit_bytes=VMEM_LIMIT),
        name="retention_ctx",
    )(log_gamma, zc, zc, zc, zc)


def _split_maps(q, lane):
    zero = jnp.zeros_like(q)
    return jnp.concatenate([jnp.where(lane < DIFF_DQK, q, zero),
                            jnp.where(lane >= DIFF_DQK, q, zero)], axis=0)


def _diff_ctx_kernel(lam_ref, q_ref, g_ref, k_ref, v_ref, o_ref, *, layer, tq, post_scale):
    lane = lax.broadcasted_iota(jnp.int32, (tq, LANES), 1)
    for h in range(HEADS):
        cs = slice(h * LANES, (h + 1) * LANES)
        s = _dot_nt(_split_maps(q_ref[0, :, cs], lane), k_ref[0, :, cs])
        p = jnp.exp2(s - jnp.max(s, axis=-1, keepdims=True))
        o = _dot(p.astype(BF16), v_ref[0, :, cs]) / jnp.sum(p, axis=-1, keepdims=True)
        d = o[:tq] - lam_ref[layer] * o[tq:]
        o_ref[0, :, cs] = (_rms(d) * post_scale * g_ref[0, :, cs].astype(F32)).astype(BF16)


def _diff_attention_ctx(lam, layer, zc, sec, n_b, n, *, post_scale):
    return pl.pallas_call(
        functools.partial(_diff_ctx_kernel, layer=layer, tq=n, post_scale=post_scale),
        grid=(n_b,),
        in_specs=[pl.BlockSpec(memory_space=pltpu.SMEM)]
        + [_z_spec(sec[name], n, SEC, True) for name in ("dq", "dg", "dk", "dv")],
        out_specs=pl.BlockSpec((1, n, BRANCH_W), lambda b: (b, 0, 0)),
        out_shape=jax.ShapeDtypeStruct((n_b, n, BRANCH_W), BF16),
        compiler_params=pltpu.CompilerParams(vmem_limit_bytes=VMEM_LIMIT),
        name="diff_attention_ctx",
    )(lam, zc, zc, zc, zc)


def _mixer_kernel(lam_ref, lg_ref, q_ref, g_ref, kc_ref, vc_ref, kl_ref, vl_ref,
                  rq_ref, rg_ref, rk_ref, rv_ref, rkc_ref, rvc_ref, od_ref, or_ref,
                  kall, vall, s0, s1, p0, p1, inc_ref, st_ref, *, layer, n, n_ctx, ts, rc,
                  post_scale):
    n_sub = n // ts
    nk = kall.shape[0]
    kall[0:n_ctx, :] = kc_ref[0]
    kall[n_ctx:, :] = kl_ref[0]
    vall[0:n_ctx, 0:LANES] = vc_ref[0]
    vall[n_ctx:, 0:LANES] = vl_ref[0]
    vall[:, LANES:] = jnp.ones((nk, LANES), BF16)
    lam = lam_ref[layer]
    lane = lax.broadcasted_iota(jnp.int32, (ts, LANES), 1)
    s_bufs, p_bufs = (s0, s1), (p0, p1)

    def stage_a(t):
        rows = slice(t * ts, (t + 1) * ts)
        s_bufs[t % 2][...] = _dot_nt(_split_maps(q_ref[0, rows, :], lane), kall[...])

    def stage_b(t):
        s = s_bufs[t % 2][...]
        p_bufs[t % 2][...] = jnp.exp2(s - jnp.max(s, axis=-1, keepdims=True)).astype(BF16)

    def stage_c(t):
        rows = slice(t * ts, (t + 1) * ts)
        oe = _dot(p_bufs[t % 2][...], vall[...])
        o = oe[:, :LANES] / oe[:, LANES:]
        d = o[:ts] - lam * o[ts:]
        od_ref[0, rows, :] = (_rms(d) * post_scale * g_ref[0, rows, :].astype(F32)).astype(BF16)

    head = pl.program_id(1)
    inc_tasks, scan_task, out_tasks = _retention_tasks(
        lg_ref[layer, 0, head], lg_ref[layer, 1, head], slice(0, LANES),
        rq_ref, rg_ref, rk_ref, rv_ref, rkc_ref, rvc_ref, or_ref, inc_ref, st_ref,
        n=n, n_ctx=n_ctx, rc=rc)
    n_steps = n_sub + 2
    extra = [[] for _ in range(n_steps)]
    half = (len(inc_tasks) + 1) // 2
    extra[0] += inc_tasks[:half]
    extra[1] += inc_tasks[half:]
    extra[2].append(scan_task)
    for i, task in enumerate(out_tasks):
        extra[2 + (i * (n_steps - 2)) // len(out_tasks)].append(task)

    for t in range(n_steps):
        if t >= 2:
            stage_c(t - 2)
        if 1 <= t <= n_sub:
            stage_b(t - 1)
        if t < n_sub:
            stage_a(t)
        for task in extra[t]:
            task()


def _latent_mixers(lam, log_gamma, layer, z, sec, n_b, n, zc, sec_c, n_ctx, *, ts, rc,
                   post_scale):
    nk = n_ctx + n
    head_block = lambda b, h: h
    lat = lambda name: _z_spec(sec[name], n, LANES, False, col_of=head_block)
    cx = lambda name: _z_spec(sec_c[name], n_ctx, LANES, True, col_of=head_block)
    out_spec = pl.BlockSpec((1, n, LANES), lambda b, h: (b, 0, h))
    out_shape = jax.ShapeDtypeStruct((n_b, n, BRANCH_W), BF16)
    o_diff, o_ret = pl.pallas_call(
        functools.partial(_mixer_kernel, layer=layer, n=n, n_ctx=n_ctx, ts=ts, rc=rc,
                          post_scale=post_scale),
        grid=(n_b, HEADS),
        in_specs=[pl.BlockSpec(memory_space=pltpu.SMEM), pl.BlockSpec(memory_space=pltpu.SMEM),
                  lat("dq"), lat("dg"), cx("dk"), cx("dv"), lat("dk"), lat("dv"),
                  lat("rq"), lat("rg"), lat("rk"), lat("rv"), cx("rk"), cx("rv")],
        out_specs=[out_spec, out_spec],
        out_shape=[out_shape, out_shape],
        scratch_shapes=[pltpu.VMEM((nk, LANES), BF16), pltpu.VMEM((nk, 2 * LANES), BF16),
                        pltpu.VMEM((2 * ts, nk), F32), pltpu.VMEM((2 * ts, nk), F32),
                        pltpu.VMEM((2 * ts, nk), BF16), pltpu.VMEM((2 * ts, nk), BF16),
                        pltpu.VMEM((n // rc, HEAD_W, 2 * HEAD_W), F32),
                        pltpu.VMEM((n // rc, HEAD_W, 2 * HEAD_W), BF16)],
        compiler_params=pltpu.CompilerParams(vmem_limit_bytes=VMEM_LIMIT),
        name="latent_mixers",
    )(lam, log_gamma, z, z, zc, zc, z, z, z, z, z, z, zc, zc)
    return o_ret, o_diff


def _merge_kernel(x_ref, gate_ref, gpost_ref, oret_ref, odiff_ref, u_ref, vn_ref, mg_ref,
                  gr_ref, gm_ref, gd_ref, ws_ref, bs_ref, wbo_ref, wo_ref, out_ref, *, tm, out_sub):
    rows = []
    for c in range(tm // CHUNK):
        cols = []
        for g in range(HEADS):
            blk = vn_ref[0, c * CHUNK:(c + 1) * CHUNK, g * LANES:(g + 1) * LANES]
            cols.append(_dot(ws_ref[g], blk) + bs_ref[g])
        rows.append(jnp.concatenate(cols, axis=1))
    sp = jnp.concatenate(rows, axis=0)
    o_mlp = (u_ref[0].astype(F32) * sp * mg_ref[0].astype(F32)).astype(BF16)

    def gated(gate2_ref, o, w):
        gate = jnp.concatenate([gate2_ref[0, 0], gate2_ref[0, 1]], axis=1).astype(F32)
        return gate * _dot(o, w)

    t = (gated(gr_ref, oret_ref[0], wbo_ref[0]) + gated(gm_ref, o_mlp, wbo_ref[1])
         + gated(gd_ref, odiff_ref[0], wbo_ref[2]))
    tb = t.astype(BF16)
    for r in range(tm // out_sub):
        rs = slice(r * out_sub, (r + 1) * out_sub)
        y = _dot(tb[rs, :], wo_ref[...])
        out_ref[0, rs, :] = x_ref[0, rs, :] + gate_ref[...] * (_rms(y) * gpost_ref[...])


def _merge(xs, l, mod4, row_of, g_post, o_ret, o_diff, z, sec, ws, bs, wbo, wo, *, tm):
    n_b, n, _ = xs.shape
    row_block = lambda b, i: i

    def zsec(name):
        return _z_spec(sec[name], tm, SEC, False, row_of=row_block)

    def zgate(k):
        return pl.BlockSpec((1, 2, tm, SEC), lambda b, i: (b, k, i, 0))

    in_specs = [
        pl.BlockSpec((1, tm, D_MODEL), lambda b, i: (b, i, 0)),
        _mod_spec(l, row_of, 2),
        _layer_spec(l, (1, D_MODEL)),
        pl.BlockSpec((1, tm, BRANCH_W), lambda b, i: (b, i, 0)),
        pl.BlockSpec((1, tm, BRANCH_W), lambda b, i: (b, i, 0)),
        zsec("mu"), zsec("mv"), zsec("mg"), zgate(0), zgate(1), zgate(2),
        _layer_spec(l, (HEADS, CHUNK, CHUNK)),
        _layer_spec(l, (HEADS, CHUNK, LANES)),
        _layer_spec(l, (N_BRANCH, BRANCH_W, D_MODEL)),
        _layer_spec(l, (D_MODEL, D_MODEL)),
    ]
    return pl.pallas_call(
        functools.partial(_merge_kernel, tm=tm, out_sub=MERGE_OUT_SUB),
        grid=(n_b, n // tm),
        in_specs=in_specs,
        out_specs=pl.BlockSpec((1, tm, D_MODEL), lambda b, i: (b, i, 0)),
        out_shape=jax.ShapeDtypeStruct(xs.shape, F32),
        compiler_params=pltpu.CompilerParams(vmem_limit_bytes=VMEM_LIMIT),
        name="merge",
    )(xs, mod4, g_post, o_ret, o_diff, z, z, z, z, z, z, ws, bs, wbo, wo)


def _rope_tables(n_lat):
    rows = n_lat // GRID_W
    row_pos = jnp.repeat(jnp.arange(rows, dtype=F32), GRID_W)
    col_pos = jnp.tile(jnp.arange(GRID_W, dtype=F32), rows)

    def angles(head_dim):
        n_freq = head_dim // 4
        inv = ROPE_BASE ** (-jnp.arange(n_freq, dtype=F32) / n_freq)
        ang = jnp.concatenate([row_pos[:, None] * inv, col_pos[:, None] * inv], axis=-1)
        return jnp.cos(ang), jnp.sin(ang)

    cos_r, sin_r = angles(HEAD_W)
    cos_d, sin_d = angles(DIFF_DQK)
    zeros = jnp.zeros_like(sin_d)
    c_r = jnp.concatenate([cos_r, cos_r], axis=-1)
    s_r = jnp.concatenate([-sin_r, sin_r], axis=-1)
    c_d = jnp.tile(jnp.concatenate([cos_d, cos_d], axis=-1), (1, 2))
    s_lo = jnp.tile(jnp.concatenate([-sin_d, zeros], axis=-1), (1, 2))
    s_hi = jnp.tile(jnp.concatenate([zeros, sin_d], axis=-1), (1, 2))
    return c_r, s_r, c_d, s_lo, s_hi


@jax.jit
def kernel(x, c, ctx, c_ctx, w_mod, b_mod, g_pre, g_post, w_in, ret_decay_logit, mlp_w_s,
           mlp_b_s, diff_lambda_q, diff_lambda_k, w_branch_out, w_out):
    n_b, n_lat, _ = x.shape
    n_ctx = ctx.shape[1]
    tables = jnp.stack(_rope_tables(n_lat))

    cond_rows = 16
    ctx_row = n_b
    cc = jnp.zeros((cond_rows, D_MODEL), F32).at[:n_b].set(c).at[ctx_row].set(c_ctx)
    mod4 = _modulation(cc, w_mod, b_mod).reshape(DEPTH, cond_rows, 1, 3 * D_MODEL)
    lat_row = lambda b: b
    ctx_row_of = lambda b: ctx_row

    log_gamma = -jax.nn.softplus(-ret_decay_logit.astype(F32))
    lam_inits = [0.8 - 0.6 * math.exp(-0.3 * l) for l in range(DEPTH)]
    lam = (jnp.exp(jnp.sum(diff_lambda_q[:, 0] * diff_lambda_k[:, 0], axis=-1))
           - jnp.exp(jnp.sum(diff_lambda_q[:, 1] * diff_lambda_k[:, 1], axis=-1))
           + jnp.asarray(lam_inits, F32)).astype(F32)

    w_sec = w_in
    ws = mlp_w_s.astype(BF16)
    bs = jnp.broadcast_to(mlp_b_s[..., None], (DEPTH, HEADS, CHUNK, LANES)).astype(F32)
    wbo = w_branch_out.astype(BF16)
    wo = w_out.astype(BF16)
    g_pre3 = g_pre.reshape(DEPTH, 1, D_MODEL)
    g_post3 = g_post.reshape(DEPTH, 1, D_MODEL)

    for l in range(DEPTH):
        last = l == DEPTH - 1
        post_scale = 1.0 - lam_inits[l]

        ctx_flat = ctx.reshape(1, n_b * n_ctx, D_MODEL)
        if last:
            zc = _projection(ctx_flat, l, g_pre3, mod4, ctx_row_of, w_sec, lambda j: j,
                             KV_KINDS, None)
            sec_c = KV_SEC
        else:
            zc = _projection(ctx_flat, l, g_pre3, mod4, ctx_row_of, w_sec, _full_src,
                             FULL_KINDS, None)
            sec_c = FULL_SEC
            co_ret = _retention_ctx(log_gamma, l, zc, sec_c, n_b, n_ctx, rc=RET_CHUNK,
                                    heads=HEADS)
            co_diff = _diff_attention_ctx(lam, l, zc, sec_c, n_b, n_ctx, post_scale=post_scale)
            ctx_next = _merge(ctx_flat, l, mod4, ctx_row_of, g_post3,
                              co_ret.reshape(1, n_b * n_ctx, BRANCH_W),
                              co_diff.reshape(1, n_b * n_ctx, BRANCH_W), zc, sec_c,
                              ws, bs, wbo, wo, tm=MERGE_TM)
            ctx_next = ctx_next.reshape(n_b, n_ctx, D_MODEL)

        z = _projection(x, l, g_pre3, mod4, lat_row, w_sec, _full_src, FULL_KINDS, tables)
        o_ret, o_diff = _latent_mixers(lam, log_gamma, l, z, FULL_SEC, n_b, n_lat, zc, sec_c,
                                       n_ctx, ts=DIFF_TS, rc=RET_CHUNK, post_scale=post_scale)
        x = _merge(x, l, mod4, lat_row, g_post3, o_ret, o_diff, z, FULL_SEC, ws, bs, wbo, wo,
                   tm=MERGE_TM)
        if not last:
            ctx = ctx_next
    return x
```

```python
import functools
import math

import jax
import jax.numpy as jnp
from jax import lax
from jax.experimental import pallas as pl
from jax.experimental.pallas import tpu as pltpu

F32 = jnp.float32
BF16 = jnp.bfloat16

D_MODEL = 1024
DEPTH = 2
GRID_W = 64
N_BRANCH = 3
BRANCH_W = D_MODEL // 2
HEADS = 4
HEAD_W = BRANCH_W // HEADS
DIFF_DQK = HEAD_W // 2
CHUNK = 128
ROPE_BASE = 10000.0
EPS = 1e-6
RET_SCALE = HEAD_W ** -0.5
DIFF_Q_SCALE = (DIFF_DQK ** -0.5) * math.log2(math.e)

LANES = 128
SEC = BRANCH_W
MERGE_COLS = N_BRANCH * D_MODEL
KV_COLS = 4 * BRANCH_W
IN_COLS = KV_COLS + 7 * BRANCH_W + MERGE_COLS
VMEM_LIMIT = 56 * 1024 * 1024

PROJ_SUB = 256
PROJ_OUT_SLOTS = 2
DIFF_TS = 256
RET_CHUNK = 256
MERGE_TM = 512
MERGE_OUT_SUB = 512

FULL_KINDS = ("sigmoid",) * 6 + ("rk", "plain", "dk", "plain", "rq", "silu", "dq", "silu",
                                 "plain", "ln", "silu")
FULL_SEC = dict(rk=6, rv=7, dk=8, dv=9, rq=10, rg=11, dq=12, dg=13, mu=14, mv=15, mg=16)
KV_KINDS = ("rk", "plain", "dk", "plain")
KV_SEC = dict(rk=0, rv=1, dk=2, dv=3)
_N_MERGE_SEC = MERGE_COLS // SEC
_N_REST_SEC = (KV_COLS + 7 * BRANCH_W) // SEC


def _full_src(j):
    return jnp.where(j < _N_MERGE_SEC, j + _N_REST_SEC, j - _N_MERGE_SEC)


def _dot(a, b):
    return jnp.dot(a, b, preferred_element_type=F32)


def _dot_nt(a, b):
    return lax.dot_general(a, b, (((1,), (1,)), ((), ())), preferred_element_type=F32)


def _rms(v):
    return v * lax.rsqrt(jnp.mean(v * v, axis=-1, keepdims=True) + EPS)


def _mod_spec(l, row_of, k):
    return pl.BlockSpec((None, None, 1, D_MODEL), lambda *idx: (l, row_of(idx[0]), 0, k))


def _layer_spec(l, shape, **kwargs):
    zeros = (0,) * len(shape)
    return pl.BlockSpec((None,) + tuple(shape), lambda *idx: (l,) + zeros, **kwargs)


def _z_spec(sec_idx, rows, width, flat, col_of=lambda *idx: 0, row_of=None):
    if row_of is None:
        row_of = (lambda *idx: idx[0]) if flat else (lambda *idx: 0)
    group_of = (lambda *idx: 0) if flat else (lambda *idx: idx[0])
    return pl.BlockSpec((1, None, rows, width),
                        lambda *idx: (group_of(*idx), sec_idx, row_of(*idx), col_of(*idx)))


def _split_bf16(a):
    hi = a.astype(BF16)
    return hi, (a - hi.astype(F32)).astype(BF16)


def _mod_kernel(c_ref, w_ref, b_ref, o_ref):
    cv = c_ref[...]
    c_hi, c_lo = _split_bf16(cv * jax.nn.sigmoid(cv))
    w_hi, w_lo = _split_bf16(w_ref[0])
    o_ref[0] = _dot(c_hi, w_hi) + _dot(c_hi, w_lo) + _dot(c_lo, w_hi) + b_ref[0]


def _modulation(cc, w_mod, b_mod):
    rows = cc.shape[0]
    return pl.pallas_call(
        _mod_kernel,
        grid=(DEPTH, 3),
        in_specs=[
            pl.BlockSpec((rows, D_MODEL), lambda l, j: (0, 0)),
            pl.BlockSpec((1, D_MODEL, D_MODEL), lambda l, j: (l, 0, j)),
            pl.BlockSpec((1, 1, D_MODEL), lambda l, j: (l, 0, j)),
        ],
        out_specs=pl.BlockSpec((1, rows, D_MODEL), lambda l, j: (l, 0, j)),
        out_shape=jax.ShapeDtypeStruct((DEPTH, rows, 3 * D_MODEL), F32),
        compiler_params=pltpu.CompilerParams(vmem_limit_bytes=VMEM_LIMIT),
        name="modulation",
    )(cc, w_mod, b_mod.reshape(DEPTH, 1, 3 * D_MODEL))


def _rope_ret(a, c, s):
    outs = []
    for h in range(HEADS):
        ah = a[:, h * LANES:(h + 1) * LANES]
        outs.append(ah * c + pltpu.roll(ah, LANES // 2, 1) * s)
    return jnp.concatenate(outs, axis=1)


def _rope_diff(a, c, s_lo, s_hi):
    outs = []
    for h in range(HEADS):
        ah = a[:, h * LANES:(h + 1) * LANES]
        outs.append(ah * c + pltpu.roll(ah, LANES - DIFF_DQK // 2, 1) * s_lo
                    + pltpu.roll(ah, DIFF_DQK // 2, 1) * s_hi)
    return jnp.concatenate(outs, axis=1)


def _proj_kernel(*refs, layer, kinds, src_of, n_src, rope, n_rows, sub):
    if rope:
        (x_ref, g_ref, shift_ref, scale_ref, w_hbm, tab_hbm, z_hbm,
         h_ref, stage, w_vmem, w_stage, sems_out, sems_w, tab_vmem, sem_tab) = refs
        cr_ref, sr_ref, cd_ref, sdl_ref, sdh_ref = [tab_vmem.at[i] for i in range(5)]
    else:
        (x_ref, g_ref, shift_ref, scale_ref, w_hbm, z_hbm,
         h_ref, stage, w_vmem, w_stage, sems_out, sems_w) = refs
    b = pl.program_id(0)
    first = b == 0
    n_sub = n_rows // sub
    n_sec = len(kinds)
    n_slots = stage.shape[0]

    n_wslots = w_stage.shape[0]

    def weight_copy(j):
        start = src_of(j) * SEC
        cols = pl.ds(start if isinstance(start, int) else pl.multiple_of(start, SEC), SEC)
        slot = j % n_wslots
        return pltpu.make_async_copy(w_hbm.at[layer, :, cols], w_stage.at[slot], sems_w.at[slot])

    def load_weights(j):
        weight_copy(j).wait()
        src = w_stage.at[j % n_wslots]
        dst = w_vmem.at[src_of(j)]
        for r in range(D_MODEL // LANES):
            rows = slice(r * LANES, (r + 1) * LANES)
            dst[rows, :] = src[rows, :].astype(BF16)

        @pl.when(j + n_wslots < n_sec)
        def _():
            weight_copy(j + n_wslots).start()

    def table_copy():
        return pltpu.make_async_copy(tab_hbm, tab_vmem, sem_tab.at[0])

    @pl.when(first)
    def _():
        for j in range(min(n_wslots, n_sec)):
            weight_copy(j).start()
        if rope:
            table_copy().start()

    gs = g_ref[...] * (1.0 + scale_ref[...])
    sh = shift_ref[...]
    for r in range(n_sub):
        xs = x_ref[0, r * sub:(r + 1) * sub, :]
        h_ref[r * sub:(r + 1) * sub, :] = (_rms(xs) * gs + sh).astype(BF16)

    if rope:
        pl.when(first)(lambda: table_copy().wait())

    def group_of(kind):
        if kind in ("sigmoid", "silu"):
            return "gate", 1.0 if kind == "silu" else 0.0
        if kind == "ln":
            return "ln", 0.0
        scale = {"rk": RET_SCALE, "dq": DIFF_Q_SCALE}.get(kind, 1.0)
        if rope and kind in ("rk", "rq"):
            return "rope_ret", scale
        if rope and kind in ("dk", "dq"):
            return "rope_diff", scale
        return "scaled", scale

    def epi_gate(a, rows, p):
        return jax.nn.sigmoid(a) * jnp.where(p > 0.5, a, 1.0)

    def epi_ln(a, rows, p):
        mu = jnp.mean(a, axis=-1, keepdims=True)
        d = a - mu
        return d * lax.rsqrt(jnp.mean(d * d, axis=-1, keepdims=True) + EPS)

    epilogues = dict(
        scaled=lambda a, rows, p: a * p,
        gate=epi_gate,
        ln=epi_ln,
        rope_ret=lambda a, rows, p: _rope_ret(a, cr_ref[rows, :], sr_ref[rows, :]) * p,
        rope_diff=lambda a, rows, p: _rope_diff(a, cd_ref[rows, :], sdl_ref[rows, :],
                                                sdh_ref[rows, :]) * p)
    groups = {}
    for i, kind in enumerate(kinds):
        name, p = group_of(kind)
        groups.setdefault(name, []).append((i, p))

    def writeback(j):
        slot = j % n_slots
        return pltpu.make_async_copy(stage.at[slot], z_hbm.at[b, j], sems_out.at[slot])

    def run(epilogue, j, p):
        w_sec = w_vmem.at[src_of(j)]
        out = stage.at[j % n_slots]
        for r in range(n_sub):
            rows = slice(r * sub, (r + 1) * sub)
            out[rows, :] = epilogue(_dot(h_ref[rows, :], w_sec[...]), rows, p).astype(BF16)

    def section(j, carry):
        pl.when(first)(functools.partial(load_weights, j))

        @pl.when(j >= n_slots)
        def _():
            writeback(j - n_slots).wait()

        for name, members in sorted(groups.items()):
            cond = j == members[0][0]
            p = jnp.float32(members[0][1])
            for i, p_i in members[1:]:
                cond = cond | (j == i)
                if p_i != members[0][1]:
                    p = jnp.where(j == i, p_i, p)
            pl.when(cond)(functools.partial(run, epilogues[name], j, p))
        writeback(j).start()
        return carry

    lax.fori_loop(0, n_sec, section, 0)
    for j in range(max(0, n_sec - n_slots), n_sec):
        writeback(j).wait()


def _projection(xs, l, g_pre, mod4, row_of, w_all, src_of, kinds, tables):
    n_g, n_rows, _ = xs.shape
    n_sec = len(kinds)
    rope = tables is not None
    n_src = w_all.shape[2] // SEC if src_of is _full_src else n_sec
    in_specs = [
        pl.BlockSpec((1, n_rows, D_MODEL), lambda b: (b, 0, 0)),
        _layer_spec(l, (1, D_MODEL)),
        _mod_spec(l, row_of, 0),
        _mod_spec(l, row_of, 1),
        pl.BlockSpec(memory_space=pl.ANY),
    ]
    args = [xs, g_pre, mod4, mod4, w_all]
    scratch = [pltpu.VMEM((n_rows, D_MODEL), BF16),
               pltpu.VMEM((PROJ_OUT_SLOTS, n_rows, SEC), BF16),
               pltpu.VMEM((n_src, D_MODEL, SEC), BF16),
               pltpu.VMEM((2, D_MODEL, SEC), F32),
               pltpu.SemaphoreType.DMA((PROJ_OUT_SLOTS,)),
               pltpu.SemaphoreType.DMA((2,))]
    if rope:
        in_specs.append(pl.BlockSpec(memory_space=pl.ANY))
        args.append(tables)
        scratch += [pltpu.VMEM(tables.shape, F32), pltpu.SemaphoreType.DMA((1,))]
    return pl.pallas_call(
        functools.partial(_proj_kernel, layer=l, kinds=kinds, src_of=src_of, n_src=n_src,
                          rope=rope, n_rows=n_rows, sub=PROJ_SUB),
        grid=(n_g,),
        in_specs=in_specs,
        out_specs=pl.BlockSpec(memory_space=pl.ANY),
        out_shape=jax.ShapeDtypeStruct((n_g, n_sec, n_rows, SEC), BF16),
        scratch_shapes=scratch,
        compiler_params=pltpu.CompilerParams(vmem_limit_bytes=VMEM_LIMIT),
        name="projection_rope" if rope else "projection",
    )(*args)


def _retention_tasks(lgf, lgb, cs, q_ref, g_ref, k_ref, v_ref, kc_ref, vc_ref, o_ref, inc_ref,
                     st_ref, *, n, n_ctx, rc):
    nc = n // rc
    pos = lax.broadcasted_iota(jnp.int32, (rc, LANES), 0).astype(F32)
    kdf = jnp.exp(lgf * (rc - 1.0 - pos)).astype(BF16)
    kdb = jnp.exp(lgb * pos).astype(BF16)
    cdf = jnp.exp(jnp.full((1, LANES), lgf * rc, F32))
    cdb = jnp.exp(jnp.full((1, LANES), lgb * rc, F32))
    qdf = jnp.exp(lgf * (pos + 1.0)).astype(BF16)
    qdb = jnp.exp(lgb * (rc - pos)).astype(BF16)
    ii = lax.broadcasted_iota(jnp.int32, (rc, rc), 0)
    jj = lax.broadcasted_iota(jnp.int32, (rc, rc), 1)
    dist = (ii - jj).astype(F32)
    dmat = (jnp.where(dist >= 0, jnp.exp(lgf * jnp.maximum(dist, 0.0)), 0.0)
            + jnp.where(dist <= 0, jnp.exp(lgb * jnp.maximum(-dist, 0.0)), 0.0))
    state = {}

    def increment(kr, vr, nn):
        rows = slice(nn * rc, (nn + 1) * rc)
        kk = kr[0, rows, cs]
        kcat = jnp.concatenate([kk * kdf, kk * kdb], axis=1)
        vt = vr[0, rows, cs].astype(F32).T.astype(BF16)
        return _dot(vt, kcat)

    def init_task():
        sf = jnp.zeros((HEAD_W, HEAD_W), F32)
        sb = jnp.zeros((HEAD_W, HEAD_W), F32)
        if n_ctx:
            incs = [increment(kc_ref, vc_ref, nn) for nn in range(n_ctx // rc)]
            for inc in incs:
                sf = cdf * sf + inc[:, :HEAD_W]
            for inc in reversed(incs):
                sb = cdb * sb + inc[:, HEAD_W:]
        state["sf"], state["sb"] = sf, sb

    def inc_task(nn):
        inc_ref[nn] = increment(k_ref, v_ref, nn)

    def scan_task():
        sf, sb = state["sf"], state["sb"]
        for nn in range(nc):
            st_ref[nn, :, 0:HEAD_W] = sf.astype(BF16)
            sf = cdf * sf + inc_ref[nn, :, 0:HEAD_W]
        for nn in reversed(range(nc)):
            st_ref[nn, :, HEAD_W:] = sb.astype(BF16)
            sb = cdb * sb + inc_ref[nn, :, HEAD_W:]

    def out_task(nn):
        rows = slice(nn * rc, (nn + 1) * rc)
        q = q_ref[0, rows, cs]
        s = _dot_nt(q, k_ref[0, rows, cs]) * dmat
        inner = _dot(s.astype(BF16), v_ref[0, rows, cs])
        qcat = jnp.concatenate([q * qdf, q * qdb], axis=1)
        out = inner + _dot_nt(qcat, st_ref[nn])
        o_ref[0, rows, cs] = _rms(out).astype(BF16) * g_ref[0, rows, cs]

    inc_tasks = [init_task] + [functools.partial(inc_task, nn) for nn in range(nc)]
    return inc_tasks, scan_task, [functools.partial(out_task, nn) for nn in range(nc)]


def _ret_kernel(lg_ref, q_ref, g_ref, k_ref, v_ref, o_ref, inc_ref, st_ref, *, layer, n, rc,
                heads):
    tasks = []
    for hh in range(heads):
        head = pl.program_id(1) * heads + hh
        tasks.append(_retention_tasks(
            lg_ref[layer, 0, head], lg_ref[layer, 1, head], slice(hh * LANES, (hh + 1) * LANES),
            q_ref, g_ref, k_ref, v_ref, None, None, o_ref, inc_ref.at[hh], st_ref.at[hh],
            n=n, n_ctx=0, rc=rc))
    for i in range(len(tasks[0][0])):
        for inc_tasks, _, _ in tasks:
            inc_tasks[i]()
    for _, scan_task, _ in tasks:
        scan_task()
    for i in range(len(tasks[0][2])):
        for _, _, out_tasks in tasks:
            out_tasks[i]()


def _retention_ctx(log_gamma, layer, zc, sec, n_b, n, *, rc, heads):
    width = heads * LANES
    head_block = lambda b, h: h
    return pl.pallas_call(
        functools.partial(_ret_kernel, layer=layer, n=n, rc=rc, heads=heads),
        grid=(n_b, HEADS // heads),
        in_specs=[pl.BlockSpec(memory_space=pltpu.SMEM)]
        + [_z_spec(sec[name], n, width, True, col_of=head_block)
           for name in ("rq", "rg", "rk", "rv")],
        out_specs=pl.BlockSpec((1, n, width), lambda b, h: (b, 0, h)),
        out_shape=jax.ShapeDtypeStruct((n_b, n, BRANCH_W), BF16),
        scratch_shapes=[pltpu.VMEM((heads, n // rc, HEAD_W, 2 * HEAD_W), F32),
                        pltpu.VMEM((heads, n // rc, HEAD_W, 2 * HEAD_W), BF16)],
        compiler_params=pltpu.CompilerParams(vmem_limit_bytes=VMEM_LIMIT),
        name="retention_ctx",
    )(log_gamma, zc, zc, zc, zc)


def _split_maps(q, lane):
    zero = jnp.zeros_like(q)
    return jnp.concatenate([jnp.where(lane < DIFF_DQK, q, zero),
                            jnp.where(lane >= DIFF_DQK, q, zero)], axis=0)


def _diff_ctx_kernel(lam_ref, q_ref, g_ref, k_ref, v_ref, o_ref, *, layer, tq, post_scale):
    lane = lax.broadcasted_iota(jnp.int32, (tq, LANES), 1)
    for h in range(HEADS):
        cs = slice(h * LANES, (h + 1) * LANES)
        s = _dot_nt(_split_maps(q_ref[0, :, cs], lane), k_ref[0, :, cs])
        p = jnp.exp2(s - jnp.max(s, axis=-1, keepdims=True))
        o = _dot(p.astype(BF16), v_ref[0, :, cs]) / jnp.sum(p, axis=-1, keepdims=True)
        d = o[:tq] - lam_ref[layer] * o[tq:]
        o_ref[0, :, cs] = (_rms(d) * post_scale * g_ref[0, :, cs].astype(F32)).astype(BF16)


def _diff_attention_ctx(lam, layer, zc, sec, n_b, n, *, post_scale):
    return pl.pallas_call(
        functools.partial(_diff_ctx_kernel, layer=layer, tq=n, post_scale=post_scale),
        grid=(n_b,),
        in_specs=[pl.BlockSpec(memory_space=pltpu.SMEM)]
        + [_z_spec(sec[name], n, SEC, True) for name in ("dq", "dg", "dk", "dv")],
        out_specs=pl.BlockSpec((1, n, BRANCH_W), lambda b: (b, 0, 0)),
        out_shape=jax.ShapeDtypeStruct((n_b, n, BRANCH_W), BF16),
        compiler_params=pltpu.CompilerParams(vmem_limit_bytes=VMEM_LIMIT),
        name="diff_attention_ctx",
    )(lam, zc, zc, zc, zc)


def _mixer_kernel(lam_ref, lg_ref, q_ref, g_ref, kc_ref, vc_ref, kl_ref, vl_ref,
                  rq_ref, rg_ref, rk_ref, rv_ref, rkc_ref, rvc_ref, od_ref, or_ref,
                  kall, vall, s0, s1, p0, p1, inc_ref, st_ref, *, layer, n, n_ctx, ts, rc,
                  post_scale):
    n_sub = n // ts
    nk = kall.shape[0]
    kall[0:n_ctx, :] = kc_ref[0]
    kall[n_ctx:, :] = kl_ref[0]
    vall[0:n_ctx, 0:LANES] = vc_ref[0]
    vall[n_ctx:, 0:LANES] = vl_ref[0]
    vall[:, LANES:] = jnp.ones((nk, LANES), BF16)
    lam = lam_ref[layer]
    lane = lax.broadcasted_iota(jnp.int32, (ts, LANES), 1)
    s_bufs, p_bufs = (s0, s1), (p0, p1)

    def stage_a(t):
        rows = slice(t * ts, (t + 1) * ts)
        s_bufs[t % 2][...] = _dot_nt(_split_maps(q_ref[0, rows, :], lane), kall[...])

    def stage_b(t):
        s = s_bufs[t % 2][...]
        p_bufs[t % 2][...] = jnp.exp2(s - jnp.max(s, axis=-1, keepdims=True)).astype(BF16)

    def stage_c(t):
        rows = slice(t * ts, (t + 1) * ts)
        oe = _dot(p_bufs[t % 2][...], vall[...])
        o = oe[:, :LANES] / oe[:, LANES:]
        d = o[:ts] - lam * o[ts:]
        od_ref[0, rows, :] = (_rms(d) * post_scale * g_ref[0, rows, :].astype(F32)).astype(BF16)

    head = pl.program_id(1)
    inc_tasks, scan_task, out_tasks = _retention_tasks(
        lg_ref[layer, 0, head], lg_ref[layer, 1, head], slice(0, LANES),
        rq_ref, rg_ref, rk_ref, rv_ref, rkc_ref, rvc_ref, or_ref, inc_ref, st_ref,
        n=n, n_ctx=n_ctx, rc=rc)
    n_steps = n_sub + 2
    extra = [[] for _ in range(n_steps)]
    half = (len(inc_tasks) + 1) // 2
    extra[0] += inc_tasks[:half]
    extra[1] += inc_tasks[half:]
    extra[2].append(scan_task)
    for i, task in enumerate(out_tasks):
        extra[2 + (i * (n_steps - 2)) // len(out_tasks)].append(task)

    for t in range(n_steps):
        if t >= 2:
            stage_c(t - 2)
        if 1 <= t <= n_sub:
            stage_b(t - 1)
        if t < n_sub:
            stage_a(t)
        for task in extra[t]:
            task()


def _latent_mixers(lam, log_gamma, layer, z, sec, n_b, n, zc, sec_c, n_ctx, *, ts, rc,
                   post_scale):
    nk = n_ctx + n
    head_block = lambda b, h: h
    lat = lambda name: _z_spec(sec[name], n, LANES, False, col_of=head_block)
    cx = lambda name: _z_spec(sec_c[name], n_ctx, LANES, True, col_of=head_block)
    out_spec = pl.BlockSpec((1, n, LANES), lambda b, h: (b, 0, h))
    out_shape = jax.ShapeDtypeStruct((n_b, n, BRANCH_W), BF16)
    o_diff, o_ret = pl.pallas_call(
        functools.partial(_mixer_kernel, layer=layer, n=n, n_ctx=n_ctx, ts=ts, rc=rc,
                          post_scale=post_scale),
        grid=(n_b, HEADS),
        in_specs=[pl.BlockSpec(memory_space=pltpu.SMEM), pl.BlockSpec(memory_space=pltpu.SMEM),
                  lat("dq"), lat("dg"), cx("dk"), cx("dv"), lat("dk"), lat("dv"),
                  lat("rq"), lat("rg"), lat("rk"), lat("rv"), cx("rk"), cx("rv")],
        out_specs=[out_spec, out_spec],
        out_shape=[out_shape, out_shape],
        scratch_shapes=[pltpu.VMEM((nk, LANES), BF16), pltpu.VMEM((nk, 2 * LANES), BF16),
                        pltpu.VMEM((2 * ts, nk), F32), pltpu.VMEM((2 * ts, nk), F32),
                        pltpu.VMEM((2 * ts, nk), BF16), pltpu.VMEM((2 * ts, nk), BF16),
                        pltpu.VMEM((n // rc, HEAD_W, 2 * HEAD_W), F32),
                        pltpu.VMEM((n // rc, HEAD_W, 2 * HEAD_W), BF16)],
        compiler_params=pltpu.CompilerParams(vmem_limit_bytes=VMEM_LIMIT),
        name="latent_mixers",
    )(lam, log_gamma, z, z, zc, zc, z, z, z, z, z, z, zc, zc)
    return o_ret, o_diff


def _merge_kernel(x_ref, gate_ref, gpost_ref, oret_ref, odiff_ref, u_ref, vn_ref, mg_ref,
                  gr_ref, gm_ref, gd_ref, ws_ref, bs_ref, wbo_ref, wo_ref, out_ref, *, tm, out_sub):
    rows = []
    for c in range(tm // CHUNK):
        cols = []
        for g in range(HEADS):
            blk = vn_ref[0, c * CHUNK:(c + 1) * CHUNK, g * LANES:(g + 1) * LANES]
            cols.append(_dot(ws_ref[g], blk) + bs_ref[g])
        rows.append(jnp.concatenate(cols, axis=1))
    sp = jnp.concatenate(rows, axis=0)
    o_mlp = (u_ref[0].astype(F32) * sp * mg_ref[0].astype(F32)).astype(BF16)

    def gated(gate2_ref, o, w):
        gate = jnp.concatenate([gate2_ref[0, 0], gate2_ref[0, 1]], axis=1).astype(F32)
        return gate * _dot(o, w)

    t = (gated(gr_ref, oret_ref[0], wbo_ref[0]) + gated(gm_ref, o_mlp, wbo_ref[1])
         + gated(gd_ref, odiff_ref[0], wbo_ref[2]))
    tb = t.astype(BF16)
    for r in range(tm // out_sub):
        rs = slice(r * out_sub, (r + 1) * out_sub)
        y = _dot(tb[rs, :], wo_ref[...])
        out_ref[0, rs, :] = x_ref[0, rs, :] + gate_ref[...] * (_rms(y) * gpost_ref[...])


def _merge(xs, l, mod4, row_of, g_post, o_ret, o_diff, z, sec, ws, bs, wbo, wo, *, tm):
    n_b, n, _ = xs.shape
    row_block = lambda b, i: i

    def zsec(name):
        return _z_spec(sec[name], tm, SEC, False, row_of=row_block)

    def zgate(k):
        return pl.BlockSpec((1, 2, tm, SEC), lambda b, i: (b, k, i, 0))

    in_specs = [
        pl.BlockSpec((1, tm, D_MODEL), lambda b, i: (b, i, 0)),
        _mod_spec(l, row_of, 2),
        _layer_spec(l, (1, D_MODEL)),
        pl.BlockSpec((1, tm, BRANCH_W), lambda b, i: (b, i, 0)),
        pl.BlockSpec((1, tm, BRANCH_W), lambda b, i: (b, i, 0)),
        zsec("mu"), zsec("mv"), zsec("mg"), zgate(0), zgate(1), zgate(2),
        _layer_spec(l, (HEADS, CHUNK, CHUNK)),
        _layer_spec(l, (HEADS, CHUNK, LANES)),
        _layer_spec(l, (N_BRANCH, BRANCH_W, D_MODEL)),
        _layer_spec(l, (D_MODEL, D_MODEL)),
    ]
    return pl.pallas_call(
        functools.partial(_merge_kernel, tm=tm, out_sub=MERGE_OUT_SUB),
        grid=(n_b, n // tm),
        in_specs=in_specs,
        out_specs=pl.BlockSpec((1, tm, D_MODEL), lambda b, i: (b, i, 0)),
        out_shape=jax.ShapeDtypeStruct(xs.shape, F32),
        compiler_params=pltpu.CompilerParams(vmem_limit_bytes=VMEM_LIMIT),
        name="merge",
    )(xs, mod4, g_post, o_ret, o_diff, z, z, z, z, z, z, ws, bs, wbo, wo)


def _rope_tables(n_lat):
    rows = n_lat // GRID_W
    row_pos = jnp.repeat(jnp.arange(rows, dtype=F32), GRID_W)
    col_pos = jnp.tile(jnp.arange(GRID_W, dtype=F32), rows)

    def angles(head_dim):
        n_freq = head_dim // 4
        inv = ROPE_BASE ** (-jnp.arange(n_freq, dtype=F32) / n_freq)
        ang = jnp.concatenate([row_pos[:, None] * inv, col_pos[:, None] * inv], axis=-1)
        return jnp.cos(ang), jnp.sin(ang)

    cos_r, sin_r = angles(HEAD_W)
    cos_d, sin_d = angles(DIFF_DQK)
    zeros = jnp.zeros_like(sin_d)
    c_r = jnp.concatenate([cos_r, cos_r], axis=-1)
    s_r = jnp.concatenate([-sin_r, sin_r], axis=-1)
    c_d = jnp.tile(jnp.concatenate([cos_d, cos_d], axis=-1), (1, 2))
    s_lo = jnp.tile(jnp.concatenate([-sin_d, zeros], axis=-1), (1, 2))
    s_hi = jnp.tile(jnp.concatenate([zeros, sin_d], axis=-1), (1, 2))
    return c_r, s_r, c_d, s_lo, s_hi


@jax.jit
def kernel(x, c, ctx, c_ctx, w_mod, b_mod, g_pre, g_post, w_in, ret_decay_logit, mlp_w_s,
           mlp_b_s, diff_lambda_q, diff_lambda_k, w_branch_out, w_out):
    n_b, n_lat, _ = x.shape
    n_ctx = ctx.shape[1]
    tables = jnp.stack(_rope_tables(n_lat))

    cond_rows = 16
    ctx_row = n_b
    cc = jnp.zeros((cond_rows, D_MODEL), F32).at[:n_b].set(c).at[ctx_row].set(c_ctx)
    mod4 = _modulation(cc, w_mod, b_mod).reshape(DEPTH, cond_rows, 1, 3 * D_MODEL)
    lat_row = lambda b: b
    ctx_row_of = lambda b: ctx_row

    log_gamma = -jax.nn.softplus(-ret_decay_logit.astype(F32))
    lam_inits = [0.8 - 0.6 * math.exp(-0.3 * l) for l in range(DEPTH)]
    lam = (jnp.exp(jnp.sum(diff_lambda_q[:, 0] * diff_lambda_k[:, 0], axis=-1))
           - jnp.exp(jnp.sum(diff_lambda_q[:, 1] * diff_lambda_k[:, 1], axis=-1))
           + jnp.asarray(lam_inits, F32)).astype(F32)

    w_sec = w_in
    ws = mlp_w_s.astype(BF16)
    bs = jnp.broadcast_to(mlp_b_s[..., None], (DEPTH, HEADS, CHUNK, LANES)).astype(F32)
    wbo = w_branch_out.astype(BF16)
    wo = w_out.astype(BF16)
    g_pre3 = g_pre.reshape(DEPTH, 1, D_MODEL)
    g_post3 = g_post.reshape(DEPTH, 1, D_MODEL)

    for l in range(DEPTH):
        last = l == DEPTH - 1
        post_scale = 1.0 - lam_inits[l]

        ctx_flat = ctx.reshape(1, n_b * n_ctx, D_MODEL)
        if last:
            zc = _projection(ctx_flat, l, g_pre3, mod4, ctx_row_of, w_sec, lambda j: j,
                             KV_KINDS, None)
            sec_c = KV_SEC
        else:
            zc = _projection(ctx_flat, l, g_pre3, mod4, ctx_row_of, w_sec, _full_src,
                             FULL_KINDS, None)
            sec_c = FULL_SEC
            co_ret = _retention_ctx(log_gamma, l, zc, sec_c, n_b, n_ctx, rc=RET_CHUNK,
                                    heads=HEADS)
            co_diff = _diff_attention_ctx(lam, l, zc, sec_c, n_b, n_ctx, post_scale=post_scale)
            ctx_next = _merge(ctx_flat, l, mod4, ctx_row_of, g_post3,
                              co_ret.reshape(1, n_b * n_ctx, BRANCH_W),
                              co_diff.reshape(1, n_b * n_ctx, BRANCH_W), zc, sec_c,
                              ws, bs, wbo, wo, tm=MERGE_TM)
            ctx_next = ctx_next.reshape(n_b, n_ctx, D_MODEL)

        z = _projection(x, l, g_pre3, mod4, lat_row, w_sec, _full_src, FULL_KINDS, tables)
        o_ret, o_diff = _latent_mixers(lam, log_gamma, l, z, FULL_SEC, n_b, n_lat, zc, sec_c,
                                       n_ctx, ts=DIFF_TS, rc=RET_CHUNK, post_scale=post_scale)
        x = _merge(x, l, mod4, lat_row, g_post3, o_ret, o_diff, z, FULL_SEC, ws, bs, wbo, wo,
                   tm=MERGE_TM)
        if not last:
            ctx = ctx_next
    return x
```

```python
import functools
import math

import jax
import jax.numpy as jnp
from jax import lax
from jax.experimental import pallas as pl
from jax.experimental.pallas import tpu as pltpu

F32 = jnp.float32
BF16 = jnp.bfloat16

D_MODEL = 1024
DEPTH = 2
GRID_W = 64
N_BRANCH = 3
BRANCH_W = D_MODEL // 2
HEADS = 4
HEAD_W = BRANCH_W // HEADS
DIFF_DQK = HEAD_W // 2
CHUNK = 128
ROPE_BASE = 10000.0
EPS = 1e-6
RET_SCALE = HEAD_W ** -0.5
DIFF_Q_SCALE = (DIFF_DQK ** -0.5) * math.log2(math.e)

LANES = 128
SEC = BRANCH_W
MERGE_COLS = N_BRANCH * D_MODEL
KV_COLS = 4 * BRANCH_W
IN_COLS = KV_COLS + 7 * BRANCH_W + MERGE_COLS
VMEM_LIMIT = 56 * 1024 * 1024

PROJ_SUB = 256
PROJ_OUT_SLOTS = 2
DIFF_TS = 256
RET_CHUNK = 256
MERGE_TM = 512
MERGE_OUT_SUB = 512

FULL_KINDS = ("sigmoid",) * 6 + ("rk", "plain", "dk", "plain", "rq", "silu", "dq", "silu",
                                 "plain", "ln", "silu")
FULL_SEC = dict(rk=6, rv=7, dk=8, dv=9, rq=10, rg=11, dq=12, dg=13, mu=14, mv=15, mg=16)
KV_KINDS = ("rk", "plain", "dk", "plain")
KV_SEC = dict(rk=0, rv=1, dk=2, dv=3)
_N_MERGE_SEC = MERGE_COLS // SEC
_N_REST_SEC = (KV_COLS + 7 * BRANCH_W) // SEC


def _full_src(j):
    return jnp.where(j < _N_MERGE_SEC, j + _N_REST_SEC, j - _N_MERGE_SEC)


def _dot(a, b):
    return jnp.dot(a, b, preferred_element_type=F32)


def _dot_nt(a, b):
    return lax.dot_general(a, b, (((1,), (1,)), ((), ())), preferred_element_type=F32)


def _rms(v):
    return v * lax.rsqrt(jnp.mean(v * v, axis=-1, keepdims=True) + EPS)


def _mod_spec(l, row_of, k):
    return pl.BlockSpec((None, None, 1, D_MODEL), lambda *idx: (l, row_of(idx[0]), 0, k))


def _layer_spec(l, shape, **kwargs):
    zeros = (0,) * len(shape)
    return pl.BlockSpec((None,) + tuple(shape), lambda *idx: (l,) + zeros, **kwargs)


def _z_spec(sec_idx, rows, width, flat, col_of=lambda *idx: 0, row_of=None):
    if row_of is None:
        row_of = (lambda *idx: idx[0]) if flat else (lambda *idx: 0)
    group_of = (lambda *idx: 0) if flat else (lambda *idx: idx[0])
    return pl.BlockSpec((1, None, rows, width),
                        lambda *idx: (group_of(*idx), sec_idx, row_of(*idx), col_of(*idx)))


def _split_bf16(a):
    hi = a.astype(BF16)
    return hi, (a - hi.astype(F32)).astype(BF16)


def _mod_kernel(c_ref, w_ref, b_ref, o_ref):
    cv = c_ref[...]
    c_hi, c_lo = _split_bf16(cv * jax.nn.sigmoid(cv))
    w_hi, w_lo = _split_bf16(w_ref[0])
    o_ref[0] = _dot(c_hi, w_hi) + _dot(c_hi, w_lo) + _dot(c_lo, w_hi) + b_ref[0]


def _modulation(cc, w_mod, b_mod):
    rows = cc.shape[0]
    return pl.pallas_call(
        _mod_kernel,
        grid=(DEPTH, 3),
        in_specs=[
            pl.BlockSpec((rows, D_MODEL), lambda l, j: (0, 0)),
            pl.BlockSpec((1, D_MODEL, D_MODEL), lambda l, j: (l, 0, j)),
            pl.BlockSpec((1, 1, D_MODEL), lambda l, j: (l, 0, j)),
        ],
        out_specs=pl.BlockSpec((1, rows, D_MODEL), lambda l, j: (l, 0, j)),
        out_shape=jax.ShapeDtypeStruct((DEPTH, rows, 3 * D_MODEL), F32),
        compiler_params=pltpu.CompilerParams(vmem_limit_bytes=VMEM_LIMIT),
        name="modulation",
    )(cc, w_mod, b_mod.reshape(DEPTH, 1, 3 * D_MODEL))


def _rope_ret(a, c, s):
    outs = []
    for h in range(HEADS):
        ah = a[:, h * LANES:(h + 1) * LANES]
        outs.append(ah * c + pltpu.roll(ah, LANES // 2, 1) * s)
    return jnp.concatenate(outs, axis=1)


def _rope_diff(a, c, s_lo, s_hi):
    outs = []
    for h in range(HEADS):
        ah = a[:, h * LANES:(h + 1) * LANES]
        outs.append(ah * c + pltpu.roll(ah, LANES - DIFF_DQK // 2, 1) * s_lo
                    + pltpu.roll(ah, DIFF_DQK // 2, 1) * s_hi)
    return jnp.concatenate(outs, axis=1)


def _proj_kernel(*refs, layer, kinds, src_of, n_src, rope, n_rows, sub):
    if rope:
        (x_ref, g_ref, shift_ref, scale_ref, w_hbm, tab_hbm, z_hbm,
         h_ref, stage, w_vmem, w_stage, sems_out, sems_w, tab_vmem, sem_tab) = refs
        cr_ref, sr_ref, cd_ref, sdl_ref, sdh_ref = [tab_vmem.at[i] for i in range(5)]
    else:
        (x_ref, g_ref, shift_ref, scale_ref, w_hbm, z_hbm,
         h_ref, stage, w_vmem, w_stage, sems_out, sems_w) = refs
    b = pl.program_id(0)
    first = b == 0
    n_sub = n_rows // sub
    n_sec = len(kinds)
    n_slots = stage.shape[0]

    n_wslots = w_stage.shape[0]

    def weight_copy(j):
        start = src_of(j) * SEC
        cols = pl.ds(start if isinstance(start, int) else pl.multiple_of(start, SEC), SEC)
        slot = j % n_wslots
        return pltpu.make_async_copy(w_hbm.at[layer, :, cols], w_stage.at[slot], sems_w.at[slot])

    def load_weights(j):
        weight_copy(j).wait()
        src = w_stage.at[j % n_wslots]
        dst = w_vmem.at[src_of(j)]
        for r in range(D_MODEL // LANES):
            rows = slice(r * LANES, (r + 1) * LANES)
            dst[rows, :] = src[rows, :].astype(BF16)

        @pl.when(j + n_wslots < n_sec)
        def _():
            weight_copy(j + n_wslots).start()

    def table_copy():
        return pltpu.make_async_copy(tab_hbm, tab_vmem, sem_tab.at[0])

    @pl.when(first)
    def _():
        for j in range(min(n_wslots, n_sec)):
            weight_copy(j).start()
        if rope:
            table_copy().start()

    gs = g_ref[...] * (1.0 + scale_ref[...])
    sh = shift_ref[...]
    for r in range(n_sub):
        xs = x_ref[0, r * sub:(r + 1) * sub, :]
        h_ref[r * sub:(r + 1) * sub, :] = (_rms(xs) * gs + sh).astype(BF16)

    if rope:
        pl.when(first)(lambda: table_copy().wait())

    def epi_rk(a, rows):
        if rope:
            a = _rope_ret(a, cr_ref[rows, :], sr_ref[rows, :])
        return a * RET_SCALE

    def epi_rq(a, rows):
        return _rope_ret(a, cr_ref[rows, :], sr_ref[rows, :]) if rope else a

    def epi_dk(a, rows):
        return _rope_diff(a, cd_ref[rows, :], sdl_ref[rows, :], sdh_ref[rows, :]) if rope else a

    def epi_dq(a, rows):
        return epi_dk(a, rows) * DIFF_Q_SCALE

    def epi_ln(a, rows):
        mu = jnp.mean(a, axis=-1, keepdims=True)
        d = a - mu
        return d * lax.rsqrt(jnp.mean(d * d, axis=-1, keepdims=True) + EPS)

    epilogues = dict(
        plain=lambda a, rows: a,
        sigmoid=lambda a, rows: jax.nn.sigmoid(a),
        silu=lambda a, rows: a * jax.nn.sigmoid(a),
        rk=epi_rk, rq=epi_rq, dk=epi_dk, dq=epi_dq, ln=epi_ln)

    def writeback(j):
        slot = j % n_slots
        return pltpu.make_async_copy(stage.at[slot], z_hbm.at[b, j], sems_out.at[slot])

    def run(epilogue, j):
        w_sec = w_vmem.at[src_of(j)]
        out = stage.at[j % n_slots]
        for r in range(n_sub):
            rows = slice(r * sub, (r + 1) * sub)
            out[rows, :] = epilogue(_dot(h_ref[rows, :], w_sec[...]), rows).astype(BF16)

    def section(j, carry):
        pl.when(first)(functools.partial(load_weights, j))

        @pl.when(j >= n_slots)
        def _():
            writeback(j - n_slots).wait()

        for kind in sorted(set(kinds)):
            idx = [i for i, k in enumerate(kinds) if k == kind]
            cond = j == idx[0]
            for i in idx[1:]:
                cond = cond | (j == i)
            pl.when(cond)(functools.partial(run, epilogues[kind], j))
        writeback(j).start()
        return carry

    lax.fori_loop(0, n_sec, section, 0)
    for j in range(max(0, n_sec - n_slots), n_sec):
        writeback(j).wait()


def _projection(xs, l, g_pre, mod4, row_of, w_all, src_of, kinds, tables):
    n_g, n_rows, _ = xs.shape
    n_sec = len(kinds)
    rope = tables is not None
    n_src = w_all.shape[2] // SEC if src_of is _full_src else n_sec
    in_specs = [
        pl.BlockSpec((1, n_rows, D_MODEL), lambda b: (b, 0, 0)),
        _layer_spec(l, (1, D_MODEL)),
        _mod_spec(l, row_of, 0),
        _mod_spec(l, row_of, 1),
        pl.BlockSpec(memory_space=pl.ANY),
    ]
    args = [xs, g_pre, mod4, mod4, w_all]
    scratch = [pltpu.VMEM((n_rows, D_MODEL), BF16),
               pltpu.VMEM((PROJ_OUT_SLOTS, n_rows, SEC), BF16),
               pltpu.VMEM((n_src, D_MODEL, SEC), BF16),
               pltpu.VMEM((2, D_MODEL, SEC), F32),
               pltpu.SemaphoreType.DMA((PROJ_OUT_SLOTS,)),
               pltpu.SemaphoreType.DMA((2,))]
    if rope:
        in_specs.append(pl.BlockSpec(memory_space=pl.ANY))
        args.append(tables)
        scratch += [pltpu.VMEM(tables.shape, F32), pltpu.SemaphoreType.DMA((1,))]
    return pl.pallas_call(
        functools.partial(_proj_kernel, layer=l, kinds=kinds, src_of=src_of, n_src=n_src,
                          rope=rope, n_rows=n_rows, sub=PROJ_SUB),
        grid=(n_g,),
        in_specs=in_specs,
        out_specs=pl.BlockSpec(memory_space=pl.ANY),
        out_shape=jax.ShapeDtypeStruct((n_g, n_sec, n_rows, SEC), BF16),
        scratch_shapes=scratch,
        compiler_params=pltpu.CompilerParams(vmem_limit_bytes=VMEM_LIMIT),
        name="projection_rope" if rope else "projection",
    )(*args)


def _retention_tasks(lgf, lgb, cs, q_ref, g_ref, k_ref, v_ref, kc_ref, vc_ref, o_ref, inc_ref,
                     st_ref, *, n, n_ctx, rc):
    nc = n // rc
    pos = lax.broadcasted_iota(jnp.int32, (rc, LANES), 0).astype(F32)
    kdf = jnp.exp(lgf * (rc - 1.0 - pos)).astype(BF16)
    kdb = jnp.exp(lgb * pos).astype(BF16)
    cdf = jnp.exp(jnp.full((1, LANES), lgf * rc, F32))
    cdb = jnp.exp(jnp.full((1, LANES), lgb * rc, F32))
    qdf = jnp.exp(lgf * (pos + 1.0)).astype(BF16)
    qdb = jnp.exp(lgb * (rc - pos)).astype(BF16)
    ii = lax.broadcasted_iota(jnp.int32, (rc, rc), 0)
    jj = lax.broadcasted_iota(jnp.int32, (rc, rc), 1)
    dist = (ii - jj).astype(F32)
    dmat = (jnp.where(dist >= 0, jnp.exp(lgf * jnp.maximum(dist, 0.0)), 0.0)
            + jnp.where(dist <= 0, jnp.exp(lgb * jnp.maximum(-dist, 0.0)), 0.0))
    state = {}

    def increment(kr, vr, nn):
        rows = slice(nn * rc, (nn + 1) * rc)
        kk = kr[0, rows, cs]
        kcat = jnp.concatenate([kk * kdf, kk * kdb], axis=1)
        vt = vr[0, rows, cs].astype(F32).T.astype(BF16)
        return _dot(vt, kcat)

    def init_task():
        sf = jnp.zeros((HEAD_W, HEAD_W), F32)
        sb = jnp.zeros((HEAD_W, HEAD_W), F32)
        if n_ctx:
            incs = [increment(kc_ref, vc_ref, nn) for nn in range(n_ctx // rc)]
            for inc in incs:
                sf = cdf * sf + inc[:, :HEAD_W]
            for inc in reversed(incs):
                sb = cdb * sb + inc[:, HEAD_W:]
        state["sf"], state["sb"] = sf, sb

    def inc_task(nn):
        inc_ref[nn] = increment(k_ref, v_ref, nn)

    def scan_task():
        sf, sb = state["sf"], state["sb"]
        for nn in range(nc):
            st_ref[nn, :, 0:HEAD_W] = sf.astype(BF16)
            sf = cdf * sf + inc_ref[nn, :, 0:HEAD_W]
        for nn in reversed(range(nc)):
            st_ref[nn, :, HEAD_W:] = sb.astype(BF16)
            sb = cdb * sb + inc_ref[nn, :, HEAD_W:]

    def out_task(nn):
        rows = slice(nn * rc, (nn + 1) * rc)
        q = q_ref[0, rows, cs]
        s = _dot_nt(q, k_ref[0, rows, cs]) * dmat
        inner = _dot(s.astype(BF16), v_ref[0, rows, cs])
        qcat = jnp.concatenate([q * qdf, q * qdb], axis=1)
        out = inner + _dot_nt(qcat, st_ref[nn])
        o_ref[0, rows, cs] = _rms(out).astype(BF16) * g_ref[0, rows, cs]

    inc_tasks = [init_task] + [functools.partial(inc_task, nn) for nn in range(nc)]
    return inc_tasks, scan_task, [functools.partial(out_task, nn) for nn in range(nc)]


def _ret_kernel(lg_ref, q_ref, g_ref, k_ref, v_ref, o_ref, inc_ref, st_ref, *, layer, n, rc,
                heads):
    tasks = []
    for hh in range(heads):
        head = pl.program_id(1) * heads + hh
        tasks.append(_retention_tasks(
            lg_ref[layer, 0, head], lg_ref[layer, 1, head], slice(hh * LANES, (hh + 1) * LANES),
            q_ref, g_ref, k_ref, v_ref, None, None, o_ref, inc_ref.at[hh], st_ref.at[hh],
            n=n, n_ctx=0, rc=rc))
    for i in range(len(tasks[0][0])):
        for inc_tasks, _, _ in tasks:
            inc_tasks[i]()
    for _, scan_task, _ in tasks:
        scan_task()
    for i in range(len(tasks[0][2])):
        for _, _, out_tasks in tasks:
            out_tasks[i]()


def _retention_ctx(log_gamma, layer, zc, sec, n_b, n, *, rc, heads):
    width = heads * LANES
    head_block = lambda b, h: h
    return pl.pallas_call(
        functools.partial(_ret_kernel, layer=layer, n=n, rc=rc, heads=heads),
        grid=(n_b, HEADS // heads),
        in_specs=[pl.BlockSpec(memory_space=pltpu.SMEM)]
        + [_z_spec(sec[name], n, width, True, col_of=head_block)
           for name in ("rq", "rg", "rk", "rv")],
        out_specs=pl.BlockSpec((1, n, width), lambda b, h: (b, 0, h)),
        out_shape=jax.ShapeDtypeStruct((n_b, n, BRANCH_W), BF16),
        scratch_shapes=[pltpu.VMEM((heads, n // rc, HEAD_W, 2 * HEAD_W), F32),
                        pltpu.VMEM((heads, n // rc, HEAD_W, 2 * HEAD_W), BF16)],
        compiler_params=pltpu.CompilerParams(vmem_limit_bytes=VMEM_LIMIT),
        name="retention_ctx",
    )(log_gamma, zc, zc, zc, zc)


def _split_maps(q, lane):
    zero = jnp.zeros_like(q)
    return jnp.concatenate([jnp.where(lane < DIFF_DQK, q, zero),
                            jnp.where(lane >= DIFF_DQK, q, zero)], axis=0)


def _diff_ctx_kernel(lam_ref, q_ref, g_ref, k_ref, v_ref, o_ref, *, layer, tq, post_scale):
    lane = lax.broadcasted_iota(jnp.int32, (tq, LANES), 1)
    for h in range(HEADS):
        cs = slice(h * LANES, (h + 1) * LANES)
        s = _dot_nt(_split_maps(q_ref[0, :, cs], lane), k_ref[0, :, cs])
        p = jnp.exp2(s - jnp.max(s, axis=-1, keepdims=True))
        o = _dot(p.astype(BF16), v_ref[0, :, cs]) / jnp.sum(p, axis=-1, keepdims=True)
        d = o[:tq] - lam_ref[layer] * o[tq:]
        o_ref[0, :, cs] = (_rms(d) * post_scale * g_ref[0, :, cs].astype(F32)).astype(BF16)


def _diff_attention_ctx(lam, layer, zc, sec, n_b, n, *, post_scale):
    return pl.pallas_call(
        functools.partial(_diff_ctx_kernel, layer=layer, tq=n, post_scale=post_scale),
        grid=(n_b,),
        in_specs=[pl.BlockSpec(memory_space=pltpu.SMEM)]
        + [_z_spec(sec[name], n, SEC, True) for name in ("dq", "dg", "dk", "dv")],
        out_specs=pl.BlockSpec((1, n, BRANCH_W), lambda b: (b, 0, 0)),
        out_shape=jax.ShapeDtypeStruct((n_b, n, BRANCH_W), BF16),
        compiler_params=pltpu.CompilerParams(vmem_limit_bytes=VMEM_LIMIT),
        name="diff_attention_ctx",
    )(lam, zc, zc, zc, zc)


def _mixer_kernel(lam_ref, lg_ref, q_ref, g_ref, kc_ref, vc_ref, kl_ref, vl_ref,
                  rq_ref, rg_ref, rk_ref, rv_ref, rkc_ref, rvc_ref, od_ref, or_ref,
                  kall, vt, s0, s1, p0, p1, d0, d1, inc_ref, st_ref, *, layer, n, n_ctx, ts,
                  rc, post_scale):
    n_sub = n // ts
    nk = kall.shape[0]
    kall[0:n_ctx, :] = kc_ref[0]
    kall[n_ctx:, :] = kl_ref[0]
    for c in range(nk // LANES):
        src, base = (vc_ref, 0) if c * LANES < n_ctx else (vl_ref, n_ctx)
        rows = slice(c * LANES - base, (c + 1) * LANES - base)
        vt[0:LANES, c * LANES:(c + 1) * LANES] = src[0, rows, :].astype(F32).T.astype(BF16)
    vt[LANES:, :] = jnp.ones((vt.shape[0] - LANES, nk), BF16)
    lam = lam_ref[layer]
    lane = lax.broadcasted_iota(jnp.int32, (ts, LANES), 1)
    s_bufs, p_bufs, d_bufs = (s0, s1), (p0, p1), (d0, d1)

    def stage_a(t):
        rows = slice(t * ts, (t + 1) * ts)
        s = _dot_nt(kall[...], _split_maps(q_ref[0, rows, :], lane))
        s_bufs[t % 2][...] = s
        d_bufs[t % 2][...] = jnp.max(s, axis=0, keepdims=True)

    def stage_b(t):
        p_bufs[t % 2][...] = jnp.exp2(s_bufs[t % 2][...] - d_bufs[t % 2][...]).astype(BF16)

    def stage_c(t):
        rows = slice(t * ts, (t + 1) * ts)
        oe = _dot(vt[...], p_bufs[t % 2][...])
        ot = oe[:LANES] / oe[LANES:LANES + 1]
        dt = ot[:, :ts] - lam * ot[:, ts:]
        dt = dt * lax.rsqrt(jnp.mean(dt * dt, axis=0, keepdims=True) + EPS)
        od_ref[0, rows, :] = (dt.T * post_scale * g_ref[0, rows, :].astype(F32)).astype(BF16)

    head = pl.program_id(1)
    inc_tasks, scan_task, out_tasks = _retention_tasks(
        lg_ref[layer, 0, head], lg_ref[layer, 1, head], slice(0, LANES),
        rq_ref, rg_ref, rk_ref, rv_ref, rkc_ref, rvc_ref, or_ref, inc_ref, st_ref,
        n=n, n_ctx=n_ctx, rc=rc)
    n_steps = n_sub + 2
    extra = [[] for _ in range(n_steps)]
    half = (len(inc_tasks) + 1) // 2
    extra[0] += inc_tasks[:half]
    extra[1] += inc_tasks[half:]
    extra[2].append(scan_task)
    for i, task in enumerate(out_tasks):
        extra[2 + (i * (n_steps - 2)) // len(out_tasks)].append(task)

    for t in range(n_steps):
        if t >= 2:
            stage_c(t - 2)
        if 1 <= t <= n_sub:
            stage_b(t - 1)
        if t < n_sub:
            stage_a(t)
        for task in extra[t]:
            task()


def _latent_mixers(lam, log_gamma, layer, z, sec, n_b, n, zc, sec_c, n_ctx, *, ts, rc,
                   post_scale):
    nk = n_ctx + n
    head_block = lambda b, h: h
    lat = lambda name: _z_spec(sec[name], n, LANES, False, col_of=head_block)
    cx = lambda name: _z_spec(sec_c[name], n_ctx, LANES, True, col_of=head_block)
    out_spec = pl.BlockSpec((1, n, LANES), lambda b, h: (b, 0, h))
    out_shape = jax.ShapeDtypeStruct((n_b, n, BRANCH_W), BF16)
    o_diff, o_ret = pl.pallas_call(
        functools.partial(_mixer_kernel, layer=layer, n=n, n_ctx=n_ctx, ts=ts, rc=rc,
                          post_scale=post_scale),
        grid=(n_b, HEADS),
        in_specs=[pl.BlockSpec(memory_space=pltpu.SMEM), pl.BlockSpec(memory_space=pltpu.SMEM),
                  lat("dq"), lat("dg"), cx("dk"), cx("dv"), lat("dk"), lat("dv"),
                  lat("rq"), lat("rg"), lat("rk"), lat("rv"), cx("rk"), cx("rv")],
        out_specs=[out_spec, out_spec],
        out_shape=[out_shape, out_shape],
        scratch_shapes=[pltpu.VMEM((nk, LANES), BF16), pltpu.VMEM((LANES + 16, nk), BF16),
                        pltpu.VMEM((nk, 2 * ts), F32), pltpu.VMEM((nk, 2 * ts), F32),
                        pltpu.VMEM((nk, 2 * ts), BF16), pltpu.VMEM((nk, 2 * ts), BF16),
                        pltpu.VMEM((1, 2 * ts), F32), pltpu.VMEM((1, 2 * ts), F32),
                        pltpu.VMEM((n // rc, HEAD_W, 2 * HEAD_W), F32),
                        pltpu.VMEM((n // rc, HEAD_W, 2 * HEAD_W), BF16)],
        compiler_params=pltpu.CompilerParams(vmem_limit_bytes=VMEM_LIMIT),
        name="latent_mixers",
    )(lam, log_gamma, z, z, zc, zc, z, z, z, z, z, z, zc, zc)
    return o_ret, o_diff


def _merge_kernel(x_ref, gate_ref, gpost_ref, oret_ref, odiff_ref, u_ref, vn_ref, mg_ref,
                  gr_ref, gm_ref, gd_ref, ws_ref, bs_ref, wbo_ref, wo_ref, out_ref, *, tm, out_sub):
    rows = []
    for c in range(tm // CHUNK):
        cols = []
        for g in range(HEADS):
            blk = vn_ref[0, c * CHUNK:(c + 1) * CHUNK, g * LANES:(g + 1) * LANES]
            cols.append(_dot(ws_ref[g], blk) + bs_ref[g])
        rows.append(jnp.concatenate(cols, axis=1))
    sp = jnp.concatenate(rows, axis=0)
    o_mlp = (u_ref[0].astype(F32) * sp * mg_ref[0].astype(F32)).astype(BF16)

    def gated(gate2_ref, o, w):
        gate = jnp.concatenate([gate2_ref[0, 0], gate2_ref[0, 1]], axis=1).astype(F32)
        return gate * _dot(o, w)

    t = (gated(gr_ref, oret_ref[0], wbo_ref[0]) + gated(gm_ref, o_mlp, wbo_ref[1])
         + gated(gd_ref, odiff_ref[0], wbo_ref[2]))
    tb = t.astype(BF16)
    for r in range(tm // out_sub):
        rs = slice(r * out_sub, (r + 1) * out_sub)
        y = _dot(tb[rs, :], wo_ref[...])
        out_ref[0, rs, :] = x_ref[0, rs, :] + gate_ref[...] * (_rms(y) * gpost_ref[...])


def _merge(xs, l, mod4, row_of, g_post, o_ret, o_diff, z, sec, ws, bs, wbo, wo, *, tm):
    n_b, n, _ = xs.shape
    row_block = lambda b, i: i

    def zsec(name):
        return _z_spec(sec[name], tm, SEC, False, row_of=row_block)

    def zgate(k):
        return pl.BlockSpec((1, 2, tm, SEC), lambda b, i: (b, k, i, 0))

    in_specs = [
        pl.BlockSpec((1, tm, D_MODEL), lambda b, i: (b, i, 0)),
        _mod_spec(l, row_of, 2),
        _layer_spec(l, (1, D_MODEL)),
        pl.BlockSpec((1, tm, BRANCH_W), lambda b, i: (b, i, 0)),
        pl.BlockSpec((1, tm, BRANCH_W), lambda b, i: (b, i, 0)),
        zsec("mu"), zsec("mv"), zsec("mg"), zgate(0), zgate(1), zgate(2),
        _layer_spec(l, (HEADS, CHUNK, CHUNK)),
        _layer_spec(l, (HEADS, CHUNK, LANES)),
        _layer_spec(l, (N_BRANCH, BRANCH_W, D_MODEL)),
        _layer_spec(l, (D_MODEL, D_MODEL)),
    ]
    return pl.pallas_call(
        functools.partial(_merge_kernel, tm=tm, out_sub=MERGE_OUT_SUB),
        grid=(n_b, n // tm),
        in_specs=in_specs,
        out_specs=pl.BlockSpec((1, tm, D_MODEL), lambda b, i: (b, i, 0)),
        out_shape=jax.ShapeDtypeStruct(xs.shape, F32),
        compiler_params=pltpu.CompilerParams(vmem_limit_bytes=VMEM_LIMIT),
        name="merge",
    )(xs, mod4, g_post, o_ret, o_diff, z, z, z, z, z, z, ws, bs, wbo, wo)


def _rope_tables(n_lat):
    rows = n_lat // GRID_W
    row_pos = jnp.repeat(jnp.arange(rows, dtype=F32), GRID_W)
    col_pos = jnp.tile(jnp.arange(GRID_W, dtype=F32), rows)

    def angles(head_dim):
        n_freq = head_dim // 4
        inv = ROPE_BASE ** (-jnp.arange(n_freq, dtype=F32) / n_freq)
        ang = jnp.concatenate([row_pos[:, None] * inv, col_pos[:, None] * inv], axis=-1)
        return jnp.cos(ang), jnp.sin(ang)

    cos_r, sin_r = angles(HEAD_W)
    cos_d, sin_d = angles(DIFF_DQK)
    zeros = jnp.zeros_like(sin_d)
    c_r = jnp.concatenate([cos_r, cos_r], axis=-1)
    s_r = jnp.concatenate([-sin_r, sin_r], axis=-1)
    c_d = jnp.tile(jnp.concatenate([cos_d, cos_d], axis=-1), (1, 2))
    s_lo = jnp.tile(jnp.concatenate([-sin_d, zeros], axis=-1), (1, 2))
    s_hi = jnp.tile(jnp.concatenate([zeros, sin_d], axis=-1), (1, 2))
    return c_r, s_r, c_d, s_lo, s_hi


@jax.jit
def kernel(x, c, ctx, c_ctx, w_mod, b_mod, g_pre, g_post, w_in, ret_decay_logit, mlp_w_s,
           mlp_b_s, diff_lambda_q, diff_lambda_k, w_branch_out, w_out):
    n_b, n_lat, _ = x.shape
    n_ctx = ctx.shape[1]
    tables = jnp.stack(_rope_tables(n_lat))

    cond_rows = 16
    ctx_row = n_b
    cc = jnp.zeros((cond_rows, D_MODEL), F32).at[:n_b].set(c).at[ctx_row].set(c_ctx)
    mod4 = _modulation(cc, w_mod, b_mod).reshape(DEPTH, cond_rows, 1, 3 * D_MODEL)
    lat_row = lambda b: b
    ctx_row_of = lambda b: ctx_row

    log_gamma = -jax.nn.softplus(-ret_decay_logit.astype(F32))
    lam_inits = [0.8 - 0.6 * math.exp(-0.3 * l) for l in range(DEPTH)]
    lam = (jnp.exp(jnp.sum(diff_lambda_q[:, 0] * diff_lambda_k[:, 0], axis=-1))
           - jnp.exp(jnp.sum(diff_lambda_q[:, 1] * diff_lambda_k[:, 1], axis=-1))
           + jnp.asarray(lam_inits, F32)).astype(F32)

    w_sec = w_in
    ws = mlp_w_s.astype(BF16)
    bs = jnp.broadcast_to(mlp_b_s[..., None], (DEPTH, HEADS, CHUNK, LANES)).astype(F32)
    wbo = w_branch_out.astype(BF16)
    wo = w_out.astype(BF16)
    g_pre3 = g_pre.reshape(DEPTH, 1, D_MODEL)
    g_post3 = g_post.reshape(DEPTH, 1, D_MODEL)

    for l in range(DEPTH):
        last = l == DEPTH - 1
        post_scale = 1.0 - lam_inits[l]

        ctx_flat = ctx.reshape(1, n_b * n_ctx, D_MODEL)
        if last:
            zc = _projection(ctx_flat, l, g_pre3, mod4, ctx_row_of, w_sec, lambda j: j,
                             KV_KINDS, None)
            sec_c = KV_SEC
        else:
            zc = _projection(ctx_flat, l, g_pre3, mod4, ctx_row_of, w_sec, _full_src,
                             FULL_KINDS, None)
            sec_c = FULL_SEC
            co_ret = _retention_ctx(log_gamma, l, zc, sec_c, n_b, n_ctx, rc=RET_CHUNK,
                                    heads=HEADS)
            co_diff = _diff_attention_ctx(lam, l, zc, sec_c, n_b, n_ctx, post_scale=post_scale)
            ctx_next = _merge(ctx_flat, l, mod4, ctx_row_of, g_post3,
                              co_ret.reshape(1, n_b * n_ctx, BRANCH_W),
                              co_diff.reshape(1, n_b * n_ctx, BRANCH_W), zc, sec_c,
                              ws, bs, wbo, wo, tm=MERGE_TM)
            ctx_next = ctx_next.reshape(n_b, n_ctx, D_MODEL)

        z = _projection(x, l, g_pre3, mod4, lat_row, w_sec, _full_src, FULL_KINDS, tables)
        o_ret, o_diff = _latent_mixers(lam, log_gamma, l, z, FULL_SEC, n_b, n_lat, zc, sec_c,
                                       n_ctx, ts=DIFF_TS, rc=RET_CHUNK, post_scale=post_scale)
        x = _merge(x, l, mod4, lat_row, g_post3, o_ret, o_diff, z, FULL_SEC, ws, bs, wbo, wo,
                   tm=MERGE_TM)
        if not last:
            ctx = ctx_next
    return x
```

```python
import functools
import math

import jax
import jax.numpy as jnp
from jax import lax
from jax.experimental import pallas as pl
from jax.experimental.pallas import tpu as pltpu

F32 = jnp.float32
BF16 = jnp.bfloat16

D_MODEL = 1024
DEPTH = 2
GRID_W = 64
N_BRANCH = 3
BRANCH_W = D_MODEL // 2
HEADS = 4
HEAD_W = BRANCH_W // HEADS
DIFF_DQK = HEAD_W // 2
CHUNK = 128
ROPE_BASE = 10000.0
EPS = 1e-6
RET_SCALE = HEAD_W ** -0.5
DIFF_Q_SCALE = (DIFF_DQK ** -0.5) * math.log2(math.e)

LANES = 128
SEC = BRANCH_W
MERGE_COLS = N_BRANCH * D_MODEL
KV_COLS = 4 * BRANCH_W
IN_COLS = KV_COLS + 7 * BRANCH_W + MERGE_COLS
VMEM_LIMIT = 56 * 1024 * 1024

PROJ_SUB = 256
PROJ_OUT_SLOTS = 2
DIFF_TS = 128
RET_CHUNK = 256
MERGE_TM = 512
MERGE_OUT_SUB = 512

FULL_KINDS = ("sigmoid",) * 6 + ("rk", "plain", "dk", "plain", "rq", "silu", "dq", "silu",
                                 "plain", "ln", "silu")
FULL_SEC = dict(rk=6, rv=7, dk=8, dv=9, rq=10, rg=11, dq=12, dg=13, mu=14, mv=15, mg=16)
KV_KINDS = ("rk", "plain", "dk", "plain")
KV_SEC = dict(rk=0, rv=1, dk=2, dv=3)
_N_MERGE_SEC = MERGE_COLS // SEC
_N_REST_SEC = (KV_COLS + 7 * BRANCH_W) // SEC


def _full_src(j):
    return jnp.where(j < _N_MERGE_SEC, j + _N_REST_SEC, j - _N_MERGE_SEC)


def _dot(a, b):
    return jnp.dot(a, b, preferred_element_type=F32)


def _dot_nt(a, b):
    return lax.dot_general(a, b, (((1,), (1,)), ((), ())), preferred_element_type=F32)


def _rms(v):
    return v * lax.rsqrt(jnp.mean(v * v, axis=-1, keepdims=True) + EPS)


def _mod_spec(l, row_of, k):
    return pl.BlockSpec((None, None, 1, D_MODEL), lambda *idx: (l, row_of(idx[0]), 0, k))


def _layer_spec(l, shape, **kwargs):
    zeros = (0,) * len(shape)
    return pl.BlockSpec((None,) + tuple(shape), lambda *idx: (l,) + zeros, **kwargs)


def _z_spec(sec_idx, rows, width, flat, col_of=lambda *idx: 0, row_of=None):
    if row_of is None:
        row_of = (lambda *idx: idx[0]) if flat else (lambda *idx: 0)
    group_of = (lambda *idx: 0) if flat else (lambda *idx: idx[0])
    return pl.BlockSpec((1, None, rows, width),
                        lambda *idx: (group_of(*idx), sec_idx, row_of(*idx), col_of(*idx)))


def _split_bf16(a):
    hi = a.astype(BF16)
    return hi, (a - hi.astype(F32)).astype(BF16)


def _mod_kernel(c_ref, w_ref, b_ref, o_ref):
    cv = c_ref[...]
    c_hi, c_lo = _split_bf16(cv * jax.nn.sigmoid(cv))
    w_hi, w_lo = _split_bf16(w_ref[0])
    o_ref[0] = _dot(c_hi, w_hi) + _dot(c_hi, w_lo) + _dot(c_lo, w_hi) + b_ref[0]


def _modulation(cc, w_mod, b_mod):
    rows = cc.shape[0]
    return pl.pallas_call(
        _mod_kernel,
        grid=(DEPTH, 3),
        in_specs=[
            pl.BlockSpec((rows, D_MODEL), lambda l, j: (0, 0)),
            pl.BlockSpec((1, D_MODEL, D_MODEL), lambda l, j: (l, 0, j)),
            pl.BlockSpec((1, 1, D_MODEL), lambda l, j: (l, 0, j)),
        ],
        out_specs=pl.BlockSpec((1, rows, D_MODEL), lambda l, j: (l, 0, j)),
        out_shape=jax.ShapeDtypeStruct((DEPTH, rows, 3 * D_MODEL), F32),
        compiler_params=pltpu.CompilerParams(vmem_limit_bytes=VMEM_LIMIT),
        name="modulation",
    )(cc, w_mod, b_mod.reshape(DEPTH, 1, 3 * D_MODEL))


def _rope_ret(a, c, s):
    outs = []
    for h in range(HEADS):
        ah = a[:, h * LANES:(h + 1) * LANES]
        outs.append(ah * c + pltpu.roll(ah, LANES // 2, 1) * s)
    return jnp.concatenate(outs, axis=1)


def _rope_diff(a, c, s_lo, s_hi):
    outs = []
    for h in range(HEADS):
        ah = a[:, h * LANES:(h + 1) * LANES]
        outs.append(ah * c + pltpu.roll(ah, LANES - DIFF_DQK // 2, 1) * s_lo
                    + pltpu.roll(ah, DIFF_DQK // 2, 1) * s_hi)
    return jnp.concatenate(outs, axis=1)


def _proj_kernel(*refs, layer, kinds, src_of, n_src, rope, n_rows, sub):
    if rope:
        (x_ref, g_ref, shift_ref, scale_ref, w_hbm, tab_hbm, z_hbm,
         h_ref, stage, w_vmem, w_stage, sems_out, sems_w, tab_vmem, sem_tab) = refs
        cr_ref, sr_ref, cd_ref, sdl_ref, sdh_ref = [tab_vmem.at[i] for i in range(5)]
    else:
        (x_ref, g_ref, shift_ref, scale_ref, w_hbm, z_hbm,
         h_ref, stage, w_vmem, w_stage, sems_out, sems_w) = refs
    b = pl.program_id(0)
    first = b == 0
    n_sub = n_rows // sub
    n_sec = len(kinds)
    n_slots = stage.shape[0]

    n_wslots = w_stage.shape[0]

    def weight_copy(j):
        start = src_of(j) * SEC
        cols = pl.ds(start if isinstance(start, int) else pl.multiple_of(start, SEC), SEC)
        slot = j % n_wslots
        return pltpu.make_async_copy(w_hbm.at[layer, :, cols], w_stage.at[slot], sems_w.at[slot])

    def load_weights(j):
        weight_copy(j).wait()
        src = w_stage.at[j % n_wslots]
        dst = w_vmem.at[src_of(j)]
        for r in range(D_MODEL // LANES):
            rows = slice(r * LANES, (r + 1) * LANES)
            dst[rows, :] = src[rows, :].astype(BF16)

        @pl.when(j + n_wslots < n_sec)
        def _():
            weight_copy(j + n_wslots).start()

    def table_copy():
        return pltpu.make_async_copy(tab_hbm, tab_vmem, sem_tab.at[0])

    @pl.when(first)
    def _():
        for j in range(min(n_wslots, n_sec)):
            weight_copy(j).start()
        if rope:
            table_copy().start()

    gs = g_ref[...] * (1.0 + scale_ref[...])
    sh = shift_ref[...]
    for r in range(n_sub):
        xs = x_ref[0, r * sub:(r + 1) * sub, :]
        h_ref[r * sub:(r + 1) * sub, :] = (_rms(xs) * gs + sh).astype(BF16)

    if rope:
        pl.when(first)(lambda: table_copy().wait())

    def epi_rk(a, rows):
        if rope:
            a = _rope_ret(a, cr_ref[rows, :], sr_ref[rows, :])
        return a * RET_SCALE

    def epi_rq(a, rows):
        return _rope_ret(a, cr_ref[rows, :], sr_ref[rows, :]) if rope else a

    def epi_dk(a, rows):
        return _rope_diff(a, cd_ref[rows, :], sdl_ref[rows, :], sdh_ref[rows, :]) if rope else a

    def epi_dq(a, rows):
        return epi_dk(a, rows) * DIFF_Q_SCALE

    def epi_ln(a, rows):
        mu = jnp.mean(a, axis=-1, keepdims=True)
        d = a - mu
        return d * lax.rsqrt(jnp.mean(d * d, axis=-1, keepdims=True) + EPS)

    epilogues = dict(
        plain=lambda a, rows: a,
        sigmoid=lambda a, rows: jax.nn.sigmoid(a),
        silu=lambda a, rows: a * jax.nn.sigmoid(a),
        rk=epi_rk, rq=epi_rq, dk=epi_dk, dq=epi_dq, ln=epi_ln)

    def writeback(j):
        slot = j % n_slots
        return pltpu.make_async_copy(stage.at[slot], z_hbm.at[b, j], sems_out.at[slot])

    def run(epilogue, j):
        w_sec = w_vmem.at[src_of(j)]
        out = stage.at[j % n_slots]
        for r in range(n_sub):
            rows = slice(r * sub, (r + 1) * sub)
            out[rows, :] = epilogue(_dot(h_ref[rows, :], w_sec[...]), rows).astype(BF16)

    def section(j, carry):
        pl.when(first)(functools.partial(load_weights, j))

        @pl.when(j >= n_slots)
        def _():
            writeback(j - n_slots).wait()

        for kind in sorted(set(kinds)):
            idx = [i for i, k in enumerate(kinds) if k == kind]
            cond = j == idx[0]
            for i in idx[1:]:
                cond = cond | (j == i)
            pl.when(cond)(functools.partial(run, epilogues[kind], j))
        writeback(j).start()
        return carry

    lax.fori_loop(0, n_sec, section, 0)
    for j in range(max(0, n_sec - n_slots), n_sec):
        writeback(j).wait()


def _projection(xs, l, g_pre, mod4, row_of, w_all, src_of, kinds, tables):
    n_g, n_rows, _ = xs.shape
    n_sec = len(kinds)
    rope = tables is not None
    n_src = w_all.shape[2] // SEC if src_of is _full_src else n_sec
    in_specs = [
        pl.BlockSpec((1, n_rows, D_MODEL), lambda b: (b, 0, 0)),
        _layer_spec(l, (1, D_MODEL)),
        _mod_spec(l, row_of, 0),
        _mod_spec(l, row_of, 1),
        pl.BlockSpec(memory_space=pl.ANY),
    ]
    args = [xs, g_pre, mod4, mod4, w_all]
    scratch = [pltpu.VMEM((n_rows, D_MODEL), BF16),
               pltpu.VMEM((PROJ_OUT_SLOTS, n_rows, SEC), BF16),
               pltpu.VMEM((n_src, D_MODEL, SEC), BF16),
               pltpu.VMEM((2, D_MODEL, SEC), F32),
               pltpu.SemaphoreType.DMA((PROJ_OUT_SLOTS,)),
               pltpu.SemaphoreType.DMA((2,))]
    if rope:
        in_specs.append(pl.BlockSpec(memory_space=pl.ANY))
        args.append(tables)
        scratch += [pltpu.VMEM(tables.shape, F32), pltpu.SemaphoreType.DMA((1,))]
    return pl.pallas_call(
        functools.partial(_proj_kernel, layer=l, kinds=kinds, src_of=src_of, n_src=n_src,
                          rope=rope, n_rows=n_rows, sub=PROJ_SUB),
        grid=(n_g,),
        in_specs=in_specs,
        out_specs=pl.BlockSpec(memory_space=pl.ANY),
        out_shape=jax.ShapeDtypeStruct((n_g, n_sec, n_rows, SEC), BF16),
        scratch_shapes=scratch,
        compiler_params=pltpu.CompilerParams(vmem_limit_bytes=VMEM_LIMIT),
        name="projection_rope" if rope else "projection",
    )(*args)


def _retention_tasks(lgf, lgb, cs, q_ref, g_ref, k_ref, v_ref, kc_ref, vc_ref, o_ref, inc_ref,
                     st_ref, *, n, n_ctx, rc):
    nc = n // rc
    pos = lax.broadcasted_iota(jnp.int32, (rc, LANES), 0).astype(F32)
    kdf = jnp.exp(lgf * (rc - 1.0 - pos)).astype(BF16)
    kdb = jnp.exp(lgb * pos).astype(BF16)
    cdf = jnp.exp(jnp.full((1, LANES), lgf * rc, F32))
    cdb = jnp.exp(jnp.full((1, LANES), lgb * rc, F32))
    qdf = jnp.exp(lgf * (pos + 1.0)).astype(BF16)
    qdb = jnp.exp(lgb * (rc - pos)).astype(BF16)
    ii = lax.broadcasted_iota(jnp.int32, (rc, rc), 0)
    jj = lax.broadcasted_iota(jnp.int32, (rc, rc), 1)
    dist = (ii - jj).astype(F32)
    dmat = (jnp.where(dist >= 0, jnp.exp(lgf * jnp.maximum(dist, 0.0)), 0.0)
            + jnp.where(dist <= 0, jnp.exp(lgb * jnp.maximum(-dist, 0.0)), 0.0))
    state = {}

    def increment(kr, vr, nn):
        rows = slice(nn * rc, (nn + 1) * rc)
        kk = kr[0, rows, cs]
        kcat = jnp.concatenate([kk * kdf, kk * kdb], axis=1)
        vt = vr[0, rows, cs].astype(F32).T.astype(BF16)
        return _dot(vt, kcat)

    def init_task():
        sf = jnp.zeros((HEAD_W, HEAD_W), F32)
        sb = jnp.zeros((HEAD_W, HEAD_W), F32)
        if n_ctx:
            incs = [increment(kc_ref, vc_ref, nn) for nn in range(n_ctx // rc)]
            for inc in incs:
                sf = cdf * sf + inc[:, :HEAD_W]
            for inc in reversed(incs):
                sb = cdb * sb + inc[:, HEAD_W:]
        state["sf"], state["sb"] = sf, sb

    def inc_task(nn):
        inc_ref[nn] = increment(k_ref, v_ref, nn)

    def scan_task():
        sf, sb = state["sf"], state["sb"]
        for nn in range(nc):
            st_ref[nn, :, 0:HEAD_W] = sf.astype(BF16)
            sf = cdf * sf + inc_ref[nn, :, 0:HEAD_W]
        for nn in reversed(range(nc)):
            st_ref[nn, :, HEAD_W:] = sb.astype(BF16)
            sb = cdb * sb + inc_ref[nn, :, HEAD_W:]

    def out_task(nn):
        rows = slice(nn * rc, (nn + 1) * rc)
        q = q_ref[0, rows, cs]
        s = _dot_nt(q, k_ref[0, rows, cs]) * dmat
        inner = _dot(s.astype(BF16), v_ref[0, rows, cs])
        qcat = jnp.concatenate([q * qdf, q * qdb], axis=1)
        out = inner + _dot_nt(qcat, st_ref[nn])
        o_ref[0, rows, cs] = _rms(out).astype(BF16) * g_ref[0, rows, cs]

    inc_tasks = [init_task] + [functools.partial(inc_task, nn) for nn in range(nc)]
    return inc_tasks, scan_task, [functools.partial(out_task, nn) for nn in range(nc)]


def _ret_kernel(lg_ref, q_ref, g_ref, k_ref, v_ref, o_ref, inc_ref, st_ref, *, layer, n, rc,
                heads):
    tasks = []
    for hh in range(heads):
        head = pl.program_id(1) * heads + hh
        tasks.append(_retention_tasks(
            lg_ref[layer, 0, head], lg_ref[layer, 1, head], slice(hh * LANES, (hh + 1) * LANES),
            q_ref, g_ref, k_ref, v_ref, None, None, o_ref, inc_ref.at[hh], st_ref.at[hh],
            n=n, n_ctx=0, rc=rc))
    for i in range(len(tasks[0][0])):
        for inc_tasks, _, _ in tasks:
            inc_tasks[i]()
    for _, scan_task, _ in tasks:
        scan_task()
    for i in range(len(tasks[0][2])):
        for _, _, out_tasks in tasks:
            out_tasks[i]()


def _retention_ctx(log_gamma, layer, zc, sec, n_b, n, *, rc, heads):
    width = heads * LANES
    head_block = lambda b, h: h
    return pl.pallas_call(
        functools.partial(_ret_kernel, layer=layer, n=n, rc=rc, heads=heads),
        grid=(n_b, HEADS // heads),
        in_specs=[pl.BlockSpec(memory_space=pltpu.SMEM)]
        + [_z_spec(sec[name], n, width, True, col_of=head_block)
           for name in ("rq", "rg", "rk", "rv")],
        out_specs=pl.BlockSpec((1, n, width), lambda b, h: (b, 0, h)),
        out_shape=jax.ShapeDtypeStruct((n_b, n, BRANCH_W), BF16),
        scratch_shapes=[pltpu.VMEM((heads, n // rc, HEAD_W, 2 * HEAD_W), F32),
                        pltpu.VMEM((heads, n // rc, HEAD_W, 2 * HEAD_W), BF16)],
        compiler_params=pltpu.CompilerParams(vmem_limit_bytes=VMEM_LIMIT),
        name="retention_ctx",
    )(log_gamma, zc, zc, zc, zc)


def _split_maps(q, lane):
    zero = jnp.zeros_like(q)
    return jnp.concatenate([jnp.where(lane < DIFF_DQK, q, zero),
                            jnp.where(lane >= DIFF_DQK, q, zero)], axis=0)


def _diff_ctx_kernel(lam_ref, q_ref, g_ref, k_ref, v_ref, o_ref, *, layer, tq, post_scale):
    lane = lax.broadcasted_iota(jnp.int32, (tq, LANES), 1)
    for h in range(HEADS):
        cs = slice(h * LANES, (h + 1) * LANES)
        s = _dot_nt(_split_maps(q_ref[0, :, cs], lane), k_ref[0, :, cs])
        p = jnp.exp2(s - jnp.max(s, axis=-1, keepdims=True))
        o = _dot(p.astype(BF16), v_ref[0, :, cs]) / jnp.sum(p, axis=-1, keepdims=True)
        d = o[:tq] - lam_ref[layer] * o[tq:]
        o_ref[0, :, cs] = (_rms(d) * post_scale * g_ref[0, :, cs].astype(F32)).astype(BF16)


def _diff_attention_ctx(lam, layer, zc, sec, n_b, n, *, post_scale):
    return pl.pallas_call(
        functools.partial(_diff_ctx_kernel, layer=layer, tq=n, post_scale=post_scale),
        grid=(n_b,),
        in_specs=[pl.BlockSpec(memory_space=pltpu.SMEM)]
        + [_z_spec(sec[name], n, SEC, True) for name in ("dq", "dg", "dk", "dv")],
        out_specs=pl.BlockSpec((1, n, BRANCH_W), lambda b: (b, 0, 0)),
        out_shape=jax.ShapeDtypeStruct((n_b, n, BRANCH_W), BF16),
        compiler_params=pltpu.CompilerParams(vmem_limit_bytes=VMEM_LIMIT),
        name="diff_attention_ctx",
    )(lam, zc, zc, zc, zc)


def _mixer_kernel(lam_ref, lg_ref, q_ref, g_ref, kc_ref, vc_ref, kl_ref, vl_ref,
                  rq_ref, rg_ref, rk_ref, rv_ref, rkc_ref, rvc_ref, od_ref, or_ref,
                  kall, vt, s0, s1, p0, p1, d0, d1, inc_ref, st_ref, *, layer, n, n_ctx, ts,
                  rc, post_scale):
    n_sub = n // ts
    nk = kall.shape[0]
    kall[0:n_ctx, :] = kc_ref[0]
    kall[n_ctx:, :] = kl_ref[0]
    for c in range(nk // LANES):
        src, base = (vc_ref, 0) if c * LANES < n_ctx else (vl_ref, n_ctx)
        rows = slice(c * LANES - base, (c + 1) * LANES - base)
        vt[0:LANES, c * LANES:(c + 1) * LANES] = src[0, rows, :].astype(F32).T.astype(BF16)
    vt[LANES:, :] = jnp.ones((vt.shape[0] - LANES, nk), BF16)
    lam = lam_ref[layer]
    lane = lax.broadcasted_iota(jnp.int32, (ts, LANES), 1)
    s_bufs, p_bufs, d_bufs = (s0, s1), (p0, p1), (d0, d1)

    def stage_a(t):
        rows = slice(t * ts, (t + 1) * ts)
        s = _dot_nt(kall[...], _split_maps(q_ref[0, rows, :], lane))
        s_bufs[t % 2][...] = s
        d_bufs[t % 2][...] = jnp.max(s, axis=0, keepdims=True)

    def stage_b(t):
        p_bufs[t % 2][...] = jnp.exp2(s_bufs[t % 2][...] - d_bufs[t % 2][...]).astype(BF16)

    def stage_c(t):
        rows = slice(t * ts, (t + 1) * ts)
        oe = _dot(vt[...], p_bufs[t % 2][...])
        ot = oe[:LANES] / oe[LANES:LANES + 1]
        dt = ot[:, :ts] - lam * ot[:, ts:]
        dt = dt * lax.rsqrt(jnp.mean(dt * dt, axis=0, keepdims=True) + EPS)
        od_ref[0, rows, :] = (dt.T * post_scale * g_ref[0, rows, :].astype(F32)).astype(BF16)

    head = pl.program_id(1)
    inc_tasks, scan_task, out_tasks = _retention_tasks(
        lg_ref[layer, 0, head], lg_ref[layer, 1, head], slice(0, LANES),
        rq_ref, rg_ref, rk_ref, rv_ref, rkc_ref, rvc_ref, or_ref, inc_ref, st_ref,
        n=n, n_ctx=n_ctx, rc=rc)
    n_steps = n_sub + 2
    extra = [[] for _ in range(n_steps)]
    half = (len(inc_tasks) + 1) // 2
    extra[0] += inc_tasks[:half]
    extra[1] += inc_tasks[half:]
    extra[2].append(scan_task)
    for i, task in enumerate(out_tasks):
        extra[2 + (i * (n_steps - 2)) // len(out_tasks)].append(task)

    for t in range(n_steps):
        if t >= 2:
            stage_c(t - 2)
        if 1 <= t <= n_sub:
            stage_b(t - 1)
        if t < n_sub:
            stage_a(t)
        for task in extra[t]:
            task()


def _latent_mixers(lam, log_gamma, layer, z, sec, n_b, n, zc, sec_c, n_ctx, *, ts, rc,
                   post_scale):
    nk = n_ctx + n
    head_block = lambda b, h: h
    lat = lambda name: _z_spec(sec[name], n, LANES, False, col_of=head_block)
    cx = lambda name: _z_spec(sec_c[name], n_ctx, LANES, True, col_of=head_block)
    out_spec = pl.BlockSpec((1, n, LANES), lambda b, h: (b, 0, h))
    out_shape = jax.ShapeDtypeStruct((n_b, n, BRANCH_W), BF16)
    o_diff, o_ret = pl.pallas_call(
        functools.partial(_mixer_kernel, layer=layer, n=n, n_ctx=n_ctx, ts=ts, rc=rc,
                          post_scale=post_scale),
        grid=(n_b, HEADS),
        in_specs=[pl.BlockSpec(memory_space=pltpu.SMEM), pl.BlockSpec(memory_space=pltpu.SMEM),
                  lat("dq"), lat("dg"), cx("dk"), cx("dv"), lat("dk"), lat("dv"),
                  lat("rq"), lat("rg"), lat("rk"), lat("rv"), cx("rk"), cx("rv")],
        out_specs=[out_spec, out_spec],
        out_shape=[out_shape, out_shape],
        scratch_shapes=[pltpu.VMEM((nk, LANES), BF16), pltpu.VMEM((LANES + 16, nk), BF16),
                        pltpu.VMEM((nk, 2 * ts), F32), pltpu.VMEM((nk, 2 * ts), F32),
                        pltpu.VMEM((nk, 2 * ts), BF16), pltpu.VMEM((nk, 2 * ts), BF16),
                        pltpu.VMEM((1, 2 * ts), F32), pltpu.VMEM((1, 2 * ts), F32),
                        pltpu.VMEM((n // rc, HEAD_W, 2 * HEAD_W), F32),
                        pltpu.VMEM((n // rc, HEAD_W, 2 * HEAD_W), BF16)],
        compiler_params=pltpu.CompilerParams(vmem_limit_bytes=VMEM_LIMIT),
        name="latent_mixers",
    )(lam, log_gamma, z, z, zc, zc, z, z, z, z, z, z, zc, zc)
    return o_ret, o_diff


def _merge_kernel(x_ref, gate_ref, gpost_ref, oret_ref, odiff_ref, u_ref, vn_ref, mg_ref,
                  gr_ref, gm_ref, gd_ref, ws_ref, bs_ref, wbo_ref, wo_ref, out_ref, *, tm, out_sub):
    rows = []
    for c in range(tm // CHUNK):
        cols = []
        for g in range(HEADS):
            blk = vn_ref[0, c * CHUNK:(c + 1) * CHUNK, g * LANES:(g + 1) * LANES]
            cols.append(_dot(ws_ref[g], blk) + bs_ref[g])
        rows.append(jnp.concatenate(cols, axis=1))
    sp = jnp.concatenate(rows, axis=0)
    o_mlp = (u_ref[0].astype(F32) * sp * mg_ref[0].astype(F32)).astype(BF16)

    def gated(gate2_ref, o, w):
        gate = jnp.concatenate([gate2_ref[0, 0], gate2_ref[0, 1]], axis=1).astype(F32)
        return gate * _dot(o, w)

    t = (gated(gr_ref, oret_ref[0], wbo_ref[0]) + gated(gm_ref, o_mlp, wbo_ref[1])
         + gated(gd_ref, odiff_ref[0], wbo_ref[2]))
    tb = t.astype(BF16)
    for r in range(tm // out_sub):
        rs = slice(r * out_sub, (r + 1) * out_sub)
        y = _dot(tb[rs, :], wo_ref[...])
        out_ref[0, rs, :] = x_ref[0, rs, :] + gate_ref[...] * (_rms(y) * gpost_ref[...])


def _merge(xs, l, mod4, row_of, g_post, o_ret, o_diff, z, sec, ws, bs, wbo, wo, *, tm):
    n_b, n, _ = xs.shape
    row_block = lambda b, i: i

    def zsec(name):
        return _z_spec(sec[name], tm, SEC, False, row_of=row_block)

    def zgate(k):
        return pl.BlockSpec((1, 2, tm, SEC), lambda b, i: (b, k, i, 0))

    in_specs = [
        pl.BlockSpec((1, tm, D_MODEL), lambda b, i: (b, i, 0)),
        _mod_spec(l, row_of, 2),
        _layer_spec(l, (1, D_MODEL)),
        pl.BlockSpec((1, tm, BRANCH_W), lambda b, i: (b, i, 0)),
        pl.BlockSpec((1, tm, BRANCH_W), lambda b, i: (b, i, 0)),
        zsec("mu"), zsec("mv"), zsec("mg"), zgate(0), zgate(1), zgate(2),
        _layer_spec(l, (HEADS, CHUNK, CHUNK)),
        _layer_spec(l, (HEADS, CHUNK, LANES)),
        _layer_spec(l, (N_BRANCH, BRANCH_W, D_MODEL)),
        _layer_spec(l, (D_MODEL, D_MODEL)),
    ]
    return pl.pallas_call(
        functools.partial(_merge_kernel, tm=tm, out_sub=MERGE_OUT_SUB),
        grid=(n_b, n // tm),
        in_specs=in_specs,
        out_specs=pl.BlockSpec((1, tm, D_MODEL), lambda b, i: (b, i, 0)),
        out_shape=jax.ShapeDtypeStruct(xs.shape, F32),
        compiler_params=pltpu.CompilerParams(vmem_limit_bytes=VMEM_LIMIT),
        name="merge",
    )(xs, mod4, g_post, o_ret, o_diff, z, z, z, z, z, z, ws, bs, wbo, wo)


def _rope_tables(n_lat):
    rows = n_lat // GRID_W
    row_pos = jnp.repeat(jnp.arange(rows, dtype=F32), GRID_W)
    col_pos = jnp.tile(jnp.arange(GRID_W, dtype=F32), rows)

    def angles(head_dim):
        n_freq = head_dim // 4
        inv = ROPE_BASE ** (-jnp.arange(n_freq, dtype=F32) / n_freq)
        ang = jnp.concatenate([row_pos[:, None] * inv, col_pos[:, None] * inv], axis=-1)
        return jnp.cos(ang), jnp.sin(ang)

    cos_r, sin_r = angles(HEAD_W)
    cos_d, sin_d = angles(DIFF_DQK)
    zeros = jnp.zeros_like(sin_d)
    c_r = jnp.concatenate([cos_r, cos_r], axis=-1)
    s_r = jnp.concatenate([-sin_r, sin_r], axis=-1)
    c_d = jnp.tile(jnp.concatenate([cos_d, cos_d], axis=-1), (1, 2))
    s_lo = jnp.tile(jnp.concatenate([-sin_d, zeros], axis=-1), (1, 2))
    s_hi = jnp.tile(jnp.concatenate([zeros, sin_d], axis=-1), (1, 2))
    return c_r, s_r, c_d, s_lo, s_hi


@jax.jit
def kernel(x, c, ctx, c_ctx, w_mod, b_mod, g_pre, g_post, w_in, ret_decay_logit, mlp_w_s,
           mlp_b_s, diff_lambda_q, diff_lambda_k, w_branch_out, w_out):
    n_b, n_lat, _ = x.shape
    n_ctx = ctx.shape[1]
    tables = jnp.stack(_rope_tables(n_lat))

    cond_rows = 16
    ctx_row = n_b
    cc = jnp.zeros((cond_rows, D_MODEL), F32).at[:n_b].set(c).at[ctx_row].set(c_ctx)
    mod4 = _modulation(cc, w_mod, b_mod).reshape(DEPTH, cond_rows, 1, 3 * D_MODEL)
    lat_row = lambda b: b
    ctx_row_of = lambda b: ctx_row

    log_gamma = -jax.nn.softplus(-ret_decay_logit.astype(F32))
    lam_inits = [0.8 - 0.6 * math.exp(-0.3 * l) for l in range(DEPTH)]
    lam = (jnp.exp(jnp.sum(diff_lambda_q[:, 0] * diff_lambda_k[:, 0], axis=-1))
           - jnp.exp(jnp.sum(diff_lambda_q[:, 1] * diff_lambda_k[:, 1], axis=-1))
           + jnp.asarray(lam_inits, F32)).astype(F32)

    w_sec = w_in
    ws = mlp_w_s.astype(BF16)
    bs = jnp.broadcast_to(mlp_b_s[..., None], (DEPTH, HEADS, CHUNK, LANES)).astype(F32)
    wbo = w_branch_out.astype(BF16)
    wo = w_out.astype(BF16)
    g_pre3 = g_pre.reshape(DEPTH, 1, D_MODEL)
    g_post3 = g_post.reshape(DEPTH, 1, D_MODEL)

    for l in range(DEPTH):
        last = l == DEPTH - 1
        post_scale = 1.0 - lam_inits[l]

        ctx_flat = ctx.reshape(1, n_b * n_ctx, D_MODEL)
        if last:
            zc = _projection(ctx_flat, l, g_pre3, mod4, ctx_row_of, w_sec, lambda j: j,
                             KV_KINDS, None)
            sec_c = KV_SEC
        else:
            zc = _projection(ctx_flat, l, g_pre3, mod4, ctx_row_of, w_sec, _full_src,
                             FULL_KINDS, None)
            sec_c = FULL_SEC
            co_ret = _retention_ctx(log_gamma, l, zc, sec_c, n_b, n_ctx, rc=RET_CHUNK,
                                    heads=HEADS)
            co_diff = _diff_attention_ctx(lam, l, zc, sec_c, n_b, n_ctx, post_scale=post_scale)
            ctx_next = _merge(ctx_flat, l, mod4, ctx_row_of, g_post3,
                              co_ret.reshape(1, n_b * n_ctx, BRANCH_W),
                              co_diff.reshape(1, n_b * n_ctx, BRANCH_W), zc, sec_c,
                              ws, bs, wbo, wo, tm=MERGE_TM)
            ctx_next = ctx_next.reshape(n_b, n_ctx, D_MODEL)

        z = _projection(x, l, g_pre3, mod4, lat_row, w_sec, _full_src, FULL_KINDS, tables)
        o_ret, o_diff = _latent_mixers(lam, log_gamma, l, z, FULL_SEC, n_b, n_lat, zc, sec_c,
                                       n_ctx, ts=DIFF_TS, rc=RET_CHUNK, post_scale=post_scale)
        x = _merge(x, l, mod4, lat_row, g_post3, o_ret, o_diff, z, FULL_SEC, ws, bs, wbo, wo,
                   tm=MERGE_TM)
        if not last:
            ctx = ctx_next
    return x
```

```python
import functools
import math

import jax
import jax.numpy as jnp
from jax import lax
from jax.experimental import pallas as pl
from jax.experimental.pallas import tpu as pltpu

F32 = jnp.float32
BF16 = jnp.bfloat16

D_MODEL = 1024
DEPTH = 2
GRID_W = 64
N_BRANCH = 3
BRANCH_W = D_MODEL // 2
HEADS = 4
HEAD_W = BRANCH_W // HEADS
DIFF_DQK = HEAD_W // 2
CHUNK = 128
ROPE_BASE = 10000.0
EPS = 1e-6
RET_SCALE = HEAD_W ** -0.5
DIFF_Q_SCALE = (DIFF_DQK ** -0.5) * math.log2(math.e)

LANES = 128
SEC = BRANCH_W
MERGE_COLS = N_BRANCH * D_MODEL
KV_COLS = 4 * BRANCH_W
IN_COLS = KV_COLS + 7 * BRANCH_W + MERGE_COLS
VMEM_LIMIT = 56 * 1024 * 1024

PROJ_SUB = 256
PROJ_OUT_SLOTS = 2
DIFF_TS = 512
RET_CHUNK = 256
MERGE_TM = 512
MERGE_OUT_SUB = 512

FULL_KINDS = ("sigmoid",) * 6 + ("rk", "plain", "dk", "plain", "rq", "silu", "dq", "silu",
                                 "plain", "ln", "silu")
FULL_SEC = dict(rk=6, rv=7, dk=8, dv=9, rq=10, rg=11, dq=12, dg=13, mu=14, mv=15, mg=16)
KV_KINDS = ("rk", "plain", "dk", "plain")
KV_SEC = dict(rk=0, rv=1, dk=2, dv=3)
_N_MERGE_SEC = MERGE_COLS // SEC
_N_REST_SEC = (KV_COLS + 7 * BRANCH_W) // SEC


def _full_src(j):
    return jnp.where(j < _N_MERGE_SEC, j + _N_REST_SEC, j - _N_MERGE_SEC)


def _dot(a, b):
    return jnp.dot(a, b, preferred_element_type=F32)


def _dot_nt(a, b):
    return lax.dot_general(a, b, (((1,), (1,)), ((), ())), preferred_element_type=F32)


def _rms(v):
    return v * lax.rsqrt(jnp.mean(v * v, axis=-1, keepdims=True) + EPS)


def _mod_spec(l, row_of, k):
    return pl.BlockSpec((None, None, 1, D_MODEL), lambda *idx: (l, row_of(idx[0]), 0, k))


def _layer_spec(l, shape, **kwargs):
    zeros = (0,) * len(shape)
    return pl.BlockSpec((None,) + tuple(shape), lambda *idx: (l,) + zeros, **kwargs)


def _z_spec(sec_idx, rows, width, flat, col_of=lambda *idx: 0, row_of=None):
    if row_of is None:
        row_of = (lambda *idx: idx[0]) if flat else (lambda *idx: 0)
    group_of = (lambda *idx: 0) if flat else (lambda *idx: idx[0])
    return pl.BlockSpec((1, None, rows, width),
                        lambda *idx: (group_of(*idx), sec_idx, row_of(*idx), col_of(*idx)))


def _split_bf16(a):
    hi = a.astype(BF16)
    return hi, (a - hi.astype(F32)).astype(BF16)


def _mod_kernel(c_ref, w_ref, b_ref, o_ref):
    cv = c_ref[...]
    c_hi, c_lo = _split_bf16(cv * jax.nn.sigmoid(cv))
    w_hi, w_lo = _split_bf16(w_ref[0])
    o_ref[0] = _dot(c_hi, w_hi) + _dot(c_hi, w_lo) + _dot(c_lo, w_hi) + b_ref[0]


def _modulation(cc, w_mod, b_mod):
    rows = cc.shape[0]
    return pl.pallas_call(
        _mod_kernel,
        grid=(DEPTH, 3),
        in_specs=[
            pl.BlockSpec((rows, D_MODEL), lambda l, j: (0, 0)),
            pl.BlockSpec((1, D_MODEL, D_MODEL), lambda l, j: (l, 0, j)),
            pl.BlockSpec((1, 1, D_MODEL), lambda l, j: (l, 0, j)),
        ],
        out_specs=pl.BlockSpec((1, rows, D_MODEL), lambda l, j: (l, 0, j)),
        out_shape=jax.ShapeDtypeStruct((DEPTH, rows, 3 * D_MODEL), F32),
        compiler_params=pltpu.CompilerParams(vmem_limit_bytes=VMEM_LIMIT),
        name="modulation",
    )(cc, w_mod, b_mod.reshape(DEPTH, 1, 3 * D_MODEL))


def _rope_ret(a, c, s):
    outs = []
    for h in range(HEADS):
        ah = a[:, h * LANES:(h + 1) * LANES]
        outs.append(ah * c + pltpu.roll(ah, LANES // 2, 1) * s)
    return jnp.concatenate(outs, axis=1)


def _rope_diff(a, c, s_lo, s_hi):
    outs = []
    for h in range(HEADS):
        ah = a[:, h * LANES:(h + 1) * LANES]
        outs.append(ah * c + pltpu.roll(ah, LANES - DIFF_DQK // 2, 1) * s_lo
                    + pltpu.roll(ah, DIFF_DQK // 2, 1) * s_hi)
    return jnp.concatenate(outs, axis=1)


def _proj_kernel(*refs, layer, kinds, src_of, n_src, rope, n_rows, sub):
    if rope:
        (x_ref, g_ref, shift_ref, scale_ref, w_hbm, tab_hbm, z_hbm,
         h_ref, stage, w_vmem, w_stage, sems_out, sems_w, tab_vmem, sem_tab) = refs
        cr_ref, sr_ref, cd_ref, sdl_ref, sdh_ref = [tab_vmem.at[i] for i in range(5)]
    else:
        (x_ref, g_ref, shift_ref, scale_ref, w_hbm, z_hbm,
         h_ref, stage, w_vmem, w_stage, sems_out, sems_w) = refs
    b = pl.program_id(0)
    first = b == 0
    n_sub = n_rows // sub
    n_sec = len(kinds)
    n_slots = stage.shape[0]

    n_wslots = w_stage.shape[0]

    def weight_copy(j):
        start = src_of(j) * SEC
        cols = pl.ds(start if isinstance(start, int) else pl.multiple_of(start, SEC), SEC)
        slot = j % n_wslots
        return pltpu.make_async_copy(w_hbm.at[layer, :, cols], w_stage.at[slot], sems_w.at[slot])

    def load_weights(j):
        weight_copy(j).wait()
        src = w_stage.at[j % n_wslots]
        dst = w_vmem.at[src_of(j)]
        for r in range(D_MODEL // LANES):
            rows = slice(r * LANES, (r + 1) * LANES)
            dst[rows, :] = src[rows, :].astype(BF16)

        @pl.when(j + n_wslots < n_sec)
        def _():
            weight_copy(j + n_wslots).start()

    def table_copy():
        return pltpu.make_async_copy(tab_hbm, tab_vmem, sem_tab.at[0])

    @pl.when(first)
    def _():
        for j in range(min(n_wslots, n_sec)):
            weight_copy(j).start()
        if rope:
            table_copy().start()

    gs = g_ref[...] * (1.0 + scale_ref[...])
    sh = shift_ref[...]
    for r in range(n_sub):
        xs = x_ref[0, r * sub:(r + 1) * sub, :]
        h_ref[r * sub:(r + 1) * sub, :] = (_rms(xs) * gs + sh).astype(BF16)

    if rope:
        pl.when(first)(lambda: table_copy().wait())

    def epi_rk(a, rows):
        if rope:
            a = _rope_ret(a, cr_ref[rows, :], sr_ref[rows, :])
        return a * RET_SCALE

    def epi_rq(a, rows):
        return _rope_ret(a, cr_ref[rows, :], sr_ref[rows, :]) if rope else a

    def epi_dk(a, rows):
        return _rope_diff(a, cd_ref[rows, :], sdl_ref[rows, :], sdh_ref[rows, :]) if rope else a

    def epi_dq(a, rows):
        return epi_dk(a, rows) * DIFF_Q_SCALE

    def epi_ln(a, rows):
        mu = jnp.mean(a, axis=-1, keepdims=True)
        d = a - mu
        return d * lax.rsqrt(jnp.mean(d * d, axis=-1, keepdims=True) + EPS)

    epilogues = dict(
        plain=lambda a, rows: a,
        sigmoid=lambda a, rows: jax.nn.sigmoid(a),
        silu=lambda a, rows: a * jax.nn.sigmoid(a),
        rk=epi_rk, rq=epi_rq, dk=epi_dk, dq=epi_dq, ln=epi_ln)

    def writeback(j):
        slot = j % n_slots
        return pltpu.make_async_copy(stage.at[slot], z_hbm.at[b, j], sems_out.at[slot])

    def run(epilogue, j):
        w_sec = w_vmem.at[src_of(j)]
        out = stage.at[j % n_slots]
        for r in range(n_sub):
            rows = slice(r * sub, (r + 1) * sub)
            out[rows, :] = epilogue(_dot(h_ref[rows, :], w_sec[...]), rows).astype(BF16)

    def section(j, carry):
        pl.when(first)(functools.partial(load_weights, j))

        @pl.when(j >= n_slots)
        def _():
            writeback(j - n_slots).wait()

        for kind in sorted(set(kinds)):
            idx = [i for i, k in enumerate(kinds) if k == kind]
            cond = j == idx[0]
            for i in idx[1:]:
                cond = cond | (j == i)
            pl.when(cond)(functools.partial(run, epilogues[kind], j))
        writeback(j).start()
        return carry

    lax.fori_loop(0, n_sec, section, 0)
    for j in range(max(0, n_sec - n_slots), n_sec):
        writeback(j).wait()


def _projection(xs, l, g_pre, mod4, row_of, w_all, src_of, kinds, tables):
    n_g, n_rows, _ = xs.shape
    n_sec = len(kinds)
    rope = tables is not None
    n_src = w_all.shape[2] // SEC if src_of is _full_src else n_sec
    in_specs = [
        pl.BlockSpec((1, n_rows, D_MODEL), lambda b: (b, 0, 0)),
        _layer_spec(l, (1, D_MODEL)),
        _mod_spec(l, row_of, 0),
        _mod_spec(l, row_of, 1),
        pl.BlockSpec(memory_space=pl.ANY),
    ]
    args = [xs, g_pre, mod4, mod4, w_all]
    scratch = [pltpu.VMEM((n_rows, D_MODEL), BF16),
               pltpu.VMEM((PROJ_OUT_SLOTS, n_rows, SEC), BF16),
               pltpu.VMEM((n_src, D_MODEL, SEC), BF16),
               pltpu.VMEM((2, D_MODEL, SEC), F32),
               pltpu.SemaphoreType.DMA((PROJ_OUT_SLOTS,)),
               pltpu.SemaphoreType.DMA((2,))]
    if rope:
        in_specs.append(pl.BlockSpec(memory_space=pl.ANY))
        args.append(tables)
        scratch += [pltpu.VMEM(tables.shape, F32), pltpu.SemaphoreType.DMA((1,))]
    return pl.pallas_call(
        functools.partial(_proj_kernel, layer=l, kinds=kinds, src_of=src_of, n_src=n_src,
                          rope=rope, n_rows=n_rows, sub=PROJ_SUB),
        grid=(n_g,),
        in_specs=in_specs,
        out_specs=pl.BlockSpec(memory_space=pl.ANY),
        out_shape=jax.ShapeDtypeStruct((n_g, n_sec, n_rows, SEC), BF16),
        scratch_shapes=scratch,
        compiler_params=pltpu.CompilerParams(vmem_limit_bytes=VMEM_LIMIT),
        name="projection_rope" if rope else "projection",
    )(*args)


def _retention_tasks(lgf, lgb, cs, q_ref, g_ref, k_ref, v_ref, kc_ref, vc_ref, o_ref, inc_ref,
                     st_ref, *, n, n_ctx, rc):
    nc = n // rc
    pos = lax.broadcasted_iota(jnp.int32, (rc, LANES), 0).astype(F32)
    kdf = jnp.exp(lgf * (rc - 1.0 - pos)).astype(BF16)
    kdb = jnp.exp(lgb * pos).astype(BF16)
    cdf = jnp.exp(jnp.full((1, LANES), lgf * rc, F32))
    cdb = jnp.exp(jnp.full((1, LANES), lgb * rc, F32))
    qdf = jnp.exp(lgf * (pos + 1.0)).astype(BF16)
    qdb = jnp.exp(lgb * (rc - pos)).astype(BF16)
    ii = lax.broadcasted_iota(jnp.int32, (rc, rc), 0)
    jj = lax.broadcasted_iota(jnp.int32, (rc, rc), 1)
    dist = (ii - jj).astype(F32)
    dmat = (jnp.where(dist >= 0, jnp.exp(lgf * jnp.maximum(dist, 0.0)), 0.0)
            + jnp.where(dist <= 0, jnp.exp(lgb * jnp.maximum(-dist, 0.0)), 0.0))
    state = {}

    def increment(kr, vr, nn):
        rows = slice(nn * rc, (nn + 1) * rc)
        kk = kr[0, rows, cs]
        kcat = jnp.concatenate([kk * kdf, kk * kdb], axis=1)
        vt = vr[0, rows, cs].astype(F32).T.astype(BF16)
        return _dot(vt, kcat)

    def init_task():
        sf = jnp.zeros((HEAD_W, HEAD_W), F32)
        sb = jnp.zeros((HEAD_W, HEAD_W), F32)
        if n_ctx:
            incs = [increment(kc_ref, vc_ref, nn) for nn in range(n_ctx // rc)]
            for inc in incs:
                sf = cdf * sf + inc[:, :HEAD_W]
            for inc in reversed(incs):
                sb = cdb * sb + inc[:, HEAD_W:]
        state["sf"], state["sb"] = sf, sb

    def inc_task(nn):
        inc_ref[nn] = increment(k_ref, v_ref, nn)

    def scan_task():
        sf, sb = state["sf"], state["sb"]
        for nn in range(nc):
            st_ref[nn, :, 0:HEAD_W] = sf.astype(BF16)
            sf = cdf * sf + inc_ref[nn, :, 0:HEAD_W]
        for nn in reversed(range(nc)):
            st_ref[nn, :, HEAD_W:] = sb.astype(BF16)
            sb = cdb * sb + inc_ref[nn, :, HEAD_W:]

    def out_task(nn):
        rows = slice(nn * rc, (nn + 1) * rc)
        q = q_ref[0, rows, cs]
        s = _dot_nt(q, k_ref[0, rows, cs]) * dmat
        inner = _dot(s.astype(BF16), v_ref[0, rows, cs])
        qcat = jnp.concatenate([q * qdf, q * qdb], axis=1)
        out = inner + _dot_nt(qcat, st_ref[nn])
        o_ref[0, rows, cs] = _rms(out).astype(BF16) * g_ref[0, rows, cs]

    inc_tasks = [init_task] + [functools.partial(inc_task, nn) for nn in range(nc)]
    return inc_tasks, scan_task, [functools.partial(out_task, nn) for nn in range(nc)]


def _ret_kernel(lg_ref, q_ref, g_ref, k_ref, v_ref, o_ref, inc_ref, st_ref, *, layer, n, rc,
                heads):
    tasks = []
    for hh in range(heads):
        head = pl.program_id(1) * heads + hh
        tasks.append(_retention_tasks(
            lg_ref[layer, 0, head], lg_ref[layer, 1, head], slice(hh * LANES, (hh + 1) * LANES),
            q_ref, g_ref, k_ref, v_ref, None, None, o_ref, inc_ref.at[hh], st_ref.at[hh],
            n=n, n_ctx=0, rc=rc))
    for i in range(len(tasks[0][0])):
        for inc_tasks, _, _ in tasks:
            inc_tasks[i]()
    for _, scan_task, _ in tasks:
        scan_task()
    for i in range(len(tasks[0][2])):
        for _, _, out_tasks in tasks:
            out_tasks[i]()


def _retention_ctx(log_gamma, layer, zc, sec, n_b, n, *, rc, heads):
    width = heads * LANES
    head_block = lambda b, h: h
    return pl.pallas_call(
        functools.partial(_ret_kernel, layer=layer, n=n, rc=rc, heads=heads),
        grid=(n_b, HEADS // heads),
        in_specs=[pl.BlockSpec(memory_space=pltpu.SMEM)]
        + [_z_spec(sec[name], n, width, True, col_of=head_block)
           for name in ("rq", "rg", "rk", "rv")],
        out_specs=pl.BlockSpec((1, n, width), lambda b, h: (b, 0, h)),
        out_shape=jax.ShapeDtypeStruct((n_b, n, BRANCH_W), BF16),
        scratch_shapes=[pltpu.VMEM((heads, n // rc, HEAD_W, 2 * HEAD_W), F32),
                        pltpu.VMEM((heads, n // rc, HEAD_W, 2 * HEAD_W), BF16)],
        compiler_params=pltpu.CompilerParams(vmem_limit_bytes=VMEM_LIMIT),
        name="retention_ctx",
    )(log_gamma, zc, zc, zc, zc)


def _split_maps(q, lane):
    zero = jnp.zeros_like(q)
    return jnp.concatenate([jnp.where(lane < DIFF_DQK, q, zero),
                            jnp.where(lane >= DIFF_DQK, q, zero)], axis=0)


def _diff_ctx_kernel(lam_ref, q_ref, g_ref, k_ref, v_ref, o_ref, *, layer, tq, post_scale):
    lane = lax.broadcasted_iota(jnp.int32, (tq, LANES), 1)
    for h in range(HEADS):
        cs = slice(h * LANES, (h + 1) * LANES)
        s = _dot_nt(_split_maps(q_ref[0, :, cs], lane), k_ref[0, :, cs])
        p = jnp.exp2(s - jnp.max(s, axis=-1, keepdims=True))
        o = _dot(p.astype(BF16), v_ref[0, :, cs]) / jnp.sum(p, axis=-1, keepdims=True)
        d = o[:tq] - lam_ref[layer] * o[tq:]
        o_ref[0, :, cs] = (_rms(d) * post_scale * g_ref[0, :, cs].astype(F32)).astype(BF16)


def _diff_attention_ctx(lam, layer, zc, sec, n_b, n, *, post_scale):
    return pl.pallas_call(
        functools.partial(_diff_ctx_kernel, layer=layer, tq=n, post_scale=post_scale),
        grid=(n_b,),
        in_specs=[pl.BlockSpec(memory_space=pltpu.SMEM)]
        + [_z_spec(sec[name], n, SEC, True) for name in ("dq", "dg", "dk", "dv")],
        out_specs=pl.BlockSpec((1, n, BRANCH_W), lambda b: (b, 0, 0)),
        out_shape=jax.ShapeDtypeStruct((n_b, n, BRANCH_W), BF16),
        compiler_params=pltpu.CompilerParams(vmem_limit_bytes=VMEM_LIMIT),
        name="diff_attention_ctx",
    )(lam, zc, zc, zc, zc)


def _mixer_kernel(lam_ref, lg_ref, q_ref, g_ref, kc_ref, vc_ref, kl_ref, vl_ref,
                  rq_ref, rg_ref, rk_ref, rv_ref, rkc_ref, rvc_ref, od_ref, or_ref,
                  kall, vt, s0, s1, p0, p1, d0, d1, inc_ref, st_ref, *, layer, n, n_ctx, ts,
                  rc, post_scale):
    n_sub = n // ts
    nk = kall.shape[0]
    kall[0:n_ctx, :] = kc_ref[0]
    kall[n_ctx:, :] = kl_ref[0]
    for c in range(nk // LANES):
        src, base = (vc_ref, 0) if c * LANES < n_ctx else (vl_ref, n_ctx)
        rows = slice(c * LANES - base, (c + 1) * LANES - base)
        vt[0:LANES, c * LANES:(c + 1) * LANES] = src[0, rows, :].astype(F32).T.astype(BF16)
    vt[LANES:, :] = jnp.ones((vt.shape[0] - LANES, nk), BF16)
    lam = lam_ref[layer]
    lane = lax.broadcasted_iota(jnp.int32, (ts, LANES), 1)
    s_bufs, p_bufs, d_bufs = (s0, s1), (p0, p1), (d0, d1)

    def stage_a(t):
        rows = slice(t * ts, (t + 1) * ts)
        s = _dot_nt(kall[...], _split_maps(q_ref[0, rows, :], lane))
        s_bufs[t % 2][...] = s
        d_bufs[t % 2][...] = jnp.max(s, axis=0, keepdims=True)

    def stage_b(t):
        p_bufs[t % 2][...] = jnp.exp2(s_bufs[t % 2][...] - d_bufs[t % 2][...]).astype(BF16)

    def stage_c(t):
        rows = slice(t * ts, (t + 1) * ts)
        oe = _dot(vt[...], p_bufs[t % 2][...])
        ot = oe[:LANES] / oe[LANES:LANES + 1]
        dt = ot[:, :ts] - lam * ot[:, ts:]
        dt = dt * lax.rsqrt(jnp.mean(dt * dt, axis=0, keepdims=True) + EPS)
        od_ref[0, rows, :] = (dt.T * post_scale * g_ref[0, rows, :].astype(F32)).astype(BF16)

    head = pl.program_id(1)
    inc_tasks, scan_task, out_tasks = _retention_tasks(
        lg_ref[layer, 0, head], lg_ref[layer, 1, head], slice(0, LANES),
        rq_ref, rg_ref, rk_ref, rv_ref, rkc_ref, rvc_ref, or_ref, inc_ref, st_ref,
        n=n, n_ctx=n_ctx, rc=rc)
    n_steps = n_sub + 2
    extra = [[] for _ in range(n_steps)]
    half = (len(inc_tasks) + 1) // 2
    extra[0] += inc_tasks[:half]
    extra[1] += inc_tasks[half:]
    extra[2].append(scan_task)
    for i, task in enumerate(out_tasks):
        extra[2 + (i * (n_steps - 2)) // len(out_tasks)].append(task)

    for t in range(n_steps):
        if t >= 2:
            stage_c(t - 2)
        if 1 <= t <= n_sub:
            stage_b(t - 1)
        if t < n_sub:
            stage_a(t)
        for task in extra[t]:
            task()


def _latent_mixers(lam, log_gamma, layer, z, sec, n_b, n, zc, sec_c, n_ctx, *, ts, rc,
                   post_scale):
    nk = n_ctx + n
    head_block = lambda b, h: h
    lat = lambda name: _z_spec(sec[name], n, LANES, False, col_of=head_block)
    cx = lambda name: _z_spec(sec_c[name], n_ctx, LANES, True, col_of=head_block)
    out_spec = pl.BlockSpec((1, n, LANES), lambda b, h: (b, 0, h))
    out_shape = jax.ShapeDtypeStruct((n_b, n, BRANCH_W), BF16)
    o_diff, o_ret = pl.pallas_call(
        functools.partial(_mixer_kernel, layer=layer, n=n, n_ctx=n_ctx, ts=ts, rc=rc,
                          post_scale=post_scale),
        grid=(n_b, HEADS),
        in_specs=[pl.BlockSpec(memory_space=pltpu.SMEM), pl.BlockSpec(memory_space=pltpu.SMEM),
                  lat("dq"), lat("dg"), cx("dk"), cx("dv"), lat("dk"), lat("dv"),
                  lat("rq"), lat("rg"), lat("rk"), lat("rv"), cx("rk"), cx("rv")],
        out_specs=[out_spec, out_spec],
        out_shape=[out_shape, out_shape],
        scratch_shapes=[pltpu.VMEM((nk, LANES), BF16), pltpu.VMEM((LANES + 16, nk), BF16),
                        pltpu.VMEM((nk, 2 * ts), F32), pltpu.VMEM((nk, 2 * ts), F32),
                        pltpu.VMEM((nk, 2 * ts), BF16), pltpu.VMEM((nk, 2 * ts), BF16),
                        pltpu.VMEM((1, 2 * ts), F32), pltpu.VMEM((1, 2 * ts), F32),
                        pltpu.VMEM((n // rc, HEAD_W, 2 * HEAD_W), F32),
                        pltpu.VMEM((n // rc, HEAD_W, 2 * HEAD_W), BF16)],
        compiler_params=pltpu.CompilerParams(vmem_limit_bytes=VMEM_LIMIT),
        name="latent_mixers",
    )(lam, log_gamma, z, z, zc, zc, z, z, z, z, z, z, zc, zc)
    return o_ret, o_diff


def _merge_kernel(x_ref, gate_ref, gpost_ref, oret_ref, odiff_ref, u_ref, vn_ref, mg_ref,
                  gr_ref, gm_ref, gd_ref, ws_ref, bs_ref, wbo_ref, wo_ref, out_ref, *, tm, out_sub):
    rows = []
    for c in range(tm // CHUNK):
        cols = []
        for g in range(HEADS):
            blk = vn_ref[0, c * CHUNK:(c + 1) * CHUNK, g * LANES:(g + 1) * LANES]
            cols.append(_dot(ws_ref[g], blk) + bs_ref[g])
        rows.append(jnp.concatenate(cols, axis=1))
    sp = jnp.concatenate(rows, axis=0)
    o_mlp = (u_ref[0].astype(F32) * sp * mg_ref[0].astype(F32)).astype(BF16)

    def gated(gate2_ref, o, w):
        gate = jnp.concatenate([gate2_ref[0, 0], gate2_ref[0, 1]], axis=1).astype(F32)
        return gate * _dot(o, w)

    t = (gated(gr_ref, oret_ref[0], wbo_ref[0]) + gated(gm_ref, o_mlp, wbo_ref[1])
         + gated(gd_ref, odiff_ref[0], wbo_ref[2]))
    tb = t.astype(BF16)
    for r in range(tm // out_sub):
        rs = slice(r * out_sub, (r + 1) * out_sub)
        y = _dot(tb[rs, :], wo_ref[...])
        out_ref[0, rs, :] = x_ref[0, rs, :] + gate_ref[...] * (_rms(y) * gpost_ref[...])


def _merge(xs, l, mod4, row_of, g_post, o_ret, o_diff, z, sec, ws, bs, wbo, wo, *, tm):
    n_b, n, _ = xs.shape
    row_block = lambda b, i: i

    def zsec(name):
        return _z_spec(sec[name], tm, SEC, False, row_of=row_block)

    def zgate(k):
        return pl.BlockSpec((1, 2, tm, SEC), lambda b, i: (b, k, i, 0))

    in_specs = [
        pl.BlockSpec((1, tm, D_MODEL), lambda b, i: (b, i, 0)),
        _mod_spec(l, row_of, 2),
        _layer_spec(l, (1, D_MODEL)),
        pl.BlockSpec((1, tm, BRANCH_W), lambda b, i: (b, i, 0)),
        pl.BlockSpec((1, tm, BRANCH_W), lambda b, i: (b, i, 0)),
        zsec("mu"), zsec("mv"), zsec("mg"), zgate(0), zgate(1), zgate(2),
        _layer_spec(l, (HEADS, CHUNK, CHUNK)),
        _layer_spec(l, (HEADS, CHUNK, LANES)),
        _layer_spec(l, (N_BRANCH, BRANCH_W, D_MODEL)),
        _layer_spec(l, (D_MODEL, D_MODEL)),
    ]
    return pl.pallas_call(
        functools.partial(_merge_kernel, tm=tm, out_sub=MERGE_OUT_SUB),
        grid=(n_b, n // tm),
        in_specs=in_specs,
        out_specs=pl.BlockSpec((1, tm, D_MODEL), lambda b, i: (b, i, 0)),
        out_shape=jax.ShapeDtypeStruct(xs.shape, F32),
        compiler_params=pltpu.CompilerParams(vmem_limit_bytes=VMEM_LIMIT),
        name="merge",
    )(xs, mod4, g_post, o_ret, o_diff, z, z, z, z, z, z, ws, bs, wbo, wo)


def _rope_tables(n_lat):
    rows = n_lat // GRID_W
    row_pos = jnp.repeat(jnp.arange(rows, dtype=F32), GRID_W)
    col_pos = jnp.tile(jnp.arange(GRID_W, dtype=F32), rows)

    def angles(head_dim):
        n_freq = head_dim // 4
        inv = ROPE_BASE ** (-jnp.arange(n_freq, dtype=F32) / n_freq)
        ang = jnp.concatenate([row_pos[:, None] * inv, col_pos[:, None] * inv], axis=-1)
        return jnp.cos(ang), jnp.sin(ang)

    cos_r, sin_r = angles(HEAD_W)
    cos_d, sin_d = angles(DIFF_DQK)
    zeros = jnp.zeros_like(sin_d)
    c_r = jnp.concatenate([cos_r, cos_r], axis=-1)
    s_r = jnp.concatenate([-sin_r, sin_r], axis=-1)
    c_d = jnp.tile(jnp.concatenate([cos_d, cos_d], axis=-1), (1, 2))
    s_lo = jnp.tile(jnp.concatenate([-sin_d, zeros], axis=-1), (1, 2))
    s_hi = jnp.tile(jnp.concatenate([zeros, sin_d], axis=-1), (1, 2))
    return c_r, s_r, c_d, s_lo, s_hi


@jax.jit
def kernel(x, c, ctx, c_ctx, w_mod, b_mod, g_pre, g_post, w_in, ret_decay_logit, mlp_w_s,
           mlp_b_s, diff_lambda_q, diff_lambda_k, w_branch_out, w_out):
    n_b, n_lat, _ = x.shape
    n_ctx = ctx.shape[1]
    tables = jnp.stack(_rope_tables(n_lat))

    cond_rows = 16
    ctx_row = n_b
    cc = jnp.zeros((cond_rows, D_MODEL), F32).at[:n_b].set(c).at[ctx_row].set(c_ctx)
    mod4 = _modulation(cc, w_mod, b_mod).reshape(DEPTH, cond_rows, 1, 3 * D_MODEL)
    lat_row = lambda b: b
    ctx_row_of = lambda b: ctx_row

    log_gamma = -jax.nn.softplus(-ret_decay_logit.astype(F32))
    lam_inits = [0.8 - 0.6 * math.exp(-0.3 * l) for l in range(DEPTH)]
    lam = (jnp.exp(jnp.sum(diff_lambda_q[:, 0] * diff_lambda_k[:, 0], axis=-1))
           - jnp.exp(jnp.sum(diff_lambda_q[:, 1] * diff_lambda_k[:, 1], axis=-1))
           + jnp.asarray(lam_inits, F32)).astype(F32)

    w_sec = w_in
    ws = mlp_w_s.astype(BF16)
    bs = jnp.broadcast_to(mlp_b_s[..., None], (DEPTH, HEADS, CHUNK, LANES)).astype(F32)
    wbo = w_branch_out.astype(BF16)
    wo = w_out.astype(BF16)
    g_pre3 = g_pre.reshape(DEPTH, 1, D_MODEL)
    g_post3 = g_post.reshape(DEPTH, 1, D_MODEL)

    for l in range(DEPTH):
        last = l == DEPTH - 1
        post_scale = 1.0 - lam_inits[l]

        ctx_flat = ctx.reshape(1, n_b * n_ctx, D_MODEL)
        if last:
            zc = _projection(ctx_flat, l, g_pre3, mod4, ctx_row_of, w_sec, lambda j: j,
                             KV_KINDS, None)
            sec_c = KV_SEC
        else:
            zc = _projection(ctx_flat, l, g_pre3, mod4, ctx_row_of, w_sec, _full_src,
                             FULL_KINDS, None)
            sec_c = FULL_SEC
            co_ret = _retention_ctx(log_gamma, l, zc, sec_c, n_b, n_ctx, rc=RET_CHUNK,
                                    heads=HEADS)
            co_diff = _diff_attention_ctx(lam, l, zc, sec_c, n_b, n_ctx, post_scale=post_scale)
            ctx_next = _merge(ctx_flat, l, mod4, ctx_row_of, g_post3,
                              co_ret.reshape(1, n_b * n_ctx, BRANCH_W),
                              co_diff.reshape(1, n_b * n_ctx, BRANCH_W), zc, sec_c,
                              ws, bs, wbo, wo, tm=MERGE_TM)
            ctx_next = ctx_next.reshape(n_b, n_ctx, D_MODEL)

        z = _projection(x, l, g_pre3, mod4, lat_row, w_sec, _full_src, FULL_KINDS, tables)
        o_ret, o_diff = _latent_mixers(lam, log_gamma, l, z, FULL_SEC, n_b, n_lat, zc, sec_c,
                                       n_ctx, ts=DIFF_TS, rc=RET_CHUNK, post_scale=post_scale)
        x = _merge(x, l, mod4, lat_row, g_post3, o_ret, o_diff, z, FULL_SEC, ws, bs, wbo, wo,
                   tm=MERGE_TM)
        if not last:
            ctx = ctx_next
    return x
```

```python
import functools
import math

import jax
import jax.numpy as jnp
from jax import lax
from jax.experimental import pallas as pl
from jax.experimental.pallas import tpu as pltpu

F32 = jnp.float32
BF16 = jnp.bfloat16

D_MODEL = 1024
DEPTH = 2
GRID_W = 64
N_BRANCH = 3
BRANCH_W = D_MODEL // 2
HEADS = 4
HEAD_W = BRANCH_W // HEADS
DIFF_DQK = HEAD_W // 2
CHUNK = 128
ROPE_BASE = 10000.0
EPS = 1e-6
RET_SCALE = HEAD_W ** -0.5
DIFF_Q_SCALE = (DIFF_DQK ** -0.5) * math.log2(math.e)

LANES = 128
SEC = BRANCH_W
MERGE_COLS = N_BRANCH * D_MODEL
KV_COLS = 4 * BRANCH_W
IN_COLS = KV_COLS + 7 * BRANCH_W + MERGE_COLS
VMEM_LIMIT = 56 * 1024 * 1024

PROJ_SUB = 256
PROJ_OUT_SLOTS = 2
DIFF_TS = 256
RET_CHUNK = 256
MERGE_TM = 512
MERGE_OUT_SUB = 512

FULL_KINDS = ("sigmoid",) * 6 + ("rk", "plain", "dk", "plain", "rq", "silu", "dq", "silu",
                                 "plain", "ln", "silu")
FULL_SEC = dict(rk=6, rv=7, dk=8, dv=9, rq=10, rg=11, dq=12, dg=13, mu=14, mv=15, mg=16)
KV_KINDS = ("rk", "plain", "dk", "plain")
KV_SEC = dict(rk=0, rv=1, dk=2, dv=3)
_N_MERGE_SEC = MERGE_COLS // SEC
_N_REST_SEC = (KV_COLS + 7 * BRANCH_W) // SEC


def _full_src(j):
    return jnp.where(j < _N_MERGE_SEC, j + _N_REST_SEC, j - _N_MERGE_SEC)


def _dot(a, b):
    return jnp.dot(a, b, preferred_element_type=F32)


def _dot_nt(a, b):
    return lax.dot_general(a, b, (((1,), (1,)), ((), ())), preferred_element_type=F32)


def _rms(v):
    return v * lax.rsqrt(jnp.mean(v * v, axis=-1, keepdims=True) + EPS)


def _mod_spec(l, row_of, k):
    return pl.BlockSpec((None, None, 1, D_MODEL), lambda *idx: (l, row_of(idx[0]), 0, k))


def _layer_spec(l, shape, **kwargs):
    zeros = (0,) * len(shape)
    return pl.BlockSpec((None,) + tuple(shape), lambda *idx: (l,) + zeros, **kwargs)


def _z_spec(sec_idx, rows, width, flat, col_of=lambda *idx: 0, row_of=None):
    if row_of is None:
        row_of = (lambda *idx: idx[0]) if flat else (lambda *idx: 0)
    group_of = (lambda *idx: 0) if flat else (lambda *idx: idx[0])
    return pl.BlockSpec((1, None, rows, width),
                        lambda *idx: (group_of(*idx), sec_idx, row_of(*idx), col_of(*idx)))


def _mod_kernel(c_ref, w_ref, b_ref, o_ref):
    cv = c_ref[...]
    sc = (cv * jax.nn.sigmoid(cv)).astype(BF16)
    o_ref[0] = _dot(sc, w_ref[0].astype(BF16)) + b_ref[0]


def _modulation(cc, w_mod, b_mod):
    rows = cc.shape[0]
    return pl.pallas_call(
        _mod_kernel,
        grid=(DEPTH, 3),
        in_specs=[
            pl.BlockSpec((rows, D_MODEL), lambda l, j: (0, 0)),
            pl.BlockSpec((1, D_MODEL, D_MODEL), lambda l, j: (l, 0, j)),
            pl.BlockSpec((1, 1, D_MODEL), lambda l, j: (l, 0, j)),
        ],
        out_specs=pl.BlockSpec((1, rows, D_MODEL), lambda l, j: (l, 0, j)),
        out_shape=jax.ShapeDtypeStruct((DEPTH, rows, 3 * D_MODEL), F32),
        compiler_params=pltpu.CompilerParams(vmem_limit_bytes=VMEM_LIMIT),
        name="modulation",
    )(cc, w_mod, b_mod.reshape(DEPTH, 1, 3 * D_MODEL))


def _rope_ret(a, c, s):
    outs = []
    for h in range(HEADS):
        ah = a[:, h * LANES:(h + 1) * LANES]
        outs.append(ah * c + pltpu.roll(ah, LANES // 2, 1) * s)
    return jnp.concatenate(outs, axis=1)


def _rope_diff(a, c, s_lo, s_hi):
    outs = []
    for h in range(HEADS):
        ah = a[:, h * LANES:(h + 1) * LANES]
        outs.append(ah * c + pltpu.roll(ah, LANES - DIFF_DQK // 2, 1) * s_lo
                    + pltpu.roll(ah, DIFF_DQK // 2, 1) * s_hi)
    return jnp.concatenate(outs, axis=1)


def _proj_kernel(*refs, layer, kinds, src_of, n_src, rope, n_rows, sub):
    if rope:
        (x_ref, g_ref, shift_ref, scale_ref, w_hbm, tab_hbm, z_hbm,
         h_ref, stage, w_vmem, w_stage, sems_out, sems_w, tab_vmem, sem_tab) = refs
        cr_ref, sr_ref, cd_ref, sdl_ref, sdh_ref = [tab_vmem.at[i] for i in range(5)]
    else:
        (x_ref, g_ref, shift_ref, scale_ref, w_hbm, z_hbm,
         h_ref, stage, w_vmem, w_stage, sems_out, sems_w) = refs
    b = pl.program_id(0)
    first = b == 0
    n_sub = n_rows // sub
    n_sec = len(kinds)
    n_slots = stage.shape[0]

    n_wslots = w_stage.shape[0]

    def weight_copy(j):
        start = src_of(j) * SEC
        cols = pl.ds(start if isinstance(start, int) else pl.multiple_of(start, SEC), SEC)
        slot = j % n_wslots
        return pltpu.make_async_copy(w_hbm.at[layer, :, cols], w_stage.at[slot], sems_w.at[slot])

    def load_weights(j):
        weight_copy(j).wait()
        src = w_stage.at[j % n_wslots]
        dst = w_vmem.at[src_of(j)]
        for r in range(D_MODEL // LANES):
            rows = slice(r * LANES, (r + 1) * LANES)
            dst[rows, :] = src[rows, :].astype(BF16)

        @pl.when(j + n_wslots < n_sec)
        def _():
            weight_copy(j + n_wslots).start()

    def table_copy():
        return pltpu.make_async_copy(tab_hbm, tab_vmem, sem_tab.at[0])

    @pl.when(first)
    def _():
        for j in range(min(n_wslots, n_sec)):
            weight_copy(j).start()
        if rope:
            table_copy().start()

    gs = g_ref[...] * (1.0 + scale_ref[...])
    sh = shift_ref[...]
    for r in range(n_sub):
        xs = x_ref[0, r * sub:(r + 1) * sub, :]
        h_ref[r * sub:(r + 1) * sub, :] = (_rms(xs) * gs + sh).astype(BF16)

    if rope:
        pl.when(first)(lambda: table_copy().wait())

    def epi_rk(a, rows):
        if rope:
            a = _rope_ret(a, cr_ref[rows, :], sr_ref[rows, :])
        return a * RET_SCALE

    def epi_rq(a, rows):
        return _rope_ret(a, cr_ref[rows, :], sr_ref[rows, :]) if rope else a

    def epi_dk(a, rows):
        return _rope_diff(a, cd_ref[rows, :], sdl_ref[rows, :], sdh_ref[rows, :]) if rope else a

    def epi_dq(a, rows):
        return epi_dk(a, rows) * DIFF_Q_SCALE

    def epi_ln(a, rows):
        mu = jnp.mean(a, axis=-1, keepdims=True)
        d = a - mu
        return d * lax.rsqrt(jnp.mean(d * d, axis=-1, keepdims=True) + EPS)

    epilogues = dict(
        plain=lambda a, rows: a,
        sigmoid=lambda a, rows: jax.nn.sigmoid(a),
        silu=lambda a, rows: a * jax.nn.sigmoid(a),
        rk=epi_rk, rq=epi_rq, dk=epi_dk, dq=epi_dq, ln=epi_ln)

    def writeback(j):
        slot = j % n_slots
        return pltpu.make_async_copy(stage.at[slot], z_hbm.at[b, j], sems_out.at[slot])

    def run(epilogue, j):
        w_sec = w_vmem.at[src_of(j)]
        out = stage.at[j % n_slots]
        for r in range(n_sub):
            rows = slice(r * sub, (r + 1) * sub)
            out[rows, :] = epilogue(_dot(h_ref[rows, :], w_sec[...]), rows).astype(BF16)

    def section(j, carry):
        pl.when(first)(functools.partial(load_weights, j))

        @pl.when(j >= n_slots)
        def _():
            writeback(j - n_slots).wait()

        for kind in sorted(set(kinds)):
            idx = [i for i, k in enumerate(kinds) if k == kind]
            cond = j == idx[0]
            for i in idx[1:]:
                cond = cond | (j == i)
            pl.when(cond)(functools.partial(run, epilogues[kind], j))
        writeback(j).start()
        return carry

    lax.fori_loop(0, n_sec, section, 0)
    for j in range(max(0, n_sec - n_slots), n_sec):
        writeback(j).wait()


def _projection(xs, l, g_pre, mod4, row_of, w_all, src_of, kinds, tables):
    n_g, n_rows, _ = xs.shape
    n_sec = len(kinds)
    rope = tables is not None
    n_src = w_all.shape[2] // SEC if src_of is _full_src else n_sec
    in_specs = [
        pl.BlockSpec((1, n_rows, D_MODEL), lambda b: (b, 0, 0)),
        _layer_spec(l, (1, D_MODEL)),
        _mod_spec(l, row_of, 0),
        _mod_spec(l, row_of, 1),
        pl.BlockSpec(memory_space=pl.ANY),
    ]
    args = [xs, g_pre, mod4, mod4, w_all]
    scratch = [pltpu.VMEM((n_rows, D_MODEL), BF16),
               pltpu.VMEM((PROJ_OUT_SLOTS, n_rows, SEC), BF16),
               pltpu.VMEM((n_src, D_MODEL, SEC), BF16),
               pltpu.VMEM((2, D_MODEL, SEC), F32),
               pltpu.SemaphoreType.DMA((PROJ_OUT_SLOTS,)),
               pltpu.SemaphoreType.DMA((2,))]
    if rope:
        in_specs.append(pl.BlockSpec(memory_space=pl.ANY))
        args.append(tables)
        scratch += [pltpu.VMEM(tables.shape, F32), pltpu.SemaphoreType.DMA((1,))]
    return pl.pallas_call(
        functools.partial(_proj_kernel, layer=l, kinds=kinds, src_of=src_of, n_src=n_src,
                          rope=rope, n_rows=n_rows, sub=PROJ_SUB),
        grid=(n_g,),
        in_specs=in_specs,
        out_specs=pl.BlockSpec(memory_space=pl.ANY),
        out_shape=jax.ShapeDtypeStruct((n_g, n_sec, n_rows, SEC), BF16),
        scratch_shapes=scratch,
        compiler_params=pltpu.CompilerParams(vmem_limit_bytes=VMEM_LIMIT),
        name="projection_rope" if rope else "projection",
    )(*args)


def _retention_tasks(lgf, lgb, cs, q_ref, g_ref, k_ref, v_ref, kc_ref, vc_ref, o_ref, inc_ref,
                     st_ref, *, n, n_ctx, rc):
    nc = n // rc
    pos = lax.broadcasted_iota(jnp.int32, (rc, LANES), 0).astype(F32)
    kdf = jnp.exp(lgf * (rc - 1.0 - pos)).astype(BF16)
    kdb = jnp.exp(lgb * pos).astype(BF16)
    cdf = jnp.exp(jnp.full((1, LANES), lgf * rc, F32))
    cdb = jnp.exp(jnp.full((1, LANES), lgb * rc, F32))
    qdf = jnp.exp(lgf * (pos + 1.0)).astype(BF16)
    qdb = jnp.exp(lgb * (rc - pos)).astype(BF16)
    ii = lax.broadcasted_iota(jnp.int32, (rc, rc), 0)
    jj = lax.broadcasted_iota(jnp.int32, (rc, rc), 1)
    dist = (ii - jj).astype(F32)
    dmat = (jnp.where(dist >= 0, jnp.exp(lgf * jnp.maximum(dist, 0.0)), 0.0)
            + jnp.where(dist <= 0, jnp.exp(lgb * jnp.maximum(-dist, 0.0)), 0.0))
    state = {}

    def increment(kr, vr, nn):
        rows = slice(nn * rc, (nn + 1) * rc)
        kk = kr[0, rows, cs]
        kcat = jnp.concatenate([kk * kdf, kk * kdb], axis=1)
        vt = vr[0, rows, cs].astype(F32).T.astype(BF16)
        return _dot(vt, kcat)

    def init_task():
        sf = jnp.zeros((HEAD_W, HEAD_W), F32)
        sb = jnp.zeros((HEAD_W, HEAD_W), F32)
        if n_ctx:
            incs = [increment(kc_ref, vc_ref, nn) for nn in range(n_ctx // rc)]
            for inc in incs:
                sf = cdf * sf + inc[:, :HEAD_W]
            for inc in reversed(incs):
                sb = cdb * sb + inc[:, HEAD_W:]
        state["sf"], state["sb"] = sf, sb

    def inc_task(nn):
        inc_ref[nn] = increment(k_ref, v_ref, nn)

    def scan_task():
        sf, sb = state["sf"], state["sb"]
        for nn in range(nc):
            st_ref[nn, :, 0:HEAD_W] = sf.astype(BF16)
            sf = cdf * sf + inc_ref[nn, :, 0:HEAD_W]
        for nn in reversed(range(nc)):
            st_ref[nn, :, HEAD_W:] = sb.astype(BF16)
            sb = cdb * sb + inc_ref[nn, :, HEAD_W:]

    def out_task(nn):
        rows = slice(nn * rc, (nn + 1) * rc)
        q = q_ref[0, rows, cs]
        s = _dot_nt(q, k_ref[0, rows, cs]) * dmat
        inner = _dot(s.astype(BF16), v_ref[0, rows, cs])
        qcat = jnp.concatenate([q * qdf, q * qdb], axis=1)
        out = inner + _dot_nt(qcat, st_ref[nn])
        o_ref[0, rows, cs] = _rms(out).astype(BF16) * g_ref[0, rows, cs]

    inc_tasks = [init_task] + [functools.partial(inc_task, nn) for nn in range(nc)]
    return inc_tasks, scan_task, [functools.partial(out_task, nn) for nn in range(nc)]


def _ret_kernel(lg_ref, q_ref, g_ref, k_ref, v_ref, o_ref, inc_ref, st_ref, *, layer, n, rc,
                heads):
    tasks = []
    for hh in range(heads):
        head = pl.program_id(1) * heads + hh
        tasks.append(_retention_tasks(
            lg_ref[layer, 0, head], lg_ref[layer, 1, head], slice(hh * LANES, (hh + 1) * LANES),
            q_ref, g_ref, k_ref, v_ref, None, None, o_ref, inc_ref.at[hh], st_ref.at[hh],
            n=n, n_ctx=0, rc=rc))
    for i in range(len(tasks[0][0])):
        for inc_tasks, _, _ in tasks:
            inc_tasks[i]()
    for _, scan_task, _ in tasks:
        scan_task()
    for i in range(len(tasks[0][2])):
        for _, _, out_tasks in tasks:
            out_tasks[i]()


def _retention_ctx(log_gamma, layer, zc, sec, n_b, n, *, rc, heads):
    width = heads * LANES
    head_block = lambda b, h: h
    return pl.pallas_call(
        functools.partial(_ret_kernel, layer=layer, n=n, rc=rc, heads=heads),
        grid=(n_b, HEADS // heads),
        in_specs=[pl.BlockSpec(memory_space=pltpu.SMEM)]
        + [_z_spec(sec[name], n, width, True, col_of=head_block)
           for name in ("rq", "rg", "rk", "rv")],
        out_specs=pl.BlockSpec((1, n, width), lambda b, h: (b, 0, h)),
        out_shape=jax.ShapeDtypeStruct((n_b, n, BRANCH_W), BF16),
        scratch_shapes=[pltpu.VMEM((heads, n // rc, HEAD_W, 2 * HEAD_W), F32),
                        pltpu.VMEM((heads, n // rc, HEAD_W, 2 * HEAD_W), BF16)],
        compiler_params=pltpu.CompilerParams(vmem_limit_bytes=VMEM_LIMIT),
        name="retention_ctx",
    )(log_gamma, zc, zc, zc, zc)


def _split_maps(q, lane):
    zero = jnp.zeros_like(q)
    return jnp.concatenate([jnp.where(lane < DIFF_DQK, q, zero),
                            jnp.where(lane >= DIFF_DQK, q, zero)], axis=0)


def _diff_ctx_kernel(lam_ref, q_ref, g_ref, k_ref, v_ref, o_ref, *, layer, tq, post_scale):
    lane = lax.broadcasted_iota(jnp.int32, (tq, LANES), 1)
    for h in range(HEADS):
        cs = slice(h * LANES, (h + 1) * LANES)
        s = _dot_nt(_split_maps(q_ref[0, :, cs], lane), k_ref[0, :, cs])
        p = jnp.exp2(s - jnp.max(s, axis=-1, keepdims=True))
        o = _dot(p.astype(BF16), v_ref[0, :, cs]) / jnp.sum(p, axis=-1, keepdims=True)
        d = o[:tq] - lam_ref[layer] * o[tq:]
        o_ref[0, :, cs] = (_rms(d) * post_scale * g_ref[0, :, cs].astype(F32)).astype(BF16)


def _diff_attention_ctx(lam, layer, zc, sec, n_b, n, *, post_scale):
    return pl.pallas_call(
        functools.partial(_diff_ctx_kernel, layer=layer, tq=n, post_scale=post_scale),
        grid=(n_b,),
        in_specs=[pl.BlockSpec(memory_space=pltpu.SMEM)]
        + [_z_spec(sec[name], n, SEC, True) for name in ("dq", "dg", "dk", "dv")],
        out_specs=pl.BlockSpec((1, n, BRANCH_W), lambda b: (b, 0, 0)),
        out_shape=jax.ShapeDtypeStruct((n_b, n, BRANCH_W), BF16),
        compiler_params=pltpu.CompilerParams(vmem_limit_bytes=VMEM_LIMIT),
        name="diff_attention_ctx",
    )(lam, zc, zc, zc, zc)


def _mixer_kernel(lam_ref, lg_ref, q_ref, g_ref, kc_ref, vc_ref, kl_ref, vl_ref,
                  rq_ref, rg_ref, rk_ref, rv_ref, rkc_ref, rvc_ref, od_ref, or_ref,
                  kall, vt, s0, s1, p0, p1, d0, d1, inc_ref, st_ref, *, layer, n, n_ctx, ts,
                  rc, post_scale):
    n_sub = n // ts
    nk = kall.shape[0]
    kall[0:n_ctx, :] = kc_ref[0]
    kall[n_ctx:, :] = kl_ref[0]
    for c in range(nk // LANES):
        src, base = (vc_ref, 0) if c * LANES < n_ctx else (vl_ref, n_ctx)
        rows = slice(c * LANES - base, (c + 1) * LANES - base)
        vt[0:LANES, c * LANES:(c + 1) * LANES] = src[0, rows, :].astype(F32).T.astype(BF16)
    vt[LANES:, :] = jnp.ones((vt.shape[0] - LANES, nk), BF16)
    lam = lam_ref[layer]
    lane = lax.broadcasted_iota(jnp.int32, (ts, LANES), 1)
    s_bufs, p_bufs, d_bufs = (s0, s1), (p0, p1), (d0, d1)

    def stage_a(t):
        rows = slice(t * ts, (t + 1) * ts)
        s = _dot_nt(kall[...], _split_maps(q_ref[0, rows, :], lane))
        s_bufs[t % 2][...] = s
        d_bufs[t % 2][...] = jnp.max(s, axis=0, keepdims=True)

    def stage_b(t):
        p_bufs[t % 2][...] = jnp.exp2(s_bufs[t % 2][...] - d_bufs[t % 2][...]).astype(BF16)

    def stage_c(t):
        rows = slice(t * ts, (t + 1) * ts)
        oe = _dot(vt[...], p_bufs[t % 2][...])
        ot = oe[:LANES] / oe[LANES:LANES + 1]
        dt = ot[:, :ts] - lam * ot[:, ts:]
        dt = dt * lax.rsqrt(jnp.mean(dt * dt, axis=0, keepdims=True) + EPS)
        od_ref[0, rows, :] = (dt.T * post_scale * g_ref[0, rows, :].astype(F32)).astype(BF16)

    head = pl.program_id(1)
    inc_tasks, scan_task, out_tasks = _retention_tasks(
        lg_ref[layer, 0, head], lg_ref[layer, 1, head], slice(0, LANES),
        rq_ref, rg_ref, rk_ref, rv_ref, rkc_ref, rvc_ref, or_ref, inc_ref, st_ref,
        n=n, n_ctx=n_ctx, rc=rc)
    n_steps = n_sub + 2
    extra = [[] for _ in range(n_steps)]
    half = (len(inc_tasks) + 1) // 2
    extra[0] += inc_tasks[:half]
    extra[1] += inc_tasks[half:]
    extra[2].append(scan_task)
    for i, task in enumerate(out_tasks):
        extra[2 + (i * (n_steps - 2)) // len(out_tasks)].append(task)

    for t in range(n_steps):
        if t >= 2:
            stage_c(t - 2)
        if 1 <= t <= n_sub:
            stage_b(t - 1)
        if t < n_sub:
            stage_a(t)
        for task in extra[t]:
            task()


def _latent_mixers(lam, log_gamma, layer, z, sec, n_b, n, zc, sec_c, n_ctx, *, ts, rc,
                   post_scale):
    nk = n_ctx + n
    head_block = lambda b, h: h
    lat = lambda name: _z_spec(sec[name], n, LANES, False, col_of=head_block)
    cx = lambda name: _z_spec(sec_c[name], n_ctx, LANES, True, col_of=head_block)
    out_spec = pl.BlockSpec((1, n, LANES), lambda b, h: (b, 0, h))
    out_shape = jax.ShapeDtypeStruct((n_b, n, BRANCH_W), BF16)
    o_diff, o_ret = pl.pallas_call(
        functools.partial(_mixer_kernel, layer=layer, n=n, n_ctx=n_ctx, ts=ts, rc=rc,
                          post_scale=post_scale),
        grid=(n_b, HEADS),
        in_specs=[pl.BlockSpec(memory_space=pltpu.SMEM), pl.BlockSpec(memory_space=pltpu.SMEM),
                  lat("dq"), lat("dg"), cx("dk"), cx("dv"), lat("dk"), lat("dv"),
                  lat("rq"), lat("rg"), lat("rk"), lat("rv"), cx("rk"), cx("rv")],
        out_specs=[out_spec, out_spec],
        out_shape=[out_shape, out_shape],
        scratch_shapes=[pltpu.VMEM((nk, LANES), BF16), pltpu.VMEM((LANES + 16, nk), BF16),
                        pltpu.VMEM((nk, 2 * ts), F32), pltpu.VMEM((nk, 2 * ts), F32),
                        pltpu.VMEM((nk, 2 * ts), BF16), pltpu.VMEM((nk, 2 * ts), BF16),
                        pltpu.VMEM((1, 2 * ts), F32), pltpu.VMEM((1, 2 * ts), F32),
                        pltpu.VMEM((n // rc, HEAD_W, 2 * HEAD_W), F32),
                        pltpu.VMEM((n // rc, HEAD_W, 2 * HEAD_W), BF16)],
        compiler_params=pltpu.CompilerParams(vmem_limit_bytes=VMEM_LIMIT),
        name="latent_mixers",
    )(lam, log_gamma, z, z, zc, zc, z, z, z, z, z, z, zc, zc)
    return o_ret, o_diff


def _merge_kernel(x_ref, gate_ref, gpost_ref, oret_ref, odiff_ref, u_ref, vn_ref, mg_ref,
                  gr_ref, gm_ref, gd_ref, ws_ref, bs_ref, wbo_ref, wo_ref, out_ref, *, tm, out_sub):
    n_chunks = tm // CHUNK
    cols = []
    for g in range(HEADS):
        gcols = slice(g * LANES, (g + 1) * LANES)
        side = jnp.concatenate([vn_ref[0, c * CHUNK:(c + 1) * CHUNK, gcols]
                                for c in range(n_chunks)], axis=1)
        mixed = _dot(ws_ref[g], side)
        cols.append(jnp.concatenate([mixed[:, c * LANES:(c + 1) * LANES] + bs_ref[g]
                                     for c in range(n_chunks)], axis=0))
    sp = jnp.concatenate(cols, axis=1)
    o_mlp = (u_ref[0].astype(F32) * sp * mg_ref[0].astype(F32)).astype(BF16)

    def gated(gate2_ref, o, w):
        gate = jnp.concatenate([gate2_ref[0, 0], gate2_ref[0, 1]], axis=1).astype(F32)
        return gate * _dot(o, w)

    t = (gated(gr_ref, oret_ref[0], wbo_ref[0]) + gated(gm_ref, o_mlp, wbo_ref[1])
         + gated(gd_ref, odiff_ref[0], wbo_ref[2]))
    tb = t.astype(BF16)
    for r in range(tm // out_sub):
        rs = slice(r * out_sub, (r + 1) * out_sub)
        y = _dot(tb[rs, :], wo_ref[...])
        out_ref[0, rs, :] = x_ref[0, rs, :] + gate_ref[...] * (_rms(y) * gpost_ref[...])


def _merge(xs, l, mod4, row_of, g_post, o_ret, o_diff, z, sec, ws, bs, wbo, wo, *, tm):
    n_b, n, _ = xs.shape
    row_block = lambda b, i: i

    def zsec(name):
        return _z_spec(sec[name], tm, SEC, False, row_of=row_block)

    def zgate(k):
        return pl.BlockSpec((1, 2, tm, SEC), lambda b, i: (b, k, i, 0))

    in_specs = [
        pl.BlockSpec((1, tm, D_MODEL), lambda b, i: (b, i, 0)),
        _mod_spec(l, row_of, 2),
        _layer_spec(l, (1, D_MODEL)),
        pl.BlockSpec((1, tm, BRANCH_W), lambda b, i: (b, i, 0)),
        pl.BlockSpec((1, tm, BRANCH_W), lambda b, i: (b, i, 0)),
        zsec("mu"), zsec("mv"), zsec("mg"), zgate(0), zgate(1), zgate(2),
        _layer_spec(l, (HEADS, CHUNK, CHUNK)),
        _layer_spec(l, (HEADS, CHUNK, LANES)),
        _layer_spec(l, (N_BRANCH, BRANCH_W, D_MODEL)),
        _layer_spec(l, (D_MODEL, D_MODEL)),
    ]
    return pl.pallas_call(
        functools.partial(_merge_kernel, tm=tm, out_sub=MERGE_OUT_SUB),
        grid=(n_b, n // tm),
        in_specs=in_specs,
        out_specs=pl.BlockSpec((1, tm, D_MODEL), lambda b, i: (b, i, 0)),
        out_shape=jax.ShapeDtypeStruct(xs.shape, F32),
        compiler_params=pltpu.CompilerParams(vmem_limit_bytes=VMEM_LIMIT),
        name="merge",
    )(xs, mod4, g_post, o_ret, o_diff, z, z, z, z, z, z, ws, bs, wbo, wo)


def _rope_tables(n_lat):
    rows = n_lat // GRID_W
    row_pos = jnp.repeat(jnp.arange(rows, dtype=F32), GRID_W)
    col_pos = jnp.tile(jnp.arange(GRID_W, dtype=F32), rows)

    def angles(head_dim):
        n_freq = head_dim // 4
        inv = ROPE_BASE ** (-jnp.arange(n_freq, dtype=F32) / n_freq)
        ang = jnp.concatenate([row_pos[:, None] * inv, col_pos[:, None] * inv], axis=-1)
        return jnp.cos(ang), jnp.sin(ang)

    cos_r, sin_r = angles(HEAD_W)
    cos_d, sin_d = angles(DIFF_DQK)
    zeros = jnp.zeros_like(sin_d)
    c_r = jnp.concatenate([cos_r, cos_r], axis=-1)
    s_r = jnp.concatenate([-sin_r, sin_r], axis=-1)
    c_d = jnp.tile(jnp.concatenate([cos_d, cos_d], axis=-1), (1, 2))
    s_lo = jnp.tile(jnp.concatenate([-sin_d, zeros], axis=-1), (1, 2))
    s_hi = jnp.tile(jnp.concatenate([zeros, sin_d], axis=-1), (1, 2))
    return c_r, s_r, c_d, s_lo, s_hi


@jax.jit
def kernel(x, c, ctx, c_ctx, w_mod, b_mod, g_pre, g_post, w_in, ret_decay_logit, mlp_w_s,
           mlp_b_s, diff_lambda_q, diff_lambda_k, w_branch_out, w_out):
    n_b, n_lat, _ = x.shape
    n_ctx = ctx.shape[1]
    tables = jnp.stack(_rope_tables(n_lat))

    cond_rows = 16
    ctx_row = n_b
    cc = jnp.zeros((cond_rows, D_MODEL), F32).at[:n_b].set(c).at[ctx_row].set(c_ctx)
    mod4 = _modulation(cc, w_mod, b_mod).reshape(DEPTH, cond_rows, 1, 3 * D_MODEL)
    lat_row = lambda b: b
    ctx_row_of = lambda b: ctx_row

    log_gamma = -jax.nn.softplus(-ret_decay_logit.astype(F32))
    lam_inits = [0.8 - 0.6 * math.exp(-0.3 * l) for l in range(DEPTH)]
    lam = (jnp.exp(jnp.sum(diff_lambda_q[:, 0] * diff_lambda_k[:, 0], axis=-1))
           - jnp.exp(jnp.sum(diff_lambda_q[:, 1] * diff_lambda_k[:, 1], axis=-1))
           + jnp.asarray(lam_inits, F32)).astype(F32)

    w_sec = w_in
    ws = mlp_w_s.astype(BF16)
    bs = jnp.broadcast_to(mlp_b_s[..., None], (DEPTH, HEADS, CHUNK, LANES)).astype(F32)
    wbo = w_branch_out.astype(BF16)
    wo = w_out.astype(BF16)
    g_pre3 = g_pre.reshape(DEPTH, 1, D_MODEL)
    g_post3 = g_post.reshape(DEPTH, 1, D_MODEL)

    for l in range(DEPTH):
        last = l == DEPTH - 1
        post_scale = 1.0 - lam_inits[l]

        ctx_flat = ctx.reshape(1, n_b * n_ctx, D_MODEL)
        if last:
            zc = _projection(ctx_flat, l, g_pre3, mod4, ctx_row_of, w_sec, lambda j: j,
                             KV_KINDS, None)
            sec_c = KV_SEC
        else:
            zc = _projection(ctx_flat, l, g_pre3, mod4, ctx_row_of, w_sec, _full_src,
                             FULL_KINDS, None)
            sec_c = FULL_SEC
            co_ret = _retention_ctx(log_gamma, l, zc, sec_c, n_b, n_ctx, rc=RET_CHUNK,
                                    heads=HEADS)
            co_diff = _diff_attention_ctx(lam, l, zc, sec_c, n_b, n_ctx, post_scale=post_scale)
            ctx_next = _merge(ctx_flat, l, mod4, ctx_row_of, g_post3,
                              co_ret.reshape(1, n_b * n_ctx, BRANCH_W),
                              co_diff.reshape(1, n_b * n_ctx, BRANCH_W), zc, sec_c,
                              ws, bs, wbo, wo, tm=MERGE_TM)
            ctx_next = ctx_next.reshape(n_b, n_ctx, D_MODEL)

        z = _projection(x, l, g_pre3, mod4, lat_row, w_sec, _full_src, FULL_KINDS, tables)
        o_ret, o_diff = _latent_mixers(lam, log_gamma, l, z, FULL_SEC, n_b, n_lat, zc, sec_c,
                                       n_ctx, ts=DIFF_TS, rc=RET_CHUNK, post_scale=post_scale)
        x = _merge(x, l, mod4, lat_row, g_post3, o_ret, o_diff, z, FULL_SEC, ws, bs, wbo, wo,
                   tm=MERGE_TM)
        if not last:
            ctx = ctx_next
    return x
```

```python
import functools
import math

import jax
import jax.numpy as jnp
from jax import lax
from jax.experimental import pallas as pl
from jax.experimental.pallas import tpu as pltpu

F32 = jnp.float32
BF16 = jnp.bfloat16

D_MODEL = 1024
DEPTH = 2
GRID_W = 64
N_BRANCH = 3
BRANCH_W = D_MODEL // 2
HEADS = 4
HEAD_W = BRANCH_W // HEADS
DIFF_DQK = HEAD_W // 2
CHUNK = 128
ROPE_BASE = 10000.0
EPS = 1e-6
RET_SCALE = HEAD_W ** -0.5
DIFF_Q_SCALE = (DIFF_DQK ** -0.5) * math.log2(math.e)

LANES = 128
SEC = BRANCH_W
MERGE_COLS = N_BRANCH * D_MODEL
KV_COLS = 4 * BRANCH_W
IN_COLS = KV_COLS + 7 * BRANCH_W + MERGE_COLS
VMEM_LIMIT = 56 * 1024 * 1024

PROJ_SUB = 256
PROJ_SLAB_BIG = 512
PROJ_SLAB_SMALL = 128
PROJ_OUT_SLOTS = 2
DIFF_TS = 256
RET_CHUNK = 256
MERGE_TM = 512
MERGE_OUT_SUB = 512

FULL_KINDS = ("sigmoid",) * 6 + ("rk", "plain", "dk", "plain", "rq", "silu", "dq", "silu",
                                 "plain", "ln", "silu")
FULL_SEC = dict(rk=6, rv=7, dk=8, dv=9, rq=10, rg=11, dq=12, dg=13, mu=14, mv=15, mg=16)
KV_KINDS = ("rk", "plain", "dk", "plain")
KV_SEC = dict(rk=0, rv=1, dk=2, dv=3)
_N_MERGE_SEC = MERGE_COLS // SEC
_N_REST_SEC = (KV_COLS + 7 * BRANCH_W) // SEC


def _full_src(j):
    return jnp.where(j < _N_MERGE_SEC, j + _N_REST_SEC, j - _N_MERGE_SEC)


def _dot(a, b):
    return jnp.dot(a, b, preferred_element_type=F32)


def _dot_nt(a, b):
    return lax.dot_general(a, b, (((1,), (1,)), ((), ())), preferred_element_type=F32)


def _rms(v):
    return v * lax.rsqrt(jnp.mean(v * v, axis=-1, keepdims=True) + EPS)


def _mod_spec(l, row_of, k):
    return pl.BlockSpec((None, None, 1, D_MODEL), lambda *idx: (l, row_of(idx[0]), 0, k))


def _layer_spec(l, shape, **kwargs):
    zeros = (0,) * len(shape)
    return pl.BlockSpec((None,) + tuple(shape), lambda *idx: (l,) + zeros, **kwargs)


def _z_spec(sec_idx, rows, width, flat, col_of=lambda *idx: 0, row_of=None):
    if row_of is None:
        row_of = (lambda *idx: idx[0]) if flat else (lambda *idx: 0)
    group_of = (lambda *idx: 0) if flat else (lambda *idx: idx[0])
    return pl.BlockSpec((1, None, rows, width),
                        lambda *idx: (group_of(*idx), sec_idx, row_of(*idx), col_of(*idx)))


def _split_bf16(a):
    hi = a.astype(BF16)
    return hi, (a - hi.astype(F32)).astype(BF16)


def _mod_kernel(c_ref, w_ref, b_ref, o_ref):
    cv = c_ref[...]
    c_hi, c_lo = _split_bf16(cv * jax.nn.sigmoid(cv))
    w_hi, w_lo = _split_bf16(w_ref[0])
    o_ref[0] = _dot(c_hi, w_hi) + _dot(c_hi, w_lo) + _dot(c_lo, w_hi) + b_ref[0]


def _modulation(cc, w_mod, b_mod):
    rows = cc.shape[0]
    return pl.pallas_call(
        _mod_kernel,
        grid=(DEPTH, 3),
        in_specs=[
            pl.BlockSpec((rows, D_MODEL), lambda l, j: (0, 0)),
            pl.BlockSpec((1, D_MODEL, D_MODEL), lambda l, j: (l, 0, j)),
            pl.BlockSpec((1, 1, D_MODEL), lambda l, j: (l, 0, j)),
        ],
        out_specs=pl.BlockSpec((1, rows, D_MODEL), lambda l, j: (l, 0, j)),
        out_shape=jax.ShapeDtypeStruct((DEPTH, rows, 3 * D_MODEL), F32),
        compiler_params=pltpu.CompilerParams(vmem_limit_bytes=VMEM_LIMIT),
        name="modulation",
    )(cc, w_mod, b_mod.reshape(DEPTH, 1, 3 * D_MODEL))


def _rope_ret(a, c, s):
    outs = []
    for h in range(HEADS):
        ah = a[:, h * LANES:(h + 1) * LANES]
        outs.append(ah * c + pltpu.roll(ah, LANES // 2, 1) * s)
    return jnp.concatenate(outs, axis=1)


def _rope_diff(a, c, s_lo, s_hi):
    outs = []
    for h in range(HEADS):
        ah = a[:, h * LANES:(h + 1) * LANES]
        outs.append(ah * c + pltpu.roll(ah, LANES - DIFF_DQK // 2, 1) * s_lo
                    + pltpu.roll(ah, DIFF_DQK // 2, 1) * s_hi)
    return jnp.concatenate(outs, axis=1)


def _slab_sizes(n_rows):
    sizes, left = [], n_rows
    while left > 2 * PROJ_SLAB_BIG:
        sizes.append(PROJ_SLAB_BIG)
        left -= PROJ_SLAB_BIG
    while left > 2 * PROJ_SLAB_SMALL:
        sizes.append(left // 2)
        left -= left // 2
    sizes += [left // 2, left - left // 2]
    return sizes


def _proj_kernel(*refs, layer, kinds, src_of, n_src, rope, n_rows, sub):
    if rope:
        (x_ref, g_ref, shift_ref, scale_ref, w_hbm, tab_hbm, z_hbm,
         h_ref, stage, w_vmem, w_stage, sems_out, sems_w, tab_vmem, sem_tab) = refs
        cr_ref, sr_ref, cd_ref, sdl_ref, sdh_ref = [tab_vmem.at[i] for i in range(5)]
    else:
        (x_ref, g_ref, shift_ref, scale_ref, w_hbm, z_hbm,
         h_ref, stage, w_vmem, w_stage, sems_out, sems_w) = refs
    b = pl.program_id(0)
    first = b == 0
    n_sub = n_rows // sub
    n_sec = len(kinds)
    n_slots = stage.shape[0]

    n_wslots = w_stage.shape[0]

    def weight_copy(j):
        start = src_of(j) * SEC
        cols = pl.ds(start if isinstance(start, int) else pl.multiple_of(start, SEC), SEC)
        slot = j % n_wslots
        return pltpu.make_async_copy(w_hbm.at[layer, :, cols], w_stage.at[slot], sems_w.at[slot])

    def load_weights(j):
        weight_copy(j).wait()
        src = w_stage.at[j % n_wslots]
        dst = w_vmem.at[src_of(j)]
        for r in range(D_MODEL // LANES):
            rows = slice(r * LANES, (r + 1) * LANES)
            dst[rows, :] = src[rows, :].astype(BF16)

        @pl.when(j + n_wslots < n_sec)
        def _():
            weight_copy(j + n_wslots).start()

    def table_copy():
        return pltpu.make_async_copy(tab_hbm, tab_vmem, sem_tab.at[0])

    @pl.when(first)
    def _():
        for j in range(min(n_wslots, n_sec)):
            weight_copy(j).start()
        if rope:
            table_copy().start()

    gs = g_ref[...] * (1.0 + scale_ref[...])
    sh = shift_ref[...]
    for r in range(n_sub):
        xs = x_ref[0, r * sub:(r + 1) * sub, :]
        h_ref[r * sub:(r + 1) * sub, :] = (_rms(xs) * gs + sh).astype(BF16)

    if rope:
        pl.when(first)(lambda: table_copy().wait())

    def epi_rk(a, rows):
        if rope:
            a = _rope_ret(a, cr_ref[rows, :], sr_ref[rows, :])
        return a * RET_SCALE

    def epi_rq(a, rows):
        return _rope_ret(a, cr_ref[rows, :], sr_ref[rows, :]) if rope else a

    def epi_dk(a, rows):
        return _rope_diff(a, cd_ref[rows, :], sdl_ref[rows, :], sdh_ref[rows, :]) if rope else a

    def epi_dq(a, rows):
        return epi_dk(a, rows) * DIFF_Q_SCALE

    def epi_ln(a, rows):
        mu = jnp.mean(a, axis=-1, keepdims=True)
        d = a - mu
        return d * lax.rsqrt(jnp.mean(d * d, axis=-1, keepdims=True) + EPS)

    epilogues = dict(
        plain=lambda a, rows: a,
        sigmoid=lambda a, rows: jax.nn.sigmoid(a),
        silu=lambda a, rows: a * jax.nn.sigmoid(a),
        rk=epi_rk, rq=epi_rq, dk=epi_dk, dq=epi_dq, ln=epi_ln)

    def writeback(j):
        slot = j % n_slots
        return pltpu.make_async_copy(stage.at[slot], z_hbm.at[b, j], sems_out.at[slot])

    def run(epilogue, j):
        w_sec = w_vmem.at[src_of(j)]
        out = stage.at[j % n_slots]
        start = 0
        for size in _slab_sizes(n_rows):
            rows = slice(start, start + size)
            out[rows, :] = epilogue(_dot(h_ref[rows, :], w_sec[...]), rows).astype(BF16)
            start += size

    def section(j, carry):
        pl.when(first)(functools.partial(load_weights, j))

        @pl.when(j >= n_slots)
        def _():
            writeback(j - n_slots).wait()

        for kind in sorted(set(kinds)):
            idx = [i for i, k in enumerate(kinds) if k == kind]
            cond = j == idx[0]
            for i in idx[1:]:
                cond = cond | (j == i)
            pl.when(cond)(functools.partial(run, epilogues[kind], j))
        writeback(j).start()
        return carry

    lax.fori_loop(0, n_sec, section, 0)
    for j in range(max(0, n_sec - n_slots), n_sec):
        writeback(j).wait()


def _projection(xs, l, g_pre, mod4, row_of, w_all, src_of, kinds, tables):
    n_g, n_rows, _ = xs.shape
    n_sec = len(kinds)
    rope = tables is not None
    n_src = w_all.shape[2] // SEC if src_of is _full_src else n_sec
    in_specs = [
        pl.BlockSpec((1, n_rows, D_MODEL), lambda b: (b, 0, 0)),
        _layer_spec(l, (1, D_MODEL)),
        _mod_spec(l, row_of, 0),
        _mod_spec(l, row_of, 1),
        pl.BlockSpec(memory_space=pl.ANY),
    ]
    args = [xs, g_pre, mod4, mod4, w_all]
    scratch = [pltpu.VMEM((n_rows, D_MODEL), BF16),
               pltpu.VMEM((PROJ_OUT_SLOTS, n_rows, SEC), BF16),
               pltpu.VMEM((n_src, D_MODEL, SEC), BF16),
               pltpu.VMEM((2, D_MODEL, SEC), F32),
               pltpu.SemaphoreType.DMA((PROJ_OUT_SLOTS,)),
               pltpu.SemaphoreType.DMA((2,))]
    if rope:
        in_specs.append(pl.BlockSpec(memory_space=pl.ANY))
        args.append(tables)
        scratch += [pltpu.VMEM(tables.shape, F32), pltpu.SemaphoreType.DMA((1,))]
    return pl.pallas_call(
        functools.partial(_proj_kernel, layer=l, kinds=kinds, src_of=src_of, n_src=n_src,
                          rope=rope, n_rows=n_rows, sub=PROJ_SUB),
        grid=(n_g,),
        in_specs=in_specs,
        out_specs=pl.BlockSpec(memory_space=pl.ANY),
        out_shape=jax.ShapeDtypeStruct((n_g, n_sec, n_rows, SEC), BF16),
        scratch_shapes=scratch,
        compiler_params=pltpu.CompilerParams(vmem_limit_bytes=VMEM_LIMIT),
        name="projection_rope" if rope else "projection",
    )(*args)


def _retention_tasks(lgf, lgb, cs, q_ref, g_ref, k_ref, v_ref, kc_ref, vc_ref, o_ref, inc_ref,
                     st_ref, *, n, n_ctx, rc):
    nc = n // rc
    pos = lax.broadcasted_iota(jnp.int32, (rc, LANES), 0).astype(F32)
    kdf = jnp.exp(lgf * (rc - 1.0 - pos)).astype(BF16)
    kdb = jnp.exp(lgb * pos).astype(BF16)
    cdf = jnp.exp(jnp.full((1, LANES), lgf * rc, F32))
    cdb = jnp.exp(jnp.full((1, LANES), lgb * rc, F32))
    qdf = jnp.exp(lgf * (pos + 1.0)).astype(BF16)
    qdb = jnp.exp(lgb * (rc - pos)).astype(BF16)
    ii = lax.broadcasted_iota(jnp.int32, (rc, rc), 0)
    jj = lax.broadcasted_iota(jnp.int32, (rc, rc), 1)
    dist = (ii - jj).astype(F32)
    dmat = (jnp.where(dist >= 0, jnp.exp(lgf * jnp.maximum(dist, 0.0)), 0.0)
            + jnp.where(dist <= 0, jnp.exp(lgb * jnp.maximum(-dist, 0.0)), 0.0))
    state = {}

    def increment(kr, vr, nn):
        rows = slice(nn * rc, (nn + 1) * rc)
        kk = kr[0, rows, cs]
        kcat = jnp.concatenate([kk * kdf, kk * kdb], axis=1)
        vt = vr[0, rows, cs].astype(F32).T.astype(BF16)
        return _dot(vt, kcat)

    def init_task():
        sf = jnp.zeros((HEAD_W, HEAD_W), F32)
        sb = jnp.zeros((HEAD_W, HEAD_W), F32)
        if n_ctx:
            incs = [increment(kc_ref, vc_ref, nn) for nn in range(n_ctx // rc)]
            for inc in incs:
                sf = cdf * sf + inc[:, :HEAD_W]
            for inc in reversed(incs):
                sb = cdb * sb + inc[:, HEAD_W:]
        state["sf"], state["sb"] = sf, sb

    def inc_task(nn):
        inc_ref[nn] = increment(k_ref, v_ref, nn)

    def scan_task():
        sf, sb = state["sf"], state["sb"]
        for nn in range(nc):
            st_ref[nn, :, 0:HEAD_W] = sf.astype(BF16)
            sf = cdf * sf + inc_ref[nn, :, 0:HEAD_W]
        for nn in reversed(range(nc)):
            st_ref[nn, :, HEAD_W:] = sb.astype(BF16)
            sb = cdb * sb + inc_ref[nn, :, HEAD_W:]

    def out_task(nn):
        rows = slice(nn * rc, (nn + 1) * rc)
        q = q_ref[0, rows, cs]
        s = _dot_nt(q, k_ref[0, rows, cs]) * dmat
        inner = _dot(s.astype(BF16), v_ref[0, rows, cs])
        qcat = jnp.concatenate([q * qdf, q * qdb], axis=1)
        out = inner + _dot_nt(qcat, st_ref[nn])
        o_ref[0, rows, cs] = _rms(out).astype(BF16) * g_ref[0, rows, cs]

    inc_tasks = [init_task] + [functools.partial(inc_task, nn) for nn in range(nc)]
    return inc_tasks, scan_task, [functools.partial(out_task, nn) for nn in range(nc)]


def _ret_kernel(lg_ref, q_ref, g_ref, k_ref, v_ref, o_ref, inc_ref, st_ref, *, layer, n, rc,
                heads):
    tasks = []
    for hh in range(heads):
        head = pl.program_id(1) * heads + hh
        tasks.append(_retention_tasks(
            lg_ref[layer, 0, head], lg_ref[layer, 1, head], slice(hh * LANES, (hh + 1) * LANES),
            q_ref, g_ref, k_ref, v_ref, None, None, o_ref, inc_ref.at[hh], st_ref.at[hh],
            n=n, n_ctx=0, rc=rc))
    for i in range(len(tasks[0][0])):
        for inc_tasks, _, _ in tasks:
            inc_tasks[i]()
    for _, scan_task, _ in tasks:
        scan_task()
    for i in range(len(tasks[0][2])):
        for _, _, out_tasks in tasks:
            out_tasks[i]()


def _retention_ctx(log_gamma, layer, zc, sec, n_b, n, *, rc, heads):
    width = heads * LANES
    head_block = lambda b, h: h
    return pl.pallas_call(
        functools.partial(_ret_kernel, layer=layer, n=n, rc=rc, heads=heads),
        grid=(n_b, HEADS // heads),
        in_specs=[pl.BlockSpec(memory_space=pltpu.SMEM)]
        + [_z_spec(sec[name], n, width, True, col_of=head_block)
           for name in ("rq", "rg", "rk", "rv")],
        out_specs=pl.BlockSpec((1, n, width), lambda b, h: (b, 0, h)),
        out_shape=jax.ShapeDtypeStruct((n_b, n, BRANCH_W), BF16),
        scratch_shapes=[pltpu.VMEM((heads, n // rc, HEAD_W, 2 * HEAD_W), F32),
                        pltpu.VMEM((heads, n // rc, HEAD_W, 2 * HEAD_W), BF16)],
        compiler_params=pltpu.CompilerParams(vmem_limit_bytes=VMEM_LIMIT),
        name="retention_ctx",
    )(log_gamma, zc, zc, zc, zc)


def _split_maps(q, lane):
    zero = jnp.zeros_like(q)
    return jnp.concatenate([jnp.where(lane < DIFF_DQK, q, zero),
                            jnp.where(lane >= DIFF_DQK, q, zero)], axis=0)


def _diff_ctx_kernel(lam_ref, q_ref, g_ref, k_ref, v_ref, o_ref, *, layer, tq, post_scale):
    lane = lax.broadcasted_iota(jnp.int32, (tq, LANES), 1)
    for h in range(HEADS):
        cs = slice(h * LANES, (h + 1) * LANES)
        s = _dot_nt(_split_maps(q_ref[0, :, cs], lane), k_ref[0, :, cs])
        p = jnp.exp2(s - jnp.max(s, axis=-1, keepdims=True))
        o = _dot(p.astype(BF16), v_ref[0, :, cs]) / jnp.sum(p, axis=-1, keepdims=True)
        d = o[:tq] - lam_ref[layer] * o[tq:]
        o_ref[0, :, cs] = (_rms(d) * post_scale * g_ref[0, :, cs].astype(F32)).astype(BF16)


def _diff_attention_ctx(lam, layer, zc, sec, n_b, n, *, post_scale):
    return pl.pallas_call(
        functools.partial(_diff_ctx_kernel, layer=layer, tq=n, post_scale=post_scale),
        grid=(n_b,),
        in_specs=[pl.BlockSpec(memory_space=pltpu.SMEM)]
        + [_z_spec(sec[name], n, SEC, True) for name in ("dq", "dg", "dk", "dv")],
        out_specs=pl.BlockSpec((1, n, BRANCH_W), lambda b: (b, 0, 0)),
        out_shape=jax.ShapeDtypeStruct((n_b, n, BRANCH_W), BF16),
        compiler_params=pltpu.CompilerParams(vmem_limit_bytes=VMEM_LIMIT),
        name="diff_attention_ctx",
    )(lam, zc, zc, zc, zc)


def _mixer_kernel(lam_ref, lg_ref, q_ref, g_ref, kc_ref, vc_ref, kl_ref, vl_ref,
                  rq_ref, rg_ref, rk_ref, rv_ref, rkc_ref, rvc_ref, od_ref, or_ref,
                  kall, vt, s0, s1, p0, p1, d0, d1, inc_ref, st_ref, *, layer, n, n_ctx, ts,
                  rc, post_scale):
    n_sub = n // ts
    nk = kall.shape[0]
    kall[0:n_ctx, :] = kc_ref[0]
    kall[n_ctx:, :] = kl_ref[0]
    for c in range(nk // LANES):
        src, base = (vc_ref, 0) if c * LANES < n_ctx else (vl_ref, n_ctx)
        rows = slice(c * LANES - base, (c + 1) * LANES - base)
        vt[0:LANES, c * LANES:(c + 1) * LANES] = src[0, rows, :].astype(F32).T.astype(BF16)
    vt[LANES:, :] = jnp.ones((vt.shape[0] - LANES, nk), BF16)
    lam = lam_ref[layer]
    lane = lax.broadcasted_iota(jnp.int32, (ts, LANES), 1)
    s_bufs, p_bufs, d_bufs = (s0, s1), (p0, p1), (d0, d1)

    def stage_a(t):
        rows = slice(t * ts, (t + 1) * ts)
        s = _dot_nt(kall[...], _split_maps(q_ref[0, rows, :], lane))
        s_bufs[t % 2][...] = s
        d_bufs[t % 2][...] = jnp.max(s, axis=0, keepdims=True)

    def stage_b(t):
        p_bufs[t % 2][...] = jnp.exp2(s_bufs[t % 2][...] - d_bufs[t % 2][...]).astype(BF16)

    def stage_c(t):
        rows = slice(t * ts, (t + 1) * ts)
        oe = _dot(vt[...], p_bufs[t % 2][...])
        ot = oe[:LANES] / oe[LANES:LANES + 1]
        dt = ot[:, :ts] - lam * ot[:, ts:]
        dt = dt * lax.rsqrt(jnp.mean(dt * dt, axis=0, keepdims=True) + EPS)
        od_ref[0, rows, :] = (dt.T * post_scale * g_ref[0, rows, :].astype(F32)).astype(BF16)

    head = pl.program_id(1)
    inc_tasks, scan_task, out_tasks = _retention_tasks(
        lg_ref[layer, 0, head], lg_ref[layer, 1, head], slice(0, LANES),
        rq_ref, rg_ref, rk_ref, rv_ref, rkc_ref, rvc_ref, or_ref, inc_ref, st_ref,
        n=n, n_ctx=n_ctx, rc=rc)
    n_steps = n_sub + 2
    extra = [[] for _ in range(n_steps)]
    half = (len(inc_tasks) + 1) // 2
    extra[0] += inc_tasks[:half]
    extra[1] += inc_tasks[half:]
    extra[2].append(scan_task)
    for i, task in enumerate(out_tasks):
        extra[2 + (i * (n_steps - 2)) // len(out_tasks)].append(task)

    for t in range(n_steps):
        if t >= 2:
            stage_c(t - 2)
        if 1 <= t <= n_sub:
            stage_b(t - 1)
        if t < n_sub:
            stage_a(t)
        for task in extra[t]:
            task()


def _latent_mixers(lam, log_gamma, layer, z, sec, n_b, n, zc, sec_c, n_ctx, *, ts, rc,
                   post_scale):
    nk = n_ctx + n
    head_block = lambda b, h: h
    lat = lambda name: _z_spec(sec[name], n, LANES, False, col_of=head_block)
    cx = lambda name: _z_spec(sec_c[name], n_ctx, LANES, True, col_of=head_block)
    out_spec = pl.BlockSpec((1, n, LANES), lambda b, h: (b, 0, h))
    out_shape = jax.ShapeDtypeStruct((n_b, n, BRANCH_W), BF16)
    o_diff, o_ret = pl.pallas_call(
        functools.partial(_mixer_kernel, layer=layer, n=n, n_ctx=n_ctx, ts=ts, rc=rc,
                          post_scale=post_scale),
        grid=(n_b, HEADS),
        in_specs=[pl.BlockSpec(memory_space=pltpu.SMEM), pl.BlockSpec(memory_space=pltpu.SMEM),
                  lat("dq"), lat("dg"), cx("dk"), cx("dv"), lat("dk"), lat("dv"),
                  lat("rq"), lat("rg"), lat("rk"), lat("rv"), cx("rk"), cx("rv")],
        out_specs=[out_spec, out_spec],
        out_shape=[out_shape, out_shape],
        scratch_shapes=[pltpu.VMEM((nk, LANES), BF16), pltpu.VMEM((LANES + 16, nk), BF16),
                        pltpu.VMEM((nk, 2 * ts), F32), pltpu.VMEM((nk, 2 * ts), F32),
                        pltpu.VMEM((nk, 2 * ts), BF16), pltpu.VMEM((nk, 2 * ts), BF16),
                        pltpu.VMEM((1, 2 * ts), F32), pltpu.VMEM((1, 2 * ts), F32),
                        pltpu.VMEM((n // rc, HEAD_W, 2 * HEAD_W), F32),
                        pltpu.VMEM((n // rc, HEAD_W, 2 * HEAD_W), BF16)],
        compiler_params=pltpu.CompilerParams(vmem_limit_bytes=VMEM_LIMIT),
        name="latent_mixers",
    )(lam, log_gamma, z, z, zc, zc, z, z, z, z, z, z, zc, zc)
    return o_ret, o_diff


def _merge_kernel(x_ref, gate_ref, gpost_ref, oret_ref, odiff_ref, u_ref, vn_ref, mg_ref,
                  gr_ref, gm_ref, gd_ref, ws_ref, bs_ref, wbo_ref, wo_ref, out_ref, *, tm, out_sub):
    rows = []
    for c in range(tm // CHUNK):
        cols = []
        for g in range(HEADS):
            blk = vn_ref[0, c * CHUNK:(c + 1) * CHUNK, g * LANES:(g + 1) * LANES]
            cols.append(_dot(ws_ref[g], blk) + bs_ref[g])
        rows.append(jnp.concatenate(cols, axis=1))
    sp = jnp.concatenate(rows, axis=0)
    o_mlp = (u_ref[0].astype(F32) * sp * mg_ref[0].astype(F32)).astype(BF16)

    def gated(gate2_ref, o, w):
        gate = jnp.concatenate([gate2_ref[0, 0], gate2_ref[0, 1]], axis=1).astype(F32)
        return gate * _dot(o, w)

    t = (gated(gr_ref, oret_ref[0], wbo_ref[0]) + gated(gm_ref, o_mlp, wbo_ref[1])
         + gated(gd_ref, odiff_ref[0], wbo_ref[2]))
    tb = t.astype(BF16)
    for r in range(tm // out_sub):
        rs = slice(r * out_sub, (r + 1) * out_sub)
        y = _dot(tb[rs, :], wo_ref[...])
        out_ref[0, rs, :] = x_ref[0, rs, :] + gate_ref[...] * (_rms(y) * gpost_ref[...])


def _merge(xs, l, mod4, row_of, g_post, o_ret, o_diff, z, sec, ws, bs, wbo, wo, *, tm):
    n_b, n, _ = xs.shape
    row_block = lambda b, i: i

    def zsec(name):
        return _z_spec(sec[name], tm, SEC, False, row_of=row_block)

    def zgate(k):
        return pl.BlockSpec((1, 2, tm, SEC), lambda b, i: (b, k, i, 0))

    in_specs = [
        pl.BlockSpec((1, tm, D_MODEL), lambda b, i: (b, i, 0)),
        _mod_spec(l, row_of, 2),
        _layer_spec(l, (1, D_MODEL)),
        pl.BlockSpec((1, tm, BRANCH_W), lambda b, i: (b, i, 0)),
        pl.BlockSpec((1, tm, BRANCH_W), lambda b, i: (b, i, 0)),
        zsec("mu"), zsec("mv"), zsec("mg"), zgate(0), zgate(1), zgate(2),
        _layer_spec(l, (HEADS, CHUNK, CHUNK)),
        _layer_spec(l, (HEADS, CHUNK, LANES)),
        _layer_spec(l, (N_BRANCH, BRANCH_W, D_MODEL)),
        _layer_spec(l, (D_MODEL, D_MODEL)),
    ]
    return pl.pallas_call(
        functools.partial(_merge_kernel, tm=tm, out_sub=MERGE_OUT_SUB),
        grid=(n_b, n // tm),
        in_specs=in_specs,
        out_specs=pl.BlockSpec((1, tm, D_MODEL), lambda b, i: (b, i, 0)),
        out_shape=jax.ShapeDtypeStruct(xs.shape, F32),
        compiler_params=pltpu.CompilerParams(vmem_limit_bytes=VMEM_LIMIT),
        name="merge",
    )(xs, mod4, g_post, o_ret, o_diff, z, z, z, z, z, z, ws, bs, wbo, wo)


def _rope_tables(n_lat):
    rows = n_lat // GRID_W
    row_pos = jnp.repeat(jnp.arange(rows, dtype=F32), GRID_W)
    col_pos = jnp.tile(jnp.arange(GRID_W, dtype=F32), rows)

    def angles(head_dim):
        n_freq = head_dim // 4
        inv = ROPE_BASE ** (-jnp.arange(n_freq, dtype=F32) / n_freq)
        ang = jnp.concatenate([row_pos[:, None] * inv, col_pos[:, None] * inv], axis=-1)
        return jnp.cos(ang), jnp.sin(ang)

    cos_r, sin_r = angles(HEAD_W)
    cos_d, sin_d = angles(DIFF_DQK)
    zeros = jnp.zeros_like(sin_d)
    c_r = jnp.concatenate([cos_r, cos_r], axis=-1)
    s_r = jnp.concatenate([-sin_r, sin_r], axis=-1)
    c_d = jnp.tile(jnp.concatenate([cos_d, cos_d], axis=-1), (1, 2))
    s_lo = jnp.tile(jnp.concatenate([-sin_d, zeros], axis=-1), (1, 2))
    s_hi = jnp.tile(jnp.concatenate([zeros, sin_d], axis=-1), (1, 2))
    return c_r, s_r, c_d, s_lo, s_hi


@jax.jit
def kernel(x, c, ctx, c_ctx, w_mod, b_mod, g_pre, g_post, w_in, ret_decay_logit, mlp_w_s,
           mlp_b_s, diff_lambda_q, diff_lambda_k, w_branch_out, w_out):
    n_b, n_lat, _ = x.shape
    n_ctx = ctx.shape[1]
    tables = jnp.stack(_rope_tables(n_lat))

    cond_rows = 16
    ctx_row = n_b
    cc = jnp.zeros((cond_rows, D_MODEL), F32).at[:n_b].set(c).at[ctx_row].set(c_ctx)
    mod4 = _modulation(cc, w_mod, b_mod).reshape(DEPTH, cond_rows, 1, 3 * D_MODEL)
    lat_row = lambda b: b
    ctx_row_of = lambda b: ctx_row

    log_gamma = -jax.nn.softplus(-ret_decay_logit.astype(F32))
    lam_inits = [0.8 - 0.6 * math.exp(-0.3 * l) for l in range(DEPTH)]
    lam = (jnp.exp(jnp.sum(diff_lambda_q[:, 0] * diff_lambda_k[:, 0], axis=-1))
           - jnp.exp(jnp.sum(diff_lambda_q[:, 1] * diff_lambda_k[:, 1], axis=-1))
           + jnp.asarray(lam_inits, F32)).astype(F32)

    w_sec = w_in
    ws = mlp_w_s.astype(BF16)
    bs = jnp.broadcast_to(mlp_b_s[..., None], (DEPTH, HEADS, CHUNK, LANES)).astype(F32)
    wbo = w_branch_out.astype(BF16)
    wo = w_out.astype(BF16)
    g_pre3 = g_pre.reshape(DEPTH, 1, D_MODEL)
    g_post3 = g_post.reshape(DEPTH, 1, D_MODEL)

    for l in range(DEPTH):
        last = l == DEPTH - 1
        post_scale = 1.0 - lam_inits[l]

        ctx_flat = ctx.reshape(1, n_b * n_ctx, D_MODEL)
        if last:
            zc = _projection(ctx_flat, l, g_pre3, mod4, ctx_row_of, w_sec, lambda j: j,
                             KV_KINDS, None)
            sec_c = KV_SEC
        else:
            zc = _projection(ctx_flat, l, g_pre3, mod4, ctx_row_of, w_sec, _full_src,
                             FULL_KINDS, None)
            sec_c = FULL_SEC
            co_ret = _retention_ctx(log_gamma, l, zc, sec_c, n_b, n_ctx, rc=RET_CHUNK,
                                    heads=HEADS)
            co_diff = _diff_attention_ctx(lam, l, zc, sec_c, n_b, n_ctx, post_scale=post_scale)
            ctx_next = _merge(ctx_flat, l, mod4, ctx_row_of, g_post3,
                              co_ret.reshape(1, n_b * n_ctx, BRANCH_W),
                              co_diff.reshape(1, n_b * n_ctx, BRANCH_W), zc, sec_c,
                              ws, bs, wbo, wo, tm=MERGE_TM)
            ctx_next = ctx_next.reshape(n_b, n_ctx, D_MODEL)

        z = _projection(x, l, g_pre3, mod4, lat_row, w_sec, _full_src, FULL_KINDS, tables)
        o_ret, o_diff = _latent_mixers(lam, log_gamma, l, z, FULL_SEC, n_b, n_lat, zc, sec_c,
                                       n_ctx, ts=DIFF_TS, rc=RET_CHUNK, post_scale=post_scale)
        x = _merge(x, l, mod4, lat_row, g_post3, o_ret, o_diff, z, FULL_SEC, ws, bs, wbo, wo,
                   tm=MERGE_TM)
        if not last:
            ctx = ctx_next
    return x
```

```python
import functools
import math

import jax
import jax.numpy as jnp
from jax import lax
from jax.experimental import pallas as pl
from jax.experimental.pallas import tpu as pltpu

F32 = jnp.float32
BF16 = jnp.bfloat16

D_MODEL = 1024
DEPTH = 2
GRID_W = 64
N_BRANCH = 3
BRANCH_W = D_MODEL // 2
HEADS = 4
HEAD_W = BRANCH_W // HEADS
DIFF_DQK = HEAD_W // 2
CHUNK = 128
ROPE_BASE = 10000.0
EPS = 1e-6
RET_SCALE = HEAD_W ** -0.5
DIFF_Q_SCALE = (DIFF_DQK ** -0.5) * math.log2(math.e)

LANES = 128
SEC = BRANCH_W
MERGE_COLS = N_BRANCH * D_MODEL
KV_COLS = 4 * BRANCH_W
IN_COLS = KV_COLS + 7 * BRANCH_W + MERGE_COLS
VMEM_LIMIT = 56 * 1024 * 1024

PROJ_SUB = 256
PROJ_OUT_SLOTS = 2
DIFF_TS = 256
RET_CHUNK = 256
MERGE_TM = 512
MERGE_OUT_SUB = 256

FULL_KINDS = ("sigmoid",) * 6 + ("rk", "plain", "dk", "plain", "rq", "silu", "dq", "silu",
                                 "plain", "ln", "silu")
FULL_SEC = dict(rk=6, rv=7, dk=8, dv=9, rq=10, rg=11, dq=12, dg=13, mu=14, mv=15, mg=16)
KV_KINDS = ("rk", "plain", "dk", "plain")
KV_SEC = dict(rk=0, rv=1, dk=2, dv=3)
_N_MERGE_SEC = MERGE_COLS // SEC
_N_REST_SEC = (KV_COLS + 7 * BRANCH_W) // SEC


def _full_src(j):
    return jnp.where(j < _N_MERGE_SEC, j + _N_REST_SEC, j - _N_MERGE_SEC)


def _dot(a, b):
    return jnp.dot(a, b, preferred_element_type=F32)


def _dot_nt(a, b):
    return lax.dot_general(a, b, (((1,), (1,)), ((), ())), preferred_element_type=F32)


def _rms(v):
    return v * lax.rsqrt(jnp.mean(v * v, axis=-1, keepdims=True) + EPS)


def _mod_spec(l, row_of, k):
    return pl.BlockSpec((None, None, 1, D_MODEL), lambda *idx: (l, row_of(idx[0]), 0, k))


def _layer_spec(l, shape, **kwargs):
    zeros = (0,) * len(shape)
    return pl.BlockSpec((None,) + tuple(shape), lambda *idx: (l,) + zeros, **kwargs)


def _z_spec(sec_idx, rows, width, flat, col_of=lambda *idx: 0, row_of=None):
    if row_of is None:
        row_of = (lambda *idx: idx[0]) if flat else (lambda *idx: 0)
    group_of = (lambda *idx: 0) if flat else (lambda *idx: idx[0])
    return pl.BlockSpec((1, None, rows, width),
                        lambda *idx: (group_of(*idx), sec_idx, row_of(*idx), col_of(*idx)))


def _split_bf16(a):
    hi = a.astype(BF16)
    return hi, (a - hi.astype(F32)).astype(BF16)


def _mod_kernel(c_ref, w_ref, b_ref, o_ref):
    cv = c_ref[...]
    c_hi, c_lo = _split_bf16(cv * jax.nn.sigmoid(cv))
    w_hi, w_lo = _split_bf16(w_ref[0])
    o_ref[0] = _dot(c_hi, w_hi) + _dot(c_hi, w_lo) + _dot(c_lo, w_hi) + b_ref[0]


def _modulation(cc, w_mod, b_mod):
    rows = cc.shape[0]
    return pl.pallas_call(
        _mod_kernel,
        grid=(DEPTH, 3),
        in_specs=[
            pl.BlockSpec((rows, D_MODEL), lambda l, j: (0, 0)),
            pl.BlockSpec((1, D_MODEL, D_MODEL), lambda l, j: (l, 0, j)),
            pl.BlockSpec((1, 1, D_MODEL), lambda l, j: (l, 0, j)),
        ],
        out_specs=pl.BlockSpec((1, rows, D_MODEL), lambda l, j: (l, 0, j)),
        out_shape=jax.ShapeDtypeStruct((DEPTH, rows, 3 * D_MODEL), F32),
        compiler_params=pltpu.CompilerParams(vmem_limit_bytes=VMEM_LIMIT),
        name="modulation",
    )(cc, w_mod, b_mod.reshape(DEPTH, 1, 3 * D_MODEL))


def _rope_ret(a, c, s):
    outs = []
    for h in range(HEADS):
        ah = a[:, h * LANES:(h + 1) * LANES]
        outs.append(ah * c + pltpu.roll(ah, LANES // 2, 1) * s)
    return jnp.concatenate(outs, axis=1)


def _rope_diff(a, c, s_lo, s_hi):
    outs = []
    for h in range(HEADS):
        ah = a[:, h * LANES:(h + 1) * LANES]
        outs.append(ah * c + pltpu.roll(ah, LANES - DIFF_DQK // 2, 1) * s_lo
                    + pltpu.roll(ah, DIFF_DQK // 2, 1) * s_hi)
    return jnp.concatenate(outs, axis=1)


def _proj_kernel(*refs, layer, kinds, src_of, n_src, rope, n_rows, sub):
    if rope:
        (x_ref, g_ref, shift_ref, scale_ref, w_hbm, tab_hbm, z_hbm,
         h_ref, stage, w_vmem, w_stage, sems_out, sems_w, tab_vmem, sem_tab) = refs
        cr_ref, sr_ref, cd_ref, sdl_ref, sdh_ref = [tab_vmem.at[i] for i in range(5)]
    else:
        (x_ref, g_ref, shift_ref, scale_ref, w_hbm, z_hbm,
         h_ref, stage, w_vmem, w_stage, sems_out, sems_w) = refs
    b = pl.program_id(0)
    first = b == 0
    n_sub = n_rows // sub
    n_sec = len(kinds)
    n_slots = stage.shape[0]

    n_wslots = w_stage.shape[0]

    def weight_copy(j):
        start = src_of(j) * SEC
        cols = pl.ds(start if isinstance(start, int) else pl.multiple_of(start, SEC), SEC)
        slot = j % n_wslots
        return pltpu.make_async_copy(w_hbm.at[layer, :, cols], w_stage.at[slot], sems_w.at[slot])

    def load_weights(j):
        weight_copy(j).wait()
        src = w_stage.at[j % n_wslots]
        dst = w_vmem.at[src_of(j)]
        for r in range(D_MODEL // LANES):
            rows = slice(r * LANES, (r + 1) * LANES)
            dst[rows, :] = src[rows, :].astype(BF16)

        @pl.when(j + n_wslots < n_sec)
        def _():
            weight_copy(j + n_wslots).start()

    def table_copy():
        return pltpu.make_async_copy(tab_hbm, tab_vmem, sem_tab.at[0])

    @pl.when(first)
    def _():
        for j in range(min(n_wslots, n_sec)):
            weight_copy(j).start()
        if rope:
            table_copy().start()

    gs = g_ref[...] * (1.0 + scale_ref[...])
    sh = shift_ref[...]
    for r in range(n_sub):
        xs = x_ref[0, r * sub:(r + 1) * sub, :]
        h_ref[r * sub:(r + 1) * sub, :] = (_rms(xs) * gs + sh).astype(BF16)

    if rope:
        pl.when(first)(lambda: table_copy().wait())

    def epi_rk(a, rows):
        if rope:
            a = _rope_ret(a, cr_ref[rows, :], sr_ref[rows, :])
        return a * RET_SCALE

    def epi_rq(a, rows):
        return _rope_ret(a, cr_ref[rows, :], sr_ref[rows, :]) if rope else a

    def epi_dk(a, rows):
        return _rope_diff(a, cd_ref[rows, :], sdl_ref[rows, :], sdh_ref[rows, :]) if rope else a

    def epi_dq(a, rows):
        return epi_dk(a, rows) * DIFF_Q_SCALE

    def epi_ln(a, rows):
        mu = jnp.mean(a, axis=-1, keepdims=True)
        d = a - mu
        return d * lax.rsqrt(jnp.mean(d * d, axis=-1, keepdims=True) + EPS)

    epilogues = dict(
        plain=lambda a, rows: a,
        sigmoid=lambda a, rows: jax.nn.sigmoid(a),
        silu=lambda a, rows: a * jax.nn.sigmoid(a),
        rk=epi_rk, rq=epi_rq, dk=epi_dk, dq=epi_dq, ln=epi_ln)

    def writeback(j):
        slot = j % n_slots
        return pltpu.make_async_copy(stage.at[slot], z_hbm.at[b, j], sems_out.at[slot])

    def run(epilogue, j):
        w_sec = w_vmem.at[src_of(j)]
        out = stage.at[j % n_slots]
        for r in range(n_sub):
            rows = slice(r * sub, (r + 1) * sub)
            out[rows, :] = epilogue(_dot(h_ref[rows, :], w_sec[...]), rows).astype(BF16)

    def section(j, carry):
        pl.when(first)(functools.partial(load_weights, j))

        @pl.when(j >= n_slots)
        def _():
            writeback(j - n_slots).wait()

        for kind in sorted(set(kinds)):
            idx = [i for i, k in enumerate(kinds) if k == kind]
            cond = j == idx[0]
            for i in idx[1:]:
                cond = cond | (j == i)
            pl.when(cond)(functools.partial(run, epilogues[kind], j))
        writeback(j).start()
        return carry

    lax.fori_loop(0, n_sec, section, 0)
    for j in range(max(0, n_sec - n_slots), n_sec):
        writeback(j).wait()


def _projection(xs, l, g_pre, mod4, row_of, w_all, src_of, kinds, tables):
    n_g, n_rows, _ = xs.shape
    n_sec = len(kinds)
    rope = tables is not None
    n_src = w_all.shape[2] // SEC if src_of is _full_src else n_sec
    in_specs = [
        pl.BlockSpec((1, n_rows, D_MODEL), lambda b: (b, 0, 0)),
        _layer_spec(l, (1, D_MODEL)),
        _mod_spec(l, row_of, 0),
        _mod_spec(l, row_of, 1),
        pl.BlockSpec(memory_space=pl.ANY),
    ]
    args = [xs, g_pre, mod4, mod4, w_all]
    scratch = [pltpu.VMEM((n_rows, D_MODEL), BF16),
               pltpu.VMEM((PROJ_OUT_SLOTS, n_rows, SEC), BF16),
               pltpu.VMEM((n_src, D_MODEL, SEC), BF16),
               pltpu.VMEM((2, D_MODEL, SEC), F32),
               pltpu.SemaphoreType.DMA((PROJ_OUT_SLOTS,)),
               pltpu.SemaphoreType.DMA((2,))]
    if rope:
        in_specs.append(pl.BlockSpec(memory_space=pl.ANY))
        args.append(tables)
        scratch += [pltpu.VMEM(tables.shape, F32), pltpu.SemaphoreType.DMA((1,))]
    return pl.pallas_call(
        functools.partial(_proj_kernel, layer=l, kinds=kinds, src_of=src_of, n_src=n_src,
                          rope=rope, n_rows=n_rows, sub=PROJ_SUB),
        grid=(n_g,),
        in_specs=in_specs,
        out_specs=pl.BlockSpec(memory_space=pl.ANY),
        out_shape=jax.ShapeDtypeStruct((n_g, n_sec, n_rows, SEC), BF16),
        scratch_shapes=scratch,
        compiler_params=pltpu.CompilerParams(vmem_limit_bytes=VMEM_LIMIT),
        name="projection_rope" if rope else "projection",
    )(*args)


def _retention_tasks(lgf, lgb, cs, q_ref, g_ref, k_ref, v_ref, kc_ref, vc_ref, o_ref, inc_ref,
                     st_ref, *, n, n_ctx, rc):
    nc = n // rc
    pos = lax.broadcasted_iota(jnp.int32, (rc, LANES), 0).astype(F32)
    kdf = jnp.exp(lgf * (rc - 1.0 - pos)).astype(BF16)
    kdb = jnp.exp(lgb * pos).astype(BF16)
    cdf = jnp.exp(jnp.full((1, LANES), lgf * rc, F32))
    cdb = jnp.exp(jnp.full((1, LANES), lgb * rc, F32))
    qdf = jnp.exp(lgf * (pos + 1.0)).astype(BF16)
    qdb = jnp.exp(lgb * (rc - pos)).astype(BF16)
    ii = lax.broadcasted_iota(jnp.int32, (rc, rc), 0)
    jj = lax.broadcasted_iota(jnp.int32, (rc, rc), 1)
    dist = (ii - jj).astype(F32)
    dmat = (jnp.where(dist >= 0, jnp.exp(lgf * jnp.maximum(dist, 0.0)), 0.0)
            + jnp.where(dist <= 0, jnp.exp(lgb * jnp.maximum(-dist, 0.0)), 0.0))
    state = {}

    def increment(kr, vr, nn):
        rows = slice(nn * rc, (nn + 1) * rc)
        kk = kr[0, rows, cs]
        kcat = jnp.concatenate([kk * kdf, kk * kdb], axis=1)
        vt = vr[0, rows, cs].astype(F32).T.astype(BF16)
        return _dot(vt, kcat)

    def init_task():
        sf = jnp.zeros((HEAD_W, HEAD_W), F32)
        sb = jnp.zeros((HEAD_W, HEAD_W), F32)
        if n_ctx:
            incs = [increment(kc_ref, vc_ref, nn) for nn in range(n_ctx // rc)]
            for inc in incs:
                sf = cdf * sf + inc[:, :HEAD_W]
            for inc in reversed(incs):
                sb = cdb * sb + inc[:, HEAD_W:]
        state["sf"], state["sb"] = sf, sb

    def inc_task(nn):
        inc_ref[nn] = increment(k_ref, v_ref, nn)

    def scan_task():
        sf, sb = state["sf"], state["sb"]
        for nn in range(nc):
            st_ref[nn, :, 0:HEAD_W] = sf.astype(BF16)
            sf = cdf * sf + inc_ref[nn, :, 0:HEAD_W]
        for nn in reversed(range(nc)):
            st_ref[nn, :, HEAD_W:] = sb.astype(BF16)
            sb = cdb * sb + inc_ref[nn, :, HEAD_W:]

    def out_task(nn):
        rows = slice(nn * rc, (nn + 1) * rc)
        q = q_ref[0, rows, cs]
        s = _dot_nt(q, k_ref[0, rows, cs]) * dmat
        inner = _dot(s.astype(BF16), v_ref[0, rows, cs])
        qcat = jnp.concatenate([q * qdf, q * qdb], axis=1)
        out = inner + _dot_nt(qcat, st_ref[nn])
        o_ref[0, rows, cs] = _rms(out).astype(BF16) * g_ref[0, rows, cs]

    inc_tasks = [init_task] + [functools.partial(inc_task, nn) for nn in range(nc)]
    return inc_tasks, scan_task, [functools.partial(out_task, nn) for nn in range(nc)]


def _ret_kernel(lg_ref, q_ref, g_ref, k_ref, v_ref, o_ref, inc_ref, st_ref, *, layer, n, rc,
                heads):
    tasks = []
    for hh in range(heads):
        head = pl.program_id(1) * heads + hh
        tasks.append(_retention_tasks(
            lg_ref[layer, 0, head], lg_ref[layer, 1, head], slice(hh * LANES, (hh + 1) * LANES),
            q_ref, g_ref, k_ref, v_ref, None, None, o_ref, inc_ref.at[hh], st_ref.at[hh],
            n=n, n_ctx=0, rc=rc))
    for i in range(len(tasks[0][0])):
        for inc_tasks, _, _ in tasks:
            inc_tasks[i]()
    for _, scan_task, _ in tasks:
        scan_task()
    for i in range(len(tasks[0][2])):
        for _, _, out_tasks in tasks:
            out_tasks[i]()


def _retention_ctx(log_gamma, layer, zc, sec, n_b, n, *, rc, heads):
    width = heads * LANES
    head_block = lambda b, h: h
    return pl.pallas_call(
        functools.partial(_ret_kernel, layer=layer, n=n, rc=rc, heads=heads),
        grid=(n_b, HEADS // heads),
        in_specs=[pl.BlockSpec(memory_space=pltpu.SMEM)]
        + [_z_spec(sec[name], n, width, True, col_of=head_block)
           for name in ("rq", "rg", "rk", "rv")],
        out_specs=pl.BlockSpec((1, n, width), lambda b, h: (b, 0, h)),
        out_shape=jax.ShapeDtypeStruct((n_b, n, BRANCH_W), BF16),
        scratch_shapes=[pltpu.VMEM((heads, n // rc, HEAD_W, 2 * HEAD_W), F32),
                        pltpu.VMEM((heads, n // rc, HEAD_W, 2 * HEAD_W), BF16)],
        compiler_params=pltpu.CompilerParams(vmem_limit_bytes=VMEM_LIMIT),
        name="retention_ctx",
    )(log_gamma, zc, zc, zc, zc)


def _split_maps(q, lane):
    zero = jnp.zeros_like(q)
    return jnp.concatenate([jnp.where(lane < DIFF_DQK, q, zero),
                            jnp.where(lane >= DIFF_DQK, q, zero)], axis=0)


def _diff_ctx_kernel(lam_ref, q_ref, g_ref, k_ref, v_ref, o_ref, *, layer, tq, post_scale):
    lane = lax.broadcasted_iota(jnp.int32, (tq, LANES), 1)
    for h in range(HEADS):
        cs = slice(h * LANES, (h + 1) * LANES)
        s = _dot_nt(_split_maps(q_ref[0, :, cs], lane), k_ref[0, :, cs])
        p = jnp.exp2(s - jnp.max(s, axis=-1, keepdims=True))
        o = _dot(p.astype(BF16), v_ref[0, :, cs]) / jnp.sum(p, axis=-1, keepdims=True)
        d = o[:tq] - lam_ref[layer] * o[tq:]
        o_ref[0, :, cs] = (_rms(d) * post_scale * g_ref[0, :, cs].astype(F32)).astype(BF16)


def _diff_attention_ctx(lam, layer, zc, sec, n_b, n, *, post_scale):
    return pl.pallas_call(
        functools.partial(_diff_ctx_kernel, layer=layer, tq=n, post_scale=post_scale),
        grid=(n_b,),
        in_specs=[pl.BlockSpec(memory_space=pltpu.SMEM)]
        + [_z_spec(sec[name], n, SEC, True) for name in ("dq", "dg", "dk", "dv")],
        out_specs=pl.BlockSpec((1, n, BRANCH_W), lambda b: (b, 0, 0)),
        out_shape=jax.ShapeDtypeStruct((n_b, n, BRANCH_W), BF16),
        compiler_params=pltpu.CompilerParams(vmem_limit_bytes=VMEM_LIMIT),
        name="diff_attention_ctx",
    )(lam, zc, zc, zc, zc)


def _mixer_kernel(lam_ref, lg_ref, q_ref, g_ref, kc_ref, vc_ref, kl_ref, vl_ref,
                  rq_ref, rg_ref, rk_ref, rv_ref, rkc_ref, rvc_ref, od_ref, or_ref,
                  kall, vt, s0, s1, p0, p1, d0, d1, inc_ref, st_ref, *, layer, n, n_ctx, ts,
                  rc, post_scale):
    n_sub = n // ts
    nk = kall.shape[0]
    kall[0:n_ctx, :] = kc_ref[0]
    kall[n_ctx:, :] = kl_ref[0]
    for c in range(nk // LANES):
        src, base = (vc_ref, 0) if c * LANES < n_ctx else (vl_ref, n_ctx)
        rows = slice(c * LANES - base, (c + 1) * LANES - base)
        vt[0:LANES, c * LANES:(c + 1) * LANES] = src[0, rows, :].astype(F32).T.astype(BF16)
    vt[LANES:, :] = jnp.ones((vt.shape[0] - LANES, nk), BF16)
    lam = lam_ref[layer]
    lane = lax.broadcasted_iota(jnp.int32, (ts, LANES), 1)
    s_bufs, p_bufs, d_bufs = (s0, s1), (p0, p1), (d0, d1)

    def stage_a(t):
        rows = slice(t * ts, (t + 1) * ts)
        s = _dot_nt(kall[...], _split_maps(q_ref[0, rows, :], lane))
        s_bufs[t % 2][...] = s
        d_bufs[t % 2][...] = jnp.max(s, axis=0, keepdims=True)

    def stage_b(t):
        p_bufs[t % 2][...] = jnp.exp2(s_bufs[t % 2][...] - d_bufs[t % 2][...]).astype(BF16)

    def stage_c(t):
        rows = slice(t * ts, (t + 1) * ts)
        oe = _dot(vt[...], p_bufs[t % 2][...])
        ot = oe[:LANES] / oe[LANES:LANES + 1]
        dt = ot[:, :ts] - lam * ot[:, ts:]
        dt = dt * lax.rsqrt(jnp.mean(dt * dt, axis=0, keepdims=True) + EPS)
        od_ref[0, rows, :] = (dt.T * post_scale * g_ref[0, rows, :].astype(F32)).astype(BF16)

    head = pl.program_id(1)
    inc_tasks, scan_task, out_tasks = _retention_tasks(
        lg_ref[layer, 0, head], lg_ref[layer, 1, head], slice(0, LANES),
        rq_ref, rg_ref, rk_ref, rv_ref, rkc_ref, rvc_ref, or_ref, inc_ref, st_ref,
        n=n, n_ctx=n_ctx, rc=rc)
    n_steps = n_sub + 2
    extra = [[] for _ in range(n_steps)]
    half = (len(inc_tasks) + 1) // 2
    extra[0] += inc_tasks[:half]
    extra[1] += inc_tasks[half:]
    extra[2].append(scan_task)
    for i, task in enumerate(out_tasks):
        extra[2 + (i * (n_steps - 2)) // len(out_tasks)].append(task)

    for t in range(n_steps):
        if t >= 2:
            stage_c(t - 2)
        if 1 <= t <= n_sub:
            stage_b(t - 1)
        if t < n_sub:
            stage_a(t)
        for task in extra[t]:
            task()


def _latent_mixers(lam, log_gamma, layer, z, sec, n_b, n, zc, sec_c, n_ctx, *, ts, rc,
                   post_scale):
    nk = n_ctx + n
    head_block = lambda b, h: h
    lat = lambda name: _z_spec(sec[name], n, LANES, False, col_of=head_block)
    cx = lambda name: _z_spec(sec_c[name], n_ctx, LANES, True, col_of=head_block)
    out_spec = pl.BlockSpec((1, n, LANES), lambda b, h: (b, 0, h))
    out_shape = jax.ShapeDtypeStruct((n_b, n, BRANCH_W), BF16)
    o_diff, o_ret = pl.pallas_call(
        functools.partial(_mixer_kernel, layer=layer, n=n, n_ctx=n_ctx, ts=ts, rc=rc,
                          post_scale=post_scale),
        grid=(n_b, HEADS),
        in_specs=[pl.BlockSpec(memory_space=pltpu.SMEM), pl.BlockSpec(memory_space=pltpu.SMEM),
                  lat("dq"), lat("dg"), cx("dk"), cx("dv"), lat("dk"), lat("dv"),
                  lat("rq"), lat("rg"), lat("rk"), lat("rv"), cx("rk"), cx("rv")],
        out_specs=[out_spec, out_spec],
        out_shape=[out_shape, out_shape],
        scratch_shapes=[pltpu.VMEM((nk, LANES), BF16), pltpu.VMEM((LANES + 16, nk), BF16),
                        pltpu.VMEM((nk, 2 * ts), F32), pltpu.VMEM((nk, 2 * ts), F32),
                        pltpu.VMEM((nk, 2 * ts), BF16), pltpu.VMEM((nk, 2 * ts), BF16),
                        pltpu.VMEM((1, 2 * ts), F32), pltpu.VMEM((1, 2 * ts), F32),
                        pltpu.VMEM((n // rc, HEAD_W, 2 * HEAD_W), F32),
                        pltpu.VMEM((n // rc, HEAD_W, 2 * HEAD_W), BF16)],
        compiler_params=pltpu.CompilerParams(vmem_limit_bytes=VMEM_LIMIT),
        name="latent_mixers",
    )(lam, log_gamma, z, z, zc, zc, z, z, z, z, z, z, zc, zc)
    return o_ret, o_diff


def _merge_kernel(x_ref, gate_ref, gpost_ref, oret_ref, odiff_ref, u_ref, vn_ref, mg_ref,
                  gr_ref, gm_ref, gd_ref, ws_ref, bs_ref, wbo_ref, wo_ref, out_ref, *, tm, out_sub):
    def gated(gate2_ref, rs, o, w):
        gate = jnp.concatenate([gate2_ref[0, 0, rs, :], gate2_ref[0, 1, rs, :]], axis=1)
        return gate.astype(F32) * _dot(o, w)

    for r in range(tm // out_sub):
        rs = slice(r * out_sub, (r + 1) * out_sub)
        rows = []
        for c in range(out_sub // CHUNK):
            cs = slice(r * out_sub + c * CHUNK, r * out_sub + (c + 1) * CHUNK)
            cols = []
            for g in range(HEADS):
                cols.append(_dot(ws_ref[g], vn_ref[0, cs, g * LANES:(g + 1) * LANES]) + bs_ref[g])
            rows.append(jnp.concatenate(cols, axis=1))
        sp = jnp.concatenate(rows, axis=0)
        o_mlp = (u_ref[0, rs, :].astype(F32) * sp * mg_ref[0, rs, :].astype(F32)).astype(BF16)
        t = (gated(gr_ref, rs, oret_ref[0, rs, :], wbo_ref[0])
             + gated(gm_ref, rs, o_mlp, wbo_ref[1])
             + gated(gd_ref, rs, odiff_ref[0, rs, :], wbo_ref[2]))
        y = _dot(t.astype(BF16), wo_ref[...])
        out_ref[0, rs, :] = x_ref[0, rs, :] + gate_ref[...] * (_rms(y) * gpost_ref[...])


def _merge(xs, l, mod4, row_of, g_post, o_ret, o_diff, z, sec, ws, bs, wbo, wo, *, tm):
    n_b, n, _ = xs.shape
    row_block = lambda b, i: i

    def zsec(name):
        return _z_spec(sec[name], tm, SEC, False, row_of=row_block)

    def zgate(k):
        return pl.BlockSpec((1, 2, tm, SEC), lambda b, i: (b, k, i, 0))

    in_specs = [
        pl.BlockSpec((1, tm, D_MODEL), lambda b, i: (b, i, 0)),
        _mod_spec(l, row_of, 2),
        _layer_spec(l, (1, D_MODEL)),
        pl.BlockSpec((1, tm, BRANCH_W), lambda b, i: (b, i, 0)),
        pl.BlockSpec((1, tm, BRANCH_W), lambda b, i: (b, i, 0)),
        zsec("mu"), zsec("mv"), zsec("mg"), zgate(0), zgate(1), zgate(2),
        _layer_spec(l, (HEADS, CHUNK, CHUNK)),
        _layer_spec(l, (HEADS, CHUNK, LANES)),
        _layer_spec(l, (N_BRANCH, BRANCH_W, D_MODEL)),
        _layer_spec(l, (D_MODEL, D_MODEL)),
    ]
    return pl.pallas_call(
        functools.partial(_merge_kernel, tm=tm, out_sub=MERGE_OUT_SUB),
        grid=(n_b, n // tm),
        in_specs=in_specs,
        out_specs=pl.BlockSpec((1, tm, D_MODEL), lambda b, i: (b, i, 0)),
        out_shape=jax.ShapeDtypeStruct(xs.shape, F32),
        compiler_params=pltpu.CompilerParams(vmem_limit_bytes=VMEM_LIMIT),
        name="merge",
    )(xs, mod4, g_post, o_ret, o_diff, z, z, z, z, z, z, ws, bs, wbo, wo)


def _rope_tables(n_lat):
    rows = n_lat // GRID_W
    row_pos = jnp.repeat(jnp.arange(rows, dtype=F32), GRID_W)
    col_pos = jnp.tile(jnp.arange(GRID_W, dtype=F32), rows)

    def angles(head_dim):
        n_freq = head_dim // 4
        inv = ROPE_BASE ** (-jnp.arange(n_freq, dtype=F32) / n_freq)
        ang = jnp.concatenate([row_pos[:, None] * inv, col_pos[:, None] * inv], axis=-1)
        return jnp.cos(ang), jnp.sin(ang)

    cos_r, sin_r = angles(HEAD_W)
    cos_d, sin_d = angles(DIFF_DQK)
    zeros = jnp.zeros_like(sin_d)
    c_r = jnp.concatenate([cos_r, cos_r], axis=-1)
    s_r = jnp.concatenate([-sin_r, sin_r], axis=-1)
    c_d = jnp.tile(jnp.concatenate([cos_d, cos_d], axis=-1), (1, 2))
    s_lo = jnp.tile(jnp.concatenate([-sin_d, zeros], axis=-1), (1, 2))
    s_hi = jnp.tile(jnp.concatenate([zeros, sin_d], axis=-1), (1, 2))
    return c_r, s_r, c_d, s_lo, s_hi


@jax.jit
def kernel(x, c, ctx, c_ctx, w_mod, b_mod, g_pre, g_post, w_in, ret_decay_logit, mlp_w_s,
           mlp_b_s, diff_lambda_q, diff_lambda_k, w_branch_out, w_out):
    n_b, n_lat, _ = x.shape
    n_ctx = ctx.shape[1]
    tables = jnp.stack(_rope_tables(n_lat))

    cond_rows = 16
    ctx_row = n_b
    cc = jnp.zeros((cond_rows, D_MODEL), F32).at[:n_b].set(c).at[ctx_row].set(c_ctx)
    mod4 = _modulation(cc, w_mod, b_mod).reshape(DEPTH, cond_rows, 1, 3 * D_MODEL)
    lat_row = lambda b: b
    ctx_row_of = lambda b: ctx_row

    log_gamma = -jax.nn.softplus(-ret_decay_logit.astype(F32))
    lam_inits = [0.8 - 0.6 * math.exp(-0.3 * l) for l in range(DEPTH)]
    lam = (jnp.exp(jnp.sum(diff_lambda_q[:, 0] * diff_lambda_k[:, 0], axis=-1))
           - jnp.exp(jnp.sum(diff_lambda_q[:, 1] * diff_lambda_k[:, 1], axis=-1))
           + jnp.asarray(lam_inits, F32)).astype(F32)

    w_sec = w_in
    ws = mlp_w_s.astype(BF16)
    bs = jnp.broadcast_to(mlp_b_s[..., None], (DEPTH, HEADS, CHUNK, LANES)).astype(F32)
    wbo = w_branch_out.astype(BF16)
    wo = w_out.astype(BF16)
    g_pre3 = g_pre.reshape(DEPTH, 1, D_MODEL)
    g_post3 = g_post.reshape(DEPTH, 1, D_MODEL)

    for l in range(DEPTH):
        last = l == DEPTH - 1
        post_scale = 1.0 - lam_inits[l]

        ctx_flat = ctx.reshape(1, n_b * n_ctx, D_MODEL)
        if last:
            zc = _projection(ctx_flat, l, g_pre3, mod4, ctx_row_of, w_sec, lambda j: j,
                             KV_KINDS, None)
            sec_c = KV_SEC
        else:
            zc = _projection(ctx_flat, l, g_pre3, mod4, ctx_row_of, w_sec, _full_src,
                             FULL_KINDS, None)
            sec_c = FULL_SEC
            co_ret = _retention_ctx(log_gamma, l, zc, sec_c, n_b, n_ctx, rc=RET_CHUNK,
                                    heads=HEADS)
            co_diff = _diff_attention_ctx(lam, l, zc, sec_c, n_b, n_ctx, post_scale=post_scale)
            ctx_next = _merge(ctx_flat, l, mod4, ctx_row_of, g_post3,
                              co_ret.reshape(1, n_b * n_ctx, BRANCH_W),
                              co_diff.reshape(1, n_b * n_ctx, BRANCH_W), zc, sec_c,
                              ws, bs, wbo, wo, tm=MERGE_TM)
            ctx_next = ctx_next.reshape(n_b, n_ctx, D_MODEL)

        z = _projection(x, l, g_pre3, mod4, lat_row, w_sec, _full_src, FULL_KINDS, tables)
        o_ret, o_diff = _latent_mixers(lam, log_gamma, l, z, FULL_SEC, n_b, n_lat, zc, sec_c,
                                       n_ctx, ts=DIFF_TS, rc=RET_CHUNK, post_scale=post_scale)
        x = _merge(x, l, mod4, lat_row, g_post3, o_ret, o_diff, z, FULL_SEC, ws, bs, wbo, wo,
                   tm=MERGE_TM)
        if not last:
            ctx = ctx_next
    return x
```

```python
import functools
import math

import jax
import jax.numpy as jnp
from jax import lax
from jax.experimental import pallas as pl
from jax.experimental.pallas import tpu as pltpu

F32 = jnp.float32
BF16 = jnp.bfloat16

D_MODEL = 1024
DEPTH = 2
GRID_W = 64
N_BRANCH = 3
BRANCH_W = D_MODEL // 2
HEADS = 4
HEAD_W = BRANCH_W // HEADS
DIFF_DQK = HEAD_W // 2
CHUNK = 128
ROPE_BASE = 10000.0
EPS = 1e-6
RET_SCALE = HEAD_W ** -0.5
DIFF_Q_SCALE = (DIFF_DQK ** -0.5) * math.log2(math.e)

LANES = 128
SEC = BRANCH_W
MERGE_COLS = N_BRANCH * D_MODEL
KV_COLS = 4 * BRANCH_W
IN_COLS = KV_COLS + 7 * BRANCH_W + MERGE_COLS
VMEM_LIMIT = 56 * 1024 * 1024

PROJ_SUB = 256
PROJ_OUT_SLOTS = 2
DIFF_TS = 256
RET_CHUNK = 256
MERGE_TM = 512
MERGE_OUT_SUB = 512

FULL_KINDS = ("sigmoid",) * 6 + ("rk", "plain", "dk", "plain", "rq", "silu", "dq", "silu",
                                 "plain", "ln", "silu")
FULL_SEC = dict(rk=6, rv=7, dk=8, dv=9, rq=10, rg=11, dq=12, dg=13, mu=14, mv=15, mg=16)
KV_KINDS = ("rk", "plain", "dk", "plain")
KV_SEC = dict(rk=0, rv=1, dk=2, dv=3)
_N_MERGE_SEC = MERGE_COLS // SEC
_N_REST_SEC = (KV_COLS + 7 * BRANCH_W) // SEC


def _full_src(j):
    return jnp.where(j < _N_MERGE_SEC, j + _N_REST_SEC, j - _N_MERGE_SEC)


def _dot(a, b):
    return jnp.dot(a, b, preferred_element_type=F32)


def _dot_nt(a, b):
    return lax.dot_general(a, b, (((1,), (1,)), ((), ())), preferred_element_type=F32)


def _rms(v):
    return v * lax.rsqrt(jnp.mean(v * v, axis=-1, keepdims=True) + EPS)


def _mod_spec(l, row_of, k):
    return pl.BlockSpec((None, None, 1, D_MODEL), lambda *idx: (l, row_of(idx[0]), 0, k))


def _layer_spec(l, shape, **kwargs):
    zeros = (0,) * len(shape)
    return pl.BlockSpec((None,) + tuple(shape), lambda *idx: (l,) + zeros, **kwargs)


def _z_spec(sec_idx, rows, width, flat, col_of=lambda *idx: 0, row_of=None):
    if row_of is None:
        row_of = (lambda *idx: idx[0]) if flat else (lambda *idx: 0)
    group_of = (lambda *idx: 0) if flat else (lambda *idx: idx[0])
    return pl.BlockSpec((1, None, rows, width),
                        lambda *idx: (group_of(*idx), sec_idx, row_of(*idx), col_of(*idx)))


def _split_bf16(a):
    hi = a.astype(BF16)
    return hi, (a - hi.astype(F32)).astype(BF16)


def _mod_kernel(c_ref, w_ref, b_ref, o_ref):
    cv = c_ref[...]
    c_hi, c_lo = _split_bf16(cv * jax.nn.sigmoid(cv))
    w_hi, w_lo = _split_bf16(w_ref[0])
    o_ref[0] = _dot(c_hi, w_hi) + _dot(c_hi, w_lo) + _dot(c_lo, w_hi) + b_ref[0]


def _modulation(cc, w_mod, b_mod):
    rows = cc.shape[0]
    return pl.pallas_call(
        _mod_kernel,
        grid=(DEPTH, 3),
        in_specs=[
            pl.BlockSpec((rows, D_MODEL), lambda l, j: (0, 0)),
            pl.BlockSpec((1, D_MODEL, D_MODEL), lambda l, j: (l, 0, j)),
            pl.BlockSpec((1, 1, D_MODEL), lambda l, j: (l, 0, j)),
        ],
        out_specs=pl.BlockSpec((1, rows, D_MODEL), lambda l, j: (l, 0, j)),
        out_shape=jax.ShapeDtypeStruct((DEPTH, rows, 3 * D_MODEL), F32),
        compiler_params=pltpu.CompilerParams(vmem_limit_bytes=VMEM_LIMIT),
        name="modulation",
    )(cc, w_mod, b_mod.reshape(DEPTH, 1, 3 * D_MODEL))


def _rope_ret(a, c, s):
    outs = []
    for h in range(HEADS):
        ah = a[:, h * LANES:(h + 1) * LANES]
        outs.append(ah * c + pltpu.roll(ah, LANES // 2, 1) * s)
    return jnp.concatenate(outs, axis=1)


def _rope_diff(a, c, s_lo, s_hi):
    outs = []
    for h in range(HEADS):
        ah = a[:, h * LANES:(h + 1) * LANES]
        outs.append(ah * c + pltpu.roll(ah, LANES - DIFF_DQK // 2, 1) * s_lo
                    + pltpu.roll(ah, DIFF_DQK // 2, 1) * s_hi)
    return jnp.concatenate(outs, axis=1)


def _proj_kernel(*refs, layer, kinds, src_of, n_src, rope, n_rows, sub, skew):
    if rope:
        (x_ref, g_ref, shift_ref, scale_ref, w_hbm, tab_hbm, z_hbm,
         h_ref, stage, w_vmem, w_stage, sems_out, sems_w, tab_vmem, sem_tab) = refs
        cr_ref, sr_ref, cd_ref, sdl_ref, sdh_ref = [tab_vmem.at[i] for i in range(5)]
    else:
        (x_ref, g_ref, shift_ref, scale_ref, w_hbm, z_hbm,
         h_ref, stage, w_vmem, w_stage, sems_out, sems_w) = refs
    step = pl.program_id(0)
    if skew:
        b = step - 1
        first = step == 1
        cur, nxt = (step + 1) % 2, step % 2
    else:
        b, first, cur, nxt = step, step == 0, 0, 0
    n_sub = n_rows // sub
    n_sec = len(kinds)
    n_slots = stage.shape[0]

    n_wslots = w_stage.shape[0]

    def weight_copy(j):
        start = src_of(j) * SEC
        cols = pl.ds(start if isinstance(start, int) else pl.multiple_of(start, SEC), SEC)
        slot = j % n_wslots
        return pltpu.make_async_copy(w_hbm.at[layer, :, cols], w_stage.at[slot], sems_w.at[slot])

    def load_weights(j):
        weight_copy(j).wait()
        src = w_stage.at[j % n_wslots]
        dst = w_vmem.at[src_of(j)]
        for r in range(D_MODEL // LANES):
            rows = slice(r * LANES, (r + 1) * LANES)
            dst[rows, :] = src[rows, :].astype(BF16)

        @pl.when(j + n_wslots < n_sec)
        def _():
            weight_copy(j + n_wslots).start()

    def table_copy():
        return pltpu.make_async_copy(tab_hbm, tab_vmem, sem_tab.at[0])

    @pl.when(first)
    def _():
        for j in range(min(n_wslots, n_sec)):
            weight_copy(j).start()
        if rope:
            table_copy().start()

    gs = g_ref[...] * (1.0 + scale_ref[...])
    sh = shift_ref[...]

    def normalise(rows, slot):
        h_ref[slot, rows, :] = (_rms(x_ref[0, rows, :]) * gs + sh).astype(BF16)

    def normalise_all():
        for r in range(n_sub):
            normalise(slice(r * sub, (r + 1) * sub), nxt)

    if skew:
        pl.when(step == 0)(normalise_all)
        h_chunk = n_rows // (n_sec - 1)
    else:
        normalise_all()

    if rope:
        pl.when(first)(lambda: table_copy().wait())

    def epi_rk(a, rows):
        if rope:
            a = _rope_ret(a, cr_ref[rows, :], sr_ref[rows, :])
        return a * RET_SCALE

    def epi_rq(a, rows):
        return _rope_ret(a, cr_ref[rows, :], sr_ref[rows, :]) if rope else a

    def epi_dk(a, rows):
        return _rope_diff(a, cd_ref[rows, :], sdl_ref[rows, :], sdh_ref[rows, :]) if rope else a

    def epi_dq(a, rows):
        return epi_dk(a, rows) * DIFF_Q_SCALE

    def epi_ln(a, rows):
        mu = jnp.mean(a, axis=-1, keepdims=True)
        d = a - mu
        return d * lax.rsqrt(jnp.mean(d * d, axis=-1, keepdims=True) + EPS)

    epilogues = dict(
        plain=lambda a, rows: a,
        sigmoid=lambda a, rows: jax.nn.sigmoid(a),
        silu=lambda a, rows: a * jax.nn.sigmoid(a),
        rk=epi_rk, rq=epi_rq, dk=epi_dk, dq=epi_dq, ln=epi_ln)

    def writeback(j):
        slot = j % n_slots
        return pltpu.make_async_copy(stage.at[slot], z_hbm.at[b, j], sems_out.at[slot])

    def run(epilogue, j):
        w_sec = w_vmem.at[src_of(j)]
        out = stage.at[j % n_slots]
        h_cur = h_ref.at[cur]
        if skew:
            start = jnp.minimum(j, n_sec - 2) * h_chunk
            normalise(pl.ds(pl.multiple_of(start, h_chunk), h_chunk), nxt)
        for r in range(n_sub):
            rows = slice(r * sub, (r + 1) * sub)
            out[rows, :] = epilogue(_dot(h_cur[rows, :], w_sec[...]), rows).astype(BF16)

    def section(j, carry):
        pl.when(first)(functools.partial(load_weights, j))

        @pl.when(j >= n_slots)
        def _():
            writeback(j - n_slots).wait()

        for kind in sorted(set(kinds)):
            idx = [i for i, k in enumerate(kinds) if k == kind]
            cond = j == idx[0]
            for i in idx[1:]:
                cond = cond | (j == i)
            pl.when(cond)(functools.partial(run, epilogues[kind], j))
        writeback(j).start()
        return carry

    def all_sections():
        lax.fori_loop(0, n_sec, section, 0)
        for j in range(max(0, n_sec - n_slots), n_sec):
            writeback(j).wait()

    if skew:
        pl.when(step >= 1)(all_sections)
    else:
        all_sections()


def _projection(xs, l, g_pre, mod4, row_of, w_all, src_of, kinds, tables):
    n_g, n_rows, _ = xs.shape
    n_sec = len(kinds)
    rope = tables is not None
    n_src = w_all.shape[2] // SEC if src_of is _full_src else n_sec
    skew = n_g > 1
    group_of = (lambda s: jnp.minimum(s, n_g - 1)) if skew else (lambda s: s)
    in_specs = [
        pl.BlockSpec((1, n_rows, D_MODEL), lambda s: (group_of(s), 0, 0)),
        _layer_spec(l, (1, D_MODEL)),
        _mod_spec(l, lambda s: row_of(group_of(s)), 0),
        _mod_spec(l, lambda s: row_of(group_of(s)), 1),
        pl.BlockSpec(memory_space=pl.ANY),
    ]
    args = [xs, g_pre, mod4, mod4, w_all]
    scratch = [pltpu.VMEM((2 if skew else 1, n_rows, D_MODEL), BF16),
               pltpu.VMEM((PROJ_OUT_SLOTS, n_rows, SEC), BF16),
               pltpu.VMEM((n_src, D_MODEL, SEC), BF16),
               pltpu.VMEM((2, D_MODEL, SEC), F32),
               pltpu.SemaphoreType.DMA((PROJ_OUT_SLOTS,)),
               pltpu.SemaphoreType.DMA((2,))]
    if rope:
        in_specs.append(pl.BlockSpec(memory_space=pl.ANY))
        args.append(tables)
        scratch += [pltpu.VMEM(tables.shape, F32), pltpu.SemaphoreType.DMA((1,))]
    return pl.pallas_call(
        functools.partial(_proj_kernel, layer=l, kinds=kinds, src_of=src_of, n_src=n_src,
                          rope=rope, n_rows=n_rows, sub=PROJ_SUB, skew=skew),
        grid=(n_g + 1 if skew else n_g,),
        in_specs=in_specs,
        out_specs=pl.BlockSpec(memory_space=pl.ANY),
        out_shape=jax.ShapeDtypeStruct((n_g, n_sec, n_rows, SEC), BF16),
        scratch_shapes=scratch,
        compiler_params=pltpu.CompilerParams(vmem_limit_bytes=VMEM_LIMIT),
        name="projection_rope" if rope else "projection",
    )(*args)


def _retention_tasks(lgf, lgb, cs, q_ref, g_ref, k_ref, v_ref, kc_ref, vc_ref, o_ref, inc_ref,
                     st_ref, *, n, n_ctx, rc):
    nc = n // rc
    pos = lax.broadcasted_iota(jnp.int32, (rc, LANES), 0).astype(F32)
    kdf = jnp.exp(lgf * (rc - 1.0 - pos)).astype(BF16)
    kdb = jnp.exp(lgb * pos).astype(BF16)
    cdf = jnp.exp(jnp.full((1, LANES), lgf * rc, F32))
    cdb = jnp.exp(jnp.full((1, LANES), lgb * rc, F32))
    qdf = jnp.exp(lgf * (pos + 1.0)).astype(BF16)
    qdb = jnp.exp(lgb * (rc - pos)).astype(BF16)
    ii = lax.broadcasted_iota(jnp.int32, (rc, rc), 0)
    jj = lax.broadcasted_iota(jnp.int32, (rc, rc), 1)
    dist = (ii - jj).astype(F32)
    dmat = (jnp.where(dist >= 0, jnp.exp(lgf * jnp.maximum(dist, 0.0)), 0.0)
            + jnp.where(dist <= 0, jnp.exp(lgb * jnp.maximum(-dist, 0.0)), 0.0))
    state = {}

    def increment(kr, vr, nn):
        rows = slice(nn * rc, (nn + 1) * rc)
        kk = kr[0, rows, cs]
        kcat = jnp.concatenate([kk * kdf, kk * kdb], axis=1)
        vt = vr[0, rows, cs].astype(F32).T.astype(BF16)
        return _dot(vt, kcat)

    def init_task():
        sf = jnp.zeros((HEAD_W, HEAD_W), F32)
        sb = jnp.zeros((HEAD_W, HEAD_W), F32)
        if n_ctx:
            incs = [increment(kc_ref, vc_ref, nn) for nn in range(n_ctx // rc)]
            for inc in incs:
                sf = cdf * sf + inc[:, :HEAD_W]
            for inc in reversed(incs):
                sb = cdb * sb + inc[:, HEAD_W:]
        state["sf"], state["sb"] = sf, sb

    def inc_task(nn):
        inc_ref[nn] = increment(k_ref, v_ref, nn)

    def scan_task():
        sf, sb = state["sf"], state["sb"]
        for nn in range(nc):
            st_ref[nn, :, 0:HEAD_W] = sf.astype(BF16)
            sf = cdf * sf + inc_ref[nn, :, 0:HEAD_W]
        for nn in reversed(range(nc)):
            st_ref[nn, :, HEAD_W:] = sb.astype(BF16)
            sb = cdb * sb + inc_ref[nn, :, HEAD_W:]

    def out_task(nn):
        rows = slice(nn * rc, (nn + 1) * rc)
        q = q_ref[0, rows, cs]
        s = _dot_nt(q, k_ref[0, rows, cs]) * dmat
        inner = _dot(s.astype(BF16), v_ref[0, rows, cs])
        qcat = jnp.concatenate([q * qdf, q * qdb], axis=1)
        out = inner + _dot_nt(qcat, st_ref[nn])
        o_ref[0, rows, cs] = _rms(out).astype(BF16) * g_ref[0, rows, cs]

    inc_tasks = [init_task] + [functools.partial(inc_task, nn) for nn in range(nc)]
    return inc_tasks, scan_task, [functools.partial(out_task, nn) for nn in range(nc)]


def _ret_kernel(lg_ref, q_ref, g_ref, k_ref, v_ref, o_ref, inc_ref, st_ref, *, layer, n, rc,
                heads):
    tasks = []
    for hh in range(heads):
        head = pl.program_id(1) * heads + hh
        tasks.append(_retention_tasks(
            lg_ref[layer, 0, head], lg_ref[layer, 1, head], slice(hh * LANES, (hh + 1) * LANES),
            q_ref, g_ref, k_ref, v_ref, None, None, o_ref, inc_ref.at[hh], st_ref.at[hh],
            n=n, n_ctx=0, rc=rc))
    for i in range(len(tasks[0][0])):
        for inc_tasks, _, _ in tasks:
            inc_tasks[i]()
    for _, scan_task, _ in tasks:
        scan_task()
    for i in range(len(tasks[0][2])):
        for _, _, out_tasks in tasks:
            out_tasks[i]()


def _retention_ctx(log_gamma, layer, zc, sec, n_b, n, *, rc, heads):
    width = heads * LANES
    head_block = lambda b, h: h
    return pl.pallas_call(
        functools.partial(_ret_kernel, layer=layer, n=n, rc=rc, heads=heads),
        grid=(n_b, HEADS // heads),
        in_specs=[pl.BlockSpec(memory_space=pltpu.SMEM)]
        + [_z_spec(sec[name], n, width, True, col_of=head_block)
           for name in ("rq", "rg", "rk", "rv")],
        out_specs=pl.BlockSpec((1, n, width), lambda b, h: (b, 0, h)),
        out_shape=jax.ShapeDtypeStruct((n_b, n, BRANCH_W), BF16),
        scratch_shapes=[pltpu.VMEM((heads, n // rc, HEAD_W, 2 * HEAD_W), F32),
                        pltpu.VMEM((heads, n // rc, HEAD_W, 2 * HEAD_W), BF16)],
        compiler_params=pltpu.CompilerParams(vmem_limit_bytes=VMEM_LIMIT),
        name="retention_ctx",
    )(log_gamma, zc, zc, zc, zc)


def _split_maps(q, lane):
    zero = jnp.zeros_like(q)
    return jnp.concatenate([jnp.where(lane < DIFF_DQK, q, zero),
                            jnp.where(lane >= DIFF_DQK, q, zero)], axis=0)


def _diff_ctx_kernel(lam_ref, q_ref, g_ref, k_ref, v_ref, o_ref, *, layer, tq, post_scale):
    lane = lax.broadcasted_iota(jnp.int32, (tq, LANES), 1)
    for h in range(HEADS):
        cs = slice(h * LANES, (h + 1) * LANES)
        s = _dot_nt(_split_maps(q_ref[0, :, cs], lane), k_ref[0, :, cs])
        p = jnp.exp2(s - jnp.max(s, axis=-1, keepdims=True))
        o = _dot(p.astype(BF16), v_ref[0, :, cs]) / jnp.sum(p, axis=-1, keepdims=True)
        d = o[:tq] - lam_ref[layer] * o[tq:]
        o_ref[0, :, cs] = (_rms(d) * post_scale * g_ref[0, :, cs].astype(F32)).astype(BF16)


def _diff_attention_ctx(lam, layer, zc, sec, n_b, n, *, post_scale):
    return pl.pallas_call(
        functools.partial(_diff_ctx_kernel, layer=layer, tq=n, post_scale=post_scale),
        grid=(n_b,),
        in_specs=[pl.BlockSpec(memory_space=pltpu.SMEM)]
        + [_z_spec(sec[name], n, SEC, True) for name in ("dq", "dg", "dk", "dv")],
        out_specs=pl.BlockSpec((1, n, BRANCH_W), lambda b: (b, 0, 0)),
        out_shape=jax.ShapeDtypeStruct((n_b, n, BRANCH_W), BF16),
        compiler_params=pltpu.CompilerParams(vmem_limit_bytes=VMEM_LIMIT),
        name="diff_attention_ctx",
    )(lam, zc, zc, zc, zc)


def _mixer_kernel(lam_ref, lg_ref, q_ref, g_ref, kc_ref, vc_ref, kl_ref, vl_ref,
                  rq_ref, rg_ref, rk_ref, rv_ref, rkc_ref, rvc_ref, od_ref, or_ref,
                  kall, vt, s0, s1, p0, p1, d0, d1, inc_ref, st_ref, *, layer, n, n_ctx, ts,
                  rc, post_scale):
    n_sub = n // ts
    nk = kall.shape[0]
    kall[0:n_ctx, :] = kc_ref[0]
    kall[n_ctx:, :] = kl_ref[0]
    for c in range(nk // LANES):
        src, base = (vc_ref, 0) if c * LANES < n_ctx else (vl_ref, n_ctx)
        rows = slice(c * LANES - base, (c + 1) * LANES - base)
        vt[0:LANES, c * LANES:(c + 1) * LANES] = src[0, rows, :].astype(F32).T.astype(BF16)
    vt[LANES:, :] = jnp.ones((vt.shape[0] - LANES, nk), BF16)
    lam = lam_ref[layer]
    lane = lax.broadcasted_iota(jnp.int32, (ts, LANES), 1)
    s_bufs, p_bufs, d_bufs = (s0, s1), (p0, p1), (d0, d1)

    def stage_a(t):
        rows = slice(t * ts, (t + 1) * ts)
        s = _dot_nt(kall[...], _split_maps(q_ref[0, rows, :], lane))
        s_bufs[t % 2][...] = s
        d_bufs[t % 2][...] = jnp.max(s, axis=0, keepdims=True)

    def stage_b(t):
        p_bufs[t % 2][...] = jnp.exp2(s_bufs[t % 2][...] - d_bufs[t % 2][...]).astype(BF16)

    def stage_c(t):
        rows = slice(t * ts, (t + 1) * ts)
        oe = _dot(vt[...], p_bufs[t % 2][...])
        ot = oe[:LANES] / oe[LANES:LANES + 1]
        dt = ot[:, :ts] - lam * ot[:, ts:]
        dt = dt * lax.rsqrt(jnp.mean(dt * dt, axis=0, keepdims=True) + EPS)
        od_ref[0, rows, :] = (dt.T * post_scale * g_ref[0, rows, :].astype(F32)).astype(BF16)

    head = pl.program_id(1)
    inc_tasks, scan_task, out_tasks = _retention_tasks(
        lg_ref[layer, 0, head], lg_ref[layer, 1, head], slice(0, LANES),
        rq_ref, rg_ref, rk_ref, rv_ref, rkc_ref, rvc_ref, or_ref, inc_ref, st_ref,
        n=n, n_ctx=n_ctx, rc=rc)
    n_steps = n_sub + 2
    extra = [[] for _ in range(n_steps)]
    half = (len(inc_tasks) + 1) // 2
    extra[0] += inc_tasks[:half]
    extra[1] += inc_tasks[half:]
    extra[2].append(scan_task)
    for i, task in enumerate(out_tasks):
        extra[2 + (i * (n_steps - 2)) // len(out_tasks)].append(task)

    for t in range(n_steps):
        if t >= 2:
            stage_c(t - 2)
        if 1 <= t <= n_sub:
            stage_b(t - 1)
        if t < n_sub:
            stage_a(t)
        for task in extra[t]:
            task()


def _latent_mixers(lam, log_gamma, layer, z, sec, n_b, n, zc, sec_c, n_ctx, *, ts, rc,
                   post_scale):
    nk = n_ctx + n
    head_block = lambda b, h: h
    lat = lambda name: _z_spec(sec[name], n, LANES, False, col_of=head_block)
    cx = lambda name: _z_spec(sec_c[name], n_ctx, LANES, True, col_of=head_block)
    out_spec = pl.BlockSpec((1, n, LANES), lambda b, h: (b, 0, h))
    out_shape = jax.ShapeDtypeStruct((n_b, n, BRANCH_W), BF16)
    o_diff, o_ret = pl.pallas_call(
        functools.partial(_mixer_kernel, layer=layer, n=n, n_ctx=n_ctx, ts=ts, rc=rc,
                          post_scale=post_scale),
        grid=(n_b, HEADS),
        in_specs=[pl.BlockSpec(memory_space=pltpu.SMEM), pl.BlockSpec(memory_space=pltpu.SMEM),
                  lat("dq"), lat("dg"), cx("dk"), cx("dv"), lat("dk"), lat("dv"),
                  lat("rq"), lat("rg"), lat("rk"), lat("rv"), cx("rk"), cx("rv")],
        out_specs=[out_spec, out_spec],
        out_shape=[out_shape, out_shape],
        scratch_shapes=[pltpu.VMEM((nk, LANES), BF16), pltpu.VMEM((LANES + 16, nk), BF16),
                        pltpu.VMEM((nk, 2 * ts), F32), pltpu.VMEM((nk, 2 * ts), F32),
                        pltpu.VMEM((nk, 2 * ts), BF16), pltpu.VMEM((nk, 2 * ts), BF16),
                        pltpu.VMEM((1, 2 * ts), F32), pltpu.VMEM((1, 2 * ts), F32),
                        pltpu.VMEM((n // rc, HEAD_W, 2 * HEAD_W), F32),
                        pltpu.VMEM((n // rc, HEAD_W, 2 * HEAD_W), BF16)],
        compiler_params=pltpu.CompilerParams(vmem_limit_bytes=VMEM_LIMIT),
        name="latent_mixers",
    )(lam, log_gamma, z, z, zc, zc, z, z, z, z, z, z, zc, zc)
    return o_ret, o_diff


def _merge_kernel(x_ref, gate_ref, gpost_ref, oret_ref, odiff_ref, u_ref, vn_ref, mg_ref,
                  gr_ref, gm_ref, gd_ref, ws_ref, bs_ref, wbo_ref, wo_ref, out_ref, *, tm, out_sub):
    rows = []
    for c in range(tm // CHUNK):
        cols = []
        for g in range(HEADS):
            blk = vn_ref[0, c * CHUNK:(c + 1) * CHUNK, g * LANES:(g + 1) * LANES]
            cols.append(_dot(ws_ref[g], blk) + bs_ref[g])
        rows.append(jnp.concatenate(cols, axis=1))
    sp = jnp.concatenate(rows, axis=0)
    o_mlp = (u_ref[0].astype(F32) * sp * mg_ref[0].astype(F32)).astype(BF16)

    def gated(gate2_ref, o, w):
        gate = jnp.concatenate([gate2_ref[0, 0], gate2_ref[0, 1]], axis=1).astype(F32)
        return gate * _dot(o, w)

    t = (gated(gr_ref, oret_ref[0], wbo_ref[0]) + gated(gm_ref, o_mlp, wbo_ref[1])
         + gated(gd_ref, odiff_ref[0], wbo_ref[2]))
    tb = t.astype(BF16)
    for r in range(tm // out_sub):
        rs = slice(r * out_sub, (r + 1) * out_sub)
        y = _dot(tb[rs, :], wo_ref[...])
        out_ref[0, rs, :] = x_ref[0, rs, :] + gate_ref[...] * (_rms(y) * gpost_ref[...])


def _merge(xs, l, mod4, row_of, g_post, o_ret, o_diff, z, sec, ws, bs, wbo, wo, *, tm):
    n_b, n, _ = xs.shape
    row_block = lambda b, i: i

    def zsec(name):
        return _z_spec(sec[name], tm, SEC, False, row_of=row_block)

    def zgate(k):
        return pl.BlockSpec((1, 2, tm, SEC), lambda b, i: (b, k, i, 0))

    in_specs = [
        pl.BlockSpec((1, tm, D_MODEL), lambda b, i: (b, i, 0)),
        _mod_spec(l, row_of, 2),
        _layer_spec(l, (1, D_MODEL)),
        pl.BlockSpec((1, tm, BRANCH_W), lambda b, i: (b, i, 0)),
        pl.BlockSpec((1, tm, BRANCH_W), lambda b, i: (b, i, 0)),
        zsec("mu"), zsec("mv"), zsec("mg"), zgate(0), zgate(1), zgate(2),
        _layer_spec(l, (HEADS, CHUNK, CHUNK)),
        _layer_spec(l, (HEADS, CHUNK, LANES)),
        _layer_spec(l, (N_BRANCH, BRANCH_W, D_MODEL)),
        _layer_spec(l, (D_MODEL, D_MODEL)),
    ]
    return pl.pallas_call(
        functools.partial(_merge_kernel, tm=tm, out_sub=MERGE_OUT_SUB),
        grid=(n_b, n // tm),
        in_specs=in_specs,
        out_specs=pl.BlockSpec((1, tm, D_MODEL), lambda b, i: (b, i, 0)),
        out_shape=jax.ShapeDtypeStruct(xs.shape, F32),
        compiler_params=pltpu.CompilerParams(vmem_limit_bytes=VMEM_LIMIT),
        name="merge",
    )(xs, mod4, g_post, o_ret, o_diff, z, z, z, z, z, z, ws, bs, wbo, wo)


def _rope_tables(n_lat):
    rows = n_lat // GRID_W
    row_pos = jnp.repeat(jnp.arange(rows, dtype=F32), GRID_W)
    col_pos = jnp.tile(jnp.arange(GRID_W, dtype=F32), rows)

    def angles(head_dim):
        n_freq = head_dim // 4
        inv = ROPE_BASE ** (-jnp.arange(n_freq, dtype=F32) / n_freq)
        ang = jnp.concatenate([row_pos[:, None] * inv, col_pos[:, None] * inv], axis=-1)
        return jnp.cos(ang), jnp.sin(ang)

    cos_r, sin_r = angles(HEAD_W)
    cos_d, sin_d = angles(DIFF_DQK)
    zeros = jnp.zeros_like(sin_d)
    c_r = jnp.concatenate([cos_r, cos_r], axis=-1)
    s_r = jnp.concatenate([-sin_r, sin_r], axis=-1)
    c_d = jnp.tile(jnp.concatenate([cos_d, cos_d], axis=-1), (1, 2))
    s_lo = jnp.tile(jnp.concatenate([-sin_d, zeros], axis=-1), (1, 2))
    s_hi = jnp.tile(jnp.concatenate([zeros, sin_d], axis=-1), (1, 2))
    return c_r, s_r, c_d, s_lo, s_hi


@jax.jit
def kernel(x, c, ctx, c_ctx, w_mod, b_mod, g_pre, g_post, w_in, ret_decay_logit, mlp_w_s,
           mlp_b_s, diff_lambda_q, diff_lambda_k, w_branch_out, w_out):
    n_b, n_lat, _ = x.shape
    n_ctx = ctx.shape[1]
    tables = jnp.stack(_rope_tables(n_lat))

    cond_rows = 16
    ctx_row = n_b
    cc = jnp.zeros((cond_rows, D_MODEL), F32).at[:n_b].set(c).at[ctx_row].set(c_ctx)
    mod4 = _modulation(cc, w_mod, b_mod).reshape(DEPTH, cond_rows, 1, 3 * D_MODEL)
    lat_row = lambda b: b
    ctx_row_of = lambda b: ctx_row

    log_gamma = -jax.nn.softplus(-ret_decay_logit.astype(F32))
    lam_inits = [0.8 - 0.6 * math.exp(-0.3 * l) for l in range(DEPTH)]
    lam = (jnp.exp(jnp.sum(diff_lambda_q[:, 0] * diff_lambda_k[:, 0], axis=-1))
           - jnp.exp(jnp.sum(diff_lambda_q[:, 1] * diff_lambda_k[:, 1], axis=-1))
           + jnp.asarray(lam_inits, F32)).astype(F32)

    w_sec = w_in
    ws = mlp_w_s.astype(BF16)
    bs = jnp.broadcast_to(mlp_b_s[..., None], (DEPTH, HEADS, CHUNK, LANES)).astype(F32)
    wbo = w_branch_out.astype(BF16)
    wo = w_out.astype(BF16)
    g_pre3 = g_pre.reshape(DEPTH, 1, D_MODEL)
    g_post3 = g_post.reshape(DEPTH, 1, D_MODEL)

    for l in range(DEPTH):
        last = l == DEPTH - 1
        post_scale = 1.0 - lam_inits[l]

        ctx_flat = ctx.reshape(1, n_b * n_ctx, D_MODEL)
        if last:
            zc = _projection(ctx_flat, l, g_pre3, mod4, ctx_row_of, w_sec, lambda j: j,
                             KV_KINDS, None)
            sec_c = KV_SEC
        else:
            zc = _projection(ctx_flat, l, g_pre3, mod4, ctx_row_of, w_sec, _full_src,
                             FULL_KINDS, None)
            sec_c = FULL_SEC
            co_ret = _retention_ctx(log_gamma, l, zc, sec_c, n_b, n_ctx, rc=RET_CHUNK,
                                    heads=HEADS)
            co_diff = _diff_attention_ctx(lam, l, zc, sec_c, n_b, n_ctx, post_scale=post_scale)
            ctx_next = _merge(ctx_flat, l, mod4, ctx_row_of, g_post3,
                              co_ret.reshape(1, n_b * n_ctx, BRANCH_W),
                              co_diff.reshape(1, n_b * n_ctx, BRANCH_W), zc, sec_c,
                              ws, bs, wbo, wo, tm=MERGE_TM)
            ctx_next = ctx_next.reshape(n_b, n_ctx, D_MODEL)

        z = _projection(x, l, g_pre3, mod4, lat_row, w_sec, _full_src, FULL_KINDS, tables)
        o_ret, o_diff = _latent_mixers(lam, log_gamma, l, z, FULL_SEC, n_b, n_lat, zc, sec_c,
                                       n_ctx, ts=DIFF_TS, rc=RET_CHUNK, post_scale=post_scale)
        x = _merge(x, l, mod4, lat_row, g_post3, o_ret, o_diff, z, FULL_SEC, ws, bs, wbo, wo,
                   tm=MERGE_TM)
        if not last:
            ctx = ctx_next
    return x
```

```python
import functools
import math

import jax
import jax.numpy as jnp
from jax import lax
from jax.experimental import pallas as pl
from jax.experimental.pallas import tpu as pltpu

F32 = jnp.float32
BF16 = jnp.bfloat16

D_MODEL = 1024
DEPTH = 2
GRID_W = 64
N_BRANCH = 3
BRANCH_W = D_MODEL // 2
HEADS = 4
HEAD_W = BRANCH_W // HEADS
DIFF_DQK = HEAD_W // 2
CHUNK = 128
ROPE_BASE = 10000.0
EPS = 1e-6
RET_SCALE = HEAD_W ** -0.5
DIFF_Q_SCALE = (DIFF_DQK ** -0.5) * math.log2(math.e)

LANES = 128
SEC = BRANCH_W
MERGE_COLS = N_BRANCH * D_MODEL
KV_COLS = 4 * BRANCH_W
IN_COLS = KV_COLS + 7 * BRANCH_W + MERGE_COLS
VMEM_LIMIT = 56 * 1024 * 1024

PROJ_SUB = 256
PROJ_OUT_SLOTS = 2
DIFF_TS = 256
RET_CHUNK = 256
MERGE_TM = 512

FULL_KINDS = ("sigmoid",) * 6 + ("rk", "plain", "dk", "plain", "rq", "silu", "dq", "silu",
                                 "plain", "ln", "silu")
FULL_SEC = dict(rk=6, rv=7, dk=8, dv=9, rq=10, rg=11, dq=12, dg=13, mu=14, mv=15, mg=16)
KV_KINDS = ("rk", "plain", "dk", "plain")
KV_SEC = dict(rk=0, rv=1, dk=2, dv=3)
_N_MERGE_SEC = MERGE_COLS // SEC
_N_REST_SEC = (KV_COLS + 7 * BRANCH_W) // SEC


def _full_src(j):
    return jnp.where(j < _N_MERGE_SEC, j + _N_REST_SEC, j - _N_MERGE_SEC)


def _dot(a, b):
    return jnp.dot(a, b, preferred_element_type=F32)


def _dot_nt(a, b):
    return lax.dot_general(a, b, (((1,), (1,)), ((), ())), preferred_element_type=F32)


def _rms(v):
    return v * lax.rsqrt(jnp.mean(v * v, axis=-1, keepdims=True) + EPS)


def _mod_spec(l, row_of, k):
    return pl.BlockSpec((None, None, 1, D_MODEL), lambda *idx: (l, row_of(idx[0]), 0, k))


def _layer_spec(l, shape, **kwargs):
    zeros = (0,) * len(shape)
    return pl.BlockSpec((None,) + tuple(shape), lambda *idx: (l,) + zeros, **kwargs)


def _z_spec(sec_idx, rows, width, flat, col_of=lambda *idx: 0, row_of=None):
    if row_of is None:
        row_of = (lambda *idx: idx[0]) if flat else (lambda *idx: 0)
    group_of = (lambda *idx: 0) if flat else (lambda *idx: idx[0])
    return pl.BlockSpec((1, None, rows, width),
                        lambda *idx: (group_of(*idx), sec_idx, row_of(*idx), col_of(*idx)))


def _split_bf16(a):
    hi = a.astype(BF16)
    return hi, (a - hi.astype(F32)).astype(BF16)


def _mod_kernel(c_ref, w_ref, b_ref, o_ref):
    cv = c_ref[...]
    c_hi, c_lo = _split_bf16(cv * jax.nn.sigmoid(cv))
    w_hi, w_lo = _split_bf16(w_ref[0])
    o_ref[0] = _dot(c_hi, w_hi) + _dot(c_hi, w_lo) + _dot(c_lo, w_hi) + b_ref[0]


def _modulation(cc, w_mod, b_mod):
    rows = cc.shape[0]
    return pl.pallas_call(
        _mod_kernel,
        grid=(DEPTH, 3),
        in_specs=[
            pl.BlockSpec((rows, D_MODEL), lambda l, j: (0, 0)),
            pl.BlockSpec((1, D_MODEL, D_MODEL), lambda l, j: (l, 0, j)),
            pl.BlockSpec((1, 1, D_MODEL), lambda l, j: (l, 0, j)),
        ],
        out_specs=pl.BlockSpec((1, rows, D_MODEL), lambda l, j: (l, 0, j)),
        out_shape=jax.ShapeDtypeStruct((DEPTH, rows, 3 * D_MODEL), F32),
        compiler_params=pltpu.CompilerParams(vmem_limit_bytes=VMEM_LIMIT),
        name="modulation",
    )(cc, w_mod, b_mod.reshape(DEPTH, 1, 3 * D_MODEL))


def _rope_ret(a, c, s):
    outs = []
    for h in range(HEADS):
        ah = a[:, h * LANES:(h + 1) * LANES]
        outs.append(ah * c + pltpu.roll(ah, LANES // 2, 1) * s)
    return jnp.concatenate(outs, axis=1)


def _rope_diff(a, c, s_lo, s_hi):
    outs = []
    for h in range(HEADS):
        ah = a[:, h * LANES:(h + 1) * LANES]
        outs.append(ah * c + pltpu.roll(ah, LANES - DIFF_DQK // 2, 1) * s_lo
                    + pltpu.roll(ah, DIFF_DQK // 2, 1) * s_hi)
    return jnp.concatenate(outs, axis=1)


def _proj_kernel(*refs, layer, kinds, src_of, n_src, rope, n_rows, sub):
    if rope:
        (x_ref, g_ref, shift_ref, scale_ref, w_hbm, tab_hbm, z_hbm,
         h_ref, stage, w_vmem, w_stage, sems_out, sems_w, tab_vmem, sem_tab) = refs
        cr_ref, sr_ref, cd_ref, sdl_ref, sdh_ref = [tab_vmem.at[i] for i in range(5)]
    else:
        (x_ref, g_ref, shift_ref, scale_ref, w_hbm, z_hbm,
         h_ref, stage, w_vmem, w_stage, sems_out, sems_w) = refs
    b = pl.program_id(0)
    first = b == 0
    n_sub = n_rows // sub
    n_sec = len(kinds)
    n_slots = stage.shape[0]

    n_wslots = w_stage.shape[0]

    def weight_copy(j):
        start = src_of(j) * SEC
        cols = pl.ds(start if isinstance(start, int) else pl.multiple_of(start, SEC), SEC)
        slot = j % n_wslots
        return pltpu.make_async_copy(w_hbm.at[layer, :, cols], w_stage.at[slot], sems_w.at[slot])

    def load_weights(j):
        weight_copy(j).wait()
        src = w_stage.at[j % n_wslots]
        dst = w_vmem.at[src_of(j)]
        for r in range(D_MODEL // LANES):
            rows = slice(r * LANES, (r + 1) * LANES)
            dst[rows, :] = src[rows, :].astype(BF16)

        @pl.when(j + n_wslots < n_sec)
        def _():
            weight_copy(j + n_wslots).start()

    def table_copy():
        return pltpu.make_async_copy(tab_hbm, tab_vmem, sem_tab.at[0])

    @pl.when(first)
    def _():
        for j in range(min(n_wslots, n_sec)):
            weight_copy(j).start()
        if rope:
            table_copy().start()

    gs = g_ref[...] * (1.0 + scale_ref[...])
    sh = shift_ref[...]
    for r in range(n_sub):
        xs = x_ref[0, r * sub:(r + 1) * sub, :]
        h_ref[r * sub:(r + 1) * sub, :] = (_rms(xs) * gs + sh).astype(BF16)

    if rope:
        pl.when(first)(lambda: table_copy().wait())

    def epi_rk(a, rows):
        if rope:
            a = _rope_ret(a, cr_ref[rows, :], sr_ref[rows, :])
        return a * RET_SCALE

    def epi_rq(a, rows):
        return _rope_ret(a, cr_ref[rows, :], sr_ref[rows, :]) if rope else a

    def epi_dk(a, rows):
        return _rope_diff(a, cd_ref[rows, :], sdl_ref[rows, :], sdh_ref[rows, :]) if rope else a

    def epi_dq(a, rows):
        return epi_dk(a, rows) * DIFF_Q_SCALE

    def epi_ln(a, rows):
        mu = jnp.mean(a, axis=-1, keepdims=True)
        d = a - mu
        return d * lax.rsqrt(jnp.mean(d * d, axis=-1, keepdims=True) + EPS)

    epilogues = dict(
        plain=lambda a, rows: a,
        sigmoid=lambda a, rows: jax.nn.sigmoid(a),
        silu=lambda a, rows: a * jax.nn.sigmoid(a),
        rk=epi_rk, rq=epi_rq, dk=epi_dk, dq=epi_dq, ln=epi_ln)

    def writeback(j):
        slot = j % n_slots
        return pltpu.make_async_copy(stage.at[slot], z_hbm.at[b, j], sems_out.at[slot])

    def run(epilogue, j):
        w_sec = w_vmem.at[src_of(j)]
        out = stage.at[j % n_slots]
        for r in range(n_sub):
            rows = slice(r * sub, (r + 1) * sub)
            out[rows, :] = epilogue(_dot(h_ref[rows, :], w_sec[...]), rows).astype(BF16)

    def section(j, carry):
        pl.when(first)(functools.partial(load_weights, j))

        @pl.when(j >= n_slots)
        def _():
            writeback(j - n_slots).wait()

        for kind in sorted(set(kinds)):
            idx = [i for i, k in enumerate(kinds) if k == kind]
            cond = j == idx[0]
            for i in idx[1:]:
                cond = cond | (j == i)
            pl.when(cond)(functools.partial(run, epilogues[kind], j))
        writeback(j).start()
        return carry

    lax.fori_loop(0, n_sec, section, 0)
    for j in range(max(0, n_sec - n_slots), n_sec):
        writeback(j).wait()


def _projection(xs, l, g_pre, mod4, row_of, w_all, src_of, kinds, tables):
    n_g, n_rows, _ = xs.shape
    n_sec = len(kinds)
    rope = tables is not None
    n_src = w_all.shape[2] // SEC if src_of is _full_src else n_sec
    in_specs = [
        pl.BlockSpec((1, n_rows, D_MODEL), lambda b: (b, 0, 0)),
        _layer_spec(l, (1, D_MODEL)),
        _mod_spec(l, row_of, 0),
        _mod_spec(l, row_of, 1),
        pl.BlockSpec(memory_space=pl.ANY),
    ]
    args = [xs, g_pre, mod4, mod4, w_all]
    scratch = [pltpu.VMEM((n_rows, D_MODEL), BF16),
               pltpu.VMEM((PROJ_OUT_SLOTS, n_rows, SEC), BF16),
               pltpu.VMEM((n_src, D_MODEL, SEC), BF16),
               pltpu.VMEM((2, D_MODEL, SEC), F32),
               pltpu.SemaphoreType.DMA((PROJ_OUT_SLOTS,)),
               pltpu.SemaphoreType.DMA((2,))]
    if rope:
        in_specs.append(pl.BlockSpec(memory_space=pl.ANY))
        args.append(tables)
        scratch += [pltpu.VMEM(tables.shape, F32), pltpu.SemaphoreType.DMA((1,))]
    return pl.pallas_call(
        functools.partial(_proj_kernel, layer=l, kinds=kinds, src_of=src_of, n_src=n_src,
                          rope=rope, n_rows=n_rows, sub=PROJ_SUB),
        grid=(n_g,),
        in_specs=in_specs,
        out_specs=pl.BlockSpec(memory_space=pl.ANY),
        out_shape=jax.ShapeDtypeStruct((n_g, n_sec, n_rows, SEC), BF16),
        scratch_shapes=scratch,
        compiler_params=pltpu.CompilerParams(vmem_limit_bytes=VMEM_LIMIT),
        name="projection_rope" if rope else "projection",
    )(*args)


def _retention_tasks(lgf, lgb, cs, q_ref, g_ref, k_ref, v_ref, kc_ref, vc_ref, o_ref, inc_ref,
                     st_ref, *, n, n_ctx, rc):
    nc = n // rc
    pos = lax.broadcasted_iota(jnp.int32, (rc, LANES), 0).astype(F32)
    kdf = jnp.exp(lgf * (rc - 1.0 - pos)).astype(BF16)
    kdb = jnp.exp(lgb * pos).astype(BF16)
    cdf = jnp.exp(jnp.full((1, LANES), lgf * rc, F32))
    cdb = jnp.exp(jnp.full((1, LANES), lgb * rc, F32))
    qdf = jnp.exp(lgf * (pos + 1.0)).astype(BF16)
    qdb = jnp.exp(lgb * (rc - pos)).astype(BF16)
    ii = lax.broadcasted_iota(jnp.int32, (rc, rc), 0)
    jj = lax.broadcasted_iota(jnp.int32, (rc, rc), 1)
    dist = (ii - jj).astype(F32)
    dmat = (jnp.where(dist >= 0, jnp.exp(lgf * jnp.maximum(dist, 0.0)), 0.0)
            + jnp.where(dist <= 0, jnp.exp(lgb * jnp.maximum(-dist, 0.0)), 0.0))
    state = {}

    def increment(kr, vr, nn):
        rows = slice(nn * rc, (nn + 1) * rc)
        kk = kr[0, rows, cs]
        kcat = jnp.concatenate([kk * kdf, kk * kdb], axis=1)
        vt = vr[0, rows, cs].astype(F32).T.astype(BF16)
        return _dot(vt, kcat)

    def init_task():
        sf = jnp.zeros((HEAD_W, HEAD_W), F32)
        sb = jnp.zeros((HEAD_W, HEAD_W), F32)
        if n_ctx:
            incs = [increment(kc_ref, vc_ref, nn) for nn in range(n_ctx // rc)]
            for inc in incs:
                sf = cdf * sf + inc[:, :HEAD_W]
            for inc in reversed(incs):
                sb = cdb * sb + inc[:, HEAD_W:]
        state["sf"], state["sb"] = sf, sb

    def inc_task(nn):
        inc_ref[nn] = increment(k_ref, v_ref, nn)

    def scan_task():
        sf, sb = state["sf"], state["sb"]
        for nn in range(nc):
            st_ref[nn, :, 0:HEAD_W] = sf.astype(BF16)
            sf = cdf * sf + inc_ref[nn, :, 0:HEAD_W]
        for nn in reversed(range(nc)):
            st_ref[nn, :, HEAD_W:] = sb.astype(BF16)
            sb = cdb * sb + inc_ref[nn, :, HEAD_W:]

    def out_task(nn):
        rows = slice(nn * rc, (nn + 1) * rc)
        q = q_ref[0, rows, cs]
        s = _dot_nt(q, k_ref[0, rows, cs]) * dmat
        inner = _dot(s.astype(BF16), v_ref[0, rows, cs])
        qcat = jnp.concatenate([q * qdf, q * qdb], axis=1)
        out = inner + _dot_nt(qcat, st_ref[nn])
        o_ref[0, rows, cs] = _rms(out).astype(BF16) * g_ref[0, rows, cs]

    inc_tasks = [init_task] + [functools.partial(inc_task, nn) for nn in range(nc)]
    return inc_tasks, scan_task, [functools.partial(out_task, nn) for nn in range(nc)]


def _ret_kernel(lg_ref, q_ref, g_ref, k_ref, v_ref, o_ref, inc_ref, st_ref, *, layer, n, rc,
                heads):
    tasks = []
    for hh in range(heads):
        head = pl.program_id(1) * heads + hh
        tasks.append(_retention_tasks(
            lg_ref[layer, 0, head], lg_ref[layer, 1, head], slice(hh * LANES, (hh + 1) * LANES),
            q_ref, g_ref, k_ref, v_ref, None, None, o_ref, inc_ref.at[hh], st_ref.at[hh],
            n=n, n_ctx=0, rc=rc))
    for i in range(len(tasks[0][0])):
        for inc_tasks, _, _ in tasks:
            inc_tasks[i]()
    for _, scan_task, _ in tasks:
        scan_task()
    for i in range(len(tasks[0][2])):
        for _, _, out_tasks in tasks:
            out_tasks[i]()


def _retention_ctx(log_gamma, layer, zc, sec, n_b, n, *, rc, heads):
    width = heads * LANES
    head_block = lambda b, h: h
    return pl.pallas_call(
        functools.partial(_ret_kernel, layer=layer, n=n, rc=rc, heads=heads),
        grid=(n_b, HEADS // heads),
        in_specs=[pl.BlockSpec(memory_space=pltpu.SMEM)]
        + [_z_spec(sec[name], n, width, True, col_of=head_block)
           for name in ("rq", "rg", "rk", "rv")],
        out_specs=pl.BlockSpec((1, n, width), lambda b, h: (b, 0, h)),
        out_shape=jax.ShapeDtypeStruct((n_b, n, BRANCH_W), BF16),
        scratch_shapes=[pltpu.VMEM((heads, n // rc, HEAD_W, 2 * HEAD_W), F32),
                        pltpu.VMEM((heads, n // rc, HEAD_W, 2 * HEAD_W), BF16)],
        compiler_params=pltpu.CompilerParams(vmem_limit_bytes=VMEM_LIMIT),
        name="retention_ctx",
    )(log_gamma, zc, zc, zc, zc)


def _split_maps(q, lane):
    zero = jnp.zeros_like(q)
    return jnp.concatenate([jnp.where(lane < DIFF_DQK, q, zero),
                            jnp.where(lane >= DIFF_DQK, q, zero)], axis=0)


def _diff_ctx_kernel(lam_ref, q_ref, g_ref, k_ref, v_ref, o_ref, *, layer, tq, post_scale):
    lane = lax.broadcasted_iota(jnp.int32, (tq, LANES), 1)
    for h in range(HEADS):
        cs = slice(h * LANES, (h + 1) * LANES)
        s = _dot_nt(_split_maps(q_ref[0, :, cs], lane), k_ref[0, :, cs])
        p = jnp.exp2(s - jnp.max(s, axis=-1, keepdims=True))
        o = _dot(p.astype(BF16), v_ref[0, :, cs]) / jnp.sum(p, axis=-1, keepdims=True)
        d = o[:tq] - lam_ref[layer] * o[tq:]
        o_ref[0, :, cs] = (_rms(d) * post_scale * g_ref[0, :, cs].astype(F32)).astype(BF16)


def _diff_attention_ctx(lam, layer, zc, sec, n_b, n, *, post_scale):
    return pl.pallas_call(
        functools.partial(_diff_ctx_kernel, layer=layer, tq=n, post_scale=post_scale),
        grid=(n_b,),
        in_specs=[pl.BlockSpec(memory_space=pltpu.SMEM)]
        + [_z_spec(sec[name], n, SEC, True) for name in ("dq", "dg", "dk", "dv")],
        out_specs=pl.BlockSpec((1, n, BRANCH_W), lambda b: (b, 0, 0)),
        out_shape=jax.ShapeDtypeStruct((n_b, n, BRANCH_W), BF16),
        compiler_params=pltpu.CompilerParams(vmem_limit_bytes=VMEM_LIMIT),
        name="diff_attention_ctx",
    )(lam, zc, zc, zc, zc)


def _mixer_kernel(lam_ref, lg_ref, q_ref, g_ref, kc_ref, vc_ref, kl_ref, vl_ref,
                  rq_ref, rg_ref, rk_ref, rv_ref, rkc_ref, rvc_ref, od_ref, or_ref,
                  kall, vt, s0, s1, p0, p1, d0, d1, inc_ref, st_ref, *, layer, n, n_ctx, ts,
                  rc, post_scale):
    n_sub = n // ts
    nk = kall.shape[0]
    kall[0:n_ctx, :] = kc_ref[0]
    kall[n_ctx:, :] = kl_ref[0]
    for c in range(nk // LANES):
        src, base = (vc_ref, 0) if c * LANES < n_ctx else (vl_ref, n_ctx)
        rows = slice(c * LANES - base, (c + 1) * LANES - base)
        vt[0:LANES, c * LANES:(c + 1) * LANES] = src[0, rows, :].astype(F32).T.astype(BF16)
    vt[LANES:, :] = jnp.ones((vt.shape[0] - LANES, nk), BF16)
    lam = lam_ref[layer]
    lane = lax.broadcasted_iota(jnp.int32, (ts, LANES), 1)
    s_bufs, p_bufs, d_bufs = (s0, s1), (p0, p1), (d0, d1)

    def stage_a(t):
        rows = slice(t * ts, (t + 1) * ts)
        s = _dot_nt(kall[...], _split_maps(q_ref[0, rows, :], lane))
        s_bufs[t % 2][...] = s
        d_bufs[t % 2][...] = jnp.max(s, axis=0, keepdims=True)

    def stage_b(t):
        p_bufs[t % 2][...] = jnp.exp2(s_bufs[t % 2][...] - d_bufs[t % 2][...]).astype(BF16)

    def stage_c(t):
        rows = slice(t * ts, (t + 1) * ts)
        oe = _dot(vt[...], p_bufs[t % 2][...])
        ot = oe[:LANES] / oe[LANES:LANES + 1]
        dt = ot[:, :ts] - lam * ot[:, ts:]
        dt = dt * lax.rsqrt(jnp.mean(dt * dt, axis=0, keepdims=True) + EPS)
        od_ref[0, rows, :] = (dt.T * post_scale * g_ref[0, rows, :].astype(F32)).astype(BF16)

    head = pl.program_id(1)
    inc_tasks, scan_task, out_tasks = _retention_tasks(
        lg_ref[layer, 0, head], lg_ref[layer, 1, head], slice(0, LANES),
        rq_ref, rg_ref, rk_ref, rv_ref, rkc_ref, rvc_ref, or_ref, inc_ref, st_ref,
        n=n, n_ctx=n_ctx, rc=rc)
    n_steps = n_sub + 2
    extra = [[] for _ in range(n_steps)]
    half = (len(inc_tasks) + 1) // 2
    extra[0] += inc_tasks[:half]
    extra[1] += inc_tasks[half:]
    extra[2].append(scan_task)
    for i, task in enumerate(out_tasks):
        extra[2 + (i * (n_steps - 2)) // len(out_tasks)].append(task)

    for t in range(n_steps):
        if t >= 2:
            stage_c(t - 2)
        if 1 <= t <= n_sub:
            stage_b(t - 1)
        if t < n_sub:
            stage_a(t)
        for task in extra[t]:
            task()


def _latent_mixers(lam, log_gamma, layer, z, sec, n_b, n, zc, sec_c, n_ctx, *, ts, rc,
                   post_scale):
    nk = n_ctx + n
    head_block = lambda b, h: h
    lat = lambda name: _z_spec(sec[name], n, LANES, False, col_of=head_block)
    cx = lambda name: _z_spec(sec_c[name], n_ctx, LANES, True, col_of=head_block)
    out_spec = pl.BlockSpec((1, n, LANES), lambda b, h: (b, 0, h))
    out_shape = jax.ShapeDtypeStruct((n_b, n, BRANCH_W), BF16)
    o_diff, o_ret = pl.pallas_call(
        functools.partial(_mixer_kernel, layer=layer, n=n, n_ctx=n_ctx, ts=ts, rc=rc,
                          post_scale=post_scale),
        grid=(n_b, HEADS),
        in_specs=[pl.BlockSpec(memory_space=pltpu.SMEM), pl.BlockSpec(memory_space=pltpu.SMEM),
                  lat("dq"), lat("dg"), cx("dk"), cx("dv"), lat("dk"), lat("dv"),
                  lat("rq"), lat("rg"), lat("rk"), lat("rv"), cx("rk"), cx("rv")],
        out_specs=[out_spec, out_spec],
        out_shape=[out_shape, out_shape],
        scratch_shapes=[pltpu.VMEM((nk, LANES), BF16), pltpu.VMEM((LANES + 16, nk), BF16),
                        pltpu.VMEM((nk, 2 * ts), F32), pltpu.VMEM((nk, 2 * ts), F32),
                        pltpu.VMEM((nk, 2 * ts), BF16), pltpu.VMEM((nk, 2 * ts), BF16),
                        pltpu.VMEM((1, 2 * ts), F32), pltpu.VMEM((1, 2 * ts), F32),
                        pltpu.VMEM((n // rc, HEAD_W, 2 * HEAD_W), F32),
                        pltpu.VMEM((n // rc, HEAD_W, 2 * HEAD_W), BF16)],
        compiler_params=pltpu.CompilerParams(vmem_limit_bytes=VMEM_LIMIT),
        name="latent_mixers",
    )(lam, log_gamma, z, z, zc, zc, z, z, z, z, z, z, zc, zc)
    return o_ret, o_diff


def _merge_kernel(x_ref, gate_ref, gpost_ref, oret_ref, odiff_ref, u_ref, vn_ref, mg_ref,
                  gr_ref, gm_ref, gd_ref, ws_ref, bs_ref, wbo_ref, wo_ref, out_ref, *, tm):
    rows = []
    for c in range(tm // CHUNK):
        cols = []
        for g in range(HEADS):
            blk = vn_ref[0, c * CHUNK:(c + 1) * CHUNK, g * LANES:(g + 1) * LANES]
            cols.append(_dot(ws_ref[g], blk) + bs_ref[g])
        rows.append(jnp.concatenate(cols, axis=1))
    sp = jnp.concatenate(rows, axis=0)
    o_mlp = (u_ref[0].astype(F32) * sp * mg_ref[0].astype(F32)).astype(BF16)

    def gated(gate2_ref, o, w):
        gate = jnp.concatenate([gate2_ref[0, 0], gate2_ref[0, 1]], axis=1).astype(F32)
        return gate * _dot(o, w)

    t = (gated(gr_ref, oret_ref[0], wbo_ref[0]) + gated(gm_ref, o_mlp, wbo_ref[1])
         + gated(gd_ref, odiff_ref[0], wbo_ref[2]))
    y = _dot(t.astype(BF16), wo_ref[...])
    out_ref[0] = x_ref[0] + gate_ref[...] * (_rms(y) * gpost_ref[...])


def _merge(xs, l, mod4, row_of, g_post, o_ret, o_diff, z, sec, ws, bs, wbo, wo, *, tm):
    n_b, n, _ = xs.shape
    row_block = lambda b, i: i

    def zsec(name):
        return _z_spec(sec[name], tm, SEC, False, row_of=row_block)

    def zgate(k):
        return pl.BlockSpec((1, 2, tm, SEC), lambda b, i: (b, k, i, 0))

    in_specs = [
        pl.BlockSpec((1, tm, D_MODEL), lambda b, i: (b, i, 0)),
        _mod_spec(l, row_of, 2),
        _layer_spec(l, (1, D_MODEL)),
        pl.BlockSpec((1, tm, BRANCH_W), lambda b, i: (b, i, 0)),
        pl.BlockSpec((1, tm, BRANCH_W), lambda b, i: (b, i, 0)),
        zsec("mu"), zsec("mv"), zsec("mg"), zgate(0), zgate(1), zgate(2),
        _layer_spec(l, (HEADS, CHUNK, CHUNK)),
        _layer_spec(l, (HEADS, CHUNK, LANES)),
        _layer_spec(l, (N_BRANCH, BRANCH_W, D_MODEL)),
        _layer_spec(l, (D_MODEL, D_MODEL)),
    ]
    return pl.pallas_call(
        functools.partial(_merge_kernel, tm=tm),
        grid=(n_b, n // tm),
        in_specs=in_specs,
        out_specs=pl.BlockSpec((1, tm, D_MODEL), lambda b, i: (b, i, 0)),
        out_shape=jax.ShapeDtypeStruct(xs.shape, F32),
        compiler_params=pltpu.CompilerParams(vmem_limit_bytes=VMEM_LIMIT),
        name="merge",
    )(xs, mod4, g_post, o_ret, o_diff, z, z, z, z, z, z, ws, bs, wbo, wo)


def _rope_tables(n_lat):
    rows = n_lat // GRID_W
    row_pos = jnp.repeat(jnp.arange(rows, dtype=F32), GRID_W)
    col_pos = jnp.tile(jnp.arange(GRID_W, dtype=F32), rows)

    def angles(head_dim):
        n_freq = head_dim // 4
        inv = ROPE_BASE ** (-jnp.arange(n_freq, dtype=F32) / n_freq)
        ang = jnp.concatenate([row_pos[:, None] * inv, col_pos[:, None] * inv], axis=-1)
        return jnp.cos(ang), jnp.sin(ang)

    cos_r, sin_r = angles(HEAD_W)
    cos_d, sin_d = angles(DIFF_DQK)
    zeros = jnp.zeros_like(sin_d)
    c_r = jnp.concatenate([cos_r, cos_r], axis=-1)
    s_r = jnp.concatenate([-sin_r, sin_r], axis=-1)
    c_d = jnp.tile(jnp.concatenate([cos_d, cos_d], axis=-1), (1, 2))
    s_lo = jnp.tile(jnp.concatenate([-sin_d, zeros], axis=-1), (1, 2))
    s_hi = jnp.tile(jnp.concatenate([zeros, sin_d], axis=-1), (1, 2))
    return c_r, s_r, c_d, s_lo, s_hi


@jax.jit
def kernel(x, c, ctx, c_ctx, w_mod, b_mod, g_pre, g_post, w_in, ret_decay_logit, mlp_w_s,
           mlp_b_s, diff_lambda_q, diff_lambda_k, w_branch_out, w_out):
    n_b, n_lat, _ = x.shape
    n_ctx = ctx.shape[1]
    tables = jnp.stack(_rope_tables(n_lat))

    cond_rows = 16
    ctx_row = n_b
    cc = jnp.zeros((cond_rows, D_MODEL), F32).at[:n_b].set(c).at[ctx_row].set(c_ctx)
    mod4 = _modulation(cc, w_mod, b_mod).reshape(DEPTH, cond_rows, 1, 3 * D_MODEL)
    lat_row = lambda b: b
    ctx_row_of = lambda b: ctx_row

    log_gamma = -jax.nn.softplus(-ret_decay_logit.astype(F32))
    lam_inits = [0.8 - 0.6 * math.exp(-0.3 * l) for l in range(DEPTH)]
    lam = (jnp.exp(jnp.sum(diff_lambda_q[:, 0] * diff_lambda_k[:, 0], axis=-1))
           - jnp.exp(jnp.sum(diff_lambda_q[:, 1] * diff_lambda_k[:, 1], axis=-1))
           + jnp.asarray(lam_inits, F32)).astype(F32)

    w_sec = w_in
    ws = mlp_w_s.astype(BF16)
    bs = jnp.broadcast_to(mlp_b_s[..., None], (DEPTH, HEADS, CHUNK, LANES)).astype(F32)
    wbo = w_branch_out.astype(BF16)
    wo = w_out.astype(BF16)
    g_pre3 = g_pre.reshape(DEPTH, 1, D_MODEL)
    g_post3 = g_post.reshape(DEPTH, 1, D_MODEL)

    for l in range(DEPTH):
        last = l == DEPTH - 1
        post_scale = 1.0 - lam_inits[l]

        ctx_flat = ctx.reshape(1, n_b * n_ctx, D_MODEL)
        if last:
            zc = _projection(ctx_flat, l, g_pre3, mod4, ctx_row_of, w_sec, lambda j: j,
                             KV_KINDS, None)
            sec_c = KV_SEC
        else:
            zc = _projection(ctx_flat, l, g_pre3, mod4, ctx_row_of, w_sec, _full_src,
                             FULL_KINDS, None)
            sec_c = FULL_SEC
            co_ret = _retention_ctx(log_gamma, l, zc, sec_c, n_b, n_ctx, rc=RET_CHUNK,
                                    heads=HEADS)
            co_diff = _diff_attention_ctx(lam, l, zc, sec_c, n_b, n_ctx, post_scale=post_scale)
            ctx_next = _merge(ctx_flat, l, mod4, ctx_row_of, g_post3,
                              co_ret.reshape(1, n_b * n_ctx, BRANCH_W),
                              co_diff.reshape(1, n_b * n_ctx, BRANCH_W), zc, sec_c,
                              ws, bs, wbo, wo, tm=MERGE_TM)
            ctx_next = ctx_next.reshape(n_b, n_ctx, D_MODEL)

        z = _projection(x, l, g_pre3, mod4, lat_row, w_sec, _full_src, FULL_KINDS, tables)
        o_ret, o_diff = _latent_mixers(lam, log_gamma, l, z, FULL_SEC, n_b, n_lat, zc, sec_c,
                                       n_ctx, ts=DIFF_TS, rc=RET_CHUNK, post_scale=post_scale)
        x = _merge(x, l, mod4, lat_row, g_post3, o_ret, o_diff, z, FULL_SEC, ws, bs, wbo, wo,
                   tm=MERGE_TM)
        if not last:
            ctx = ctx_next
    return x
```

```python
import functools
import math

import jax
import jax.numpy as jnp
from jax import lax
from jax.experimental import pallas as pl
from jax.experimental.pallas import tpu as pltpu

F32 = jnp.float32
BF16 = jnp.bfloat16

D_MODEL = 1024
DEPTH = 2
GRID_W = 64
N_BRANCH = 3
BRANCH_W = D_MODEL // 2
HEADS = 4
HEAD_W = BRANCH_W // HEADS
DIFF_DQK = HEAD_W // 2
CHUNK = 128
ROPE_BASE = 10000.0
EPS = 1e-6
RET_SCALE = HEAD_W ** -0.5
DIFF_Q_SCALE = (DIFF_DQK ** -0.5) * math.log2(math.e)

LANES = 128
SEC = BRANCH_W
MERGE_COLS = N_BRANCH * D_MODEL
KV_COLS = 4 * BRANCH_W
IN_COLS = KV_COLS + 7 * BRANCH_W + MERGE_COLS
VMEM_LIMIT = 56 * 1024 * 1024

PROJ_SUB = 256
PROJ_OUT_SLOTS = 2
DIFF_TS = 256
RET_CHUNK = 256
MERGE_TM = 512

FULL_KINDS = ("sigmoid",) * 6 + ("rk", "plain", "dk", "plain", "rq", "silu", "dq", "silu",
                                 "plain", "ln", "silu")
FULL_SEC = dict(rk=6, rv=7, dk=8, dv=9, rq=10, rg=11, dq=12, dg=13, mu=14, mv=15, mg=16)
KV_KINDS = ("rk", "plain", "dk", "plain")
KV_SEC = dict(rk=0, rv=1, dk=2, dv=3)
_N_MERGE_SEC = MERGE_COLS // SEC
_N_REST_SEC = (KV_COLS + 7 * BRANCH_W) // SEC


def _full_src(j):
    return jnp.where(j < _N_MERGE_SEC, j + _N_REST_SEC, j - _N_MERGE_SEC)


def _dot(a, b):
    return jnp.dot(a, b, preferred_element_type=F32)


def _dot_nt(a, b):
    return lax.dot_general(a, b, (((1,), (1,)), ((), ())), preferred_element_type=F32)


def _rms(v):
    return v * lax.rsqrt(jnp.mean(v * v, axis=-1, keepdims=True) + EPS)


def _mod_spec(l, row_of, k):
    return pl.BlockSpec((None, None, 1, D_MODEL), lambda *idx: (l, row_of(idx[0]), 0, k))


def _layer_spec(l, shape, **kwargs):
    zeros = (0,) * len(shape)
    return pl.BlockSpec((None,) + tuple(shape), lambda *idx: (l,) + zeros, **kwargs)


def _z_spec(sec_idx, rows, width, flat, col_of=lambda *idx: 0, row_of=None):
    if row_of is None:
        row_of = (lambda *idx: idx[0]) if flat else (lambda *idx: 0)
    group_of = (lambda *idx: 0) if flat else (lambda *idx: idx[0])
    return pl.BlockSpec((1, None, rows, width),
                        lambda *idx: (group_of(*idx), sec_idx, row_of(*idx), col_of(*idx)))


def _split_bf16(a):
    hi = a.astype(BF16)
    return hi, (a - hi.astype(F32)).astype(BF16)


def _mod_kernel(c_ref, w_ref, b_ref, o_ref):
    cv = c_ref[...]
    c_hi, c_lo = _split_bf16(cv * jax.nn.sigmoid(cv))
    w_hi, w_lo = _split_bf16(w_ref[0])
    o_ref[0] = _dot(c_hi, w_hi) + _dot(c_hi, w_lo) + _dot(c_lo, w_hi) + b_ref[0]


def _modulation(cc, w_mod, b_mod):
    rows = cc.shape[0]
    return pl.pallas_call(
        _mod_kernel,
        grid=(DEPTH, 3),
        in_specs=[
            pl.BlockSpec((rows, D_MODEL), lambda l, j: (0, 0)),
            pl.BlockSpec((1, D_MODEL, D_MODEL), lambda l, j: (l, 0, j)),
            pl.BlockSpec((1, 1, D_MODEL), lambda l, j: (l, 0, j)),
        ],
        out_specs=pl.BlockSpec((1, rows, D_MODEL), lambda l, j: (l, 0, j)),
        out_shape=jax.ShapeDtypeStruct((DEPTH, rows, 3 * D_MODEL), F32),
        compiler_params=pltpu.CompilerParams(vmem_limit_bytes=VMEM_LIMIT),
        name="modulation",
    )(cc, w_mod, b_mod.reshape(DEPTH, 1, 3 * D_MODEL))


def _rope_ret(a, c, s):
    outs = []
    for h in range(HEADS):
        ah = a[:, h * LANES:(h + 1) * LANES]
        outs.append(ah * c + pltpu.roll(ah, LANES // 2, 1) * s)
    return jnp.concatenate(outs, axis=1)


def _rope_diff(a, c, s_lo, s_hi):
    outs = []
    for h in range(HEADS):
        ah = a[:, h * LANES:(h + 1) * LANES]
        outs.append(ah * c + pltpu.roll(ah, LANES - DIFF_DQK // 2, 1) * s_lo
                    + pltpu.roll(ah, DIFF_DQK // 2, 1) * s_hi)
    return jnp.concatenate(outs, axis=1)


def _proj_kernel(*refs, layer, kinds, src_of, n_src, rope, n_rows, sub):
    if rope:
        (x_ref, g_ref, shift_ref, scale_ref, w_hbm, tab_hbm, z_hbm,
         h_ref, stage, w_vmem, w_stage, sems_out, sems_w, tab_vmem, sem_tab) = refs
        cr_ref, sr_ref, cd_ref, sdl_ref, sdh_ref = [tab_vmem.at[i] for i in range(5)]
    else:
        (x_ref, g_ref, shift_ref, scale_ref, w_hbm, z_hbm,
         h_ref, stage, w_vmem, w_stage, sems_out, sems_w) = refs
    b = pl.program_id(0)
    first = b == 0
    n_sub = n_rows // sub
    n_sec = len(kinds)
    n_slots = stage.shape[0]

    n_wslots = w_stage.shape[0]

    def weight_copy(j):
        start = src_of(j) * SEC
        cols = pl.ds(start if isinstance(start, int) else pl.multiple_of(start, SEC), SEC)
        slot = j % n_wslots
        return pltpu.make_async_copy(w_hbm.at[layer, :, cols], w_stage.at[slot], sems_w.at[slot])

    def load_weights(j):
        weight_copy(j).wait()
        src = w_stage.at[j % n_wslots]
        dst = w_vmem.at[src_of(j)]
        for r in range(D_MODEL // LANES):
            rows = slice(r * LANES, (r + 1) * LANES)
            dst[rows, :] = src[rows, :].astype(BF16)

        @pl.when(j + n_wslots < n_sec)
        def _():
            weight_copy(j + n_wslots).start()

    def table_copy():
        return pltpu.make_async_copy(tab_hbm, tab_vmem, sem_tab.at[0])

    @pl.when(first)
    def _():
        for j in range(min(n_wslots, n_sec)):
            weight_copy(j).start()
        if rope:
            table_copy().start()

    gs = g_ref[...] * (1.0 + scale_ref[...])
    sh = shift_ref[...]
    for r in range(n_sub):
        xs = x_ref[0, r * sub:(r + 1) * sub, :]
        h_ref[r * sub:(r + 1) * sub, :] = (_rms(xs) * gs + sh).astype(BF16)

    if rope:
        pl.when(first)(lambda: table_copy().wait())

    def epi_rk(a, rows):
        if rope:
            a = _rope_ret(a, cr_ref[rows, :], sr_ref[rows, :])
        return a * RET_SCALE

    def epi_rq(a, rows):
        return _rope_ret(a, cr_ref[rows, :], sr_ref[rows, :]) if rope else a

    def epi_dk(a, rows):
        return _rope_diff(a, cd_ref[rows, :], sdl_ref[rows, :], sdh_ref[rows, :]) if rope else a

    def epi_dq(a, rows):
        return epi_dk(a, rows) * DIFF_Q_SCALE

    def epi_ln(a, rows):
        mu = jnp.mean(a, axis=-1, keepdims=True)
        d = a - mu
        return d * lax.rsqrt(jnp.mean(d * d, axis=-1, keepdims=True) + EPS)

    epilogues = dict(
        plain=lambda a, rows: a,
        sigmoid=lambda a, rows: jax.nn.sigmoid(a.astype(BF16)),
        silu=lambda a, rows: a.astype(BF16) * jax.nn.sigmoid(a.astype(BF16)),
        rk=epi_rk, rq=epi_rq, dk=epi_dk, dq=epi_dq, ln=epi_ln)

    def writeback(j):
        slot = j % n_slots
        return pltpu.make_async_copy(stage.at[slot], z_hbm.at[b, j], sems_out.at[slot])

    def run(epilogue, j):
        w_sec = w_vmem.at[src_of(j)]
        out = stage.at[j % n_slots]
        for r in range(n_sub):
            rows = slice(r * sub, (r + 1) * sub)
            out[rows, :] = epilogue(_dot(h_ref[rows, :], w_sec[...]), rows).astype(BF16)

    def section(j, carry):
        pl.when(first)(functools.partial(load_weights, j))

        @pl.when(j >= n_slots)
        def _():
            writeback(j - n_slots).wait()

        for kind in sorted(set(kinds)):
            idx = [i for i, k in enumerate(kinds) if k == kind]
            cond = j == idx[0]
            for i in idx[1:]:
                cond = cond | (j == i)
            pl.when(cond)(functools.partial(run, epilogues[kind], j))
        writeback(j).start()
        return carry

    lax.fori_loop(0, n_sec, section, 0)
    for j in range(max(0, n_sec - n_slots), n_sec):
        writeback(j).wait()


def _projection(xs, l, g_pre, mod4, row_of, w_all, src_of, kinds, tables):
    n_g, n_rows, _ = xs.shape
    n_sec = len(kinds)
    rope = tables is not None
    n_src = w_all.shape[2] // SEC if src_of is _full_src else n_sec
    in_specs = [
        pl.BlockSpec((1, n_rows, D_MODEL), lambda b: (b, 0, 0)),
        _layer_spec(l, (1, D_MODEL)),
        _mod_spec(l, row_of, 0),
        _mod_spec(l, row_of, 1),
        pl.BlockSpec(memory_space=pl.ANY),
    ]
    args = [xs, g_pre, mod4, mod4, w_all]
    scratch = [pltpu.VMEM((n_rows, D_MODEL), BF16),
               pltpu.VMEM((PROJ_OUT_SLOTS, n_rows, SEC), BF16),
               pltpu.VMEM((n_src, D_MODEL, SEC), BF16),
               pltpu.VMEM((2, D_MODEL, SEC), F32),
               pltpu.SemaphoreType.DMA((PROJ_OUT_SLOTS,)),
               pltpu.SemaphoreType.DMA((2,))]
    if rope:
        in_specs.append(pl.BlockSpec(memory_space=pl.ANY))
        args.append(tables)
        scratch += [pltpu.VMEM(tables.shape, F32), pltpu.SemaphoreType.DMA((1,))]
    return pl.pallas_call(
        functools.partial(_proj_kernel, layer=l, kinds=kinds, src_of=src_of, n_src=n_src,
                          rope=rope, n_rows=n_rows, sub=PROJ_SUB),
        grid=(n_g,),
        in_specs=in_specs,
        out_specs=pl.BlockSpec(memory_space=pl.ANY),
        out_shape=jax.ShapeDtypeStruct((n_g, n_sec, n_rows, SEC), BF16),
        scratch_shapes=scratch,
        compiler_params=pltpu.CompilerParams(vmem_limit_bytes=VMEM_LIMIT),
        name="projection_rope" if rope else "projection",
    )(*args)


def _retention_tasks(lgf, lgb, cs, q_ref, g_ref, k_ref, v_ref, kc_ref, vc_ref, o_ref, inc_ref,
                     st_ref, *, n, n_ctx, rc):
    nc = n // rc
    pos = lax.broadcasted_iota(jnp.int32, (rc, LANES), 0).astype(F32)
    kdf = jnp.exp(lgf * (rc - 1.0 - pos)).astype(BF16)
    kdb = jnp.exp(lgb * pos).astype(BF16)
    cdf = jnp.exp(jnp.full((1, LANES), lgf * rc, F32))
    cdb = jnp.exp(jnp.full((1, LANES), lgb * rc, F32))
    qdf = jnp.exp(lgf * (pos + 1.0)).astype(BF16)
    qdb = jnp.exp(lgb * (rc - pos)).astype(BF16)
    ii = lax.broadcasted_iota(jnp.int32, (rc, rc), 0)
    jj = lax.broadcasted_iota(jnp.int32, (rc, rc), 1)
    dist = (ii - jj).astype(F32)
    dmat = (jnp.where(dist >= 0, jnp.exp(lgf * jnp.maximum(dist, 0.0)), 0.0)
            + jnp.where(dist <= 0, jnp.exp(lgb * jnp.maximum(-dist, 0.0)), 0.0))
    state = {}

    def increment(kr, vr, nn):
        rows = slice(nn * rc, (nn + 1) * rc)
        kk = kr[0, rows, cs]
        kcat = jnp.concatenate([kk * kdf, kk * kdb], axis=1)
        vt = vr[0, rows, cs].astype(F32).T.astype(BF16)
        return _dot(vt, kcat)

    def init_task():
        sf = jnp.zeros((HEAD_W, HEAD_W), F32)
        sb = jnp.zeros((HEAD_W, HEAD_W), F32)
        if n_ctx:
            incs = [increment(kc_ref, vc_ref, nn) for nn in range(n_ctx // rc)]
            for inc in incs:
                sf = cdf * sf + inc[:, :HEAD_W]
            for inc in reversed(incs):
                sb = cdb * sb + inc[:, HEAD_W:]
        state["sf"], state["sb"] = sf, sb

    def inc_task(nn):
        inc_ref[nn] = increment(k_ref, v_ref, nn)

    def scan_task():
        sf, sb = state["sf"], state["sb"]
        for nn in range(nc):
            st_ref[nn, :, 0:HEAD_W] = sf.astype(BF16)
            sf = cdf * sf + inc_ref[nn, :, 0:HEAD_W]
        for nn in reversed(range(nc)):
            st_ref[nn, :, HEAD_W:] = sb.astype(BF16)
            sb = cdb * sb + inc_ref[nn, :, HEAD_W:]

    def out_task(nn):
        rows = slice(nn * rc, (nn + 1) * rc)
        q = q_ref[0, rows, cs]
        s = _dot_nt(q, k_ref[0, rows, cs]) * dmat
        inner = _dot(s.astype(BF16), v_ref[0, rows, cs])
        qcat = jnp.concatenate([q * qdf, q * qdb], axis=1)
        out = inner + _dot_nt(qcat, st_ref[nn])
        o_ref[0, rows, cs] = _rms(out).astype(BF16) * g_ref[0, rows, cs]

    inc_tasks = [init_task] + [functools.partial(inc_task, nn) for nn in range(nc)]
    return inc_tasks, scan_task, [functools.partial(out_task, nn) for nn in range(nc)]


def _ret_kernel(lg_ref, q_ref, g_ref, k_ref, v_ref, o_ref, inc_ref, st_ref, *, layer, n, rc,
                heads):
    tasks = []
    for hh in range(heads):
        head = pl.program_id(1) * heads + hh
        tasks.append(_retention_tasks(
            lg_ref[layer, 0, head], lg_ref[layer, 1, head], slice(hh * LANES, (hh + 1) * LANES),
            q_ref, g_ref, k_ref, v_ref, None, None, o_ref, inc_ref.at[hh], st_ref.at[hh],
            n=n, n_ctx=0, rc=rc))
    for i in range(len(tasks[0][0])):
        for inc_tasks, _, _ in tasks:
            inc_tasks[i]()
    for _, scan_task, _ in tasks:
        scan_task()
    for i in range(len(tasks[0][2])):
        for _, _, out_tasks in tasks:
            out_tasks[i]()


def _retention_ctx(log_gamma, layer, zc, sec, n_b, n, *, rc, heads):
    width = heads * LANES
    head_block = lambda b, h: h
    return pl.pallas_call(
        functools.partial(_ret_kernel, layer=layer, n=n, rc=rc, heads=heads),
        grid=(n_b, HEADS // heads),
        in_specs=[pl.BlockSpec(memory_space=pltpu.SMEM)]
        + [_z_spec(sec[name], n, width, True, col_of=head_block)
           for name in ("rq", "rg", "rk", "rv")],
        out_specs=pl.BlockSpec((1, n, width), lambda b, h: (b, 0, h)),
        out_shape=jax.ShapeDtypeStruct((n_b, n, BRANCH_W), BF16),
        scratch_shapes=[pltpu.VMEM((heads, n // rc, HEAD_W, 2 * HEAD_W), F32),
                        pltpu.VMEM((heads, n // rc, HEAD_W, 2 * HEAD_W), BF16)],
        compiler_params=pltpu.CompilerParams(vmem_limit_bytes=VMEM_LIMIT),
        name="retention_ctx",
    )(log_gamma, zc, zc, zc, zc)


def _split_maps(q, lane):
    zero = jnp.zeros_like(q)
    return jnp.concatenate([jnp.where(lane < DIFF_DQK, q, zero),
                            jnp.where(lane >= DIFF_DQK, q, zero)], axis=0)


def _diff_ctx_kernel(lam_ref, q_ref, g_ref, k_ref, v_ref, o_ref, *, layer, tq, post_scale):
    lane = lax.broadcasted_iota(jnp.int32, (tq, LANES), 1)
    for h in range(HEADS):
        cs = slice(h * LANES, (h + 1) * LANES)
        s = _dot_nt(_split_maps(q_ref[0, :, cs], lane), k_ref[0, :, cs])
        p = jnp.exp2(s - jnp.max(s, axis=-1, keepdims=True))
        o = _dot(p.astype(BF16), v_ref[0, :, cs]) / jnp.sum(p, axis=-1, keepdims=True)
        d = o[:tq] - lam_ref[layer] * o[tq:]
        o_ref[0, :, cs] = (_rms(d) * post_scale * g_ref[0, :, cs].astype(F32)).astype(BF16)


def _diff_attention_ctx(lam, layer, zc, sec, n_b, n, *, post_scale):
    return pl.pallas_call(
        functools.partial(_diff_ctx_kernel, layer=layer, tq=n, post_scale=post_scale),
        grid=(n_b,),
        in_specs=[pl.BlockSpec(memory_space=pltpu.SMEM)]
        + [_z_spec(sec[name], n, SEC, True) for name in ("dq", "dg", "dk", "dv")],
        out_specs=pl.BlockSpec((1, n, BRANCH_W), lambda b: (b, 0, 0)),
        out_shape=jax.ShapeDtypeStruct((n_b, n, BRANCH_W), BF16),
        compiler_params=pltpu.CompilerParams(vmem_limit_bytes=VMEM_LIMIT),
        name="diff_attention_ctx",
    )(lam, zc, zc, zc, zc)


def _mixer_kernel(lam_ref, lg_ref, q_ref, g_ref, kc_ref, vc_ref, kl_ref, vl_ref,
                  rq_ref, rg_ref, rk_ref, rv_ref, rkc_ref, rvc_ref, od_ref, or_ref,
                  kall, vt, s0, s1, p0, p1, d0, d1, inc_ref, st_ref, *, layer, n, n_ctx, ts,
                  rc, post_scale):
    n_sub = n // ts
    nk = kall.shape[0]
    kall[0:n_ctx, :] = kc_ref[0]
    kall[n_ctx:, :] = kl_ref[0]
    for c in range(nk // LANES):
        src, base = (vc_ref, 0) if c * LANES < n_ctx else (vl_ref, n_ctx)
        rows = slice(c * LANES - base, (c + 1) * LANES - base)
        vt[0:LANES, c * LANES:(c + 1) * LANES] = src[0, rows, :].astype(F32).T.astype(BF16)
    vt[LANES:, :] = jnp.ones((vt.shape[0] - LANES, nk), BF16)
    lam = lam_ref[layer]
    lane = lax.broadcasted_iota(jnp.int32, (ts, LANES), 1)
    s_bufs, p_bufs, d_bufs = (s0, s1), (p0, p1), (d0, d1)

    def stage_a(t):
        rows = slice(t * ts, (t + 1) * ts)
        s = _dot_nt(kall[...], _split_maps(q_ref[0, rows, :], lane))
        s_bufs[t % 2][...] = s
        d_bufs[t % 2][...] = jnp.max(s, axis=0, keepdims=True)

    def stage_b(t):
        p_bufs[t % 2][...] = jnp.exp2(s_bufs[t % 2][...] - d_bufs[t % 2][...]).astype(BF16)

    def stage_c(t):
        rows = slice(t * ts, (t + 1) * ts)
        oe = _dot(vt[...], p_bufs[t % 2][...])
        ot = oe[:LANES] / oe[LANES:LANES + 1]
        dt = ot[:, :ts] - lam * ot[:, ts:]
        dt = dt * lax.rsqrt(jnp.mean(dt * dt, axis=0, keepdims=True) + EPS)
        od_ref[0, rows, :] = (dt.T * post_scale * g_ref[0, rows, :].astype(F32)).astype(BF16)

    head = pl.program_id(1)
    inc_tasks, scan_task, out_tasks = _retention_tasks(
        lg_ref[layer, 0, head], lg_ref[layer, 1, head], slice(0, LANES),
        rq_ref, rg_ref, rk_ref, rv_ref, rkc_ref, rvc_ref, or_ref, inc_ref, st_ref,
        n=n, n_ctx=n_ctx, rc=rc)
    n_steps = n_sub + 2
    extra = [[] for _ in range(n_steps)]
    half = (len(inc_tasks) + 1) // 2
    extra[0] += inc_tasks[:half]
    extra[1] += inc_tasks[half:]
    extra[2].append(scan_task)
    for i, task in enumerate(out_tasks):
        extra[2 + (i * (n_steps - 2)) // len(out_tasks)].append(task)

    for t in range(n_steps):
        if t >= 2:
            stage_c(t - 2)
        if 1 <= t <= n_sub:
            stage_b(t - 1)
        if t < n_sub:
            stage_a(t)
        for task in extra[t]:
            task()


def _latent_mixers(lam, log_gamma, layer, z, sec, n_b, n, zc, sec_c, n_ctx, *, ts, rc,
                   post_scale):
    nk = n_ctx + n
    head_block = lambda b, h: h
    lat = lambda name: _z_spec(sec[name], n, LANES, False, col_of=head_block)
    cx = lambda name: _z_spec(sec_c[name], n_ctx, LANES, True, col_of=head_block)
    out_spec = pl.BlockSpec((1, n, LANES), lambda b, h: (b, 0, h))
    out_shape = jax.ShapeDtypeStruct((n_b, n, BRANCH_W), BF16)
    o_diff, o_ret = pl.pallas_call(
        functools.partial(_mixer_kernel, layer=layer, n=n, n_ctx=n_ctx, ts=ts, rc=rc,
                          post_scale=post_scale),
        grid=(n_b, HEADS),
        in_specs=[pl.BlockSpec(memory_space=pltpu.SMEM), pl.BlockSpec(memory_space=pltpu.SMEM),
                  lat("dq"), lat("dg"), cx("dk"), cx("dv"), lat("dk"), lat("dv"),
                  lat("rq"), lat("rg"), lat("rk"), lat("rv"), cx("rk"), cx("rv")],
        out_specs=[out_spec, out_spec],
        out_shape=[out_shape, out_shape],
        scratch_shapes=[pltpu.VMEM((nk, LANES), BF16), pltpu.VMEM((LANES + 16, nk), BF16),
                        pltpu.VMEM((nk, 2 * ts), F32), pltpu.VMEM((nk, 2 * ts), F32),
                        pltpu.VMEM((nk, 2 * ts), BF16), pltpu.VMEM((nk, 2 * ts), BF16),
                        pltpu.VMEM((1, 2 * ts), F32), pltpu.VMEM((1, 2 * ts), F32),
                        pltpu.VMEM((n // rc, HEAD_W, 2 * HEAD_W), F32),
                        pltpu.VMEM((n // rc, HEAD_W, 2 * HEAD_W), BF16)],
        compiler_params=pltpu.CompilerParams(vmem_limit_bytes=VMEM_LIMIT),
        name="latent_mixers",
    )(lam, log_gamma, z, z, zc, zc, z, z, z, z, z, z, zc, zc)
    return o_ret, o_diff


def _merge_kernel(x_ref, gate_ref, gpost_ref, oret_ref, odiff_ref, u_ref, vn_ref, mg_ref,
                  gr_ref, gm_ref, gd_ref, ws_ref, bs_ref, wbo_ref, wo_ref, out_ref, *, tm):
    rows = []
    for c in range(tm // CHUNK):
        cols = []
        for g in range(HEADS):
            blk = vn_ref[0, c * CHUNK:(c + 1) * CHUNK, g * LANES:(g + 1) * LANES]
            cols.append(_dot(ws_ref[g], blk) + bs_ref[g])
        rows.append(jnp.concatenate(cols, axis=1))
    sp = jnp.concatenate(rows, axis=0)
    o_mlp = (u_ref[0].astype(F32) * sp * mg_ref[0].astype(F32)).astype(BF16)

    def gated(gate2_ref, o, w):
        gate = jnp.concatenate([gate2_ref[0, 0], gate2_ref[0, 1]], axis=1).astype(F32)
        return gate * _dot(o, w)

    t = (gated(gr_ref, oret_ref[0], wbo_ref[0]) + gated(gm_ref, o_mlp, wbo_ref[1])
         + gated(gd_ref, odiff_ref[0], wbo_ref[2]))
    y = _dot(t.astype(BF16), wo_ref[...])
    out_ref[0] = x_ref[0] + gate_ref[...] * (_rms(y) * gpost_ref[...])


def _merge(xs, l, mod4, row_of, g_post, o_ret, o_diff, z, sec, ws, bs, wbo, wo, *, tm):
    n_b, n, _ = xs.shape
    row_block = lambda b, i: i

    def zsec(name):
        return _z_spec(sec[name], tm, SEC, False, row_of=row_block)

    def zgate(k):
        return pl.BlockSpec((1, 2, tm, SEC), lambda b, i: (b, k, i, 0))

    in_specs = [
        pl.BlockSpec((1, tm, D_MODEL), lambda b, i: (b, i, 0)),
        _mod_spec(l, row_of, 2),
        _layer_spec(l, (1, D_MODEL)),
        pl.BlockSpec((1, tm, BRANCH_W), lambda b, i: (b, i, 0)),
        pl.BlockSpec((1, tm, BRANCH_W), lambda b, i: (b, i, 0)),
        zsec("mu"), zsec("mv"), zsec("mg"), zgate(0), zgate(1), zgate(2),
        _layer_spec(l, (HEADS, CHUNK, CHUNK)),
        _layer_spec(l, (HEADS, CHUNK, LANES)),
        _layer_spec(l, (N_BRANCH, BRANCH_W, D_MODEL)),
        _layer_spec(l, (D_MODEL, D_MODEL)),
    ]
    return pl.pallas_call(
        functools.partial(_merge_kernel, tm=tm),
        grid=(n_b, n // tm),
        in_specs=in_specs,
        out_specs=pl.BlockSpec((1, tm, D_MODEL), lambda b, i: (b, i, 0)),
        out_shape=jax.ShapeDtypeStruct(xs.shape, F32),
        compiler_params=pltpu.CompilerParams(vmem_limit_bytes=VMEM_LIMIT),
        name="merge",
    )(xs, mod4, g_post, o_ret, o_diff, z, z, z, z, z, z, ws, bs, wbo, wo)


def _rope_tables(n_lat):
    rows = n_lat // GRID_W
    row_pos = jnp.repeat(jnp.arange(rows, dtype=F32), GRID_W)
    col_pos = jnp.tile(jnp.arange(GRID_W, dtype=F32), rows)

    def angles(head_dim):
        n_freq = head_dim // 4
        inv = ROPE_BASE ** (-jnp.arange(n_freq, dtype=F32) / n_freq)
        ang = jnp.concatenate([row_pos[:, None] * inv, col_pos[:, None] * inv], axis=-1)
        return jnp.cos(ang), jnp.sin(ang)

    cos_r, sin_r = angles(HEAD_W)
    cos_d, sin_d = angles(DIFF_DQK)
    zeros = jnp.zeros_like(sin_d)
    c_r = jnp.concatenate([cos_r, cos_r], axis=-1)
    s_r = jnp.concatenate([-sin_r, sin_r], axis=-1)
    c_d = jnp.tile(jnp.concatenate([cos_d, cos_d], axis=-1), (1, 2))
    s_lo = jnp.tile(jnp.concatenate([-sin_d, zeros], axis=-1), (1, 2))
    s_hi = jnp.tile(jnp.concatenate([zeros, sin_d], axis=-1), (1, 2))
    return c_r, s_r, c_d, s_lo, s_hi


@jax.jit
def kernel(x, c, ctx, c_ctx, w_mod, b_mod, g_pre, g_post, w_in, ret_decay_logit, mlp_w_s,
           mlp_b_s, diff_lambda_q, diff_lambda_k, w_branch_out, w_out):
    n_b, n_lat, _ = x.shape
    n_ctx = ctx.shape[1]
    tables = jnp.stack(_rope_tables(n_lat))

    cond_rows = 16
    ctx_row = n_b
    cc = jnp.zeros((cond_rows, D_MODEL), F32).at[:n_b].set(c).at[ctx_row].set(c_ctx)
    mod4 = _modulation(cc, w_mod, b_mod).reshape(DEPTH, cond_rows, 1, 3 * D_MODEL)
    lat_row = lambda b: b
    ctx_row_of = lambda b: ctx_row

    log_gamma = -jax.nn.softplus(-ret_decay_logit.astype(F32))
    lam_inits = [0.8 - 0.6 * math.exp(-0.3 * l) for l in range(DEPTH)]
    lam = (jnp.exp(jnp.sum(diff_lambda_q[:, 0] * diff_lambda_k[:, 0], axis=-1))
           - jnp.exp(jnp.sum(diff_lambda_q[:, 1] * diff_lambda_k[:, 1], axis=-1))
           + jnp.asarray(lam_inits, F32)).astype(F32)

    w_sec = w_in
    ws = mlp_w_s.astype(BF16)
    bs = jnp.broadcast_to(mlp_b_s[..., None], (DEPTH, HEADS, CHUNK, LANES)).astype(F32)
    wbo = w_branch_out.astype(BF16)
    wo = w_out.astype(BF16)
    g_pre3 = g_pre.reshape(DEPTH, 1, D_MODEL)
    g_post3 = g_post.reshape(DEPTH, 1, D_MODEL)

    for l in range(DEPTH):
        last = l == DEPTH - 1
        post_scale = 1.0 - lam_inits[l]

        ctx_flat = ctx.reshape(1, n_b * n_ctx, D_MODEL)
        if last:
            zc = _projection(ctx_flat, l, g_pre3, mod4, ctx_row_of, w_sec, lambda j: j,
                             KV_KINDS, None)
            sec_c = KV_SEC
        else:
            zc = _projection(ctx_flat, l, g_pre3, mod4, ctx_row_of, w_sec, _full_src,
                             FULL_KINDS, None)
            sec_c = FULL_SEC
            co_ret = _retention_ctx(log_gamma, l, zc, sec_c, n_b, n_ctx, rc=RET_CHUNK,
                                    heads=HEADS)
            co_diff = _diff_attention_ctx(lam, l, zc, sec_c, n_b, n_ctx, post_scale=post_scale)
            ctx_next = _merge(ctx_flat, l, mod4, ctx_row_of, g_post3,
                              co_ret.reshape(1, n_b * n_ctx, BRANCH_W),
                              co_diff.reshape(1, n_b * n_ctx, BRANCH_W), zc, sec_c,
                              ws, bs, wbo, wo, tm=MERGE_TM)
            ctx_next = ctx_next.reshape(n_b, n_ctx, D_MODEL)

        z = _projection(x, l, g_pre3, mod4, lat_row, w_sec, _full_src, FULL_KINDS, tables)
        o_ret, o_diff = _latent_mixers(lam, log_gamma, l, z, FULL_SEC, n_b, n_lat, zc, sec_c,
                                       n_ctx, ts=DIFF_TS, rc=RET_CHUNK, post_scale=post_scale)
        x = _merge(x, l, mod4, lat_row, g_post3, o_ret, o_diff, z, FULL_SEC, ws, bs, wbo, wo,
                   tm=MERGE_TM)
        if not last:
            ctx = ctx_next
    return x
```

```python
import functools
import math

import jax
import jax.numpy as jnp
from jax import lax
from jax.experimental import pallas as pl
from jax.experimental.pallas import tpu as pltpu

F32 = jnp.float32
BF16 = jnp.bfloat16

D_MODEL = 1024
DEPTH = 2
GRID_W = 64
N_BRANCH = 3
BRANCH_W = D_MODEL // 2
HEADS = 4
HEAD_W = BRANCH_W // HEADS
DIFF_DQK = HEAD_W // 2
CHUNK = 128
ROPE_BASE = 10000.0
EPS = 1e-6
RET_SCALE = HEAD_W ** -0.5
DIFF_Q_SCALE = (DIFF_DQK ** -0.5) * math.log2(math.e)

LANES = 128
SEC = BRANCH_W
MERGE_COLS = N_BRANCH * D_MODEL
KV_COLS = 4 * BRANCH_W
IN_COLS = KV_COLS + 7 * BRANCH_W + MERGE_COLS
VMEM_LIMIT = 56 * 1024 * 1024

PROJ_SUB = 256
PROJ_OUT_SLOTS = 2
DIFF_TS = 256
RET_CHUNK = 256
MERGE_TM = 512

FULL_KINDS = ("sigmoid",) * 6 + ("rk", "plain", "dk", "plain", "rq", "silu", "dq", "silu",
                                 "plain", "ln", "silu")
FULL_SEC = dict(rk=6, rv=7, dk=8, dv=9, rq=10, rg=11, dq=12, dg=13, mu=14, mv=15, mg=16)
KV_KINDS = ("rk", "plain", "dk", "plain")
KV_SEC = dict(rk=0, rv=1, dk=2, dv=3)
_N_MERGE_SEC = MERGE_COLS // SEC
_N_REST_SEC = (KV_COLS + 7 * BRANCH_W) // SEC


def _full_src(j):
    return jnp.where(j < _N_MERGE_SEC, j + _N_REST_SEC, j - _N_MERGE_SEC)


def _dot(a, b):
    return jnp.dot(a, b, preferred_element_type=F32)


def _dot_nt(a, b):
    return lax.dot_general(a, b, (((1,), (1,)), ((), ())), preferred_element_type=F32)


def _rms(v):
    return v * lax.rsqrt(jnp.mean(v * v, axis=-1, keepdims=True) + EPS)


def _mod_spec(l, row_of, k):
    return pl.BlockSpec((None, None, 1, D_MODEL), lambda *idx: (l, row_of(idx[0]), 0, k))


def _layer_spec(l, shape, **kwargs):
    zeros = (0,) * len(shape)
    return pl.BlockSpec((None,) + tuple(shape), lambda *idx: (l,) + zeros, **kwargs)


def _z_spec(sec_idx, rows, width, flat, col_of=lambda *idx: 0, row_of=None):
    if row_of is None:
        row_of = (lambda *idx: idx[0]) if flat else (lambda *idx: 0)
    group_of = (lambda *idx: 0) if flat else (lambda *idx: idx[0])
    return pl.BlockSpec((1, None, rows, width),
                        lambda *idx: (group_of(*idx), sec_idx, row_of(*idx), col_of(*idx)))


def _split_bf16(a):
    hi = a.astype(BF16)
    return hi, (a - hi.astype(F32)).astype(BF16)


def _mod_kernel(c_ref, w_ref, b_ref, o_ref):
    cv = c_ref[...]
    c_hi, c_lo = _split_bf16(cv * jax.nn.sigmoid(cv))
    w_hi, w_lo = _split_bf16(w_ref[0])
    o_ref[0] = _dot(c_hi, w_hi) + _dot(c_hi, w_lo) + _dot(c_lo, w_hi) + b_ref[0]


def _modulation(cc, w_mod, b_mod):
    rows = cc.shape[0]
    return pl.pallas_call(
        _mod_kernel,
        grid=(DEPTH, 3),
        in_specs=[
            pl.BlockSpec((rows, D_MODEL), lambda l, j: (0, 0)),
            pl.BlockSpec((1, D_MODEL, D_MODEL), lambda l, j: (l, 0, j)),
            pl.BlockSpec((1, 1, D_MODEL), lambda l, j: (l, 0, j)),
        ],
        out_specs=pl.BlockSpec((1, rows, D_MODEL), lambda l, j: (l, 0, j)),
        out_shape=jax.ShapeDtypeStruct((DEPTH, rows, 3 * D_MODEL), F32),
        compiler_params=pltpu.CompilerParams(vmem_limit_bytes=VMEM_LIMIT),
        name="modulation",
    )(cc, w_mod, b_mod.reshape(DEPTH, 1, 3 * D_MODEL))


def _rope_ret(a, c, s):
    outs = []
    for h in range(HEADS):
        ah = a[:, h * LANES:(h + 1) * LANES]
        outs.append(ah * c + pltpu.roll(ah, LANES // 2, 1) * s)
    return jnp.concatenate(outs, axis=1)


def _rope_diff(a, c, s_lo, s_hi):
    outs = []
    for h in range(HEADS):
        ah = a[:, h * LANES:(h + 1) * LANES]
        outs.append(ah * c + pltpu.roll(ah, LANES - DIFF_DQK // 2, 1) * s_lo
                    + pltpu.roll(ah, DIFF_DQK // 2, 1) * s_hi)
    return jnp.concatenate(outs, axis=1)


def _proj_kernel(*refs, layer, kinds, src_of, n_src, rope, n_rows, sub):
    if rope:
        (x_ref, g_ref, shift_ref, scale_ref, w_hbm, tab_hbm, z_hbm,
         h_ref, stage, w_vmem, w_stage, sems_out, sems_w, tab_vmem, sem_tab) = refs
        cr_ref, sr_ref, cd_ref, sdl_ref, sdh_ref = [tab_vmem.at[i] for i in range(5)]
    else:
        (x_ref, g_ref, shift_ref, scale_ref, w_hbm, z_hbm,
         h_ref, stage, w_vmem, w_stage, sems_out, sems_w) = refs
    b = pl.program_id(0)
    first = b == 0
    n_sub = n_rows // sub
    n_sec = len(kinds)
    n_slots = stage.shape[0]

    n_wslots = w_stage.shape[0]

    def weight_copy(j):
        start = src_of(j) * SEC
        cols = pl.ds(start if isinstance(start, int) else pl.multiple_of(start, SEC), SEC)
        slot = j % n_wslots
        return pltpu.make_async_copy(w_hbm.at[layer, :, cols], w_stage.at[slot], sems_w.at[slot])

    def load_weights(j):
        weight_copy(j).wait()
        src = w_stage.at[j % n_wslots]
        dst = w_vmem.at[src_of(j)]
        for r in range(D_MODEL // LANES):
            rows = slice(r * LANES, (r + 1) * LANES)
            dst[rows, :] = src[rows, :].astype(BF16)

        @pl.when(j + n_wslots < n_sec)
        def _():
            weight_copy(j + n_wslots).start()

    def table_copy():
        return pltpu.make_async_copy(tab_hbm, tab_vmem, sem_tab.at[0])

    @pl.when(first)
    def _():
        for j in range(min(n_wslots, n_sec)):
            weight_copy(j).start()
        if rope:
            table_copy().start()

    gs = g_ref[...] * (1.0 + scale_ref[...])
    sh = shift_ref[...]
    for r in range(n_sub):
        xs = x_ref[0, r * sub:(r + 1) * sub, :]
        h_ref[r * sub:(r + 1) * sub, :] = (_rms(xs) * gs + sh).astype(BF16)

    if rope:
        pl.when(first)(lambda: table_copy().wait())

    def epi_rk(a, rows):
        if rope:
            a = _rope_ret(a, cr_ref[rows, :], sr_ref[rows, :])
        return a * RET_SCALE

    def epi_rq(a, rows):
        return _rope_ret(a, cr_ref[rows, :], sr_ref[rows, :]) if rope else a

    def epi_dk(a, rows):
        return _rope_diff(a, cd_ref[rows, :], sdl_ref[rows, :], sdh_ref[rows, :]) if rope else a

    def epi_dq(a, rows):
        return epi_dk(a, rows) * DIFF_Q_SCALE

    def epi_ln(a, rows):
        mu = jnp.mean(a, axis=-1, keepdims=True)
        d = a - mu
        return d * lax.rsqrt(jnp.mean(d * d, axis=-1, keepdims=True) + EPS)

    epilogues = dict(
        plain=lambda a, rows: a,
        sigmoid=lambda a, rows: jax.nn.sigmoid(a),
        silu=lambda a, rows: a * jax.nn.sigmoid(a),
        rk=epi_rk, rq=epi_rq, dk=epi_dk, dq=epi_dq, ln=epi_ln)

    def writeback(j):
        slot = j % n_slots
        return pltpu.make_async_copy(stage.at[slot], z_hbm.at[b, j], sems_out.at[slot])

    def run(epilogue, j):
        w_sec = w_vmem.at[src_of(j)]
        out = stage.at[j % n_slots]
        for r in range(n_sub):
            rows = slice(r * sub, (r + 1) * sub)
            out[rows, :] = epilogue(_dot(h_ref[rows, :], w_sec[...]), rows).astype(BF16)

    def section(j, carry):
        pl.when(first)(functools.partial(load_weights, j))

        @pl.when(j >= n_slots)
        def _():
            writeback(j - n_slots).wait()

        for kind in sorted(set(kinds)):
            idx = [i for i, k in enumerate(kinds) if k == kind]
            cond = j == idx[0]
            for i in idx[1:]:
                cond = cond | (j == i)
            pl.when(cond)(functools.partial(run, epilogues[kind], j))
        writeback(j).start()
        return carry

    lax.fori_loop(0, n_sec, section, 0)
    for j in range(max(0, n_sec - n_slots), n_sec):
        writeback(j).wait()


def _projection(xs, l, g_pre, mod4, row_of, w_all, src_of, kinds, tables):
    n_g, n_rows, _ = xs.shape
    n_sec = len(kinds)
    rope = tables is not None
    n_src = w_all.shape[2] // SEC if src_of is _full_src else n_sec
    in_specs = [
        pl.BlockSpec((1, n_rows, D_MODEL), lambda b: (b, 0, 0)),
        _layer_spec(l, (1, D_MODEL)),
        _mod_spec(l, row_of, 0),
        _mod_spec(l, row_of, 1),
        pl.BlockSpec(memory_space=pl.ANY),
    ]
    args = [xs, g_pre, mod4, mod4, w_all]
    scratch = [pltpu.VMEM((n_rows, D_MODEL), BF16),
               pltpu.VMEM((PROJ_OUT_SLOTS, n_rows, SEC), BF16),
               pltpu.VMEM((n_src, D_MODEL, SEC), BF16),
               pltpu.VMEM((2, D_MODEL, SEC), F32),
               pltpu.SemaphoreType.DMA((PROJ_OUT_SLOTS,)),
               pltpu.SemaphoreType.DMA((2,))]
    if rope:
        in_specs.append(pl.BlockSpec(memory_space=pl.ANY))
        args.append(tables)
        scratch += [pltpu.VMEM(tables.shape, F32), pltpu.SemaphoreType.DMA((1,))]
    return pl.pallas_call(
        functools.partial(_proj_kernel, layer=l, kinds=kinds, src_of=src_of, n_src=n_src,
                          rope=rope, n_rows=n_rows, sub=PROJ_SUB),
        grid=(n_g,),
        in_specs=in_specs,
        out_specs=pl.BlockSpec(memory_space=pl.ANY),
        out_shape=jax.ShapeDtypeStruct((n_g, n_sec, n_rows, SEC), BF16),
        scratch_shapes=scratch,
        compiler_params=pltpu.CompilerParams(vmem_limit_bytes=VMEM_LIMIT),
        name="projection_rope" if rope else "projection",
    )(*args)


def _retention_tasks(lgf, lgb, cs, q_ref, g_ref, k_ref, v_ref, kc_ref, vc_ref, o_ref, inc_ref,
                     st_ref, *, n, n_ctx, rc):
    nc = n // rc
    pos = lax.broadcasted_iota(jnp.int32, (rc, LANES), 0).astype(F32)
    kdf = jnp.exp(lgf * (rc - 1.0 - pos)).astype(BF16)
    kdb = jnp.exp(lgb * pos).astype(BF16)
    cdf = jnp.exp(jnp.full((1, LANES), lgf * rc, F32))
    cdb = jnp.exp(jnp.full((1, LANES), lgb * rc, F32))
    qdf = jnp.exp(lgf * (pos + 1.0)).astype(BF16)
    qdb = jnp.exp(lgb * (rc - pos)).astype(BF16)
    ii = lax.broadcasted_iota(jnp.int32, (rc, rc), 0)
    jj = lax.broadcasted_iota(jnp.int32, (rc, rc), 1)
    dist = (ii - jj).astype(F32)
    dmat = (jnp.where(dist >= 0, jnp.exp(lgf * jnp.maximum(dist, 0.0)), 0.0)
            + jnp.where(dist <= 0, jnp.exp(lgb * jnp.maximum(-dist, 0.0)), 0.0))
    state = {}

    def increment(kr, vr, nn):
        rows = slice(nn * rc, (nn + 1) * rc)
        kk = kr[0, rows, cs]
        kcat = jnp.concatenate([kk * kdf, kk * kdb], axis=1)
        vt = vr[0, rows, cs].astype(F32).T.astype(BF16)
        return _dot(vt, kcat)

    def init_task():
        sf = jnp.zeros((HEAD_W, HEAD_W), F32)
        sb = jnp.zeros((HEAD_W, HEAD_W), F32)
        if n_ctx:
            incs = [increment(kc_ref, vc_ref, nn) for nn in range(n_ctx // rc)]
            for inc in incs:
                sf = cdf * sf + inc[:, :HEAD_W]
            for inc in reversed(incs):
                sb = cdb * sb + inc[:, HEAD_W:]
        state["sf"], state["sb"] = sf, sb

    def inc_task(nn):
        inc_ref[nn] = increment(k_ref, v_ref, nn)

    def scan_task():
        sf, sb = state["sf"], state["sb"]
        for nn in range(nc):
            st_ref[nn, :, 0:HEAD_W] = sf.astype(BF16)
            sf = cdf * sf + inc_ref[nn, :, 0:HEAD_W]
        for nn in reversed(range(nc)):
            st_ref[nn, :, HEAD_W:] = sb.astype(BF16)
            sb = cdb * sb + inc_ref[nn, :, HEAD_W:]

    def out_task(nn):
        rows = slice(nn * rc, (nn + 1) * rc)
        q = q_ref[0, rows, cs]
        s = _dot_nt(q, k_ref[0, rows, cs]) * dmat
        inner = _dot(s.astype(BF16), v_ref[0, rows, cs])
        qcat = jnp.concatenate([q * qdf, q * qdb], axis=1)
        out = inner + _dot_nt(qcat, st_ref[nn])
        o_ref[0, rows, cs] = _rms(out).astype(BF16) * g_ref[0, rows, cs]

    inc_tasks = [init_task] + [functools.partial(inc_task, nn) for nn in range(nc)]
    return inc_tasks, scan_task, [functools.partial(out_task, nn) for nn in range(nc)]


def _ret_kernel(lg_ref, q_ref, g_ref, k_ref, v_ref, o_ref, inc_ref, st_ref, *, layer, n, rc,
                heads):
    tasks = []
    for hh in range(heads):
        head = pl.program_id(1) * heads + hh
        tasks.append(_retention_tasks(
            lg_ref[layer, 0, head], lg_ref[layer, 1, head], slice(hh * LANES, (hh + 1) * LANES),
            q_ref, g_ref, k_ref, v_ref, None, None, o_ref, inc_ref.at[hh], st_ref.at[hh],
            n=n, n_ctx=0, rc=rc))
    for i in range(len(tasks[0][0])):
        for inc_tasks, _, _ in tasks:
            inc_tasks[i]()
    for _, scan_task, _ in tasks:
        scan_task()
    for i in range(len(tasks[0][2])):
        for _, _, out_tasks in tasks:
            out_tasks[i]()


def _retention_ctx(log_gamma, layer, zc, sec, n_b, n, *, rc, heads):
    width = heads * LANES
    head_block = lambda b, h: h
    return pl.pallas_call(
        functools.partial(_ret_kernel, layer=layer, n=n, rc=rc, heads=heads),
        grid=(n_b, HEADS // heads),
        in_specs=[pl.BlockSpec(memory_space=pltpu.SMEM)]
        + [_z_spec(sec[name], n, width, True, col_of=head_block)
           for name in ("rq", "rg", "rk", "rv")],
        out_specs=pl.BlockSpec((1, n, width), lambda b, h: (b, 0, h)),
        out_shape=jax.ShapeDtypeStruct((n_b, n, BRANCH_W), BF16),
        scratch_shapes=[pltpu.VMEM((heads, n // rc, HEAD_W, 2 * HEAD_W), F32),
                        pltpu.VMEM((heads, n // rc, HEAD_W, 2 * HEAD_W), BF16)],
        compiler_params=pltpu.CompilerParams(vmem_limit_bytes=VMEM_LIMIT),
        name="retention_ctx",
    )(log_gamma, zc, zc, zc, zc)


def _split_maps(q, lane):
    zero = jnp.zeros_like(q)
    return jnp.concatenate([jnp.where(lane < DIFF_DQK, q, zero),
                            jnp.where(lane >= DIFF_DQK, q, zero)], axis=0)


def _diff_ctx_kernel(lam_ref, q_ref, g_ref, k_ref, v_ref, o_ref, *, layer, tq, post_scale):
    lane = lax.broadcasted_iota(jnp.int32, (tq, LANES), 1)
    for h in range(HEADS):
        cs = slice(h * LANES, (h + 1) * LANES)
        s = _dot_nt(_split_maps(q_ref[0, :, cs], lane), k_ref[0, :, cs])
        p = jnp.exp2(s - jnp.max(s, axis=-1, keepdims=True))
        o = _dot(p.astype(BF16), v_ref[0, :, cs]) / jnp.sum(p, axis=-1, keepdims=True)
        d = o[:tq] - lam_ref[layer] * o[tq:]
        o_ref[0, :, cs] = (_rms(d) * post_scale * g_ref[0, :, cs].astype(F32)).astype(BF16)


def _diff_attention_ctx(lam, layer, zc, sec, n_b, n, *, post_scale):
    return pl.pallas_call(
        functools.partial(_diff_ctx_kernel, layer=layer, tq=n, post_scale=post_scale),
        grid=(n_b,),
        in_specs=[pl.BlockSpec(memory_space=pltpu.SMEM)]
        + [_z_spec(sec[name], n, SEC, True) for name in ("dq", "dg", "dk", "dv")],
        out_specs=pl.BlockSpec((1, n, BRANCH_W), lambda b: (b, 0, 0)),
        out_shape=jax.ShapeDtypeStruct((n_b, n, BRANCH_W), BF16),
        compiler_params=pltpu.CompilerParams(vmem_limit_bytes=VMEM_LIMIT),
        name="diff_attention_ctx",
    )(lam, zc, zc, zc, zc)


def _mixer_kernel(lam_ref, lg_ref, q_ref, g_ref, kc_ref, vc_ref, kl_ref, vl_ref,
                  rq_ref, rg_ref, rk_ref, rv_ref, rkc_ref, rvc_ref, od_ref, or_ref,
                  kall, vt, s0, s1, p0, p1, d0, d1, inc_ref, st_ref, *, layer, n, n_ctx, ts,
                  rc, post_scale):
    n_sub = n // ts
    nk = kall.shape[0]
    kall[0:n_ctx, :] = kc_ref[0]
    kall[n_ctx:, :] = kl_ref[0]
    for c in range(nk // LANES):
        src, base = (vc_ref, 0) if c * LANES < n_ctx else (vl_ref, n_ctx)
        rows = slice(c * LANES - base, (c + 1) * LANES - base)
        vt[0:LANES, c * LANES:(c + 1) * LANES] = src[0, rows, :].astype(F32).T.astype(BF16)
    vt[LANES:, :] = jnp.ones((vt.shape[0] - LANES, nk), BF16)
    lam = lam_ref[layer]
    lane = lax.broadcasted_iota(jnp.int32, (ts, LANES), 1)
    s_bufs, p_bufs, d_bufs = (s0, s1), (p0, p1), (d0, d1)

    def stage_a(t):
        rows = slice(t * ts, (t + 1) * ts)
        s = _dot_nt(kall[...], _split_maps(q_ref[0, rows, :], lane))
        s_bufs[t % 2][...] = s
        d_bufs[t % 2][...] = jnp.max(s, axis=0, keepdims=True)

    def stage_b(t):
        p_bufs[t % 2][...] = jnp.exp2(s_bufs[t % 2][...] - d_bufs[t % 2][...]).astype(BF16)

    def stage_c(t):
        rows = slice(t * ts, (t + 1) * ts)
        oe = _dot(vt[...], p_bufs[t % 2][...])
        ot = oe[:LANES] / oe[LANES:LANES + 1]
        dt = ot[:, :ts] - lam * ot[:, ts:]
        dt = dt * lax.rsqrt(jnp.mean(dt * dt, axis=0, keepdims=True) + EPS)
        od_ref[0, rows, :] = (dt.T * post_scale * g_ref[0, rows, :].astype(F32)).astype(BF16)

    head = pl.program_id(1)
    inc_tasks, scan_task, out_tasks = _retention_tasks(
        lg_ref[layer, 0, head], lg_ref[layer, 1, head], slice(0, LANES),
        rq_ref, rg_ref, rk_ref, rv_ref, rkc_ref, rvc_ref, or_ref, inc_ref, st_ref,
        n=n, n_ctx=n_ctx, rc=rc)
    n_steps = n_sub + 2
    extra = [[] for _ in range(n_steps)]
    extra[0] += inc_tasks
    extra[1].append(scan_task)
    for i, task in enumerate(out_tasks):
        extra[1 + (i * (n_steps - 2)) // len(out_tasks)].append(task)

    for t in range(n_steps):
        if t >= 2:
            stage_c(t - 2)
        if 1 <= t <= n_sub:
            stage_b(t - 1)
        if t < n_sub:
            stage_a(t)
        for task in extra[t]:
            task()


def _latent_mixers(lam, log_gamma, layer, z, sec, n_b, n, zc, sec_c, n_ctx, *, ts, rc,
                   post_scale):
    nk = n_ctx + n
    head_block = lambda b, h: h
    lat = lambda name: _z_spec(sec[name], n, LANES, False, col_of=head_block)
    cx = lambda name: _z_spec(sec_c[name], n_ctx, LANES, True, col_of=head_block)
    out_spec = pl.BlockSpec((1, n, LANES), lambda b, h: (b, 0, h))
    out_shape = jax.ShapeDtypeStruct((n_b, n, BRANCH_W), BF16)
    o_diff, o_ret = pl.pallas_call(
        functools.partial(_mixer_kernel, layer=layer, n=n, n_ctx=n_ctx, ts=ts, rc=rc,
                          post_scale=post_scale),
        grid=(n_b, HEADS),
        in_specs=[pl.BlockSpec(memory_space=pltpu.SMEM), pl.BlockSpec(memory_space=pltpu.SMEM),
                  lat("dq"), lat("dg"), cx("dk"), cx("dv"), lat("dk"), lat("dv"),
                  lat("rq"), lat("rg"), lat("rk"), lat("rv"), cx("rk"), cx("rv")],
        out_specs=[out_spec, out_spec],
        out_shape=[out_shape, out_shape],
        scratch_shapes=[pltpu.VMEM((nk, LANES), BF16), pltpu.VMEM((LANES + 16, nk), BF16),
                        pltpu.VMEM((nk, 2 * ts), F32), pltpu.VMEM((nk, 2 * ts), F32),
                        pltpu.VMEM((nk, 2 * ts), BF16), pltpu.VMEM((nk, 2 * ts), BF16),
                        pltpu.VMEM((1, 2 * ts), F32), pltpu.VMEM((1, 2 * ts), F32),
                        pltpu.VMEM((n // rc, HEAD_W, 2 * HEAD_W), F32),
                        pltpu.VMEM((n // rc, HEAD_W, 2 * HEAD_W), BF16)],
        compiler_params=pltpu.CompilerParams(vmem_limit_bytes=VMEM_LIMIT),
        name="latent_mixers",
    )(lam, log_gamma, z, z, zc, zc, z, z, z, z, z, z, zc, zc)
    return o_ret, o_diff


def _merge_kernel(x_ref, gate_ref, gpost_ref, oret_ref, odiff_ref, u_ref, vn_ref, mg_ref,
                  gr_ref, gm_ref, gd_ref, ws_ref, bs_ref, wbo_ref, wo_ref, out_ref, *, tm):
    rows = []
    for c in range(tm // CHUNK):
        cols = []
        for g in range(HEADS):
            blk = vn_ref[0, c * CHUNK:(c + 1) * CHUNK, g * LANES:(g + 1) * LANES]
            cols.append(_dot(ws_ref[g], blk) + bs_ref[g])
        rows.append(jnp.concatenate(cols, axis=1))
    sp = jnp.concatenate(rows, axis=0)
    o_mlp = (u_ref[0].astype(F32) * sp * mg_ref[0].astype(F32)).astype(BF16)

    def gated(gate2_ref, o, w):
        gate = jnp.concatenate([gate2_ref[0, 0], gate2_ref[0, 1]], axis=1).astype(F32)
        return gate * _dot(o, w)

    t = (gated(gr_ref, oret_ref[0], wbo_ref[0]) + gated(gm_ref, o_mlp, wbo_ref[1])
         + gated(gd_ref, odiff_ref[0], wbo_ref[2]))
    y = _dot(t.astype(BF16), wo_ref[...])
    out_ref[0] = x_ref[0] + gate_ref[...] * (_rms(y) * gpost_ref[...])


def _merge(xs, l, mod4, row_of, g_post, o_ret, o_diff, z, sec, ws, bs, wbo, wo, *, tm):
    n_b, n, _ = xs.shape
    row_block = lambda b, i: i

    def zsec(name):
        return _z_spec(sec[name], tm, SEC, False, row_of=row_block)

    def zgate(k):
        return pl.BlockSpec((1, 2, tm, SEC), lambda b, i: (b, k, i, 0))

    in_specs = [
        pl.BlockSpec((1, tm, D_MODEL), lambda b, i: (b, i, 0)),
        _mod_spec(l, row_of, 2),
        _layer_spec(l, (1, D_MODEL)),
        pl.BlockSpec((1, tm, BRANCH_W), lambda b, i: (b, i, 0)),
        pl.BlockSpec((1, tm, BRANCH_W), lambda b, i: (b, i, 0)),
        zsec("mu"), zsec("mv"), zsec("mg"), zgate(0), zgate(1), zgate(2),
        _layer_spec(l, (HEADS, CHUNK, CHUNK)),
        _layer_spec(l, (HEADS, CHUNK, LANES)),
        _layer_spec(l, (N_BRANCH, BRANCH_W, D_MODEL)),
        _layer_spec(l, (D_MODEL, D_MODEL)),
    ]
    return pl.pallas_call(
        functools.partial(_merge_kernel, tm=tm),
        grid=(n_b, n // tm),
        in_specs=in_specs,
        out_specs=pl.BlockSpec((1, tm, D_MODEL), lambda b, i: (b, i, 0)),
        out_shape=jax.ShapeDtypeStruct(xs.shape, F32),
        compiler_params=pltpu.CompilerParams(vmem_limit_bytes=VMEM_LIMIT),
        name="merge",
    )(xs, mod4, g_post, o_ret, o_diff, z, z, z, z, z, z, ws, bs, wbo, wo)


def _rope_tables(n_lat):
    rows = n_lat // GRID_W
    row_pos = jnp.repeat(jnp.arange(rows, dtype=F32), GRID_W)
    col_pos = jnp.tile(jnp.arange(GRID_W, dtype=F32), rows)

    def angles(head_dim):
        n_freq = head_dim // 4
        inv = ROPE_BASE ** (-jnp.arange(n_freq, dtype=F32) / n_freq)
        ang = jnp.concatenate([row_pos[:, None] * inv, col_pos[:, None] * inv], axis=-1)
        return jnp.cos(ang), jnp.sin(ang)

    cos_r, sin_r = angles(HEAD_W)
    cos_d, sin_d = angles(DIFF_DQK)
    zeros = jnp.zeros_like(sin_d)
    c_r = jnp.concatenate([cos_r, cos_r], axis=-1)
    s_r = jnp.concatenate([-sin_r, sin_r], axis=-1)
    c_d = jnp.tile(jnp.concatenate([cos_d, cos_d], axis=-1), (1, 2))
    s_lo = jnp.tile(jnp.concatenate([-sin_d, zeros], axis=-1), (1, 2))
    s_hi = jnp.tile(jnp.concatenate([zeros, sin_d], axis=-1), (1, 2))
    return c_r, s_r, c_d, s_lo, s_hi


@jax.jit
def kernel(x, c, ctx, c_ctx, w_mod, b_mod, g_pre, g_post, w_in, ret_decay_logit, mlp_w_s,
           mlp_b_s, diff_lambda_q, diff_lambda_k, w_branch_out, w_out):
    n_b, n_lat, _ = x.shape
    n_ctx = ctx.shape[1]
    tables = jnp.stack(_rope_tables(n_lat))

    cond_rows = 16
    ctx_row = n_b
    cc = jnp.zeros((cond_rows, D_MODEL), F32).at[:n_b].set(c).at[ctx_row].set(c_ctx)
    mod4 = _modulation(cc, w_mod, b_mod).reshape(DEPTH, cond_rows, 1, 3 * D_MODEL)
    lat_row = lambda b: b
    ctx_row_of = lambda b: ctx_row

    log_gamma = -jax.nn.softplus(-ret_decay_logit.astype(F32))
    lam_inits = [0.8 - 0.6 * math.exp(-0.3 * l) for l in range(DEPTH)]
    lam = (jnp.exp(jnp.sum(diff_lambda_q[:, 0] * diff_lambda_k[:, 0], axis=-1))
           - jnp.exp(jnp.sum(diff_lambda_q[:, 1] * diff_lambda_k[:, 1], axis=-1))
           + jnp.asarray(lam_inits, F32)).astype(F32)

    w_sec = w_in
    ws = mlp_w_s.astype(BF16)
    bs = jnp.broadcast_to(mlp_b_s[..., None], (DEPTH, HEADS, CHUNK, LANES)).astype(F32)
    wbo = w_branch_out.astype(BF16)
    wo = w_out.astype(BF16)
    g_pre3 = g_pre.reshape(DEPTH, 1, D_MODEL)
    g_post3 = g_post.reshape(DEPTH, 1, D_MODEL)

    for l in range(DEPTH):
        last = l == DEPTH - 1
        post_scale = 1.0 - lam_inits[l]

        ctx_flat = ctx.reshape(1, n_b * n_ctx, D_MODEL)
        if last:
            zc = _projection(ctx_flat, l, g_pre3, mod4, ctx_row_of, w_sec, lambda j: j,
                             KV_KINDS, None)
            sec_c = KV_SEC
        else:
            zc = _projection(ctx_flat, l, g_pre3, mod4, ctx_row_of, w_sec, _full_src,
                             FULL_KINDS, None)
            sec_c = FULL_SEC
            co_ret = _retention_ctx(log_gamma, l, zc, sec_c, n_b, n_ctx, rc=RET_CHUNK,
                                    heads=HEADS)
            co_diff = _diff_attention_ctx(lam, l, zc, sec_c, n_b, n_ctx, post_scale=post_scale)
            ctx_next = _merge(ctx_flat, l, mod4, ctx_row_of, g_post3,
                              co_ret.reshape(1, n_b * n_ctx, BRANCH_W),
                              co_diff.reshape(1, n_b * n_ctx, BRANCH_W), zc, sec_c,
                              ws, bs, wbo, wo, tm=MERGE_TM)
            ctx_next = ctx_next.reshape(n_b, n_ctx, D_MODEL)

        z = _projection(x, l, g_pre3, mod4, lat_row, w_sec, _full_src, FULL_KINDS, tables)
        o_ret, o_diff = _latent_mixers(lam, log_gamma, l, z, FULL_SEC, n_b, n_lat, zc, sec_c,
                                       n_ctx, ts=DIFF_TS, rc=RET_CHUNK, post_scale=post_scale)
        x = _merge(x, l, mod4, lat_row, g_post3, o_ret, o_diff, z, FULL_SEC, ws, bs, wbo, wo,
                   tm=MERGE_TM)
        if not last:
            ctx = ctx_next
    return x
```
